```python
import math
import jax, jax.numpy as jnp
from jax import lax
import numpy as np

D_MODEL = 1024
BATCH = 2
SEQ = 8192
DEPTH = 1

N_HEADS = 8
HEAD_DIM = 64
N_KV_HEADS = 2
GQA_REP = N_HEADS // N_KV_HEADS
D_ATTN = N_HEADS * HEAD_DIM
D_KV = N_KV_HEADS * HEAD_DIM
D_CONV = D_MODEL - D_ATTN
N_CONV_GROUPS = 8
CONV_K = 3
D_MIX = D_ATTN + D_CONV
CMP_BLOCK = 32
CMP_STRIDE = 16
CMP_HIDDEN = 2 * HEAD_DIM
SEL_BLOCK = 64
N_SEL = 16
WINDOW = 512
Q_BLOCK = 128
N_BUCKETS = 32
MAX_DISTANCE = 128
EPS = 1e-6
NEG = -1e30
FORCE_SCORE = 1e3
IN_SPLITS = (D_ATTN, 6 * D_KV, 3 * N_HEADS, D_ATTN, D_CONV, D_CONV, D_CONV, D_CONV)
IN_COLS = sum(IN_SPLITS)

kernel_name = "hymba_nsa_shortconv_hybrid"


def rms_norm(x, w):
    xf = x.astype(jnp.float32)
    y = xf * lax.rsqrt(jnp.mean(xf * xf, axis=-1, keepdims=True) + EPS)
    return (y * w.astype(jnp.float32)).astype(x.dtype)


def t5_bucket(dist):
    n = jnp.maximum(dist, 0)
    max_exact = N_BUCKETS // 2
    nf = jnp.maximum(n, 1).astype(jnp.float32)
    large = max_exact + (jnp.log(nf / max_exact) / math.log(MAX_DISTANCE / max_exact)
                         * (N_BUCKETS - max_exact)).astype(jnp.int32)
    large = jnp.minimum(large, N_BUCKETS - 1)
    return jnp.where(n < max_exact, n, large)


def masked_softmax(s, mask):
    s = jnp.where(mask, s, NEG)
    m = jnp.max(s, axis=-1, keepdims=True)
    p = jnp.where(mask, jnp.exp(s - m), 0.0)
    return p / jnp.maximum(jnp.sum(p, axis=-1, keepdims=True), 1e-30)


def compress(kv, w1, w2, pe):
    b, s, g, dk = kv.shape
    n_chunks = s // CMP_STRIDE
    per_block = CMP_BLOCK // CMP_STRIDE
    n_cmp = n_chunks - per_block + 1
    chunks = kv.reshape(b, n_chunks, CMP_STRIDE, g, dk)
    hid = 0.0
    for m in range(per_block):
        lo, hi = m * CMP_STRIDE, (m + 1) * CMP_STRIDE
        seg = chunks[:, m:m + n_cmp] + pe[lo:hi][None, None, :, None, :]
        hid = hid + jnp.einsum('bcjgd,jdh->bgch', seg, w1[lo:hi])
    return jnp.einsum('bgch,hd->bgcd', jax.nn.silu(hid), w2)


def cmp_to_sel_overlap(n_cmp, n_sel):
    start = np.arange(n_cmp) * CMP_STRIDE
    end = start + CMP_BLOCK
    sel_start = np.arange(n_sel) * SEL_BLOCK
    ov = (start[:, None] < sel_start[None, :] + SEL_BLOCK) & (end[:, None] > sel_start[None, :])
    return jnp.asarray(ov.astype(np.float32))


def nsa_mixer(q, k_cmp, v_cmp, k_slc, v_slc, k_win, v_win, gates,
              w_ck1, w_ck2, pe_k, w_cv1, w_cv2, pe_v, rel_bias):
    b, s = q.shape[:2]
    G, R, dk = N_KV_HEADS, GQA_REP, HEAD_DIM
    qh = (q * (HEAD_DIM ** -0.5)).reshape(b, s, G, R, dk)
    kc = compress(k_cmp, w_ck1, w_ck2, pe_k)
    vc = compress(v_cmp, w_cv1, w_cv2, pe_v)
    n_cmp = kc.shape[2]
    n_sel = s // SEL_BLOCK
    k_top = min(N_SEL, n_sel)
    cmp_end = jnp.arange(n_cmp, dtype=jnp.int32) * CMP_STRIDE + CMP_BLOCK - 1
    overlap = cmp_to_sel_overlap(n_cmp, n_sel)
    ks_blocks = k_slc.transpose(0, 2, 1, 3).reshape(b, G, n_sel, SEL_BLOCK, dk)
    vs_blocks = v_slc.transpose(0, 2, 1, 3).reshape(b, G, n_sel, SEL_BLOCK, dk)
    kw = jnp.pad(k_win.transpose(0, 2, 1, 3), ((0, 0), (0, 0), (WINDOW, 0), (0, 0)))
    vw = jnp.pad(v_win.transpose(0, 2, 1, 3), ((0, 0), (0, 0), (WINDOW, 0), (0, 0)))
    tbl_g = rel_bias.reshape(N_BUCKETS, G, R).transpose(1, 0, 2)
    bi = jnp.arange(b)[:, None, None, None]
    gi = jnp.arange(G)[None, :, None, None]
    sel_ids = jnp.arange(n_sel, dtype=jnp.int32)

    n_qb = s // Q_BLOCK
    q_blocks = qh.reshape(b, n_qb, Q_BLOCK, G, R, dk).transpose(1, 0, 3, 4, 2, 5)
    g_blocks = gates.reshape(b, n_qb, Q_BLOCK, 3, G, R).transpose(1, 3, 0, 4, 5, 2)

    def head_bias(bucket):
        bias = rel_bias[bucket].astype(jnp.float32)
        return bias.transpose(2, 0, 1).reshape(G, R, bucket.shape[0], bucket.shape[1])

    def block_fn(args):
        qb, gb, i = args
        q0 = i * Q_BLOCK
        t = q0 + jnp.arange(Q_BLOCK, dtype=jnp.int32)

        s_c = jnp.einsum('bgrqd,bgcd->bgrqc', qb, kc).astype(jnp.float32)
        dist_c = t[:, None] - cmp_end[None, :]
        p_c = masked_softmax(s_c + head_bias(t5_bucket(dist_c)), dist_c >= 0)
        o_c = jnp.einsum('bgrqc,bgcd->bgrqd', p_c.astype(vc.dtype), vc)

        imp = jnp.einsum('bgrqc,cn->bgqn', p_c, overlap)
        blk_t = t // SEL_BLOCK
        causal_blk = sel_ids[None, :] <= blk_t[:, None]
        forced = (sel_ids[None, :] == 0) | (sel_ids[None, :] == blk_t[:, None]) | (sel_ids[None, :] == blk_t[:, None] - 1)
        score = jnp.where(forced, FORCE_SCORE, jnp.where(causal_blk, imp, -1.0))
        _, idx = lax.top_k(score, k_top)
        k_g = ks_blocks[bi, gi, idx]
        v_g = vs_blocks[bi, gi, idx].reshape(b, G, Q_BLOCK, k_top * SEL_BLOCK, dk)
        s_s = jnp.einsum('bgrqd,bgqnkd->bgrqnk', qb, k_g).astype(jnp.float32)
        pos_s = idx[..., None] * SEL_BLOCK + jnp.arange(SEL_BLOCK, dtype=jnp.int32)
        dist_s = t[None, None, :, None, None] - pos_s
        bias_s = tbl_g[gi[..., None], t5_bucket(dist_s)].astype(jnp.float32)
        bias_s = bias_s.transpose(0, 1, 5, 2, 3, 4)
        flat = (b, G, R, Q_BLOCK, k_top * SEL_BLOCK)
        mask_s = (dist_s >= 0)[:, :, None].reshape(b, G, 1, Q_BLOCK, k_top * SEL_BLOCK)
        p_s = masked_softmax((s_s + bias_s).reshape(flat), mask_s)
        o_s = jnp.einsum('bgrqm,bgqmd->bgrqd', p_s.astype(v_g.dtype), v_g)

        kw_blk = lax.dynamic_slice_in_dim(kw, q0, WINDOW + Q_BLOCK, axis=2)
        vw_blk = lax.dynamic_slice_in_dim(vw, q0, WINDOW + Q_BLOCK, axis=2)
        pos_w = q0 - WINDOW + jnp.arange(WINDOW + Q_BLOCK, dtype=jnp.int32)
        dist_w = t[:, None] - pos_w[None, :]
        mask_w = (dist_w >= 0) & (dist_w < WINDOW) & (pos_w[None, :] >= 0)
        s_w = jnp.einsum('bgrqd,bgkd->bgrqk', qb, kw_blk).astype(jnp.float32)
        p_w = masked_softmax(s_w + head_bias(t5_bucket(dist_w)), mask_w)
        o_w = jnp.einsum('bgrqk,bgkd->bgrqd', p_w.astype(vw_blk.dtype), vw_blk)

        return gb[0][..., None] * o_c + gb[1][..., None] * o_s + gb[2][..., None] * o_w

    out = lax.map(block_fn, (q_blocks, g_blocks, jnp.arange(n_qb, dtype=jnp.int32)))
    return out.transpose(1, 0, 4, 2, 3, 5).reshape(b, s, D_ATTN)


def short_conv_mixer(h, b_gate, c_gate, conv_w):
    u = c_gate * h
    y = lax.conv_general_dilated(u, conv_w[:, None, :].astype(u.dtype), window_strides=(1,),
                                 padding=[(CONV_K - 1, 0)],
                                 dimension_numbers=('NWC', 'WIO', 'NWC'),
                                 feature_group_count=D_CONV)
    return b_gate * y


def setup_inputs(seed: int = 0) -> dict:
    key = jax.random.key(seed)
    ks = jax.random.split(key, 16)
    f32 = jnp.float32
    nrm = lambda k, shape, scale: jax.random.normal(k, shape, f32) * scale
    return {
        "x": nrm(ks[0], (BATCH, SEQ, D_MODEL), 1.0),
        "norm_w": 1.0 + nrm(ks[1], (DEPTH, D_MODEL), 0.01),
        "w_in": nrm(ks[2], (DEPTH, D_MODEL, IN_COLS), D_MODEL ** -0.5),
        "w_ck1": nrm(ks[3], (DEPTH, CMP_BLOCK, HEAD_DIM, CMP_HIDDEN), (CMP_BLOCK * HEAD_DIM) ** -0.5),
        "w_ck2": nrm(ks[4], (DEPTH, CMP_HIDDEN, HEAD_DIM), CMP_HIDDEN ** -0.5),
        "pe_k": nrm(ks[5], (DEPTH, CMP_BLOCK, HEAD_DIM), 0.02),
        "w_cv1": nrm(ks[6], (DEPTH, CMP_BLOCK, HEAD_DIM, CMP_HIDDEN), (CMP_BLOCK * HEAD_DIM) ** -0.5),
        "w_cv2": nrm(ks[7], (DEPTH, CMP_HIDDEN, HEAD_DIM), CMP_HIDDEN ** -0.5),
        "pe_v": nrm(ks[8], (DEPTH, CMP_BLOCK, HEAD_DIM), 0.02),
        "conv_w": nrm(ks[9], (DEPTH, CONV_K, D_CONV), CONV_K ** -0.5),
        "w_out": nrm(ks[10], (DEPTH, D_MIX, D_MODEL), D_MIX ** -0.5),
        "rel_bias": nrm(ks[11], (N_BUCKETS, N_HEADS), 0.5),
        "final_norm_w": 1.0 + nrm(ks[12], (D_MODEL,), 0.01),
    }


def reference(x, norm_w, w_in, w_ck1, w_ck2, pe_k, w_cv1, w_cv2, pe_v, conv_w, w_out,
              rel_bias, final_norm_w):
    b, s, _ = x.shape
    split_at = [int(c) for c in np.cumsum(IN_SPLITS)[:-1]]
    for l in range(DEPTH):
        h = rms_norm(x, norm_w[l])
        proj = h @ w_in[l]
        q, kv, gate_logits, z_attn, conv_h, conv_b, conv_c, z_conv = jnp.split(proj, split_at, axis=-1)
        k_cmp, v_cmp, k_slc, v_slc, k_win, v_win = [
            a.reshape(b, s, N_KV_HEADS, HEAD_DIM) for a in jnp.split(kv, 6, axis=-1)]
        gates = jax.nn.sigmoid(gate_logits).reshape(b, s, 3, N_HEADS)
        attn = nsa_mixer(q.reshape(b, s, N_HEADS, HEAD_DIM), k_cmp, v_cmp, k_slc, v_slc,
                         k_win, v_win, gates, w_ck1[l], w_ck2[l], pe_k[l],
                         w_cv1[l], w_cv2[l], pe_v[l], rel_bias)
        conv = short_conv_mixer(conv_h, conv_b, conv_c, conv_w[l])
        mixed = jnp.concatenate([attn * jax.nn.silu(z_attn), conv * jax.nn.silu(z_conv)], axis=-1)
        x = x + mixed @ w_out[l]
    return rms_norm(x, final_norm_w)
```

```python
import math
from functools import partial

import numpy as np
import jax
import jax.numpy as jnp
from jax import lax
from jax.experimental import pallas as pl
from jax.experimental.pallas import tpu as pltpu

D_MODEL = 1024
N_HEADS = 8
HEAD_DIM = 64
N_KV_HEADS = 2
GQA_REP = N_HEADS // N_KV_HEADS
D_ATTN = N_HEADS * HEAD_DIM
D_KV = N_KV_HEADS * HEAD_DIM
D_CONV = D_MODEL - D_ATTN
CMP_BLOCK = 32
CMP_STRIDE = 16
CMP_HIDDEN = 2 * HEAD_DIM
SEL_BLOCK = 64
N_SEL = 16
WINDOW = 512
N_BUCKETS = 32
MAX_DISTANCE = 128
EPS = 1e-6
NEG = -1e30
FORCE_SCORE = 1e3

LANES = 128
ROW_TILE = 256
Q_TILE_CMP = 128
Q_TILE = 256
VMEM_LIMIT = 48 * 1024 * 1024

F32 = jnp.float32
BF16 = jnp.bfloat16

_C_Q = 0
_C_CMP = D_ATTN
_C_SW = _C_CMP + 2 * D_KV
_C_REST = _C_SW + 4 * D_KV
_C_GATE = _C_REST + D_ATTN + 4 * D_CONV
_C_END = _C_GATE + LANES
N_GATE = 3 * N_HEADS


def _bucket_lower_bounds():
    max_exact = N_BUCKETS // 2
    d = np.arange(2 * MAX_DISTANCE, dtype=np.int32)
    nf = np.maximum(d, 1).astype(np.float32)
    large = max_exact + (np.log(nf / np.float32(max_exact)) / np.float32(math.log(MAX_DISTANCE / max_exact))
                         * np.float32(N_BUCKETS - max_exact)).astype(np.int32)
    bucket = np.where(d < max_exact, d, np.minimum(large, N_BUCKETS - 1))
    assert np.all(np.diff(bucket) >= 0) and np.all(bucket[MAX_DISTANCE:] == N_BUCKETS - 1)
    return [int(np.argmax(bucket >= b)) for b in range(N_BUCKETS)]


_BUCKET_LO = _bucket_lower_bounds()


def _rel_bias_delta(dist, bias_of_bucket):
    last = bias_of_bucket(N_BUCKETS - 1)
    val = jnp.full(dist.shape, bias_of_bucket(0) - last, F32)
    for b in range(1, N_BUCKETS - 1):
        val = jnp.where(dist >= _BUCKET_LO[b], bias_of_bucket(b) - last, val)
    val = jnp.where(dist >= _BUCKET_LO[N_BUCKETS - 1], 0.0, val)
    return jnp.where(dist < 0, NEG, val)


def _silu(z):
    return z * jax.nn.sigmoid(z)


def _proj_kernel(x_ref, nw_ref, w_ref, qT_ref, kvc_ref, ksel_ref, vselT_ref, kwin_ref, vwinT_ref,
                 gsT_ref, rest_ref, *, tm, nsel, kd):
    st = pl.program_id(1)
    x = x_ref[0]
    ms = jnp.mean(x * x, axis=-1, keepdims=True)
    h = (x * lax.rsqrt(ms + EPS) * nw_ref[...]).astype(BF16)

    def mm(lo, hi):
        return jnp.dot(h, w_ref[:, lo:hi], preferred_element_type=F32)

    q = mm(_C_Q, _C_CMP) * (HEAD_DIM ** -0.5)
    qT_ref[0] = q.T.astype(BF16)

    kvc = mm(_C_CMP, _C_SW)
    kvc_ref[0, 0] = kvc[:, :D_KV]
    kvc_ref[1, 0] = kvc[:, D_KV:]

    a = mm(_C_SW, _C_REST)
    pos = st * tm + lax.broadcasted_iota(jnp.int32, (tm, nsel), 0)
    col = lax.broadcasted_iota(jnp.int32, (tm, nsel), 1)
    onehot = jnp.where((pos // SEL_BLOCK) == col, 1.0, 0.0).astype(BF16)
    ones = jnp.ones((HEAD_DIM, tm), BF16)
    zeros_k = jnp.zeros((tm, LANES - HEAD_DIM), BF16)
    vsT = a[:, D_KV:2 * D_KV].T
    vwT = a[:, 3 * D_KV:4 * D_KV].T
    for g in range(N_KV_HEADS):
        lo, hi = g * HEAD_DIM, (g + 1) * HEAD_DIM
        parts = [a[:, lo:hi].astype(BF16), onehot]
        if kd > HEAD_DIM + nsel:
            parts.append(jnp.zeros((tm, kd - HEAD_DIM - nsel), BF16))
        ksel_ref[0, g] = jnp.concatenate(parts, axis=1)
        vselT_ref[0, g, 0] = jnp.concatenate([vsT[lo:hi].astype(BF16), ones], axis=0)
        kwin_ref[0, g] = jnp.concatenate([a[:, 2 * D_KV + lo:2 * D_KV + hi].astype(BF16), zeros_k], axis=1)
        vwinT_ref[0, g, 0] = jnp.concatenate([vwT[lo:hi].astype(BF16), ones], axis=0)

    rest_ref[0] = mm(_C_REST, _C_GATE)
    gates = jax.nn.sigmoid(mm(_C_GATE, _C_END))
    gsT_ref[0] = gates.T[:N_GATE]


def _compress_kernel(c_ref, pe_ref, w1_ref, w2_ref, pad_ref, t_ref, *, nch):
    c = c_ref[0, 0]
    a0 = jnp.dot((c + pe_ref[0, 0]).astype(BF16), w1_ref[0, 0], preferred_element_type=F32)
    a1 = jnp.dot((c + pe_ref[0, 1]).astype(BF16), w1_ref[0, 1], preferred_element_type=F32)
    hid = a0 + pltpu.roll(a1, nch - 1, 0)
    out = jnp.dot(_silu(hid).astype(BF16), w2_ref[0], preferred_element_type=F32)
    zeros = jnp.zeros((nch, LANES - HEAD_DIM), BF16)
    for g in range(N_KV_HEADS):
        pad_ref[0, 0, g] = jnp.concatenate([out[:, g * HEAD_DIM:(g + 1) * HEAD_DIM].astype(BF16), zeros], axis=1)
    t_ref[0, 0] = out.T.astype(BF16)


def _cmp_kernel(rb_ref, qT_ref, kc_ref, vcT_ref, gs_ref, ovT_ref, ocT_ref, selb_ref, s_ref,
                *, nch, nsel, tq, k_top):
    g = pl.program_id(1)
    qt = pl.program_id(2)
    cpq = tq // CMP_STRIDE
    win = cpq + 16

    q = qT_ref[0]
    qT = jnp.concatenate([q[r * HEAD_DIM:(r + 1) * HEAD_DIM] for r in range(GQA_REP)], axis=1)
    qT = jnp.concatenate([qT, jnp.zeros((LANES - HEAD_DIM, GQA_REP * tq), BF16)], axis=0)
    s_ref[...] = jnp.dot(kc_ref[0, 0], qT, preferred_element_type=F32)

    start = pl.multiple_of(jnp.clip(cpq * qt - 16, 0, nch - win), 8)
    row = lax.broadcasted_iota(jnp.int32, (win, tq), 0) + start
    lane = lax.broadcasted_iota(jnp.int32, (win, tq), 1)
    dist = (qt * tq + lane) - (row * CMP_STRIDE + CMP_BLOCK - 1)
    delta = jnp.concatenate(
        [_rel_bias_delta(dist, lambda b, r=r: rb_ref[b, g * GQA_REP + r]) for r in range(GQA_REP)], axis=1)
    s_ref[pl.ds(start, win), :] = s_ref[pl.ds(start, win), :] + delta

    s = s_ref[...]
    crow = lax.broadcasted_iota(jnp.int32, s.shape, 0)
    s = jnp.where(crow >= cpq * (qt + 1), NEG, s)
    m = jnp.max(s, axis=0, keepdims=True)
    p = jnp.exp(s - m)
    l = jnp.sum(p, axis=0, keepdims=True)
    t4 = qt * tq + (lax.broadcasted_iota(jnp.int32, l.shape, 1) & (tq - 1))
    inv = jnp.where(t4 >= CMP_BLOCK - 1, 1.0 / jnp.maximum(l, 1e-30), 0.0)
    pn = (p * inv).astype(BF16)

    oc = jnp.dot(vcT_ref[0], pn, preferred_element_type=F32)
    for r in range(GQA_REP):
        ocT_ref[0, r * HEAD_DIM:(r + 1) * HEAD_DIM, :] = oc[:, r * tq:(r + 1) * tq] * gs_ref[0, 0, 0, r:r + 1, :]

    psum = pn[:, 0:tq].astype(F32)
    for r in range(1, GQA_REP):
        psum = psum + pn[:, r * tq:(r + 1) * tq].astype(F32)
    hi = psum.astype(BF16)
    lo = (psum - hi.astype(F32)).astype(BF16)
    ov = ovT_ref[...]
    imp = jnp.dot(ov, hi, preferred_element_type=F32) + jnp.dot(ov, lo, preferred_element_type=F32)

    n = lax.broadcasted_iota(jnp.int32, (nsel, tq), 0)
    blk = (qt * tq + lax.broadcasted_iota(jnp.int32, (nsel, tq), 1)) // SEL_BLOCK
    forced = (n == 0) | (n == blk) | (n == blk - 1)
    score = jnp.where(forced, FORCE_SCORE, jnp.where(n <= blk, imp, -1.0))

    def pick(_, carry):
        score, selb = carry
        mx = jnp.max(score, axis=0, keepdims=True)
        idx = jnp.min(jnp.where(score == mx, n, nsel), axis=0, keepdims=True)
        hit = n == idx
        return jnp.where(hit, -2.0, score), jnp.where(hit, 0.0, selb)

    _, selb = lax.fori_loop(0, k_top, pick, (score, jnp.full((nsel, tq), NEG, F32)))
    selb_ref[0, 0] = selb.astype(BF16)


def _flash_kernel(*refs, tq, nsel, kd, n_tbl_tiles, window, selected):
    if selected:
        rb_ref, qT_ref, selb_ref, k_ref, vT_ref, gs_ref, o_ref, tbl_ref, qa_ref, m_ref, acc_ref = refs
    else:
        rb_ref, qT_ref, k_ref, vT_ref, gs_ref, o_ref, tbl_ref, qa_ref, m_ref, acc_ref = refs
    g = pl.program_id(1)
    qt = pl.program_id(2)
    nq = GQA_REP * tq

    @pl.when(qt == 0)
    def _():
        for j in range(n_tbl_tiles):
            key = lax.broadcasted_iota(jnp.int32, (tq, tq), 0) + j * tq
            qry = lax.broadcasted_iota(jnp.int32, (tq, tq), 1) + (n_tbl_tiles - 1) * tq
            dist = qry - key
            for r in range(GQA_REP):
                val = _rel_bias_delta(dist, lambda b, r=r: rb_ref[b, g * GQA_REP + r])
                if window is not None:
                    val = jnp.where(dist >= window, NEG, val)
                tbl_ref[j * tq:(j + 1) * tq, r * tq:(r + 1) * tq] = val

    q = qT_ref[0]
    qa_ref[0:HEAD_DIM, :] = jnp.concatenate([q[r * HEAD_DIM:(r + 1) * HEAD_DIM] for r in range(GQA_REP)], axis=1)
    used = HEAD_DIM
    if selected:
        qa_ref[HEAD_DIM:HEAD_DIM + nsel, :] = jnp.concatenate([selb_ref[0, 0]] * GQA_REP, axis=1)
        used += nsel
    if kd > used:
        qa_ref[used:kd, :] = jnp.zeros((kd - used, nq), BF16)
    m_ref[...] = jnp.full((1, nq), NEG, F32)
    acc_ref[...] = jnp.zeros((LANES, nq), F32)

    def step(kt, tbl_tile):
        k = k_ref[0, 0, pl.ds(pl.multiple_of(kt * tq, tq), tq), :]
        s = jnp.dot(k, qa_ref[...], preferred_element_type=F32)
        if tbl_tile is not None:
            s = s + tbl_ref[tbl_tile * tq:(tbl_tile + 1) * tq, :]
        m_old = m_ref[...]
        m_new = jnp.maximum(m_old, jnp.max(s, axis=0, keepdims=True))
        p = jnp.exp(s - m_new).astype(BF16)
        acc_ref[...] = jnp.exp(m_old - m_new) * acc_ref[...] + jnp.dot(vT_ref[0, 0, kt], p, preferred_element_type=F32)
        m_ref[...] = m_new

    if selected:
        def body(kt, carry):
            step(kt, None)
            return carry
        lax.fori_loop(0, qt - (n_tbl_tiles - 1), body, 0)

    for j in range(n_tbl_tiles - 1):
        back = n_tbl_tiles - 1 - j

        @pl.when(qt >= back)
        def _(j=j, back=back):
            step(qt - back, j)
    step(qt, n_tbl_tiles - 1)

    acc = acc_ref[...]
    o = acc[0:HEAD_DIM] / jnp.maximum(acc[HEAD_DIM:HEAD_DIM + 1], 1e-30)
    for r in range(GQA_REP):
        o_ref[0, r * HEAD_DIM:(r + 1) * HEAD_DIM, :] = o[:, r * tq:(r + 1) * tq] * gs_ref[0, 0, 0, r:r + 1, :]


def _out_kernel(oc_ref, os_ref, ow_ref, za_ref, ch_ref, cb_ref, cc_ref, zc_ref, hh_ref, hc_ref,
                x_ref, cw_ref, wo_ref, fw_ref, out_ref, *, tm):
    st = pl.program_id(1)
    attn = (oc_ref[0] + os_ref[0] + ow_ref[0]).T * _silu(za_ref[0])

    u = cc_ref[0] * ch_ref[0]
    halo = jnp.where(st == 0, 0.0, hc_ref[0] * hh_ref[0])
    row = lax.broadcasted_iota(jnp.int32, u.shape, 0)
    u1 = jnp.where(row == 0, halo[7:8], pltpu.roll(u, 1, 0))
    u2 = jnp.where(row == 0, halo[6:7], jnp.where(row == 1, halo[7:8], pltpu.roll(u, 2, 0)))
    y = cw_ref[0:1] * u2 + cw_ref[1:2] * u1 + cw_ref[2:3] * u
    conv = cb_ref[0] * y * _silu(zc_ref[0])

    mixed = jnp.concatenate([attn, conv], axis=1).astype(BF16)
    y2 = x_ref[0] + jnp.dot(mixed, wo_ref[...], preferred_element_type=F32)
    ms = jnp.mean(y2 * y2, axis=-1, keepdims=True)
    out_ref[0] = y2 * lax.rsqrt(ms + EPS) * fw_ref[...]


def _overlap_t(nsel, nch):
    c = np.arange(nch)[None, :] * CMP_STRIDE
    n = np.arange(nsel)[:, None] * SEL_BLOCK
    return ((c < n + SEL_BLOCK) & (c + CMP_BLOCK > n)).astype(np.float32)


def _cparams(sem):
    return pltpu.CompilerParams(dimension_semantics=sem, vmem_limit_bytes=VMEM_LIMIT)


def kernel(x, norm_w, w_in, w_ck1, w_ck2, pe_k, w_cv1, w_cv2, pe_v, conv_w, w_out, rel_bias, final_norm_w):
    b, s, _ = x.shape
    assert norm_w.shape[0] == 1 and s % ROW_TILE == 0 and WINDOW == 2 * Q_TILE and ROW_TILE == Q_TILE
    tm = ROW_TILE
    nt = s // tm
    nch = s // CMP_STRIDE
    nsel = s // SEL_BLOCK
    k_top = min(N_SEL, nsel)
    kd = -(-(HEAD_DIM + nsel) // LANES) * LANES
    G, R = N_KV_HEADS, GQA_REP

    w = w_in[0]
    c0 = D_ATTN + 6 * D_KV
    w_all = jnp.concatenate([w[:, :c0], w[:, c0 + N_GATE:], w[:, c0:c0 + N_GATE],
                             jnp.zeros((D_MODEL, LANES - N_GATE), F32)], axis=1).astype(BF16)
    eye = jnp.eye(G, dtype=F32)

    def expand1(w1):
        w1 = w1.reshape(2, CMP_STRIDE, HEAD_DIM, CMP_HIDDEN)
        return jnp.einsum('mjdh,ab->mjadbh', w1, eye).reshape(2, CMP_STRIDE * D_KV, G * CMP_HIDDEN)

    def expand_pe(pe):
        pe = pe.reshape(2, CMP_STRIDE, 1, HEAD_DIM)
        return jnp.broadcast_to(pe, (2, CMP_STRIDE, G, HEAD_DIM)).reshape(2, 1, CMP_STRIDE * D_KV)

    def expand2(w2):
        return jnp.einsum('hd,ab->ahbd', w2, eye).reshape(G * CMP_HIDDEN, D_KV)

    w1e = jnp.stack([expand1(w_ck1[0]), expand1(w_cv1[0])]).astype(BF16)
    pee = jnp.stack([expand_pe(pe_k[0]), expand_pe(pe_v[0])])
    w2e = jnp.stack([expand2(w_ck2[0]), expand2(w_cv2[0])]).astype(BF16)
    ov_t = jnp.asarray(_overlap_t(nsel, nch), BF16)
    rel_bias = rel_bias.astype(F32)
    smem = pl.BlockSpec(memory_space=pltpu.SMEM)

    n_rest = D_ATTN + 4 * D_CONV
    qT, kvc, ksel, vselT, kwin, vwinT, gsT, rest = pl.pallas_call(
        partial(_proj_kernel, tm=tm, nsel=nsel, kd=kd),
        grid=(b, nt),
        in_specs=[pl.BlockSpec((1, tm, D_MODEL), lambda i, j: (i, j, 0)),
                  pl.BlockSpec((1, D_MODEL), lambda i, j: (0, 0)),
                  pl.BlockSpec((D_MODEL, _C_END), lambda i, j: (0, 0))],
        out_specs=[pl.BlockSpec((1, D_ATTN, tm), lambda i, j: (i, 0, j)),
                   pl.BlockSpec((2, 1, tm, D_KV), lambda i, j: (0, i, j, 0)),
                   pl.BlockSpec((1, G, tm, kd), lambda i, j: (i, 0, j, 0)),
                   pl.BlockSpec((1, G, 1, LANES, tm), lambda i, j: (i, 0, j, 0, 0)),
                   pl.BlockSpec((1, G, tm, LANES), lambda i, j: (i, 0, j, 0)),
                   pl.BlockSpec((1, G, 1, LANES, tm), lambda i, j: (i, 0, j, 0, 0)),
                   pl.BlockSpec((1, N_GATE, tm), lambda i, j: (i, 0, j)),
                   pl.BlockSpec((1, tm, n_rest), lambda i, j: (i, j, 0))],
        out_shape=[jax.ShapeDtypeStruct((b, D_ATTN, s), BF16),
                   jax.ShapeDtypeStruct((2, b, s, D_KV), F32),
                   jax.ShapeDtypeStruct((b, G, s, kd), BF16),
                   jax.ShapeDtypeStruct((b, G, nt, LANES, tm), BF16),
                   jax.ShapeDtypeStruct((b, G, s, LANES), BF16),
                   jax.ShapeDtypeStruct((b, G, nt, LANES, tm), BF16),
                   jax.ShapeDtypeStruct((b, N_GATE, s), F32),
                   jax.ShapeDtypeStruct((b, s, n_rest), F32)],
        compiler_params=_cparams(("parallel", "parallel")),
        name="in_proj",
    )(x, norm_w, w_all)
    gs = gsT.reshape(b, 3, G, R, s)

    cw = CMP_STRIDE * D_KV
    kc_pad, kcT = pl.pallas_call(
        partial(_compress_kernel, nch=nch),
        grid=(2, b),
        in_specs=[pl.BlockSpec((1, 1, nch, cw), lambda xk, i: (xk, i, 0, 0)),
                  pl.BlockSpec((1, 2, 1, cw), lambda xk, i: (xk, 0, 0, 0)),
                  pl.BlockSpec((1, 2, cw, G * CMP_HIDDEN), lambda xk, i: (xk, 0, 0, 0)),
                  pl.BlockSpec((1, G * CMP_HIDDEN, D_KV), lambda xk, i: (xk, 0, 0))],
        out_specs=[pl.BlockSpec((1, 1, G, nch, LANES), lambda xk, i: (xk, i, 0, 0, 0)),
                   pl.BlockSpec((1, 1, D_KV, nch), lambda xk, i: (xk, i, 0, 0))],
        out_shape=[jax.ShapeDtypeStruct((2, b, G, nch, LANES), BF16),
                   jax.ShapeDtypeStruct((2, b, D_KV, nch), BF16)],
        compiler_params=_cparams(("parallel", "parallel")),
        name="compress",
    )(kvc.reshape(2, b, nch, cw), pee, w1e, w2e)

    tq3 = Q_TILE_CMP
    ocT, selb = pl.pallas_call(
        partial(_cmp_kernel, nch=nch, nsel=nsel, tq=tq3, k_top=k_top),
        grid=(b, G, s // tq3),
        in_specs=[smem,
                  pl.BlockSpec((1, R * HEAD_DIM, tq3), lambda i, g, t: (i, g, t)),
                  pl.BlockSpec((1, 1, nch, LANES), lambda i, g, t: (i, g, 0, 0)),
                  pl.BlockSpec((1, HEAD_DIM, nch), lambda i, g, t: (i, g, 0)),
                  pl.BlockSpec((1, 1, 1, R, tq3), lambda i, g, t: (i, 0, g, 0, t)),
                  pl.BlockSpec((nsel, nch), lambda i, g, t: (0, 0))],
        out_specs=[pl.BlockSpec((1, R * HEAD_DIM, tq3), lambda i, g, t: (i, g, t)),
                   pl.BlockSpec((1, 1, nsel, tq3), lambda i, g, t: (i, g, 0, t))],
        out_shape=[jax.ShapeDtypeStruct((b, D_ATTN, s), F32),
                   jax.ShapeDtypeStruct((b, G, nsel, s), BF16)],
        scratch_shapes=[pltpu.VMEM((nch, R * tq3), F32)],
        compiler_params=_cparams(("parallel", "parallel", "arbitrary")),
        name="cmp_select",
    )(rel_bias, qT, kc_pad[0], kcT[1], gs, ov_t)

    tq = Q_TILE
    nq = R * tq

    def flash(selected, k_arr, vT_arr, kdim, n_tbl, window, branch, name):
        in_specs = [smem, pl.BlockSpec((1, R * HEAD_DIM, tq), lambda i, g, t: (i, g, t))]
        args = [rel_bias, qT]
        if selected:
            in_specs.append(pl.BlockSpec((1, 1, nsel, tq), lambda i, g, t: (i, g, 0, t)))
            args.append(selb)
        in_specs += [pl.BlockSpec((1, 1, s, kdim), lambda i, g, t: (i, g, 0, 0)),
                     pl.BlockSpec((1, 1, nt, LANES, tm), lambda i, g, t: (i, g, 0, 0, 0)),
                     pl.BlockSpec((1, 1, 1, R, tq), lambda i, g, t: (i, branch, g, 0, t))]
        args += [k_arr, vT_arr, gs]
        return pl.pallas_call(
            partial(_flash_kernel, tq=tq, nsel=nsel, kd=kdim, n_tbl_tiles=n_tbl, window=window, selected=selected),
            grid=(b, G, s // tq),
            in_specs=in_specs,
            out_specs=pl.BlockSpec((1, R * HEAD_DIM, tq), lambda i, g, t: (i, g, t)),
            out_shape=jax.ShapeDtypeStruct((b, D_ATTN, s), F32),
            scratch_shapes=[pltpu.VMEM((n_tbl * tq, nq), F32),
                            pltpu.VMEM((kdim, nq), BF16),
                            pltpu.VMEM((1, nq), F32),
                            pltpu.VMEM((LANES, nq), F32)],
            compiler_params=_cparams(("parallel", "parallel", "arbitrary")),
            name=name,
        )(*args)

    osT = flash(True, ksel, vselT, kd, 2, None, 1, "sel_attn")
    owT = flash(False, kwin, vwinT, LANES, 3, WINDOW, 2, "win_attn")

    hb = tm // 8
    col = lambda c: pl.BlockSpec((1, tm, D_CONV), lambda i, j, c=c: (i, j, c))
    halo = lambda c: pl.BlockSpec((1, 8, D_CONV), lambda i, j, c=c: (i, jnp.maximum(j * hb - 1, 0), c))
    oT = pl.BlockSpec((1, D_ATTN, tm), lambda i, j: (i, 0, j))
    out = pl.pallas_call(
        partial(_out_kernel, tm=tm),
        grid=(b, nt),
        in_specs=[oT, oT, oT, col(0), col(1), col(2), col(3), col(4), halo(1), halo(3),
                  pl.BlockSpec((1, tm, D_MODEL), lambda i, j: (i, j, 0)),
                  pl.BlockSpec((3, D_CONV), lambda i, j: (0, 0)),
                  pl.BlockSpec((D_MODEL, D_MODEL), lambda i, j: (0, 0)),
                  pl.BlockSpec((1, D_MODEL), lambda i, j: (0, 0))],
        out_specs=pl.BlockSpec((1, tm, D_MODEL), lambda i, j: (i, j, 0)),
        out_shape=jax.ShapeDtypeStruct((b, s, D_MODEL), F32),
        compiler_params=_cparams(("parallel", "parallel")),
        name="out_proj",
    )(ocT, osT, owT, rest, rest, rest, rest, rest, rest, rest, x, conv_w[0], w_out[0].astype(BF16),
      final_norm_w.reshape(1, D_MODEL))
    return out
```

```python
import math
from functools import partial

import numpy as np
import jax
import jax.numpy as jnp
from jax import lax
from jax.experimental import pallas as pl
from jax.experimental.pallas import tpu as pltpu

D_MODEL = 1024
N_HEADS = 8
HEAD_DIM = 64
N_KV_HEADS = 2
GQA_REP = N_HEADS // N_KV_HEADS
D_ATTN = N_HEADS * HEAD_DIM
D_KV = N_KV_HEADS * HEAD_DIM
D_CONV = D_MODEL - D_ATTN
CMP_BLOCK = 32
CMP_STRIDE = 16
CMP_HIDDEN = 2 * HEAD_DIM
SEL_BLOCK = 64
N_SEL = 16
WINDOW = 512
N_BUCKETS = 32
MAX_DISTANCE = 128
EPS = 1e-6
NEG = -1e30
FORCE_SCORE = 1e3

LOG2E = 1.4426950408889634
LANES = 128
MXU_COLS = 256
V_ROWS = HEAD_DIM + 16
ROW_TILE = 256
Q_TILE_CMP = 128
Q_TILE = 256
VMEM_LIMIT = 48 * 1024 * 1024

F32 = jnp.float32
BF16 = jnp.bfloat16

_C_Q = 0
_C_CMP = D_ATTN
_C_SW = _C_CMP + 2 * D_KV
_C_REST = _C_SW + 4 * D_KV
_C_GATE = _C_REST + D_ATTN + 4 * D_CONV
_C_END = _C_GATE + LANES
N_GATE = 3 * N_HEADS


def _bucket_lower_bounds():
    max_exact = N_BUCKETS // 2
    d = np.arange(2 * MAX_DISTANCE, dtype=np.int32)
    nf = np.maximum(d, 1).astype(np.float32)
    large = max_exact + (np.log(nf / np.float32(max_exact)) / np.float32(math.log(MAX_DISTANCE / max_exact))
                         * np.float32(N_BUCKETS - max_exact)).astype(np.int32)
    bucket = np.where(d < max_exact, d, np.minimum(large, N_BUCKETS - 1))
    assert np.all(np.diff(bucket) >= 0) and np.all(bucket[MAX_DISTANCE:] == N_BUCKETS - 1)
    return [int(np.argmax(bucket >= b)) for b in range(N_BUCKETS)]


_BUCKET_LO = _bucket_lower_bounds()


def _rel_bias_delta(dist, bias_of_bucket):
    last = bias_of_bucket(N_BUCKETS - 1)
    val = jnp.full(dist.shape, (bias_of_bucket(0) - last) * LOG2E, F32)
    for b in range(1, N_BUCKETS - 1):
        val = jnp.where(dist >= _BUCKET_LO[b], (bias_of_bucket(b) - last) * LOG2E, val)
    val = jnp.where(dist >= _BUCKET_LO[N_BUCKETS - 1], 0.0, val)
    return jnp.where(dist < 0, NEG, val)


def _silu(z):
    return z * jax.nn.sigmoid(z)


def _proj_kernel(x_ref, nw_ref, w_ref, qT_ref, kvc_ref, ksel_ref, vselT_ref, kwin_ref, vwinT_ref,
                 gsT_ref, rest_ref, *, tm, nsel, kd):
    st = pl.program_id(1)
    x = x_ref[0]
    ms = jnp.mean(x * x, axis=-1, keepdims=True)
    h = (x * lax.rsqrt(ms + EPS) * nw_ref[...]).astype(BF16)

    def mm(lo, hi):
        return jnp.dot(h, w_ref[:, lo:hi], preferred_element_type=F32)

    q = mm(_C_Q, _C_CMP) * (HEAD_DIM ** -0.5 * LOG2E)
    qT_ref[0] = q.T.astype(BF16)

    kvc = mm(_C_CMP, _C_SW)
    kvc_ref[0, 0] = kvc[:, :D_KV]
    kvc_ref[1, 0] = kvc[:, D_KV:]

    a = mm(_C_SW, _C_REST)
    pos = st * tm + lax.broadcasted_iota(jnp.int32, (tm, nsel), 0)
    col = lax.broadcasted_iota(jnp.int32, (tm, nsel), 1)
    onehot = jnp.where((pos // SEL_BLOCK) == col, 1.0, 0.0).astype(BF16)
    ones = jnp.ones((V_ROWS - HEAD_DIM, tm), BF16)
    zeros_k = jnp.zeros((tm, LANES - HEAD_DIM), BF16)
    vsT = a[:, D_KV:2 * D_KV].T
    vwT = a[:, 3 * D_KV:4 * D_KV].T
    for g in range(N_KV_HEADS):
        lo, hi = g * HEAD_DIM, (g + 1) * HEAD_DIM
        parts = [a[:, lo:hi].astype(BF16), onehot]
        if kd > HEAD_DIM + nsel:
            parts.append(jnp.zeros((tm, kd - HEAD_DIM - nsel), BF16))
        ksel_ref[0, g] = jnp.concatenate(parts, axis=1)
        vselT_ref[0, g, 0] = jnp.concatenate([vsT[lo:hi].astype(BF16), ones], axis=0)
        kwin_ref[0, g] = jnp.concatenate([a[:, 2 * D_KV + lo:2 * D_KV + hi].astype(BF16), zeros_k], axis=1)
        vwinT_ref[0, g, 0] = jnp.concatenate([vwT[lo:hi].astype(BF16), ones], axis=0)

    rest_ref[0] = mm(_C_REST, _C_GATE)
    gates = jax.nn.sigmoid(mm(_C_GATE, _C_END))
    gsT_ref[0] = gates.T[:N_GATE]


def _compress_kernel(c_ref, pe_ref, w1_ref, w2_ref, pad_ref, t_ref, *, nch):
    c = c_ref[0, 0]
    a0 = jnp.dot((c + pe_ref[0, 0]).astype(BF16), w1_ref[0, 0], preferred_element_type=F32)
    a1 = jnp.dot((c + pe_ref[0, 1]).astype(BF16), w1_ref[0, 1], preferred_element_type=F32)
    hid = a0 + pltpu.roll(a1, nch - 1, 0)
    out = jnp.dot(_silu(hid).astype(BF16), w2_ref[0], preferred_element_type=F32)
    zeros = jnp.zeros((nch, LANES - HEAD_DIM), BF16)
    for g in range(N_KV_HEADS):
        pad_ref[0, 0, g] = jnp.concatenate([out[:, g * HEAD_DIM:(g + 1) * HEAD_DIM].astype(BF16), zeros], axis=1)
    t_ref[0, 0] = out.T.astype(BF16)


def _cmp_kernel(rb_ref, qT_ref, kc_ref, vcT_ref, gs_ref, ovT_ref, ocT_ref, selb_ref, s_ref,
                *, nch, nsel, tq, k_top):
    g = pl.program_id(1)
    qt = pl.program_id(2)
    cpq = tq // CMP_STRIDE
    win = cpq + 16

    q = qT_ref[0]
    qT = jnp.concatenate([q[r * HEAD_DIM:(r + 1) * HEAD_DIM] for r in range(GQA_REP)], axis=1)
    qT = jnp.concatenate([qT, jnp.zeros((LANES - HEAD_DIM, GQA_REP * tq), BF16)], axis=0)
    s_ref[...] = jnp.dot(kc_ref[0, 0], qT, preferred_element_type=F32)

    start = pl.multiple_of(jnp.clip(cpq * qt - 16, 0, nch - win), 8)
    row = lax.broadcasted_iota(jnp.int32, (win, tq), 0) + start
    lane = lax.broadcasted_iota(jnp.int32, (win, tq), 1)
    dist = (qt * tq + lane) - (row * CMP_STRIDE + CMP_BLOCK - 1)
    delta = jnp.concatenate(
        [_rel_bias_delta(dist, lambda b, r=r: rb_ref[b, g * GQA_REP + r]) for r in range(GQA_REP)], axis=1)
    s_ref[pl.ds(start, win), :] = s_ref[pl.ds(start, win), :] + delta

    s = s_ref[...]
    crow = lax.broadcasted_iota(jnp.int32, s.shape, 0)
    s = jnp.where(crow >= cpq * (qt + 1), NEG, s)
    m = jnp.max(s, axis=0, keepdims=True)
    p = jnp.exp2(s - m)
    l = jnp.sum(p, axis=0, keepdims=True)
    t4 = qt * tq + (lax.broadcasted_iota(jnp.int32, l.shape, 1) & (tq - 1))
    inv = jnp.where(t4 >= CMP_BLOCK - 1, 1.0 / jnp.maximum(l, 1e-30), 0.0)
    pn = (p * inv).astype(BF16)

    oc = jnp.dot(vcT_ref[0], pn, preferred_element_type=F32)
    for r in range(GQA_REP):
        ocT_ref[0, r * HEAD_DIM:(r + 1) * HEAD_DIM, :] = oc[:, r * tq:(r + 1) * tq] * gs_ref[0, 0, 0, r:r + 1, :]

    psum = pn[:, 0:tq].astype(F32)
    for r in range(1, GQA_REP):
        psum = psum + pn[:, r * tq:(r + 1) * tq].astype(F32)
    hi = psum.astype(BF16)
    lo = (psum - hi.astype(F32)).astype(BF16)
    ov = ovT_ref[...]
    imp = jnp.dot(ov, hi, preferred_element_type=F32) + jnp.dot(ov, lo, preferred_element_type=F32)

    n = lax.broadcasted_iota(jnp.int32, (nsel, tq), 0)
    blk = (qt * tq + lax.broadcasted_iota(jnp.int32, (nsel, tq), 1)) // SEL_BLOCK
    forced = (n == 0) | (n == blk) | (n == blk - 1)
    score = jnp.where(forced, FORCE_SCORE, jnp.where(n <= blk, imp, -1.0))

    def pick(_, carry):
        score, selb = carry
        mx = jnp.max(score, axis=0, keepdims=True)
        idx = jnp.min(jnp.where(score == mx, n, nsel), axis=0, keepdims=True)
        hit = n == idx
        return jnp.where(hit, -2.0, score), jnp.where(hit, 0.0, selb)

    _, selb = lax.fori_loop(0, k_top, pick, (score, jnp.full((nsel, tq), NEG, F32)))
    selb_ref[0, 0] = selb.astype(BF16)


def _flash_kernel(*refs, tq, nsel, kd, n_tbl_tiles, window, selected):
    if selected:
        rb_ref, qT_ref, selb_ref, k_ref, vT_ref, gs_ref, o_ref, tbl_ref, qa_ref, m_ref, acc_ref, *slot_refs = refs
    else:
        rb_ref, qT_ref, k_ref, vT_ref, gs_ref, o_ref, tbl_ref, qa_ref, m_ref, acc_ref, *slot_refs = refs
    s_ref, mt_ref, al_ref, p_ref = (slot_refs[0:2], slot_refs[2:4], slot_refs[4:6], slot_refs[6:8])
    g = pl.program_id(1)
    qt = pl.program_id(2)
    nq = GQA_REP * tq

    @pl.when(qt == 0)
    def _():
        for j in range(n_tbl_tiles):
            key = lax.broadcasted_iota(jnp.int32, (tq, tq), 0) + j * tq
            qry = lax.broadcasted_iota(jnp.int32, (tq, tq), 1) + (n_tbl_tiles - 1) * tq
            dist = qry - key
            for r in range(GQA_REP):
                val = _rel_bias_delta(dist, lambda b, r=r: rb_ref[b, g * GQA_REP + r])
                if window is not None:
                    val = jnp.where(dist >= window, NEG, val)
                tbl_ref[j * tq:(j + 1) * tq, r * tq:(r + 1) * tq] = val

    q = qT_ref[0]
    qa_ref[0:HEAD_DIM, :] = jnp.concatenate([q[r * HEAD_DIM:(r + 1) * HEAD_DIM] for r in range(GQA_REP)], axis=1)
    used = HEAD_DIM
    if selected:
        qa_ref[HEAD_DIM:HEAD_DIM + nsel, :] = jnp.concatenate([selb_ref[0, 0]] * GQA_REP, axis=1)
        used += nsel
    if kd > used:
        qa_ref[used:kd, :] = jnp.zeros((kd - used, nq), BF16)
    m_ref[...] = jnp.full((1, nq), NEG, F32)
    acc_ref[...] = jnp.zeros((V_ROWS, nq), F32)

    def stage_qk(t, slot, tbl_tile):
        k = k_ref[0, 0, pl.ds(pl.multiple_of(t * tq, tq), tq), :]
        s = jnp.dot(k, qa_ref[...], preferred_element_type=F32)
        if tbl_tile is not None:
            s = s + tbl_ref[tbl_tile * tq:(tbl_tile + 1) * tq, :]
        s_ref[slot][...] = s
        mt_ref[slot][...] = jnp.max(s, axis=0, keepdims=True)

    def stage_exp(slot):
        m_old = m_ref[...]
        m_new = jnp.maximum(m_old, mt_ref[slot][...])
        m_ref[...] = m_new
        al_ref[slot][...] = jnp.exp2(m_old - m_new)
        p_ref[slot][...] = jnp.exp2(s_ref[slot][...] - m_new).astype(BF16)

    def stage_pv(t, slot):
        acc_ref[...] = al_ref[slot][...] * acc_ref[...] + jnp.dot(vT_ref[0, 0, t], p_ref[slot][...],
                                                                  preferred_element_type=F32)

    def serial(t, tbl_tile):
        stage_qk(t, 0, tbl_tile)
        stage_exp(0)
        stage_pv(t, 0)

    last = n_tbl_tiles - 1
    if selected:
        n = qt + 1

        def step(t, slot, tbl_tile):
            stage_pv(t - 2, slot)
            stage_exp(1 - slot)
            stage_qk(t, slot, tbl_tile)

        odd = n & 1

        @pl.when((n >= 4) & (odd == 0))
        def _():
            stage_qk(0, 1, None)
            stage_exp(1)
            stage_qk(1, 0, None)

        @pl.when((n >= 4) & (odd == 1))
        def _():
            stage_qk(0, 0, None)
            stage_exp(0)
            stage_qk(1, 1, None)
            step(2, 0, None)

        @pl.when(n >= 4)
        def _():
            first = 2 + odd

            def body(j, carry):
                t = first + 2 * j
                step(t, 1, None)
                step(t + 1, 0, None)
                return carry
            lax.fori_loop(0, (n - 2 - first) >> 1, body, 0)

            step(n - 2, 1, 0)
            step(n - 1, 0, 1)
            stage_pv(n - 2, 1)
            stage_exp(0)
            stage_pv(n - 1, 0)

        @pl.when(n == 3)
        def _():
            serial(0, None)

        @pl.when((n >= 2) & (n < 4))
        def _():
            serial(qt - 1, 0)

        @pl.when(n < 4)
        def _():
            serial(qt, 1)
    else:
        @pl.when(qt >= last)
        def _():
            stage_qk(qt - 2, 0, 0)
            stage_exp(0)
            stage_qk(qt - 1, 1, 1)
            stage_pv(qt - 2, 0)
            stage_exp(1)
            stage_qk(qt, 0, 2)
            stage_pv(qt - 1, 1)
            stage_exp(0)
            stage_pv(qt, 0)

        @pl.when(qt == 1)
        def _():
            serial(0, 1)

        @pl.when(qt < last)
        def _():
            serial(qt, 2)

    acc = acc_ref[...]
    o = acc[0:HEAD_DIM] / jnp.maximum(acc[HEAD_DIM:HEAD_DIM + 1], 1e-30)
    for r in range(GQA_REP):
        o_ref[0, r * HEAD_DIM:(r + 1) * HEAD_DIM, :] = o[:, r * tq:(r + 1) * tq] * gs_ref[0, 0, 0, r:r + 1, :]


def _out_kernel(oc_ref, os_ref, ow_ref, za_ref, ch_ref, cb_ref, cc_ref, zc_ref, hh_ref, hc_ref,
                x_ref, cw_ref, wo_ref, fw_ref, out_ref, *, tm):
    st = pl.program_id(1)
    attn = (oc_ref[0] + os_ref[0] + ow_ref[0]).T * _silu(za_ref[0])

    u = cc_ref[0] * ch_ref[0]
    halo = jnp.where(st == 0, 0.0, hc_ref[0] * hh_ref[0])
    row = lax.broadcasted_iota(jnp.int32, u.shape, 0)
    u1 = jnp.where(row == 0, halo[7:8], pltpu.roll(u, 1, 0))
    u2 = jnp.where(row == 0, halo[6:7], jnp.where(row == 1, halo[7:8], pltpu.roll(u, 2, 0)))
    y = cw_ref[0:1] * u2 + cw_ref[1:2] * u1 + cw_ref[2:3] * u
    conv = cb_ref[0] * y * _silu(zc_ref[0])

    mixed = jnp.concatenate([attn, conv], axis=1).astype(BF16)
    y2 = x_ref[0] + jnp.dot(mixed, wo_ref[...], preferred_element_type=F32)
    ms = jnp.mean(y2 * y2, axis=-1, keepdims=True)
    out_ref[0] = y2 * lax.rsqrt(ms + EPS) * fw_ref[...]


def _overlap_t(nsel, nch):
    c = np.arange(nch)[None, :] * CMP_STRIDE
    n = np.arange(nsel)[:, None] * SEL_BLOCK
    return ((c < n + SEL_BLOCK) & (c + CMP_BLOCK > n)).astype(np.float32)


def _cparams(sem):
    return pltpu.CompilerParams(dimension_semantics=sem, vmem_limit_bytes=VMEM_LIMIT)


def kernel(x, norm_w, w_in, w_ck1, w_ck2, pe_k, w_cv1, w_cv2, pe_v, conv_w, w_out, rel_bias, final_norm_w):
    b, s, _ = x.shape
    assert norm_w.shape[0] == 1 and s % ROW_TILE == 0 and WINDOW == 2 * Q_TILE and ROW_TILE == Q_TILE
    tm = ROW_TILE
    nt = s // tm
    nch = s // CMP_STRIDE
    nsel = s // SEL_BLOCK
    k_top = min(N_SEL, nsel)
    kd = -(-(HEAD_DIM + nsel) // LANES) * LANES
    G, R = N_KV_HEADS, GQA_REP

    w = w_in[0]
    c0 = D_ATTN + 6 * D_KV
    w_all = jnp.concatenate([w[:, :c0], w[:, c0 + N_GATE:], w[:, c0:c0 + N_GATE],
                             jnp.zeros((D_MODEL, LANES - N_GATE), F32)], axis=1).astype(BF16)
    eye = jnp.eye(G, dtype=F32)

    def expand1(w1):
        w1 = w1.reshape(2, CMP_STRIDE, HEAD_DIM, CMP_HIDDEN)
        return jnp.einsum('mjdh,ab->mjadbh', w1, eye).reshape(2, CMP_STRIDE * D_KV, G * CMP_HIDDEN)

    def expand_pe(pe):
        pe = pe.reshape(2, CMP_STRIDE, 1, HEAD_DIM)
        return jnp.broadcast_to(pe, (2, CMP_STRIDE, G, HEAD_DIM)).reshape(2, 1, CMP_STRIDE * D_KV)

    def expand2(w2):
        return jnp.einsum('hd,ab->ahbd', w2, eye).reshape(G * CMP_HIDDEN, D_KV)

    w1e = jnp.stack([expand1(w_ck1[0]), expand1(w_cv1[0])]).astype(BF16)
    pee = jnp.stack([expand_pe(pe_k[0]), expand_pe(pe_v[0])])
    w2e = jnp.stack([expand2(w_ck2[0]), expand2(w_cv2[0])]).astype(BF16)
    ov_t = jnp.asarray(_overlap_t(nsel, nch), BF16)
    rel_bias = rel_bias.astype(F32)
    smem = pl.BlockSpec(memory_space=pltpu.SMEM)

    n_rest = D_ATTN + 4 * D_CONV
    qT, kvc, ksel, vselT, kwin, vwinT, gsT, rest = pl.pallas_call(
        partial(_proj_kernel, tm=tm, nsel=nsel, kd=kd),
        grid=(b, nt),
        in_specs=[pl.BlockSpec((1, tm, D_MODEL), lambda i, j: (i, j, 0)),
                  pl.BlockSpec((1, D_MODEL), lambda i, j: (0, 0)),
                  pl.BlockSpec((D_MODEL, _C_END), lambda i, j: (0, 0))],
        out_specs=[pl.BlockSpec((1, D_ATTN, tm), lambda i, j: (i, 0, j)),
                   pl.BlockSpec((2, 1, tm, D_KV), lambda i, j: (0, i, j, 0)),
                   pl.BlockSpec((1, G, tm, kd), lambda i, j: (i, 0, j, 0)),
                   pl.BlockSpec((1, G, 1, V_ROWS, tm), lambda i, j: (i, 0, j, 0, 0)),
                   pl.BlockSpec((1, G, tm, LANES), lambda i, j: (i, 0, j, 0)),
                   pl.BlockSpec((1, G, 1, V_ROWS, tm), lambda i, j: (i, 0, j, 0, 0)),
                   pl.BlockSpec((1, N_GATE, tm), lambda i, j: (i, 0, j)),
                   pl.BlockSpec((1, tm, n_rest), lambda i, j: (i, j, 0))],
        out_shape=[jax.ShapeDtypeStruct((b, D_ATTN, s), BF16),
                   jax.ShapeDtypeStruct((2, b, s, D_KV), F32),
                   jax.ShapeDtypeStruct((b, G, s, kd), BF16),
                   jax.ShapeDtypeStruct((b, G, nt, V_ROWS, tm), BF16),
                   jax.ShapeDtypeStruct((b, G, s, LANES), BF16),
                   jax.ShapeDtypeStruct((b, G, nt, V_ROWS, tm), BF16),
                   jax.ShapeDtypeStruct((b, N_GATE, s), F32),
                   jax.ShapeDtypeStruct((b, s, n_rest), F32)],
        compiler_params=_cparams(("parallel", "parallel")),
        name="in_proj",
    )(x, norm_w, w_all)
    gs = gsT.reshape(b, 3, G, R, s)

    cw = CMP_STRIDE * D_KV
    kc_pad, kcT = pl.pallas_call(
        partial(_compress_kernel, nch=nch),
        grid=(2, b),
        in_specs=[pl.BlockSpec((1, 1, nch, cw), lambda xk, i: (xk, i, 0, 0)),
                  pl.BlockSpec((1, 2, 1, cw), lambda xk, i: (xk, 0, 0, 0)),
                  pl.BlockSpec((1, 2, cw, G * CMP_HIDDEN), lambda xk, i: (xk, 0, 0, 0)),
                  pl.BlockSpec((1, G * CMP_HIDDEN, D_KV), lambda xk, i: (xk, 0, 0))],
        out_specs=[pl.BlockSpec((1, 1, G, nch, LANES), lambda xk, i: (xk, i, 0, 0, 0)),
                   pl.BlockSpec((1, 1, D_KV, nch), lambda xk, i: (xk, i, 0, 0))],
        out_shape=[jax.ShapeDtypeStruct((2, b, G, nch, LANES), BF16),
                   jax.ShapeDtypeStruct((2, b, D_KV, nch), BF16)],
        compiler_params=_cparams(("parallel", "parallel")),
        name="compress",
    )(kvc.reshape(2, b, nch, cw), pee, w1e, w2e)

    tq3 = Q_TILE_CMP
    ocT, selb = pl.pallas_call(
        partial(_cmp_kernel, nch=nch, nsel=nsel, tq=tq3, k_top=k_top),
        grid=(b, G, s // tq3),
        in_specs=[smem,
                  pl.BlockSpec((1, R * HEAD_DIM, tq3), lambda i, g, t: (i, g, t)),
                  pl.BlockSpec((1, 1, nch, LANES), lambda i, g, t: (i, g, 0, 0)),
                  pl.BlockSpec((1, HEAD_DIM, nch), lambda i, g, t: (i, g, 0)),
                  pl.BlockSpec((1, 1, 1, R, tq3), lambda i, g, t: (i, 0, g, 0, t)),
                  pl.BlockSpec((nsel, nch), lambda i, g, t: (0, 0))],
        out_specs=[pl.BlockSpec((1, R * HEAD_DIM, tq3), lambda i, g, t: (i, g, t)),
                   pl.BlockSpec((1, 1, nsel, tq3), lambda i, g, t: (i, g, 0, t))],
        out_shape=[jax.ShapeDtypeStruct((b, D_ATTN, s), F32),
                   jax.ShapeDtypeStruct((b, G, nsel, s), BF16)],
        scratch_shapes=[pltpu.VMEM((nch, R * tq3), F32)],
        compiler_params=_cparams(("parallel", "parallel", "arbitrary")),
        name="cmp_select",
    )(rel_bias, qT, kc_pad[0], kcT[1], gs, ov_t)

    tq = Q_TILE
    nq = R * tq

    def flash(selected, k_arr, vT_arr, kdim, n_tbl, window, branch, name):
        in_specs = [smem, pl.BlockSpec((1, R * HEAD_DIM, tq), lambda i, g, t: (i, g, t))]
        args = [rel_bias, qT]
        if selected:
            in_specs.append(pl.BlockSpec((1, 1, nsel, tq), lambda i, g, t: (i, g, 0, t)))
            args.append(selb)
        in_specs += [pl.BlockSpec((1, 1, s, kdim), lambda i, g, t: (i, g, 0, 0)),
                     pl.BlockSpec((1, 1, nt, V_ROWS, tm), lambda i, g, t: (i, g, 0, 0, 0)),
                     pl.BlockSpec((1, 1, 1, R, tq), lambda i, g, t: (i, branch, g, 0, t))]
        args += [k_arr, vT_arr, gs]
        return pl.pallas_call(
            partial(_flash_kernel, tq=tq, nsel=nsel, kd=kdim, n_tbl_tiles=n_tbl, window=window, selected=selected),
            grid=(b, G, s // tq),
            in_specs=in_specs,
            out_specs=pl.BlockSpec((1, R * HEAD_DIM, tq), lambda i, g, t: (i, g, t)),
            out_shape=jax.ShapeDtypeStruct((b, D_ATTN, s), F32),
            scratch_shapes=[pltpu.VMEM((n_tbl * tq, nq), F32),
                            pltpu.VMEM((kdim, nq), BF16),
                            pltpu.VMEM((1, nq), F32),
                            pltpu.VMEM((V_ROWS, nq), F32),
                            pltpu.VMEM((tq, nq), F32), pltpu.VMEM((tq, nq), F32),
                            pltpu.VMEM((1, nq), F32), pltpu.VMEM((1, nq), F32),
                            pltpu.VMEM((1, nq), F32), pltpu.VMEM((1, nq), F32),
                            pltpu.VMEM((tq, nq), BF16), pltpu.VMEM((tq, nq), BF16)],
            compiler_params=_cparams(("parallel", "parallel", "arbitrary")),
            name=name,
        )(*args)

    osT = flash(True, ksel, vselT, kd, 2, None, 1, "sel_attn")
    owT = flash(False, kwin, vwinT, LANES, 3, WINDOW, 2, "win_attn")

    hb = tm // 8
    col = lambda c: pl.BlockSpec((1, tm, D_CONV), lambda i, j, c=c: (i, j, c))
    halo = lambda c: pl.BlockSpec((1, 8, D_CONV), lambda i, j, c=c: (i, jnp.maximum(j * hb - 1, 0), c))
    oT = pl.BlockSpec((1, D_ATTN, tm), lambda i, j: (i, 0, j))
    out = pl.pallas_call(
        partial(_out_kernel, tm=tm),
        grid=(b, nt),
        in_specs=[oT, oT, oT, col(0), col(1), col(2), col(3), col(4), halo(1), halo(3),
                  pl.BlockSpec((1, tm, D_MODEL), lambda i, j: (i, j, 0)),
                  pl.BlockSpec((3, D_CONV), lambda i, j: (0, 0)),
                  pl.BlockSpec((D_MODEL, D_MODEL), lambda i, j: (0, 0)),
                  pl.BlockSpec((1, D_MODEL), lambda i, j: (0, 0))],
        out_specs=pl.BlockSpec((1, tm, D_MODEL), lambda i, j: (i, j, 0)),
        out_shape=jax.ShapeDtypeStruct((b, s, D_MODEL), F32),
        compiler_params=_cparams(("parallel", "parallel")),
        name="out_proj",
    )(ocT, osT, owT, rest, rest, rest, rest, rest, rest, rest, x, conv_w[0], w_out[0].astype(BF16),
      final_norm_w.reshape(1, D_MODEL))
    return out
```

```python
import math
from functools import partial

import numpy as np
import jax
import jax.numpy as jnp
from jax import lax
from jax.experimental import pallas as pl
from jax.experimental.pallas import tpu as pltpu

D_MODEL = 1024
N_HEADS = 8
HEAD_DIM = 64
N_KV_HEADS = 2
GQA_REP = N_HEADS // N_KV_HEADS
D_ATTN = N_HEADS * HEAD_DIM
D_KV = N_KV_HEADS * HEAD_DIM
D_CONV = D_MODEL - D_ATTN
CMP_BLOCK = 32
CMP_STRIDE = 16
CMP_HIDDEN = 2 * HEAD_DIM
SEL_BLOCK = 64
N_SEL = 16
WINDOW = 512
N_BUCKETS = 32
MAX_DISTANCE = 128
EPS = 1e-6
NEG = -1e30
FORCE_SCORE = 1e3

LOG2E = 1.4426950408889634
LANES = 128
MXU_COLS = 256
V_ROWS = HEAD_DIM + 16
ROW_TILE = 256
Q_TILE_CMP = 256
Q_TILE = 256
VMEM_LIMIT = 48 * 1024 * 1024

F32 = jnp.float32
BF16 = jnp.bfloat16

_C_Q = 0
_C_CMP = D_ATTN
_C_SW = _C_CMP + 2 * D_KV
_C_REST = _C_SW + 4 * D_KV
_C_GATE = _C_REST + D_ATTN + 4 * D_CONV
_C_END = _C_GATE + LANES
N_GATE = 3 * N_HEADS


def _bucket_lower_bounds():
    max_exact = N_BUCKETS // 2
    d = np.arange(2 * MAX_DISTANCE, dtype=np.int32)
    nf = np.maximum(d, 1).astype(np.float32)
    large = max_exact + (np.log(nf / np.float32(max_exact)) / np.float32(math.log(MAX_DISTANCE / max_exact))
                         * np.float32(N_BUCKETS - max_exact)).astype(np.int32)
    bucket = np.where(d < max_exact, d, np.minimum(large, N_BUCKETS - 1))
    assert np.all(np.diff(bucket) >= 0) and np.all(bucket[MAX_DISTANCE:] == N_BUCKETS - 1)
    return [int(np.argmax(bucket >= b)) for b in range(N_BUCKETS)]


_BUCKET_LO = _bucket_lower_bounds()


def _rel_bias_delta(dist, bias_of_bucket):
    last = bias_of_bucket(N_BUCKETS - 1)
    val = jnp.full(dist.shape, (bias_of_bucket(0) - last) * LOG2E, F32)
    for b in range(1, N_BUCKETS - 1):
        val = jnp.where(dist >= _BUCKET_LO[b], (bias_of_bucket(b) - last) * LOG2E, val)
    val = jnp.where(dist >= _BUCKET_LO[N_BUCKETS - 1], 0.0, val)
    return jnp.where(dist < 0, NEG, val)


def _silu(z):
    return z * jax.nn.sigmoid(z)


def _proj_kernel(x_ref, nw_ref, w_ref, qT_ref, kvc_ref, ksel_ref, vselT_ref, kwin_ref, vwinT_ref,
                 gsT_ref, rest_ref, *, tm, nsel, kd):
    st = pl.program_id(1)
    x = x_ref[0]
    ms = jnp.mean(x * x, axis=-1, keepdims=True)
    h = (x * lax.rsqrt(ms + EPS) * nw_ref[...]).astype(BF16)

    def mm(lo, hi):
        return jnp.dot(h, w_ref[:, lo:hi], preferred_element_type=F32)

    q = mm(_C_Q, _C_CMP) * (HEAD_DIM ** -0.5 * LOG2E)
    qT_ref[0] = q.T.astype(BF16)

    kvc = mm(_C_CMP, _C_SW)
    kvc_ref[0, 0] = kvc[:, :D_KV]
    kvc_ref[1, 0] = kvc[:, D_KV:]

    a = mm(_C_SW, _C_REST)
    pos = st * tm + lax.broadcasted_iota(jnp.int32, (tm, nsel), 0)
    col = lax.broadcasted_iota(jnp.int32, (tm, nsel), 1)
    onehot = jnp.where((pos // SEL_BLOCK) == col, 1.0, 0.0).astype(BF16)
    ones = jnp.ones((V_ROWS - HEAD_DIM, tm), BF16)
    zeros_k = jnp.zeros((tm, LANES - HEAD_DIM), BF16)
    vsT = a[:, D_KV:2 * D_KV].T
    vwT = a[:, 3 * D_KV:4 * D_KV].T
    for g in range(N_KV_HEADS):
        lo, hi = g * HEAD_DIM, (g + 1) * HEAD_DIM
        parts = [a[:, lo:hi].astype(BF16), onehot]
        if kd > HEAD_DIM + nsel:
            parts.append(jnp.zeros((tm, kd - HEAD_DIM - nsel), BF16))
        ksel_ref[0, g] = jnp.concatenate(parts, axis=1)
        vselT_ref[0, g, 0] = jnp.concatenate([vsT[lo:hi].astype(BF16), ones], axis=0)
        kwin_ref[0, g] = jnp.concatenate([a[:, 2 * D_KV + lo:2 * D_KV + hi].astype(BF16), zeros_k], axis=1)
        vwinT_ref[0, g, 0] = jnp.concatenate([vwT[lo:hi].astype(BF16), ones], axis=0)

    rest_ref[0] = mm(_C_REST, _C_GATE)
    gates = jax.nn.sigmoid(mm(_C_GATE, _C_END))
    gsT_ref[0] = gates.T[:N_GATE]


def _compress_kernel(c_ref, pe_ref, w1_ref, w2_ref, lhs_tail_ref, aug_ref, t_ref, *, nch):
    c = c_ref[0, 0]
    a0 = jnp.dot((c + pe_ref[0, 0]).astype(BF16), w1_ref[0, 0], preferred_element_type=F32)
    a1 = jnp.dot((c + pe_ref[0, 1]).astype(BF16), w1_ref[0, 1], preferred_element_type=F32)
    hid = a0 + pltpu.roll(a1, nch - 1, 0)
    out = jnp.dot(_silu(hid).astype(BF16), w2_ref[0], preferred_element_type=F32)
    grp = lax.broadcasted_iota(jnp.int32, (nch, LANES - HEAD_DIM), 0) // 8
    col = lax.broadcasted_iota(jnp.int32, (nch, LANES - HEAD_DIM), 1)
    onehot = jnp.where(grp == col, 1.0, 0.0).astype(BF16)
    out_t = out.T.astype(BF16)
    for g in range(N_KV_HEADS):
        lo, hi = g * HEAD_DIM, (g + 1) * HEAD_DIM
        aug_ref[0, 0, g] = jnp.concatenate([out[:, lo:hi].astype(BF16), onehot], axis=1)
        t_ref[0, 0, g] = jnp.concatenate([out_t[lo:hi], lhs_tail_ref[...]], axis=0)


def _cmp_kernel(rb_ref, qT_ref, kc_ref, lhs_ref, gs_ref, ocT_ref, selb_ref,
                qa_ref, band_ref, acc_ref, s0_ref, s1_ref, s2_ref, s3_ref, *, nch, nsel, tq, k_top):
    g = pl.program_id(1)
    qt = pl.program_id(2)
    s_refs = (s0_ref, s1_ref, s2_ref, s3_ref)
    cpq = tq // CMP_STRIDE
    win = cpq + 16
    lim = cpq * (qt + 1)
    nq = GQA_REP * tq

    @pl.when(qt <= 1)
    def _():
        row = lax.broadcasted_iota(jnp.int32, (win, tq), 0) + jnp.maximum(cpq * qt - 16, 0)
        lane = lax.broadcasted_iota(jnp.int32, (win, tq), 1)
        dist = (qt * tq + lane) - (row * CMP_STRIDE + CMP_BLOCK - 1)
        for r in range(GQA_REP):
            band_ref[:, r * tq:(r + 1) * tq] = _rel_bias_delta(dist, lambda b, r=r: rb_ref[b, g * GQA_REP + r])

    q = qT_ref[0]
    qa_ref[0:HEAD_DIM, :] = jnp.concatenate([q[r * HEAD_DIM:(r + 1) * HEAD_DIM] for r in range(GQA_REP)], axis=1)
    grp = lax.broadcasted_iota(jnp.int32, (LANES - HEAD_DIM, nq), 0)
    qa_ref[HEAD_DIM:LANES, :] = jnp.where(grp * 8 >= lim, NEG, 0.0).astype(BF16)

    def run(rows):
        start = pl.multiple_of(jnp.clip(cpq * qt - 16, 0, rows - win), 8)
        k = kc_ref[0, 0, 0, 0:rows, :]
        lhs = lhs_ref[0, 0, 0, :, 0:rows]
        cols = [slice(r * tq, (r + 1) * tq) for r in range(GQA_REP)]

        def qk(r):
            s_refs[r][0:rows, :] = jnp.dot(k, qa_ref[:, cols[r]], preferred_element_type=F32)

        qk(0)
        for r in range(GQA_REP):
            if r + 1 < GQA_REP:
                qk(r + 1)
            sr = s_refs[r]
            sr[pl.ds(start, win), :] = sr[pl.ds(start, win), :] + band_ref[:, cols[r]]
            s = sr[0:rows, :]
            p = jnp.exp2(s - jnp.max(s, axis=0, keepdims=True)).astype(BF16)
            acc_ref[:, cols[r]] = jnp.dot(lhs, p, preferred_element_type=F32)

    half = nch // 2

    @pl.when(lim <= half)
    def _():
        run(half)

    @pl.when(lim > half)
    def _():
        run(nch)

    l = acc_ref[HEAD_DIM:HEAD_DIM + 1, :]
    t4 = qt * tq + (lax.broadcasted_iota(jnp.int32, l.shape, 1) & (tq - 1))
    inv = jnp.where(t4 >= CMP_BLOCK - 1, 1.0 / jnp.maximum(l, 1e-30), 0.0)
    imp = None
    for r in range(GQA_REP):
        cs = slice(r * tq, (r + 1) * tq)
        ocT_ref[0, r * HEAD_DIM:(r + 1) * HEAD_DIM, :] = (acc_ref[0:HEAD_DIM, cs] * inv[:, cs]
                                                          * gs_ref[0, 0, 0, r:r + 1, :])
        part = acc_ref[V_ROWS:V_ROWS + nsel, cs] * inv[:, cs]
        imp = part if imp is None else imp + part

    n = lax.broadcasted_iota(jnp.int32, (nsel, tq), 0)
    blk = (qt * tq + lax.broadcasted_iota(jnp.int32, (nsel, tq), 1)) // SEL_BLOCK
    forced = (n == 0) | (n == blk) | (n == blk - 1)
    candidate = (n <= blk) & jnp.logical_not(forced)
    score = jnp.where(forced, -2.0, jnp.where(n <= blk, imp, -1.0))

    def pick_all(_, score):
        return jnp.where(score == jnp.max(score, axis=0, keepdims=True), -2.0, score)

    taken = lax.fori_loop(0, k_top - 3, pick_all, score) == -2.0
    selb_ref[0, 0] = jnp.where(taken, 0.0, NEG).astype(BF16)
    n_taken = jnp.sum(jnp.where(taken & candidate, 1.0, 0.0), axis=0, keepdims=True)

    @pl.when(jnp.max(n_taken) > k_top - 3)
    def _():
        def pick_one(_, score):
            mx = jnp.max(score, axis=0, keepdims=True)
            idx = jnp.min(jnp.where(score == mx, n, nsel), axis=0, keepdims=True)
            return jnp.where(n == idx, -2.0, score)

        taken = lax.fori_loop(0, k_top - 3, pick_one, score) == -2.0
        selb_ref[0, 0] = jnp.where(taken, 0.0, NEG).astype(BF16)


def _flash_kernel(*refs, tq, nsel, kd, n_tbl_tiles, window, selected):
    if selected:
        rb_ref, qT_ref, selb_ref, k_ref, vT_ref, gs_ref, o_ref, tbl_ref, qa_ref, m_ref, acc_ref, *slot_refs = refs
    else:
        rb_ref, qT_ref, k_ref, vT_ref, gs_ref, o_ref, tbl_ref, qa_ref, m_ref, acc_ref, *slot_refs = refs
    s_ref, mt_ref, al_ref, p_ref = (slot_refs[0:2], slot_refs[2:4], slot_refs[4:6], slot_refs[6:8])
    g = pl.program_id(1)
    qt = pl.program_id(2)
    nq = GQA_REP * tq

    @pl.when(qt == 0)
    def _():
        for j in range(n_tbl_tiles):
            key = lax.broadcasted_iota(jnp.int32, (tq, tq), 0) + j * tq
            qry = lax.broadcasted_iota(jnp.int32, (tq, tq), 1) + (n_tbl_tiles - 1) * tq
            dist = qry - key
            for r in range(GQA_REP):
                val = _rel_bias_delta(dist, lambda b, r=r: rb_ref[b, g * GQA_REP + r])
                if window is not None:
                    val = jnp.where(dist >= window, NEG, val)
                tbl_ref[j * tq:(j + 1) * tq, r * tq:(r + 1) * tq] = val

    q = qT_ref[0]
    qa_ref[0:HEAD_DIM, :] = jnp.concatenate([q[r * HEAD_DIM:(r + 1) * HEAD_DIM] for r in range(GQA_REP)], axis=1)
    used = HEAD_DIM
    if selected:
        qa_ref[HEAD_DIM:HEAD_DIM + nsel, :] = jnp.concatenate([selb_ref[0, 0]] * GQA_REP, axis=1)
        used += nsel
    if kd > used:
        qa_ref[used:kd, :] = jnp.zeros((kd - used, nq), BF16)
    m_ref[...] = jnp.full((1, nq), NEG, F32)
    acc_ref[...] = jnp.zeros((V_ROWS, nq), F32)

    def stage_qk(t, slot, tbl_tile):
        k = k_ref[0, 0, pl.ds(pl.multiple_of(t * tq, tq), tq), :]
        s = jnp.dot(k, qa_ref[...], preferred_element_type=F32)
        if tbl_tile is not None:
            s = s + tbl_ref[tbl_tile * tq:(tbl_tile + 1) * tq, :]
        s_ref[slot][...] = s
        mt_ref[slot][...] = jnp.max(s, axis=0, keepdims=True)

    def stage_exp(slot):
        m_old = m_ref[...]
        m_new = jnp.maximum(m_old, mt_ref[slot][...])
        m_ref[...] = m_new
        al_ref[slot][...] = jnp.exp2(m_old - m_new)
        p_ref[slot][...] = jnp.exp2(s_ref[slot][...] - m_new).astype(BF16)

    def stage_pv(t, slot):
        acc_ref[...] = al_ref[slot][...] * acc_ref[...] + jnp.dot(vT_ref[0, 0, t], p_ref[slot][...],
                                                                  preferred_element_type=F32)

    def serial(t, tbl_tile):
        stage_qk(t, 0, tbl_tile)
        stage_exp(0)
        stage_pv(t, 0)

    last = n_tbl_tiles - 1
    if selected:
        n = qt + 1

        def step(t, slot, tbl_tile):
            stage_pv(t - 2, slot)
            stage_exp(1 - slot)
            stage_qk(t, slot, tbl_tile)

        odd = n & 1

        @pl.when((n >= 4) & (odd == 0))
        def _():
            stage_qk(0, 1, None)
            stage_exp(1)
            stage_qk(1, 0, None)

        @pl.when((n >= 4) & (odd == 1))
        def _():
            stage_qk(0, 0, None)
            stage_exp(0)
            stage_qk(1, 1, None)
            step(2, 0, None)

        @pl.when(n >= 4)
        def _():
            first = 2 + odd

            pairs = (n - 2 - first) >> 1

            def pair(t):
                step(t, 1, None)
                step(t + 1, 0, None)

            def body(j, carry):
                pair(first + 4 * j)
                pair(first + 4 * j + 2)
                return carry
            lax.fori_loop(0, pairs >> 1, body, 0)

            @pl.when((pairs & 1) == 1)
            def _():
                pair(n - 4)

        @pl.when(n >= 4)
        def _():
            step(n - 2, 1, 0)
            step(n - 1, 0, 1)
            stage_pv(n - 2, 1)
            stage_exp(0)
            stage_pv(n - 1, 0)

        @pl.when(n == 3)
        def _():
            serial(0, None)

        @pl.when((n >= 2) & (n < 4))
        def _():
            serial(qt - 1, 0)

        @pl.when(n < 4)
        def _():
            serial(qt, 1)
    else:
        @pl.when(qt >= last)
        def _():
            stage_qk(qt - 2, 0, 0)
            stage_exp(0)
            stage_qk(qt - 1, 1, 1)
            stage_pv(qt - 2, 0)
            stage_exp(1)
            stage_qk(qt, 0, 2)
            stage_pv(qt - 1, 1)
            stage_exp(0)
            stage_pv(qt, 0)

        @pl.when(qt == 1)
        def _():
            serial(0, 1)

        @pl.when(qt < last)
        def _():
            serial(qt, 2)

    acc = acc_ref[...]
    o = acc[0:HEAD_DIM] / jnp.maximum(acc[HEAD_DIM:HEAD_DIM + 1], 1e-30)
    for r in range(GQA_REP):
        o_ref[0, r * HEAD_DIM:(r + 1) * HEAD_DIM, :] = o[:, r * tq:(r + 1) * tq] * gs_ref[0, 0, 0, r:r + 1, :]


def _out_kernel(oc_ref, os_ref, ow_ref, za_ref, ch_ref, cb_ref, cc_ref, zc_ref, hh_ref, hc_ref,
                x_ref, cw_ref, wo_ref, fw_ref, out_ref, *, tm):
    st = pl.program_id(1)
    attn = (oc_ref[0] + os_ref[0] + ow_ref[0]).T * _silu(za_ref[0])

    u = cc_ref[0] * ch_ref[0]
    halo = jnp.where(st == 0, 0.0, hc_ref[0] * hh_ref[0])
    row = lax.broadcasted_iota(jnp.int32, u.shape, 0)
    u1 = jnp.where(row == 0, halo[7:8], pltpu.roll(u, 1, 0))
    u2 = jnp.where(row == 0, halo[6:7], jnp.where(row == 1, halo[7:8], pltpu.roll(u, 2, 0)))
    y = cw_ref[0:1] * u2 + cw_ref[1:2] * u1 + cw_ref[2:3] * u
    conv = cb_ref[0] * y * _silu(zc_ref[0])

    mixed = jnp.concatenate([attn, conv], axis=1).astype(BF16)
    y2 = x_ref[0] + jnp.dot(mixed, wo_ref[...], preferred_element_type=F32)
    ms = jnp.mean(y2 * y2, axis=-1, keepdims=True)
    out_ref[0] = y2 * lax.rsqrt(ms + EPS) * fw_ref[...]


def _overlap_t(nsel, nch):
    c = np.arange(nch)[None, :] * CMP_STRIDE
    n = np.arange(nsel)[:, None] * SEL_BLOCK
    return ((c < n + SEL_BLOCK) & (c + CMP_BLOCK > n)).astype(np.float32)


def _cparams(sem):
    return pltpu.CompilerParams(dimension_semantics=sem, vmem_limit_bytes=VMEM_LIMIT)


def kernel(x, norm_w, w_in, w_ck1, w_ck2, pe_k, w_cv1, w_cv2, pe_v, conv_w, w_out, rel_bias, final_norm_w):
    b, s, _ = x.shape
    assert norm_w.shape[0] == 1 and s % ROW_TILE == 0 and WINDOW == 2 * Q_TILE and ROW_TILE == Q_TILE
    tm = ROW_TILE
    nt = s // tm
    nch = s // CMP_STRIDE
    nsel = s // SEL_BLOCK
    k_top = min(N_SEL, nsel)
    kd = -(-(HEAD_DIM + nsel) // LANES) * LANES
    G, R = N_KV_HEADS, GQA_REP

    w = w_in[0]
    c0 = D_ATTN + 6 * D_KV
    w_all = jnp.concatenate([w[:, :c0], w[:, c0 + N_GATE:], w[:, c0:c0 + N_GATE],
                             jnp.zeros((D_MODEL, LANES - N_GATE), F32)], axis=1).astype(BF16)
    eye = jnp.eye(G, dtype=F32)

    def expand1(w1):
        w1 = w1.reshape(2, CMP_STRIDE, HEAD_DIM, CMP_HIDDEN)
        return jnp.einsum('mjdh,ab->mjadbh', w1, eye).reshape(2, CMP_STRIDE * D_KV, G * CMP_HIDDEN)

    def expand_pe(pe):
        pe = pe.reshape(2, CMP_STRIDE, 1, HEAD_DIM)
        return jnp.broadcast_to(pe, (2, CMP_STRIDE, G, HEAD_DIM)).reshape(2, 1, CMP_STRIDE * D_KV)

    def expand2(w2):
        return jnp.einsum('hd,ab->ahbd', w2, eye).reshape(G * CMP_HIDDEN, D_KV)

    w1e = jnp.stack([expand1(w_ck1[0]), expand1(w_cv1[0])]).astype(BF16)
    pee = jnp.stack([expand_pe(pe_k[0]), expand_pe(pe_v[0])])
    w2e = jnp.stack([expand2(w_ck2[0]), expand2(w_cv2[0])]).astype(BF16)
    rel_bias = rel_bias.astype(F32)
    smem = pl.BlockSpec(memory_space=pltpu.SMEM)

    n_rest = D_ATTN + 4 * D_CONV
    qT, kvc, ksel, vselT, kwin, vwinT, gsT, rest = pl.pallas_call(
        partial(_proj_kernel, tm=tm, nsel=nsel, kd=kd),
        grid=(b, nt),
        in_specs=[pl.BlockSpec((1, tm, D_MODEL), lambda i, j: (i, j, 0)),
                  pl.BlockSpec((1, D_MODEL), lambda i, j: (0, 0)),
                  pl.BlockSpec((D_MODEL, _C_END), lambda i, j: (0, 0))],
        out_specs=[pl.BlockSpec((1, D_ATTN, tm), lambda i, j: (i, 0, j)),
                   pl.BlockSpec((2, 1, tm, D_KV), lambda i, j: (0, i, j, 0)),
                   pl.BlockSpec((1, G, tm, kd), lambda i, j: (i, 0, j, 0)),
                   pl.BlockSpec((1, G, 1, V_ROWS, tm), lambda i, j: (i, 0, j, 0, 0)),
                   pl.BlockSpec((1, G, tm, LANES), lambda i, j: (i, 0, j, 0)),
                   pl.BlockSpec((1, G, 1, V_ROWS, tm), lambda i, j: (i, 0, j, 0, 0)),
                   pl.BlockSpec((1, N_GATE, tm), lambda i, j: (i, 0, j)),
                   pl.BlockSpec((1, tm, n_rest), lambda i, j: (i, j, 0))],
        out_shape=[jax.ShapeDtypeStruct((b, D_ATTN, s), BF16),
                   jax.ShapeDtypeStruct((2, b, s, D_KV), F32),
                   jax.ShapeDtypeStruct((b, G, s, kd), BF16),
                   jax.ShapeDtypeStruct((b, G, nt, V_ROWS, tm), BF16),
                   jax.ShapeDtypeStruct((b, G, s, LANES), BF16),
                   jax.ShapeDtypeStruct((b, G, nt, V_ROWS, tm), BF16),
                   jax.ShapeDtypeStruct((b, N_GATE, s), F32),
                   jax.ShapeDtypeStruct((b, s, n_rest), F32)],
        compiler_params=_cparams(("parallel", "parallel")),
        name="in_proj",
    )(x, norm_w, w_all)
    gs = gsT.reshape(b, 3, G, R, s)

    cw = CMP_STRIDE * D_KV
    lhs_rows = V_ROWS + nsel
    lhs_tail = jnp.asarray(np.concatenate([np.ones((V_ROWS - HEAD_DIM, nch), np.float32),
                                           _overlap_t(nsel, nch)], axis=0), BF16)
    kc_aug, c_lhs = pl.pallas_call(
        partial(_compress_kernel, nch=nch),
        grid=(2, b),
        in_specs=[pl.BlockSpec((1, 1, nch, cw), lambda xk, i: (xk, i, 0, 0)),
                  pl.BlockSpec((1, 2, 1, cw), lambda xk, i: (xk, 0, 0, 0)),
                  pl.BlockSpec((1, 2, cw, G * CMP_HIDDEN), lambda xk, i: (xk, 0, 0, 0)),
                  pl.BlockSpec((1, G * CMP_HIDDEN, D_KV), lambda xk, i: (xk, 0, 0)),
                  pl.BlockSpec((lhs_rows - HEAD_DIM, nch), lambda xk, i: (0, 0))],
        out_specs=[pl.BlockSpec((1, 1, G, nch, LANES), lambda xk, i: (xk, i, 0, 0, 0)),
                   pl.BlockSpec((1, 1, G, lhs_rows, nch), lambda xk, i: (xk, i, 0, 0, 0))],
        out_shape=[jax.ShapeDtypeStruct((2, b, G, nch, LANES), BF16),
                   jax.ShapeDtypeStruct((2, b, G, lhs_rows, nch), BF16)],
        compiler_params=_cparams(("parallel", "parallel")),
        name="compress",
    )(kvc.reshape(2, b, nch, cw), pee, w1e, w2e, lhs_tail)

    tq3 = Q_TILE_CMP
    cpq = tq3 // CMP_STRIDE
    assert nch // 8 <= LANES - HEAD_DIM and k_top > 3 and (nch // 2) % LANES == 0
    ocT, selb = pl.pallas_call(
        partial(_cmp_kernel, nch=nch, nsel=nsel, tq=tq3, k_top=k_top),
        grid=(b, G, s // tq3),
        in_specs=[smem,
                  pl.BlockSpec((1, R * HEAD_DIM, tq3), lambda i, g, t: (i, g, t)),
                  pl.BlockSpec((1, 1, 1, nch, LANES), lambda i, g, t: (0, i, g, 0, 0)),
                  pl.BlockSpec((1, 1, 1, lhs_rows, nch), lambda i, g, t: (1, i, g, 0, 0)),
                  pl.BlockSpec((1, 1, 1, R, tq3), lambda i, g, t: (i, 0, g, 0, t))],
        out_specs=[pl.BlockSpec((1, R * HEAD_DIM, tq3), lambda i, g, t: (i, g, t)),
                   pl.BlockSpec((1, 1, nsel, tq3), lambda i, g, t: (i, g, 0, t))],
        out_shape=[jax.ShapeDtypeStruct((b, D_ATTN, s), F32),
                   jax.ShapeDtypeStruct((b, G, nsel, s), BF16)],
        scratch_shapes=[pltpu.VMEM((LANES, R * tq3), BF16),
                        pltpu.VMEM((cpq + 16, R * tq3), F32),
                        pltpu.VMEM((lhs_rows, R * tq3), F32)]
                       + [pltpu.VMEM((nch, tq3), F32)] * R,
        compiler_params=_cparams(("parallel", "parallel", "arbitrary")),
        name="cmp_select",
    )(rel_bias, qT, kc_aug, c_lhs, gs)

    tq = Q_TILE
    nq = R * tq

    def flash(selected, k_arr, vT_arr, kdim, n_tbl, window, branch, name):
        in_specs = [smem, pl.BlockSpec((1, R * HEAD_DIM, tq), lambda i, g, t: (i, g, t))]
        args = [rel_bias, qT]
        if selected:
            in_specs.append(pl.BlockSpec((1, 1, nsel, tq), lambda i, g, t: (i, g, 0, t)))
            args.append(selb)
        in_specs += [pl.BlockSpec((1, 1, s, kdim), lambda i, g, t: (i, g, 0, 0)),
                     pl.BlockSpec((1, 1, nt, V_ROWS, tm), lambda i, g, t: (i, g, 0, 0, 0)),
                     pl.BlockSpec((1, 1, 1, R, tq), lambda i, g, t: (i, branch, g, 0, t))]
        args += [k_arr, vT_arr, gs]
        return pl.pallas_call(
            partial(_flash_kernel, tq=tq, nsel=nsel, kd=kdim, n_tbl_tiles=n_tbl, window=window, selected=selected),
            grid=(b, G, s // tq),
            in_specs=in_specs,
            out_specs=pl.BlockSpec((1, R * HEAD_DIM, tq), lambda i, g, t: (i, g, t)),
            out_shape=jax.ShapeDtypeStruct((b, D_ATTN, s), F32),
            scratch_shapes=[pltpu.VMEM((n_tbl * tq, nq), F32),
                            pltpu.VMEM((kdim, nq), BF16),
                            pltpu.VMEM((1, nq), F32),
                            pltpu.VMEM((V_ROWS, nq), F32),
                            pltpu.VMEM((tq, nq), F32), pltpu.VMEM((tq, nq), F32),
                            pltpu.VMEM((1, nq), F32), pltpu.VMEM((1, nq), F32),
                            pltpu.VMEM((1, nq), F32), pltpu.VMEM((1, nq), F32),
                            pltpu.VMEM((tq, nq), BF16), pltpu.VMEM((tq, nq), BF16)],
            compiler_params=_cparams(("parallel", "parallel", "arbitrary")),
            name=name,
        )(*args)

    osT = flash(True, ksel, vselT, kd, 2, None, 1, "sel_attn")
    owT = flash(False, kwin, vwinT, LANES, 3, WINDOW, 2, "win_attn")

    hb = tm // 8
    col = lambda c: pl.BlockSpec((1, tm, D_CONV), lambda i, j, c=c: (i, j, c))
    halo = lambda c: pl.BlockSpec((1, 8, D_CONV), lambda i, j, c=c: (i, jnp.maximum(j * hb - 1, 0), c))
    oT = pl.BlockSpec((1, D_ATTN, tm), lambda i, j: (i, 0, j))
    out = pl.pallas_call(
        partial(_out_kernel, tm=tm),
        grid=(b, nt),
        in_specs=[oT, oT, oT, col(0), col(1), col(2), col(3), col(4), halo(1), halo(3),
                  pl.BlockSpec((1, tm, D_MODEL), lambda i, j: (i, j, 0)),
                  pl.BlockSpec((3, D_CONV), lambda i, j: (0, 0)),
                  pl.BlockSpec((D_MODEL, D_MODEL), lambda i, j: (0, 0)),
                  pl.BlockSpec((1, D_MODEL), lambda i, j: (0, 0))],
        out_specs=pl.BlockSpec((1, tm, D_MODEL), lambda i, j: (i, j, 0)),
        out_shape=jax.ShapeDtypeStruct((b, s, D_MODEL), F32),
        compiler_params=_cparams(("parallel", "parallel")),
        name="out_proj",
    )(ocT, osT, owT, rest, rest, rest, rest, rest, rest, rest, x, conv_w[0], w_out[0].astype(BF16),
      final_norm_w.reshape(1, D_MODEL))
    return out
```

```python
import math
from functools import partial

import numpy as np
import jax
import jax.numpy as jnp
from jax import lax
from jax.experimental import pallas as pl
from jax.experimental.pallas import tpu as pltpu

D_MODEL = 1024
N_HEADS = 8
HEAD_DIM = 64
N_KV_HEADS = 2
GQA_REP = N_HEADS // N_KV_HEADS
D_ATTN = N_HEADS * HEAD_DIM
D_KV = N_KV_HEADS * HEAD_DIM
D_CONV = D_MODEL - D_ATTN
CMP_BLOCK = 32
CMP_STRIDE = 16
CMP_HIDDEN = 2 * HEAD_DIM
SEL_BLOCK = 64
N_SEL = 16
WINDOW = 512
N_BUCKETS = 32
MAX_DISTANCE = 128
EPS = 1e-6
NEG = -1e30
FORCE_SCORE = 1e3

LOG2E = 1.4426950408889634
LANES = 128
MXU_COLS = 256
V_ROWS = HEAD_DIM + 16
ROW_TILE = 256
Q_TILE_CMP = 256
Q_TILE = 256
MAX_LAG = 64.0
N_SLOTS = 3
VMEM_LIMIT = 48 * 1024 * 1024

F32 = jnp.float32
BF16 = jnp.bfloat16

_C_Q = 0
_C_CMP = D_ATTN
_C_SW = _C_CMP + 2 * D_KV
_C_REST = _C_SW + 4 * D_KV
_C_GATE = _C_REST + D_ATTN + 4 * D_CONV
_C_END = _C_GATE + LANES
N_GATE = 3 * N_HEADS


def _bucket_lower_bounds():
    max_exact = N_BUCKETS // 2
    d = np.arange(2 * MAX_DISTANCE, dtype=np.int32)
    nf = np.maximum(d, 1).astype(np.float32)
    large = max_exact + (np.log(nf / np.float32(max_exact)) / np.float32(math.log(MAX_DISTANCE / max_exact))
                         * np.float32(N_BUCKETS - max_exact)).astype(np.int32)
    bucket = np.where(d < max_exact, d, np.minimum(large, N_BUCKETS - 1))
    assert np.all(np.diff(bucket) >= 0) and np.all(bucket[MAX_DISTANCE:] == N_BUCKETS - 1)
    return [int(np.argmax(bucket >= b)) for b in range(N_BUCKETS)]


_BUCKET_LO = _bucket_lower_bounds()


def _rel_bias_delta(dist, bias_of_bucket):
    last = bias_of_bucket(N_BUCKETS - 1)
    val = jnp.full(dist.shape, (bias_of_bucket(0) - last) * LOG2E, F32)
    for b in range(1, N_BUCKETS - 1):
        val = jnp.where(dist >= _BUCKET_LO[b], (bias_of_bucket(b) - last) * LOG2E, val)
    val = jnp.where(dist >= _BUCKET_LO[N_BUCKETS - 1], 0.0, val)
    return jnp.where(dist < 0, NEG, val)


def _silu(z):
    return z * jax.nn.sigmoid(z)


def _proj_kernel(x_ref, nw_ref, w_ref, qT_ref, kvc_ref, ksel_ref, vselT_ref, kwin_ref, vwinT_ref,
                 gsT_ref, rest_ref, *, tm, nsel, kd):
    st = pl.program_id(1)
    x = x_ref[0]
    ms = jnp.mean(x * x, axis=-1, keepdims=True)
    h = (x * lax.rsqrt(ms + EPS) * nw_ref[...]).astype(BF16)

    def mm(lo, hi):
        return jnp.dot(h, w_ref[:, lo:hi], preferred_element_type=F32)

    q = mm(_C_Q, _C_CMP) * (HEAD_DIM ** -0.5 * LOG2E)
    qT_ref[0] = q.T.astype(BF16)

    kvc = mm(_C_CMP, _C_SW)
    kvc_ref[0, 0] = kvc[:, :D_KV]
    kvc_ref[1, 0] = kvc[:, D_KV:]

    a = mm(_C_SW, _C_REST)
    pos = st * tm + lax.broadcasted_iota(jnp.int32, (tm, nsel), 0)
    col = lax.broadcasted_iota(jnp.int32, (tm, nsel), 1)
    onehot = jnp.where((pos // SEL_BLOCK) == col, 1.0, 0.0).astype(BF16)
    ones = jnp.ones((V_ROWS - HEAD_DIM, tm), BF16)
    zeros_k = jnp.zeros((tm, LANES - HEAD_DIM), BF16)
    vsT = a[:, D_KV:2 * D_KV].T
    vwT = a[:, 3 * D_KV:4 * D_KV].T
    for g in range(N_KV_HEADS):
        lo, hi = g * HEAD_DIM, (g + 1) * HEAD_DIM
        parts = [a[:, lo:hi].astype(BF16), onehot]
        if kd > HEAD_DIM + nsel:
            parts.append(jnp.zeros((tm, kd - HEAD_DIM - nsel), BF16))
        ksel_ref[0, g] = jnp.concatenate(parts, axis=1)
        vselT_ref[0, g, 0] = jnp.concatenate([vsT[lo:hi].astype(BF16), ones], axis=0)
        kwin_ref[0, g] = jnp.concatenate([a[:, 2 * D_KV + lo:2 * D_KV + hi].astype(BF16), zeros_k], axis=1)
        vwinT_ref[0, g, 0] = jnp.concatenate([vwT[lo:hi].astype(BF16), ones], axis=0)

    rest_ref[0] = mm(_C_REST, _C_GATE)
    gates = jax.nn.sigmoid(mm(_C_GATE, _C_END))
    gsT_ref[0] = gates.T[:N_GATE]


def _compress_kernel(c_ref, pe_ref, w1_ref, w2_ref, lhs_tail_ref, aug_ref, t_ref, *, nch):
    c = c_ref[0, 0]
    a0 = jnp.dot((c + pe_ref[0, 0]).astype(BF16), w1_ref[0, 0], preferred_element_type=F32)
    a1 = jnp.dot((c + pe_ref[0, 1]).astype(BF16), w1_ref[0, 1], preferred_element_type=F32)
    hid = a0 + pltpu.roll(a1, nch - 1, 0)
    out = jnp.dot(_silu(hid).astype(BF16), w2_ref[0], preferred_element_type=F32)
    grp = lax.broadcasted_iota(jnp.int32, (nch, LANES - HEAD_DIM), 0) // 8
    col = lax.broadcasted_iota(jnp.int32, (nch, LANES - HEAD_DIM), 1)
    onehot = jnp.where(grp == col, 1.0, 0.0).astype(BF16)
    out_t = out.T.astype(BF16)
    for g in range(N_KV_HEADS):
        lo, hi = g * HEAD_DIM, (g + 1) * HEAD_DIM
        aug_ref[0, 0, g] = jnp.concatenate([out[:, lo:hi].astype(BF16), onehot], axis=1)
        t_ref[0, 0, g] = jnp.concatenate([out_t[lo:hi], lhs_tail_ref[...]], axis=0)


def _cmp_kernel(rb_ref, qT_ref, kc_ref, lhs_ref, gs_ref, ocT_ref, selb_ref,
                qa_ref, band_ref, acc_ref, s0_ref, s1_ref, s2_ref, s3_ref, *, nch, nsel, tq, k_top):
    g = pl.program_id(1)
    qt = pl.program_id(2)
    s_refs = (s0_ref, s1_ref, s2_ref, s3_ref)
    cpq = tq // CMP_STRIDE
    win = cpq + 16
    lim = cpq * (qt + 1)
    nq = GQA_REP * tq

    @pl.when(qt <= 1)
    def _():
        row = lax.broadcasted_iota(jnp.int32, (win, tq), 0) + jnp.maximum(cpq * qt - 16, 0)
        lane = lax.broadcasted_iota(jnp.int32, (win, tq), 1)
        dist = (qt * tq + lane) - (row * CMP_STRIDE + CMP_BLOCK - 1)
        for r in range(GQA_REP):
            band_ref[:, r * tq:(r + 1) * tq] = _rel_bias_delta(dist, lambda b, r=r: rb_ref[b, g * GQA_REP + r])

    q = qT_ref[0]
    qa_ref[0:HEAD_DIM, :] = jnp.concatenate([q[r * HEAD_DIM:(r + 1) * HEAD_DIM] for r in range(GQA_REP)], axis=1)
    grp = lax.broadcasted_iota(jnp.int32, (LANES - HEAD_DIM, nq), 0)
    qa_ref[HEAD_DIM:LANES, :] = jnp.where(grp * 8 >= lim, NEG, 0.0).astype(BF16)

    def run(rows):
        start = pl.multiple_of(jnp.clip(cpq * qt - 16, 0, rows - win), 8)
        k = kc_ref[0, 0, 0, 0:rows, :]
        lhs = lhs_ref[0, 0, 0, :, 0:rows]
        cols = [slice(r * tq, (r + 1) * tq) for r in range(GQA_REP)]

        def qk(r):
            s_refs[r][0:rows, :] = jnp.dot(k, qa_ref[:, cols[r]], preferred_element_type=F32)

        qk(0)
        for r in range(GQA_REP):
            if r + 1 < GQA_REP:
                qk(r + 1)
            sr = s_refs[r]
            sr[pl.ds(start, win), :] = sr[pl.ds(start, win), :] + band_ref[:, cols[r]]
            s = sr[0:rows, :]
            p = jnp.exp2(s - jnp.max(s, axis=0, keepdims=True)).astype(BF16)
            acc_ref[:, cols[r]] = jnp.dot(lhs, p, preferred_element_type=F32)

    half = nch // 2

    @pl.when(lim <= half)
    def _():
        run(half)

    @pl.when(lim > half)
    def _():
        run(nch)

    l = acc_ref[HEAD_DIM:HEAD_DIM + 1, :]
    t4 = qt * tq + (lax.broadcasted_iota(jnp.int32, l.shape, 1) & (tq - 1))
    inv = jnp.where(t4 >= CMP_BLOCK - 1, 1.0 / jnp.maximum(l, 1e-30), 0.0)
    imp = None
    for r in range(GQA_REP):
        cs = slice(r * tq, (r + 1) * tq)
        ocT_ref[0, r * HEAD_DIM:(r + 1) * HEAD_DIM, :] = (acc_ref[0:HEAD_DIM, cs] * inv[:, cs]
                                                          * gs_ref[0, 0, 0, r:r + 1, :])
        part = acc_ref[V_ROWS:V_ROWS + nsel, cs] * inv[:, cs]
        imp = part if imp is None else imp + part

    n = lax.broadcasted_iota(jnp.int32, (nsel, tq), 0)
    blk = (qt * tq + lax.broadcasted_iota(jnp.int32, (nsel, tq), 1)) // SEL_BLOCK
    forced = (n == 0) | (n == blk) | (n == blk - 1)
    candidate = (n <= blk) & jnp.logical_not(forced)
    score = jnp.where(forced, -2.0, jnp.where(n <= blk, imp, -1.0))

    def pick_all(_, score):
        return jnp.where(score == jnp.max(score, axis=0, keepdims=True), -2.0, score)

    taken = lax.fori_loop(0, k_top - 3, pick_all, score) == -2.0
    selb_ref[0, 0] = jnp.where(taken, 0.0, NEG).astype(BF16)
    n_taken = jnp.sum(jnp.where(taken & candidate, 1.0, 0.0), axis=0, keepdims=True)

    @pl.when(jnp.max(n_taken) > k_top - 3)
    def _():
        def pick_one(_, score):
            mx = jnp.max(score, axis=0, keepdims=True)
            idx = jnp.min(jnp.where(score == mx, n, nsel), axis=0, keepdims=True)
            return jnp.where(n == idx, -2.0, score)

        taken = lax.fori_loop(0, k_top - 3, pick_one, score) == -2.0
        selb_ref[0, 0] = jnp.where(taken, 0.0, NEG).astype(BF16)


def _flash_kernel(*refs, tq, nsel, kd, n_tbl_tiles, window, selected):
    if selected:
        rb_ref, qT_ref, selb_ref, k_ref, vT_ref, gs_ref, o_ref, tbl_ref, qa_ref, m_ref, acc_ref, lag_ref, *slot_refs = refs
    else:
        rb_ref, qT_ref, k_ref, vT_ref, gs_ref, o_ref, tbl_ref, qa_ref, m_ref, acc_ref, lag_ref, *slot_refs = refs
    s_ref, mt_ref, al_ref, p_ref = (slot_refs[i * N_SLOTS:(i + 1) * N_SLOTS] for i in range(4))
    g = pl.program_id(1)
    qt = pl.program_id(2)
    nq = GQA_REP * tq

    @pl.when(qt == 0)
    def _():
        for j in range(n_tbl_tiles):
            key = lax.broadcasted_iota(jnp.int32, (tq, tq), 0) + j * tq
            qry = lax.broadcasted_iota(jnp.int32, (tq, tq), 1) + (n_tbl_tiles - 1) * tq
            dist = qry - key
            for r in range(GQA_REP):
                val = _rel_bias_delta(dist, lambda b, r=r: rb_ref[b, g * GQA_REP + r])
                if window is not None:
                    val = jnp.where(dist >= window, NEG, val)
                tbl_ref[j * tq:(j + 1) * tq, r * tq:(r + 1) * tq] = val

    q = qT_ref[0]
    qa_ref[0:HEAD_DIM, :] = jnp.concatenate([q[r * HEAD_DIM:(r + 1) * HEAD_DIM] for r in range(GQA_REP)], axis=1)
    used = HEAD_DIM
    if selected:
        qa_ref[HEAD_DIM:HEAD_DIM + nsel, :] = jnp.concatenate([selb_ref[0, 0]] * GQA_REP, axis=1)
        used += nsel
    if kd > used:
        qa_ref[used:kd, :] = jnp.zeros((kd - used, nq), BF16)
    m_ref[...] = jnp.full((1, nq), NEG, F32)
    acc_ref[...] = jnp.zeros((V_ROWS, nq), F32)

    def stage_qk(t, slot, tbl_tile):
        k = k_ref[0, 0, pl.ds(pl.multiple_of(t * tq, tq), tq), :]
        s = jnp.dot(k, qa_ref[...], preferred_element_type=F32)
        if tbl_tile is not None:
            s = s + tbl_ref[tbl_tile * tq:(tbl_tile + 1) * tq, :]
        s_ref[slot][...] = s
        mt_ref[slot][...] = jnp.max(s, axis=0, keepdims=True)

    def stage_exp(slot):
        m_old = m_ref[...]
        m_new = jnp.maximum(m_old, mt_ref[slot][...])
        m_ref[...] = m_new
        al_ref[slot][...] = jnp.exp2(m_old - m_new)
        p_ref[slot][...] = jnp.exp2(s_ref[slot][...] - m_new).astype(BF16)

    def stage_pv(t, slot):
        acc_ref[...] = al_ref[slot][...] * acc_ref[...] + jnp.dot(vT_ref[0, 0, t], p_ref[slot][...],
                                                                  preferred_element_type=F32)

    def serial(t, tbl_tile):
        stage_qk(t, 0, tbl_tile)
        stage_exp(0)
        stage_pv(t, 0)

    last = n_tbl_tiles - 1

    def step(t, slot, tbl_tile):
        stage_pv(t - 2, (slot + 1) % N_SLOTS)
        stage_exp((slot + 2) % N_SLOTS)
        stage_qk(t, slot, tbl_tile)

    def stage_lagged(t, slot, tbl_tile):
        k = k_ref[0, 0, pl.ds(pl.multiple_of(t * tq, tq), tq), :]
        m = m_ref[...]
        s = jnp.dot(k, qa_ref[...], preferred_element_type=F32)
        if tbl_tile is not None:
            s = s + tbl_ref[tbl_tile * tq:(tbl_tile + 1) * tq, :]
        mt = jnp.max(s, axis=0, keepdims=True)
        p_ref[slot][...] = jnp.exp2((s - m).astype(BF16))
        m_new = jnp.maximum(m, mt)
        m_ref[...] = m_new
        al_ref[1 - slot][...] = jnp.exp2(m - m_new)
        lag_ref[...] = jnp.maximum(lag_ref[...], mt - m)

    def lagged_step(t, slot, tbl_tile):
        stage_pv(t - 1, 1 - slot)
        stage_lagged(t, slot, tbl_tile)

    if selected:
        n = qt + 1

        @pl.when(n >= 4)
        def _():
            lead = (n - 4) & 1
            serial(0, None)

            @pl.when(lead == 1)
            def _():
                serial(1, None)

            lag_ref[...] = jnp.full((1, nq), NEG, F32)
            al_ref[0][...] = jnp.ones((1, nq), F32)
            stage_lagged(lead + 1, 0, None)

            def body(j, carry):
                t = lead + 2 + 2 * j
                lagged_step(t, 1, None)
                lagged_step(t + 1, 0, None)
                return carry
            lax.fori_loop(0, (n - 4 - lead) >> 1, body, 0)

            lagged_step(n - 2, 1, 0)
            lagged_step(n - 1, 0, 1)
            stage_pv(n - 1, 0)

            @pl.when(jnp.max(lag_ref[...]) > MAX_LAG)
            def _():
                m_ref[...] = jnp.full((1, nq), NEG, F32)
                acc_ref[...] = jnp.zeros((V_ROWS, nq), F32)

                def redo(t, carry):
                    serial(t, None)
                    return carry
                lax.fori_loop(0, n - 2, redo, 0)
                serial(n - 2, 0)
                serial(n - 1, 1)

        @pl.when(n == 3)
        def _():
            serial(0, None)

        @pl.when((n >= 2) & (n < 4))
        def _():
            serial(qt - 1, 0)

        @pl.when(n < 4)
        def _():
            serial(qt, 1)
    else:
        @pl.when(qt >= last)
        def _():
            stage_qk(qt - 2, 0, 0)
            stage_exp(0)
            stage_qk(qt - 1, 1, 1)
            step(qt, 2, 2)
            stage_pv(qt - 1, 1)
            stage_exp(2)
            stage_pv(qt, 2)

        @pl.when(qt == 1)
        def _():
            serial(0, 1)

        @pl.when(qt < last)
        def _():
            serial(qt, 2)

    acc = acc_ref[...]
    o = acc[0:HEAD_DIM] / jnp.maximum(acc[HEAD_DIM:HEAD_DIM + 1], 1e-30)
    for r in range(GQA_REP):
        o_ref[0, r * HEAD_DIM:(r + 1) * HEAD_DIM, :] = o[:, r * tq:(r + 1) * tq] * gs_ref[0, 0, 0, r:r + 1, :]


def _out_kernel(oc_ref, os_ref, ow_ref, za_ref, ch_ref, cb_ref, cc_ref, zc_ref, hh_ref, hc_ref,
                x_ref, cw_ref, wo_ref, fw_ref, out_ref, *, tm):
    st = pl.program_id(1)
    attn = (oc_ref[0] + os_ref[0] + ow_ref[0]).T * _silu(za_ref[0])

    u = cc_ref[0] * ch_ref[0]
    halo = jnp.where(st == 0, 0.0, hc_ref[0] * hh_ref[0])
    row = lax.broadcasted_iota(jnp.int32, u.shape, 0)
    u1 = jnp.where(row == 0, halo[7:8], pltpu.roll(u, 1, 0))
    u2 = jnp.where(row == 0, halo[6:7], jnp.where(row == 1, halo[7:8], pltpu.roll(u, 2, 0)))
    y = cw_ref[0:1] * u2 + cw_ref[1:2] * u1 + cw_ref[2:3] * u
    conv = cb_ref[0] * y * _silu(zc_ref[0])

    mixed = jnp.concatenate([attn, conv], axis=1).astype(BF16)
    y2 = x_ref[0] + jnp.dot(mixed, wo_ref[...], preferred_element_type=F32)
    ms = jnp.mean(y2 * y2, axis=-1, keepdims=True)
    out_ref[0] = y2 * lax.rsqrt(ms + EPS) * fw_ref[...]


def _overlap_t(nsel, nch):
    c = np.arange(nch)[None, :] * CMP_STRIDE
    n = np.arange(nsel)[:, None] * SEL_BLOCK
    return ((c < n + SEL_BLOCK) & (c + CMP_BLOCK > n)).astype(np.float32)


def _cparams(sem):
    return pltpu.CompilerParams(dimension_semantics=sem, vmem_limit_bytes=VMEM_LIMIT)


def kernel(x, norm_w, w_in, w_ck1, w_ck2, pe_k, w_cv1, w_cv2, pe_v, conv_w, w_out, rel_bias, final_norm_w):
    b, s, _ = x.shape
    assert norm_w.shape[0] == 1 and s % ROW_TILE == 0 and WINDOW == 2 * Q_TILE and ROW_TILE == Q_TILE
    tm = ROW_TILE
    nt = s // tm
    nch = s // CMP_STRIDE
    nsel = s // SEL_BLOCK
    k_top = min(N_SEL, nsel)
    kd = -(-(HEAD_DIM + nsel) // LANES) * LANES
    G, R = N_KV_HEADS, GQA_REP

    w = w_in[0]
    c0 = D_ATTN + 6 * D_KV
    w_all = jnp.concatenate([w[:, :c0], w[:, c0 + N_GATE:], w[:, c0:c0 + N_GATE],
                             jnp.zeros((D_MODEL, LANES - N_GATE), F32)], axis=1).astype(BF16)
    eye = jnp.eye(G, dtype=F32)

    def expand1(w1):
        w1 = w1.reshape(2, CMP_STRIDE, HEAD_DIM, CMP_HIDDEN)
        return jnp.einsum('mjdh,ab->mjadbh', w1, eye).reshape(2, CMP_STRIDE * D_KV, G * CMP_HIDDEN)

    def expand_pe(pe):
        pe = pe.reshape(2, CMP_STRIDE, 1, HEAD_DIM)
        return jnp.broadcast_to(pe, (2, CMP_STRIDE, G, HEAD_DIM)).reshape(2, 1, CMP_STRIDE * D_KV)

    def expand2(w2):
        return jnp.einsum('hd,ab->ahbd', w2, eye).reshape(G * CMP_HIDDEN, D_KV)

    w1e = jnp.stack([expand1(w_ck1[0]), expand1(w_cv1[0])]).astype(BF16)
    pee = jnp.stack([expand_pe(pe_k[0]), expand_pe(pe_v[0])])
    w2e = jnp.stack([expand2(w_ck2[0]), expand2(w_cv2[0])]).astype(BF16)
    rel_bias = rel_bias.astype(F32)
    smem = pl.BlockSpec(memory_space=pltpu.SMEM)

    n_rest = D_ATTN + 4 * D_CONV
    qT, kvc, ksel, vselT, kwin, vwinT, gsT, rest = pl.pallas_call(
        partial(_proj_kernel, tm=tm, nsel=nsel, kd=kd),
        grid=(b, nt),
        in_specs=[pl.BlockSpec((1, tm, D_MODEL), lambda i, j: (i, j, 0)),
                  pl.BlockSpec((1, D_MODEL), lambda i, j: (0, 0)),
                  pl.BlockSpec((D_MODEL, _C_END), lambda i, j: (0, 0))],
        out_specs=[pl.BlockSpec((1, D_ATTN, tm), lambda i, j: (i, 0, j)),
                   pl.BlockSpec((2, 1, tm, D_KV), lambda i, j: (0, i, j, 0)),
                   pl.BlockSpec((1, G, tm, kd), lambda i, j: (i, 0, j, 0)),
                   pl.BlockSpec((1, G, 1, V_ROWS, tm), lambda i, j: (i, 0, j, 0, 0)),
                   pl.BlockSpec((1, G, tm, LANES), lambda i, j: (i, 0, j, 0)),
                   pl.BlockSpec((1, G, 1, V_ROWS, tm), lambda i, j: (i, 0, j, 0, 0)),
                   pl.BlockSpec((1, N_GATE, tm), lambda i, j: (i, 0, j)),
                   pl.BlockSpec((1, tm, n_rest), lambda i, j: (i, j, 0))],
        out_shape=[jax.ShapeDtypeStruct((b, D_ATTN, s), BF16),
                   jax.ShapeDtypeStruct((2, b, s, D_KV), F32),
                   jax.ShapeDtypeStruct((b, G, s, kd), BF16),
                   jax.ShapeDtypeStruct((b, G, nt, V_ROWS, tm), BF16),
                   jax.ShapeDtypeStruct((b, G, s, LANES), BF16),
                   jax.ShapeDtypeStruct((b, G, nt, V_ROWS, tm), BF16),
                   jax.ShapeDtypeStruct((b, N_GATE, s), F32),
                   jax.ShapeDtypeStruct((b, s, n_rest), F32)],
        compiler_params=_cparams(("parallel", "parallel")),
        name="in_proj",
    )(x, norm_w, w_all)
    gs = gsT.reshape(b, 3, G, R, s)

    cw = CMP_STRIDE * D_KV
    lhs_rows = V_ROWS + nsel
    lhs_tail = jnp.asarray(np.concatenate([np.ones((V_ROWS - HEAD_DIM, nch), np.float32),
                                           _overlap_t(nsel, nch)], axis=0), BF16)
    kc_aug, c_lhs = pl.pallas_call(
        partial(_compress_kernel, nch=nch),
        grid=(2, b),
        in_specs=[pl.BlockSpec((1, 1, nch, cw), lambda xk, i: (xk, i, 0, 0)),
                  pl.BlockSpec((1, 2, 1, cw), lambda xk, i: (xk, 0, 0, 0)),
                  pl.BlockSpec((1, 2, cw, G * CMP_HIDDEN), lambda xk, i: (xk, 0, 0, 0)),
                  pl.BlockSpec((1, G * CMP_HIDDEN, D_KV), lambda xk, i: (xk, 0, 0)),
                  pl.BlockSpec((lhs_rows - HEAD_DIM, nch), lambda xk, i: (0, 0))],
        out_specs=[pl.BlockSpec((1, 1, G, nch, LANES), lambda xk, i: (xk, i, 0, 0, 0)),
                   pl.BlockSpec((1, 1, G, lhs_rows, nch), lambda xk, i: (xk, i, 0, 0, 0))],
        out_shape=[jax.ShapeDtypeStruct((2, b, G, nch, LANES), BF16),
                   jax.ShapeDtypeStruct((2, b, G, lhs_rows, nch), BF16)],
        compiler_params=_cparams(("parallel", "parallel")),
        name="compress",
    )(kvc.reshape(2, b, nch, cw), pee, w1e, w2e, lhs_tail)

    tq3 = Q_TILE_CMP
    cpq = tq3 // CMP_STRIDE
    assert nch // 8 <= LANES - HEAD_DIM and k_top > 3 and (nch // 2) % LANES == 0
    ocT, selb = pl.pallas_call(
        partial(_cmp_kernel, nch=nch, nsel=nsel, tq=tq3, k_top=k_top),
        grid=(b, G, s // tq3),
        in_specs=[smem,
                  pl.BlockSpec((1, R * HEAD_DIM, tq3), lambda i, g, t: (i, g, t)),
                  pl.BlockSpec((1, 1, 1, nch, LANES), lambda i, g, t: (0, i, g, 0, 0)),
                  pl.BlockSpec((1, 1, 1, lhs_rows, nch), lambda i, g, t: (1, i, g, 0, 0)),
                  pl.BlockSpec((1, 1, 1, R, tq3), lambda i, g, t: (i, 0, g, 0, t))],
        out_specs=[pl.BlockSpec((1, R * HEAD_DIM, tq3), lambda i, g, t: (i, g, t)),
                   pl.BlockSpec((1, 1, nsel, tq3), lambda i, g, t: (i, g, 0, t))],
        out_shape=[jax.ShapeDtypeStruct((b, D_ATTN, s), F32),
                   jax.ShapeDtypeStruct((b, G, nsel, s), BF16)],
        scratch_shapes=[pltpu.VMEM((LANES, R * tq3), BF16),
                        pltpu.VMEM((cpq + 16, R * tq3), F32),
                        pltpu.VMEM((lhs_rows, R * tq3), F32)]
                       + [pltpu.VMEM((nch, tq3), F32)] * R,
        compiler_params=_cparams(("parallel", "parallel", "arbitrary")),
        name="cmp_select",
    )(rel_bias, qT, kc_aug, c_lhs, gs)

    tq = Q_TILE
    nq = R * tq

    def flash(selected, k_arr, vT_arr, kdim, n_tbl, window, branch, name):
        in_specs = [smem, pl.BlockSpec((1, R * HEAD_DIM, tq), lambda i, g, t: (i, g, t))]
        args = [rel_bias, qT]
        if selected:
            in_specs.append(pl.BlockSpec((1, 1, nsel, tq), lambda i, g, t: (i, g, 0, t)))
            args.append(selb)
        in_specs += [pl.BlockSpec((1, 1, s, kdim), lambda i, g, t: (i, g, 0, 0)),
                     pl.BlockSpec((1, 1, nt, V_ROWS, tm), lambda i, g, t: (i, g, 0, 0, 0)),
                     pl.BlockSpec((1, 1, 1, R, tq), lambda i, g, t: (i, branch, g, 0, t))]
        args += [k_arr, vT_arr, gs]
        return pl.pallas_call(
            partial(_flash_kernel, tq=tq, nsel=nsel, kd=kdim, n_tbl_tiles=n_tbl, window=window, selected=selected),
            grid=(b, G, s // tq),
            in_specs=in_specs,
            out_specs=pl.BlockSpec((1, R * HEAD_DIM, tq), lambda i, g, t: (i, g, t)),
            out_shape=jax.ShapeDtypeStruct((b, D_ATTN, s), F32),
            scratch_shapes=[pltpu.VMEM((n_tbl * tq, nq), F32),
                            pltpu.VMEM((kdim, nq), BF16),
                            pltpu.VMEM((1, nq), F32),
                            pltpu.VMEM((V_ROWS, nq), F32),
                            pltpu.VMEM((1, nq), F32)]
                           + [pltpu.VMEM((tq, nq), F32)] * N_SLOTS
                           + [pltpu.VMEM((1, nq), F32)] * N_SLOTS
                           + [pltpu.VMEM((1, nq), F32)] * N_SLOTS
                           + [pltpu.VMEM((tq, nq), BF16)] * N_SLOTS,
            compiler_params=_cparams(("parallel", "parallel", "arbitrary")),
            name=name,
        )(*args)

    osT = flash(True, ksel, vselT, kd, 2, None, 1, "sel_attn")
    owT = flash(False, kwin, vwinT, LANES, 3, WINDOW, 2, "win_attn")

    hb = tm // 8
    col = lambda c: pl.BlockSpec((1, tm, D_CONV), lambda i, j, c=c: (i, j, c))
    halo = lambda c: pl.BlockSpec((1, 8, D_CONV), lambda i, j, c=c: (i, jnp.maximum(j * hb - 1, 0), c))
    oT = pl.BlockSpec((1, D_ATTN, tm), lambda i, j: (i, 0, j))
    out = pl.pallas_call(
        partial(_out_kernel, tm=tm),
        grid=(b, nt),
        in_specs=[oT, oT, oT, col(0), col(1), col(2), col(3), col(4), halo(1), halo(3),
                  pl.BlockSpec((1, tm, D_MODEL), lambda i, j: (i, j, 0)),
                  pl.BlockSpec((3, D_CONV), lambda i, j: (0, 0)),
                  pl.BlockSpec((D_MODEL, D_MODEL), lambda i, j: (0, 0)),
                  pl.BlockSpec((1, D_MODEL), lambda i, j: (0, 0))],
        out_specs=pl.BlockSpec((1, tm, D_MODEL), lambda i, j: (i, j, 0)),
        out_shape=jax.ShapeDtypeStruct((b, s, D_MODEL), F32),
        compiler_params=_cparams(("parallel", "parallel")),
        name="out_proj",
    )(ocT, osT, owT, rest, rest, rest, rest, rest, rest, rest, x, conv_w[0], w_out[0].astype(BF16),
      final_norm_w.reshape(1, D_MODEL))
    return out
```

```python
import math
from functools import partial

import numpy as np
import jax
import jax.numpy as jnp
from jax import lax
from jax.experimental import pallas as pl
from jax.experimental.pallas import tpu as pltpu

D_MODEL = 1024
N_HEADS = 8
HEAD_DIM = 64
N_KV_HEADS = 2
GQA_REP = N_HEADS // N_KV_HEADS
D_ATTN = N_HEADS * HEAD_DIM
D_KV = N_KV_HEADS * HEAD_DIM
D_CONV = D_MODEL - D_ATTN
CMP_BLOCK = 32
CMP_STRIDE = 16
CMP_HIDDEN = 2 * HEAD_DIM
SEL_BLOCK = 64
N_SEL = 16
WINDOW = 512
N_BUCKETS = 32
MAX_DISTANCE = 128
EPS = 1e-6
NEG = -1e30
FORCE_SCORE = 1e3

LOG2E = 1.4426950408889634
LANES = 128
MXU_COLS = 256
V_ROWS = HEAD_DIM + 16
ROW_TILE = 256
Q_TILE_CMP = 256
Q_TILE = 256
HALO_ROWS = 16
MAX_LAG = 64.0
N_SLOTS = 3
VMEM_LIMIT = 48 * 1024 * 1024

F32 = jnp.float32
BF16 = jnp.bfloat16

_C_Q = 0
_C_CMP = D_ATTN
_C_SW = _C_CMP + 2 * D_KV
_C_REST = _C_SW + 4 * D_KV
_C_GATE = _C_REST + D_ATTN + 4 * D_CONV
_C_END = _C_GATE + LANES
N_GATE = 3 * N_HEADS


def _bucket_lower_bounds():
    max_exact = N_BUCKETS // 2
    d = np.arange(2 * MAX_DISTANCE, dtype=np.int32)
    nf = np.maximum(d, 1).astype(np.float32)
    large = max_exact + (np.log(nf / np.float32(max_exact)) / np.float32(math.log(MAX_DISTANCE / max_exact))
                         * np.float32(N_BUCKETS - max_exact)).astype(np.int32)
    bucket = np.where(d < max_exact, d, np.minimum(large, N_BUCKETS - 1))
    assert np.all(np.diff(bucket) >= 0) and np.all(bucket[MAX_DISTANCE:] == N_BUCKETS - 1)
    return [int(np.argmax(bucket >= b)) for b in range(N_BUCKETS)]


_BUCKET_LO = _bucket_lower_bounds()


def _rel_bias_delta(dist, bias_of_bucket):
    last = bias_of_bucket(N_BUCKETS - 1)
    val = jnp.full(dist.shape, (bias_of_bucket(0) - last) * LOG2E, F32)
    for b in range(1, N_BUCKETS - 1):
        val = jnp.where(dist >= _BUCKET_LO[b], (bias_of_bucket(b) - last) * LOG2E, val)
    val = jnp.where(dist >= _BUCKET_LO[N_BUCKETS - 1], 0.0, val)
    return jnp.where(dist < 0, NEG, val)


def _silu(z):
    return z * jax.nn.sigmoid(z)


def _proj_kernel(x_ref, nw_ref, w_ref, qT_ref, kvc_ref, ksel_ref, vselT_ref, kwin_ref, vwinT_ref,
                 gsT_ref, rest_ref, kv_scr, *, tm, nsel, kd):
    st = pl.program_id(1)
    x = x_ref[0]
    ms = jnp.mean(x * x, axis=-1, keepdims=True)
    h = (x * lax.rsqrt(ms + EPS) * nw_ref[...]).astype(BF16)

    def mm(lo, hi):
        return jnp.dot(h, w_ref[:, lo:hi], preferred_element_type=F32)

    q = mm(_C_Q, _C_CMP) * (HEAD_DIM ** -0.5 * LOG2E)
    qT_ref[0] = q.T.astype(BF16)

    kvc = mm(_C_CMP, _C_SW)
    for xk in range(2):
        kv_scr[xk] = kvc[:, xk * D_KV:(xk + 1) * D_KV]
        for j in range(CMP_STRIDE):
            kvc_ref[xk, 0, :, j * D_KV:(j + 1) * D_KV] = kv_scr[xk, pl.ds(j, tm // CMP_STRIDE, stride=CMP_STRIDE), :]

    a = mm(_C_SW, _C_REST)
    pos = st * tm + lax.broadcasted_iota(jnp.int32, (tm, nsel), 0)
    col = lax.broadcasted_iota(jnp.int32, (tm, nsel), 1)
    onehot = jnp.where((pos // SEL_BLOCK) == col, 1.0, 0.0).astype(BF16)
    ones = jnp.ones((V_ROWS - HEAD_DIM, tm), BF16)
    zeros_k = jnp.zeros((tm, LANES - HEAD_DIM), BF16)
    vsT = a[:, D_KV:2 * D_KV].T
    vwT = a[:, 3 * D_KV:4 * D_KV].T
    for g in range(N_KV_HEADS):
        lo, hi = g * HEAD_DIM, (g + 1) * HEAD_DIM
        parts = [a[:, lo:hi].astype(BF16), onehot]
        if kd > HEAD_DIM + nsel:
            parts.append(jnp.zeros((tm, kd - HEAD_DIM - nsel), BF16))
        ksel_ref[0, g] = jnp.concatenate(parts, axis=1)
        vselT_ref[0, g, 0] = jnp.concatenate([vsT[lo:hi].astype(BF16), ones], axis=0)
        kwin_ref[0, g] = jnp.concatenate([a[:, 2 * D_KV + lo:2 * D_KV + hi].astype(BF16), zeros_k], axis=1)
        vwinT_ref[0, g, 0] = jnp.concatenate([vwT[lo:hi].astype(BF16), ones], axis=0)

    rest_ref[0] = mm(_C_REST, _C_GATE).astype(BF16)
    gates = jax.nn.sigmoid(mm(_C_GATE, _C_END))
    gsT_ref[0] = gates.T[:N_GATE]


def _compress_kernel(c_ref, pe_ref, w1_ref, w2_ref, lhs_tail_ref, aug_ref, t_ref, *, nch):
    c = c_ref[0, 0]
    a0 = jnp.dot((c + pe_ref[0, 0]).astype(BF16), w1_ref[0, 0], preferred_element_type=F32)
    a1 = jnp.dot((c + pe_ref[0, 1]).astype(BF16), w1_ref[0, 1], preferred_element_type=F32)
    hid = a0 + pltpu.roll(a1, nch - 1, 0)
    out = jnp.dot(_silu(hid).astype(BF16), w2_ref[0], preferred_element_type=F32)
    grp = lax.broadcasted_iota(jnp.int32, (nch, LANES - HEAD_DIM), 0) // 8
    col = lax.broadcasted_iota(jnp.int32, (nch, LANES - HEAD_DIM), 1)
    onehot = jnp.where(grp == col, 1.0, 0.0).astype(BF16)
    out_t = out.T.astype(BF16)
    for g in range(N_KV_HEADS):
        lo, hi = g * HEAD_DIM, (g + 1) * HEAD_DIM
        aug_ref[0, 0, g] = jnp.concatenate([out[:, lo:hi].astype(BF16), onehot], axis=1)
        t_ref[0, 0, g] = jnp.concatenate([out_t[lo:hi], lhs_tail_ref[...]], axis=0)


def _cmp_kernel(rb_ref, qT_ref, kc_ref, lhs_ref, gs_ref, ocT_ref, selb_ref,
                qa_ref, band_ref, acc_ref, s0_ref, s1_ref, s2_ref, s3_ref, *, nch, nsel, tq, k_top):
    g = pl.program_id(1)
    qt = pl.program_id(2)
    s_refs = (s0_ref, s1_ref, s2_ref, s3_ref)
    cpq = tq // CMP_STRIDE
    win = cpq + 16
    lim = cpq * (qt + 1)
    nq = GQA_REP * tq

    @pl.when(qt <= 1)
    def _():
        row = lax.broadcasted_iota(jnp.int32, (win, tq), 0) + jnp.maximum(cpq * qt - 16, 0)
        lane = lax.broadcasted_iota(jnp.int32, (win, tq), 1)
        dist = (qt * tq + lane) - (row * CMP_STRIDE + CMP_BLOCK - 1)
        for r in range(GQA_REP):
            band_ref[:, r * tq:(r + 1) * tq] = _rel_bias_delta(dist, lambda b, r=r: rb_ref[b, g * GQA_REP + r])

    q = qT_ref[0]
    qa_ref[0:HEAD_DIM, :] = jnp.concatenate([q[r * HEAD_DIM:(r + 1) * HEAD_DIM] for r in range(GQA_REP)], axis=1)
    grp = lax.broadcasted_iota(jnp.int32, (LANES - HEAD_DIM, nq), 0)
    qa_ref[HEAD_DIM:LANES, :] = jnp.where(grp * 8 >= lim, NEG, 0.0).astype(BF16)

    def run(rows):
        start = pl.multiple_of(jnp.clip(cpq * qt - 16, 0, rows - win), 8)
        k = kc_ref[0, 0, 0, 0:rows, :]
        lhs = lhs_ref[0, 0, 0, :, 0:rows]
        cols = [slice(r * tq, (r + 1) * tq) for r in range(GQA_REP)]

        def qk(r):
            s_refs[r][0:rows, :] = jnp.dot(k, qa_ref[:, cols[r]], preferred_element_type=F32)

        qk(0)
        for r in range(GQA_REP):
            if r + 1 < GQA_REP:
                qk(r + 1)
            sr = s_refs[r]
            sr[pl.ds(start, win), :] = sr[pl.ds(start, win), :] + band_ref[:, cols[r]]
            s = sr[0:rows, :]
            p = jnp.exp2(s - jnp.max(s, axis=0, keepdims=True)).astype(BF16)
            acc_ref[:, cols[r]] = jnp.dot(lhs, p, preferred_element_type=F32)

    half = nch // 2

    @pl.when(lim <= half)
    def _():
        run(half)

    @pl.when(lim > half)
    def _():
        run(nch)

    l = acc_ref[HEAD_DIM:HEAD_DIM + 1, :]
    t4 = qt * tq + (lax.broadcasted_iota(jnp.int32, l.shape, 1) & (tq - 1))
    inv = jnp.where(t4 >= CMP_BLOCK - 1, 1.0 / jnp.maximum(l, 1e-30), 0.0)
    imp = None
    for r in range(GQA_REP):
        cs = slice(r * tq, (r + 1) * tq)
        ocT_ref[0, r * HEAD_DIM:(r + 1) * HEAD_DIM, :] = (acc_ref[0:HEAD_DIM, cs] * inv[:, cs]
                                                          * gs_ref[0, 0, 0, r:r + 1, :]).astype(BF16)
        part = acc_ref[V_ROWS:V_ROWS + nsel, cs] * inv[:, cs]
        imp = part if imp is None else imp + part

    n = lax.broadcasted_iota(jnp.int32, (nsel, tq), 0)
    blk = (qt * tq + lax.broadcasted_iota(jnp.int32, (nsel, tq), 1)) // SEL_BLOCK
    forced = (n == 0) | (n == blk) | (n == blk - 1)
    candidate = (n <= blk) & jnp.logical_not(forced)
    score = jnp.where(forced, -2.0, jnp.where(n <= blk, imp, -1.0))

    def pick_all(_, score):
        return jnp.where(score == jnp.max(score, axis=0, keepdims=True), -2.0, score)

    taken = lax.fori_loop(0, k_top - 3, pick_all, score) == -2.0
    selb_ref[0, 0] = jnp.where(taken, 0.0, NEG).astype(BF16)
    n_taken = jnp.sum(jnp.where(taken & candidate, 1.0, 0.0), axis=0, keepdims=True)

    @pl.when(jnp.max(n_taken) > k_top - 3)
    def _():
        def pick_one(_, score):
            mx = jnp.max(score, axis=0, keepdims=True)
            idx = jnp.min(jnp.where(score == mx, n, nsel), axis=0, keepdims=True)
            return jnp.where(n == idx, -2.0, score)

        taken = lax.fori_loop(0, k_top - 3, pick_one, score) == -2.0
        selb_ref[0, 0] = jnp.where(taken, 0.0, NEG).astype(BF16)


def _flash_kernel(*refs, tq, nsel, kd, n_tbl_tiles, window, selected):
    if selected:
        rb_ref, qT_ref, selb_ref, k_ref, vT_ref, gs_ref, o_ref, tbl_ref, qa_ref, m_ref, acc_ref, lag_ref, *slot_refs = refs
    else:
        rb_ref, qT_ref, k_ref, vT_ref, gs_ref, o_ref, tbl_ref, qa_ref, m_ref, acc_ref, lag_ref, *slot_refs = refs
    s_ref, mt_ref, al_ref, p_ref = (slot_refs[i * N_SLOTS:(i + 1) * N_SLOTS] for i in range(4))
    g = pl.program_id(1)
    qt = pl.program_id(2)
    nq = GQA_REP * tq

    @pl.when(qt == 0)
    def _():
        for j in range(n_tbl_tiles):
            key = lax.broadcasted_iota(jnp.int32, (tq, tq), 0) + j * tq
            qry = lax.broadcasted_iota(jnp.int32, (tq, tq), 1) + (n_tbl_tiles - 1) * tq
            dist = qry - key
            for r in range(GQA_REP):
                val = _rel_bias_delta(dist, lambda b, r=r: rb_ref[b, g * GQA_REP + r])
                if window is not None:
                    val = jnp.where(dist >= window, NEG, val)
                tbl_ref[j * tq:(j + 1) * tq, r * tq:(r + 1) * tq] = val

    q = qT_ref[0]
    qa_ref[0:HEAD_DIM, :] = jnp.concatenate([q[r * HEAD_DIM:(r + 1) * HEAD_DIM] for r in range(GQA_REP)], axis=1)
    used = HEAD_DIM
    if selected:
        qa_ref[HEAD_DIM:HEAD_DIM + nsel, :] = jnp.concatenate([selb_ref[0, 0]] * GQA_REP, axis=1)
        used += nsel
    if kd > used:
        qa_ref[used:kd, :] = jnp.zeros((kd - used, nq), BF16)
    m_ref[...] = jnp.full((1, nq), NEG, F32)
    acc_ref[...] = jnp.zeros((V_ROWS, nq), F32)

    def stage_qk(t, slot, tbl_tile):
        k = k_ref[0, 0, pl.ds(pl.multiple_of(t * tq, tq), tq), :]
        s = jnp.dot(k, qa_ref[...], preferred_element_type=F32)
        if tbl_tile is not None:
            s = s + tbl_ref[tbl_tile * tq:(tbl_tile + 1) * tq, :]
        s_ref[slot][...] = s
        mt_ref[slot][...] = jnp.max(s, axis=0, keepdims=True)

    def stage_exp(slot):
        m_old = m_ref[...]
        m_new = jnp.maximum(m_old, mt_ref[slot][...])
        m_ref[...] = m_new
        al_ref[slot][...] = jnp.exp2(m_old - m_new)
        p_ref[slot][...] = jnp.exp2(s_ref[slot][...] - m_new).astype(BF16)

    def stage_pv(t, slot):
        acc_ref[...] = al_ref[slot][...] * acc_ref[...] + jnp.dot(vT_ref[0, 0, t], p_ref[slot][...],
                                                                  preferred_element_type=F32)

    def serial(t, tbl_tile):
        stage_qk(t, 0, tbl_tile)
        stage_exp(0)
        stage_pv(t, 0)

    last = n_tbl_tiles - 1

    def step(t, slot, tbl_tile):
        stage_pv(t - 2, (slot + 1) % N_SLOTS)
        stage_exp((slot + 2) % N_SLOTS)
        stage_qk(t, slot, tbl_tile)

    def stage_lagged(t, slot, tbl_tile):
        k = k_ref[0, 0, pl.ds(pl.multiple_of(t * tq, tq), tq), :]
        m = m_ref[...]
        s = jnp.dot(k, qa_ref[...], preferred_element_type=F32)
        if tbl_tile is not None:
            s = s + tbl_ref[tbl_tile * tq:(tbl_tile + 1) * tq, :]
        mt = jnp.max(s, axis=0, keepdims=True)
        p_ref[slot][...] = jnp.exp2((s - m).astype(BF16))
        m_new = jnp.maximum(m, mt)
        m_ref[...] = m_new
        al_ref[1 - slot][...] = jnp.exp2(m - m_new)
        lag_ref[...] = jnp.maximum(lag_ref[...], mt - m)

    def lagged_step(t, slot, tbl_tile):
        stage_pv(t - 1, 1 - slot)
        stage_lagged(t, slot, tbl_tile)

    if selected:
        n = qt + 1

        @pl.when(n >= 4)
        def _():
            lead = (n - 4) & 1
            serial(0, None)

            @pl.when(lead == 1)
            def _():
                serial(1, None)

            lag_ref[...] = jnp.full((1, nq), NEG, F32)
            al_ref[0][...] = jnp.ones((1, nq), F32)
            stage_lagged(lead + 1, 0, None)

            def body(j, carry):
                t = lead + 2 + 2 * j
                lagged_step(t, 1, None)
                lagged_step(t + 1, 0, None)
                return carry
            lax.fori_loop(0, (n - 4 - lead) >> 1, body, 0)

            lagged_step(n - 2, 1, 0)
            lagged_step(n - 1, 0, 1)
            stage_pv(n - 1, 0)

            @pl.when(jnp.max(lag_ref[...]) > MAX_LAG)
            def _():
                m_ref[...] = jnp.full((1, nq), NEG, F32)
                acc_ref[...] = jnp.zeros((V_ROWS, nq), F32)

                def redo(t, carry):
                    serial(t, None)
                    return carry
                lax.fori_loop(0, n - 2, redo, 0)
                serial(n - 2, 0)
                serial(n - 1, 1)

        @pl.when(n == 3)
        def _():
            serial(0, None)

        @pl.when((n >= 2) & (n < 4))
        def _():
            serial(qt - 1, 0)

        @pl.when(n < 4)
        def _():
            serial(qt, 1)
    else:
        @pl.when(qt >= last)
        def _():
            stage_qk(qt - 2, 0, 0)
            stage_exp(0)
            stage_qk(qt - 1, 1, 1)
            step(qt, 2, 2)
            stage_pv(qt - 1, 1)
            stage_exp(2)
            stage_pv(qt, 2)

        @pl.when(qt == 1)
        def _():
            serial(0, 1)

        @pl.when(qt < last)
        def _():
            serial(qt, 2)

    acc = acc_ref[...]
    o = acc[0:HEAD_DIM] / jnp.maximum(acc[HEAD_DIM:HEAD_DIM + 1], 1e-30)
    for r in range(GQA_REP):
        o_ref[0, r * HEAD_DIM:(r + 1) * HEAD_DIM, :] = (o[:, r * tq:(r + 1) * tq]
                                                        * gs_ref[0, 0, 0, r:r + 1, :]).astype(BF16)


def _out_kernel(oc_ref, os_ref, ow_ref, za_ref, ch_ref, cb_ref, cc_ref, zc_ref, hh_ref, hc_ref,
                x_ref, cw_ref, wo_ref, fw_ref, out_ref, *, tm):
    st = pl.program_id(1)
    f32 = lambda ref: ref[0].astype(F32)
    attn = (f32(oc_ref) + f32(os_ref) + f32(ow_ref)).T * _silu(f32(za_ref))

    u = f32(cc_ref) * f32(ch_ref)
    halo = jnp.where(st == 0, 0.0, f32(hc_ref) * f32(hh_ref))
    row = lax.broadcasted_iota(jnp.int32, u.shape, 0)
    h1, h2 = halo[HALO_ROWS - 1:HALO_ROWS], halo[HALO_ROWS - 2:HALO_ROWS - 1]
    u1 = jnp.where(row == 0, h1, pltpu.roll(u, 1, 0))
    u2 = jnp.where(row == 0, h2, jnp.where(row == 1, h1, pltpu.roll(u, 2, 0)))
    y = cw_ref[0:1] * u2 + cw_ref[1:2] * u1 + cw_ref[2:3] * u
    conv = f32(cb_ref) * y * _silu(f32(zc_ref))

    mixed = jnp.concatenate([attn, conv], axis=1).astype(BF16)
    y2 = x_ref[0] + jnp.dot(mixed, wo_ref[...], preferred_element_type=F32)
    ms = jnp.mean(y2 * y2, axis=-1, keepdims=True)
    out_ref[0] = y2 * lax.rsqrt(ms + EPS) * fw_ref[...]


def _overlap_t(nsel, nch):
    c = np.arange(nch)[None, :] * CMP_STRIDE
    n = np.arange(nsel)[:, None] * SEL_BLOCK
    return ((c < n + SEL_BLOCK) & (c + CMP_BLOCK > n)).astype(np.float32)


def _cparams(sem):
    return pltpu.CompilerParams(dimension_semantics=sem, vmem_limit_bytes=VMEM_LIMIT)


def kernel(x, norm_w, w_in, w_ck1, w_ck2, pe_k, w_cv1, w_cv2, pe_v, conv_w, w_out, rel_bias, final_norm_w):
    b, s, _ = x.shape
    assert norm_w.shape[0] == 1 and s % ROW_TILE == 0 and WINDOW == 2 * Q_TILE and ROW_TILE == Q_TILE
    tm = ROW_TILE
    nt = s // tm
    nch = s // CMP_STRIDE
    nsel = s // SEL_BLOCK
    k_top = min(N_SEL, nsel)
    kd = -(-(HEAD_DIM + nsel) // LANES) * LANES
    G, R = N_KV_HEADS, GQA_REP

    w = w_in[0].astype(BF16)
    c0 = D_ATTN + 6 * D_KV
    w_all = jnp.concatenate([w[:, :c0], w[:, c0 + N_GATE:], w[:, c0:c0 + N_GATE],
                             jnp.zeros((D_MODEL, LANES - N_GATE), BF16)], axis=1)
    eye = jnp.eye(G, dtype=F32)

    def expand1(w1):
        w1 = w1.astype(BF16).reshape(2, CMP_STRIDE, HEAD_DIM, CMP_HIDDEN)
        z = jnp.zeros_like(w1)
        rows = [jnp.concatenate([w1 if a == g else z for a in range(G)], axis=-1) for g in range(G)]
        return jnp.stack(rows, axis=2).reshape(2, CMP_STRIDE * D_KV, G * CMP_HIDDEN)

    def expand_pe(pe):
        pe = pe.reshape(2, CMP_STRIDE, 1, HEAD_DIM)
        return jnp.broadcast_to(pe, (2, CMP_STRIDE, G, HEAD_DIM)).reshape(2, 1, CMP_STRIDE * D_KV)

    def expand2(w2):
        return jnp.einsum('hd,ab->ahbd', w2, eye).reshape(G * CMP_HIDDEN, D_KV)

    w1e = jnp.stack([expand1(w_ck1[0]), expand1(w_cv1[0])]).astype(BF16)
    pee = jnp.stack([expand_pe(pe_k[0]), expand_pe(pe_v[0])])
    w2e = jnp.stack([expand2(w_ck2[0]), expand2(w_cv2[0])]).astype(BF16)
    rel_bias = rel_bias.astype(F32)
    smem = pl.BlockSpec(memory_space=pltpu.SMEM)

    n_rest = D_ATTN + 4 * D_CONV
    cw = CMP_STRIDE * D_KV
    qT, kvc, ksel, vselT, kwin, vwinT, gsT, rest = pl.pallas_call(
        partial(_proj_kernel, tm=tm, nsel=nsel, kd=kd),
        grid=(b, nt),
        in_specs=[pl.BlockSpec((1, tm, D_MODEL), lambda i, j: (i, j, 0)),
                  pl.BlockSpec((1, D_MODEL), lambda i, j: (0, 0)),
                  pl.BlockSpec((D_MODEL, _C_END), lambda i, j: (0, 0))],
        out_specs=[pl.BlockSpec((1, D_ATTN, tm), lambda i, j: (i, 0, j)),
                   pl.BlockSpec((2, 1, tm // CMP_STRIDE, cw), lambda i, j: (0, i, j, 0)),
                   pl.BlockSpec((1, G, tm, kd), lambda i, j: (i, 0, j, 0)),
                   pl.BlockSpec((1, G, 1, V_ROWS, tm), lambda i, j: (i, 0, j, 0, 0)),
                   pl.BlockSpec((1, G, tm, LANES), lambda i, j: (i, 0, j, 0)),
                   pl.BlockSpec((1, G, 1, V_ROWS, tm), lambda i, j: (i, 0, j, 0, 0)),
                   pl.BlockSpec((1, N_GATE, tm), lambda i, j: (i, 0, j)),
                   pl.BlockSpec((1, tm, n_rest), lambda i, j: (i, j, 0))],
        out_shape=[jax.ShapeDtypeStruct((b, D_ATTN, s), BF16),
                   jax.ShapeDtypeStruct((2, b, nch, cw), F32),
                   jax.ShapeDtypeStruct((b, G, s, kd), BF16),
                   jax.ShapeDtypeStruct((b, G, nt, V_ROWS, tm), BF16),
                   jax.ShapeDtypeStruct((b, G, s, LANES), BF16),
                   jax.ShapeDtypeStruct((b, G, nt, V_ROWS, tm), BF16),
                   jax.ShapeDtypeStruct((b, N_GATE, s), F32),
                   jax.ShapeDtypeStruct((b, s, n_rest), BF16)],
        scratch_shapes=[pltpu.VMEM((2, tm, D_KV), F32)],
        compiler_params=_cparams(("parallel", "parallel")),
        name="in_proj",
    )(x, norm_w, w_all)
    gs = gsT.reshape(b, 3, G, R, s)

    lhs_rows = V_ROWS + nsel
    lhs_tail = jnp.asarray(np.concatenate([np.ones((V_ROWS - HEAD_DIM, nch), np.float32),
                                           _overlap_t(nsel, nch)], axis=0), BF16)
    kc_aug, c_lhs = pl.pallas_call(
        partial(_compress_kernel, nch=nch),
        grid=(2, b),
        in_specs=[pl.BlockSpec((1, 1, nch, cw), lambda xk, i: (xk, i, 0, 0)),
                  pl.BlockSpec((1, 2, 1, cw), lambda xk, i: (xk, 0, 0, 0)),
                  pl.BlockSpec((1, 2, cw, G * CMP_HIDDEN), lambda xk, i: (xk, 0, 0, 0)),
                  pl.BlockSpec((1, G * CMP_HIDDEN, D_KV), lambda xk, i: (xk, 0, 0)),
                  pl.BlockSpec((lhs_rows - HEAD_DIM, nch), lambda xk, i: (0, 0))],
        out_specs=[pl.BlockSpec((1, 1, G, nch, LANES), lambda xk, i: (xk, i, 0, 0, 0)),
                   pl.BlockSpec((1, 1, G, lhs_rows, nch), lambda xk, i: (xk, i, 0, 0, 0))],
        out_shape=[jax.ShapeDtypeStruct((2, b, G, nch, LANES), BF16),
                   jax.ShapeDtypeStruct((2, b, G, lhs_rows, nch), BF16)],
        compiler_params=_cparams(("parallel", "parallel")),
        name="compress",
    )(kvc, pee, w1e, w2e, lhs_tail)

    tq3 = Q_TILE_CMP
    cpq = tq3 // CMP_STRIDE
    assert nch // 8 <= LANES - HEAD_DIM and k_top > 3 and (nch // 2) % LANES == 0
    ocT, selb = pl.pallas_call(
        partial(_cmp_kernel, nch=nch, nsel=nsel, tq=tq3, k_top=k_top),
        grid=(b, G, s // tq3),
        in_specs=[smem,
                  pl.BlockSpec((1, R * HEAD_DIM, tq3), lambda i, g, t: (i, g, t)),
                  pl.BlockSpec((1, 1, 1, nch, LANES), lambda i, g, t: (0, i, g, 0, 0)),
                  pl.BlockSpec((1, 1, 1, lhs_rows, nch), lambda i, g, t: (1, i, g, 0, 0)),
                  pl.BlockSpec((1, 1, 1, R, tq3), lambda i, g, t: (i, 0, g, 0, t))],
        out_specs=[pl.BlockSpec((1, R * HEAD_DIM, tq3), lambda i, g, t: (i, g, t)),
                   pl.BlockSpec((1, 1, nsel, tq3), lambda i, g, t: (i, g, 0, t))],
        out_shape=[jax.ShapeDtypeStruct((b, D_ATTN, s), BF16),
                   jax.ShapeDtypeStruct((b, G, nsel, s), BF16)],
        scratch_shapes=[pltpu.VMEM((LANES, R * tq3), BF16),
                        pltpu.VMEM((cpq + 16, R * tq3), F32),
                        pltpu.VMEM((lhs_rows, R * tq3), F32)]
                       + [pltpu.VMEM((nch, tq3), F32)] * R,
        compiler_params=_cparams(("parallel", "parallel", "arbitrary")),
        name="cmp_select",
    )(rel_bias, qT, kc_aug, c_lhs, gs)

    tq = Q_TILE
    nq = R * tq

    def flash(selected, k_arr, vT_arr, kdim, n_tbl, window, branch, name):
        in_specs = [smem, pl.BlockSpec((1, R * HEAD_DIM, tq), lambda i, g, t: (i, g, t))]
        args = [rel_bias, qT]
        if selected:
            in_specs.append(pl.BlockSpec((1, 1, nsel, tq), lambda i, g, t: (i, g, 0, t)))
            args.append(selb)
        in_specs += [pl.BlockSpec((1, 1, s, kdim), lambda i, g, t: (i, g, 0, 0)),
                     pl.BlockSpec((1, 1, nt, V_ROWS, tm), lambda i, g, t: (i, g, 0, 0, 0)),
                     pl.BlockSpec((1, 1, 1, R, tq), lambda i, g, t: (i, branch, g, 0, t))]
        args += [k_arr, vT_arr, gs]
        return pl.pallas_call(
            partial(_flash_kernel, tq=tq, nsel=nsel, kd=kdim, n_tbl_tiles=n_tbl, window=window, selected=selected),
            grid=(b, G, s // tq),
            in_specs=in_specs,
            out_specs=pl.BlockSpec((1, R * HEAD_DIM, tq), lambda i, g, t: (i, g, t)),
            out_shape=jax.ShapeDtypeStruct((b, D_ATTN, s), BF16),
            scratch_shapes=[pltpu.VMEM((n_tbl * tq, nq), F32),
                            pltpu.VMEM((kdim, nq), BF16),
                            pltpu.VMEM((1, nq), F32),
                            pltpu.VMEM((V_ROWS, nq), F32),
                            pltpu.VMEM((1, nq), F32)]
                           + [pltpu.VMEM((tq, nq), F32)] * N_SLOTS
                           + [pltpu.VMEM((1, nq), F32)] * N_SLOTS
                           + [pltpu.VMEM((1, nq), F32)] * N_SLOTS
                           + [pltpu.VMEM((tq, nq), BF16)] * N_SLOTS,
            compiler_params=_cparams(("parallel", "parallel", "arbitrary")),
            name=name,
        )(*args)

    osT = flash(True, ksel, vselT, kd, 2, None, 1, "sel_attn")
    owT = flash(False, kwin, vwinT, LANES, 3, WINDOW, 2, "win_attn")

    hb = tm // HALO_ROWS
    col = lambda c: pl.BlockSpec((1, tm, D_CONV), lambda i, j, c=c: (i, j, c))
    halo = lambda c: pl.BlockSpec((1, HALO_ROWS, D_CONV), lambda i, j, c=c: (i, jnp.maximum(j * hb - 1, 0), c))
    oT = pl.BlockSpec((1, D_ATTN, tm), lambda i, j: (i, 0, j))
    out = pl.pallas_call(
        partial(_out_kernel, tm=tm),
        grid=(b, nt),
        in_specs=[oT, oT, oT, col(0), col(1), col(2), col(3), col(4), halo(1), halo(3),
                  pl.BlockSpec((1, tm, D_MODEL), lambda i, j: (i, j, 0)),
                  pl.BlockSpec((3, D_CONV), lambda i, j: (0, 0)),
                  pl.BlockSpec((D_MODEL, D_MODEL), lambda i, j: (0, 0)),
                  pl.BlockSpec((1, D_MODEL), lambda i, j: (0, 0))],
        out_specs=pl.BlockSpec((1, tm, D_MODEL), lambda i, j: (i, j, 0)),
        out_shape=jax.ShapeDtypeStruct((b, s, D_MODEL), F32),
        compiler_params=_cparams(("parallel", "parallel")),
        name="out_proj",
    )(ocT, osT, owT, rest, rest, rest, rest, rest, rest, rest, x, conv_w[0], w_out[0].astype(BF16),
      final_norm_w.reshape(1, D_MODEL))
    return out
```

```python
import math
from functools import partial

import numpy as np
import jax
import jax.numpy as jnp
from jax import lax
from jax.experimental import pallas as pl
from jax.experimental.pallas import tpu as pltpu

D_MODEL = 1024
N_HEADS = 8
HEAD_DIM = 64
N_KV_HEADS = 2
GQA_REP = N_HEADS // N_KV_HEADS
D_ATTN = N_HEADS * HEAD_DIM
D_KV = N_KV_HEADS * HEAD_DIM
D_CONV = D_MODEL - D_ATTN
CMP_BLOCK = 32
CMP_STRIDE = 16
CMP_HIDDEN = 2 * HEAD_DIM
SEL_BLOCK = 64
N_SEL = 16
WINDOW = 512
N_BUCKETS = 32
MAX_DISTANCE = 128
EPS = 1e-6
NEG = -1e30
FORCE_SCORE = 1e3

LOG2E = 1.4426950408889634
LANES = 128
MXU_COLS = 256
V_ROWS = HEAD_DIM + 16
ROW_TILE = 512
KEY_TILE = 256
Q_TILE_CMP = 256
Q_TILE = 256
HALO_ROWS = 16
MAX_LAG = 64.0
N_SLOTS = 3
VMEM_LIMIT = 48 * 1024 * 1024

F32 = jnp.float32
BF16 = jnp.bfloat16

_C_Q = 0
_C_CMP = D_ATTN
_C_SW = _C_CMP + 2 * D_KV
_C_REST = _C_SW + 4 * D_KV
_C_GATE = _C_REST + D_ATTN + 4 * D_CONV
_C_END = _C_GATE + LANES
N_GATE = 3 * N_HEADS


def _bucket_lower_bounds():
    max_exact = N_BUCKETS // 2
    d = np.arange(2 * MAX_DISTANCE, dtype=np.int32)
    nf = np.maximum(d, 1).astype(np.float32)
    large = max_exact + (np.log(nf / np.float32(max_exact)) / np.float32(math.log(MAX_DISTANCE / max_exact))
                         * np.float32(N_BUCKETS - max_exact)).astype(np.int32)
    bucket = np.where(d < max_exact, d, np.minimum(large, N_BUCKETS - 1))
    assert np.all(np.diff(bucket) >= 0) and np.all(bucket[MAX_DISTANCE:] == N_BUCKETS - 1)
    return [int(np.argmax(bucket >= b)) for b in range(N_BUCKETS)]


_BUCKET_LO = _bucket_lower_bounds()


def _rel_bias_delta(dist, bias_of_bucket):
    last = bias_of_bucket(N_BUCKETS - 1)
    val = jnp.full(dist.shape, (bias_of_bucket(0) - last) * LOG2E, F32)
    for b in range(1, N_BUCKETS - 1):
        val = jnp.where(dist >= _BUCKET_LO[b], (bias_of_bucket(b) - last) * LOG2E, val)
    val = jnp.where(dist >= _BUCKET_LO[N_BUCKETS - 1], 0.0, val)
    return jnp.where(dist < 0, NEG, val)


def _silu(z):
    return z * jax.nn.sigmoid(z)


def _proj_kernel(x_ref, nw_ref, w_ref, qT_ref, kvc_ref, ksel_ref, vselT_ref, kwin_ref, vwinT_ref,
                 gsT_ref, rest_ref, kv_scr, *, tm, nsel, kd):
    st = pl.program_id(1)
    x = x_ref[0]
    ms = jnp.mean(x * x, axis=-1, keepdims=True)
    h = (x * lax.rsqrt(ms + EPS) * nw_ref[...]).astype(BF16)

    def mm(lo, hi):
        return jnp.dot(h, w_ref[:, lo:hi], preferred_element_type=F32)

    q = mm(_C_Q, _C_CMP) * (HEAD_DIM ** -0.5 * LOG2E)
    qT_ref[0] = q.T.astype(BF16)

    kvc = mm(_C_CMP, _C_SW)
    for xk in range(2):
        kv_scr[xk] = kvc[:, xk * D_KV:(xk + 1) * D_KV]
        for j in range(CMP_STRIDE):
            kvc_ref[xk, 0, :, j * D_KV:(j + 1) * D_KV] = kv_scr[xk, pl.ds(j, tm // CMP_STRIDE, stride=CMP_STRIDE), :]

    a = mm(_C_SW, _C_REST)
    pos = st * tm + lax.broadcasted_iota(jnp.int32, (tm, nsel), 0)
    col = lax.broadcasted_iota(jnp.int32, (tm, nsel), 1)
    onehot = jnp.where((pos // SEL_BLOCK) == col, 1.0, 0.0).astype(BF16)
    ones = jnp.ones((V_ROWS - HEAD_DIM, tm), BF16)
    zeros_k = jnp.zeros((tm, LANES - HEAD_DIM), BF16)
    vsT = a[:, D_KV:2 * D_KV].T
    vwT = a[:, 3 * D_KV:4 * D_KV].T
    for g in range(N_KV_HEADS):
        lo, hi = g * HEAD_DIM, (g + 1) * HEAD_DIM
        parts = [a[:, lo:hi].astype(BF16), onehot]
        if kd > HEAD_DIM + nsel:
            parts.append(jnp.zeros((tm, kd - HEAD_DIM - nsel), BF16))
        ksel_ref[0, g] = jnp.concatenate(parts, axis=1)
        kwin_ref[0, g] = jnp.concatenate([a[:, 2 * D_KV + lo:2 * D_KV + hi].astype(BF16), zeros_k], axis=1)
        vs = jnp.concatenate([vsT[lo:hi].astype(BF16), ones], axis=0)
        vw = jnp.concatenate([vwT[lo:hi].astype(BF16), ones], axis=0)
        for kt in range(tm // KEY_TILE):
            vselT_ref[0, g, kt] = vs[:, kt * KEY_TILE:(kt + 1) * KEY_TILE]
            vwinT_ref[0, g, kt] = vw[:, kt * KEY_TILE:(kt + 1) * KEY_TILE]

    rest_ref[0] = mm(_C_REST, _C_GATE).astype(BF16)
    gates = jax.nn.sigmoid(mm(_C_GATE, _C_END))
    gsT_ref[0] = gates.T[:N_GATE]


def _compress_kernel(c_ref, pe_ref, w1_ref, w2_ref, lhs_tail_ref, aug_ref, t_ref, *, nch):
    c = c_ref[0, 0]
    a0 = jnp.dot((c + pe_ref[0, 0]).astype(BF16), w1_ref[0, 0], preferred_element_type=F32)
    a1 = jnp.dot((c + pe_ref[0, 1]).astype(BF16), w1_ref[0, 1], preferred_element_type=F32)
    hid = a0 + pltpu.roll(a1, nch - 1, 0)
    out = jnp.dot(_silu(hid).astype(BF16), w2_ref[0], preferred_element_type=F32)
    grp = lax.broadcasted_iota(jnp.int32, (nch, LANES - HEAD_DIM), 0) // 8
    col = lax.broadcasted_iota(jnp.int32, (nch, LANES - HEAD_DIM), 1)
    onehot = jnp.where(grp == col, 1.0, 0.0).astype(BF16)
    out_t = out.T.astype(BF16)
    for g in range(N_KV_HEADS):
        lo, hi = g * HEAD_DIM, (g + 1) * HEAD_DIM
        aug_ref[0, 0, g] = jnp.concatenate([out[:, lo:hi].astype(BF16), onehot], axis=1)
        t_ref[0, 0, g] = jnp.concatenate([out_t[lo:hi], lhs_tail_ref[...]], axis=0)


def _cmp_kernel(rb_ref, qT_ref, kc_ref, lhs_ref, gs_ref, ocT_ref, selb_ref,
                qa_ref, band_ref, acc_ref, s0_ref, s1_ref, s2_ref, s3_ref, *, nch, nsel, tq, k_top):
    g = pl.program_id(1)
    qt = pl.program_id(2)
    s_refs = (s0_ref, s1_ref, s2_ref, s3_ref)
    cpq = tq // CMP_STRIDE
    win = cpq + 16
    lim = cpq * (qt + 1)
    nq = GQA_REP * tq

    @pl.when(qt <= 1)
    def _():
        row = lax.broadcasted_iota(jnp.int32, (win, tq), 0) + jnp.maximum(cpq * qt - 16, 0)
        lane = lax.broadcasted_iota(jnp.int32, (win, tq), 1)
        dist = (qt * tq + lane) - (row * CMP_STRIDE + CMP_BLOCK - 1)
        for r in range(GQA_REP):
            band_ref[:, r * tq:(r + 1) * tq] = _rel_bias_delta(dist, lambda b, r=r: rb_ref[b, g * GQA_REP + r])

    q = qT_ref[0]
    qa_ref[0:HEAD_DIM, :] = jnp.concatenate([q[r * HEAD_DIM:(r + 1) * HEAD_DIM] for r in range(GQA_REP)], axis=1)
    grp = lax.broadcasted_iota(jnp.int32, (LANES - HEAD_DIM, nq), 0)
    qa_ref[HEAD_DIM:LANES, :] = jnp.where(grp * 8 >= lim, NEG, 0.0).astype(BF16)

    def run(rows):
        start = pl.multiple_of(jnp.clip(cpq * qt - 16, 0, rows - win), 8)
        k = kc_ref[0, 0, 0, 0:rows, :]
        lhs = lhs_ref[0, 0, 0, :, 0:rows]
        cols = [slice(r * tq, (r + 1) * tq) for r in range(GQA_REP)]

        def qk(r):
            s_refs[r][0:rows, :] = jnp.dot(k, qa_ref[:, cols[r]], preferred_element_type=F32)

        qk(0)
        for r in range(GQA_REP):
            if r + 1 < GQA_REP:
                qk(r + 1)
            sr = s_refs[r]
            sr[pl.ds(start, win), :] = sr[pl.ds(start, win), :] + band_ref[:, cols[r]]
            s = sr[0:rows, :]
            p = jnp.exp2(s - jnp.max(s, axis=0, keepdims=True)).astype(BF16)
            acc_ref[:, cols[r]] = jnp.dot(lhs, p, preferred_element_type=F32)

    half = nch // 2

    @pl.when(lim <= half)
    def _():
        run(half)

    @pl.when(lim > half)
    def _():
        run(nch)

    l = acc_ref[HEAD_DIM:HEAD_DIM + 1, :]
    t4 = qt * tq + (lax.broadcasted_iota(jnp.int32, l.shape, 1) & (tq - 1))
    inv = jnp.where(t4 >= CMP_BLOCK - 1, 1.0 / jnp.maximum(l, 1e-30), 0.0)
    imp = None
    for r in range(GQA_REP):
        cs = slice(r * tq, (r + 1) * tq)
        ocT_ref[0, r * HEAD_DIM:(r + 1) * HEAD_DIM, :] = (acc_ref[0:HEAD_DIM, cs] * inv[:, cs]
                                                          * gs_ref[0, 0, 0, r:r + 1, :]).astype(BF16)
        part = acc_ref[V_ROWS:V_ROWS + nsel, cs] * inv[:, cs]
        imp = part if imp is None else imp + part

    n = lax.broadcasted_iota(jnp.int32, (nsel, tq), 0)
    blk = (qt * tq + lax.broadcasted_iota(jnp.int32, (nsel, tq), 1)) // SEL_BLOCK
    forced = (n == 0) | (n == blk) | (n == blk - 1)
    candidate = (n <= blk) & jnp.logical_not(forced)
    score = jnp.where(forced, -2.0, jnp.where(n <= blk, imp, -1.0))

    def pick_all(_, score):
        return jnp.where(score == jnp.max(score, axis=0, keepdims=True), -2.0, score)

    taken = lax.fori_loop(0, k_top - 3, pick_all, score) == -2.0
    selb_ref[0, 0] = jnp.where(taken, 0.0, NEG).astype(BF16)
    n_taken = jnp.sum(jnp.where(taken & candidate, 1.0, 0.0), axis=0, keepdims=True)

    @pl.when(jnp.max(n_taken) > k_top - 3)
    def _():
        def pick_one(_, score):
            mx = jnp.max(score, axis=0, keepdims=True)
            idx = jnp.min(jnp.where(score == mx, n, nsel), axis=0, keepdims=True)
            return jnp.where(n == idx, -2.0, score)

        taken = lax.fori_loop(0, k_top - 3, pick_one, score) == -2.0
        selb_ref[0, 0] = jnp.where(taken, 0.0, NEG).astype(BF16)


def _flash_kernel(*refs, tq, nsel, kd, n_tbl_tiles, window, selected):
    if selected:
        rb_ref, qT_ref, selb_ref, k_ref, vT_ref, gs_ref, o_ref, tbl_ref, qa_ref, m_ref, acc_ref, lag_ref, *slot_refs = refs
    else:
        rb_ref, qT_ref, k_ref, vT_ref, gs_ref, o_ref, tbl_ref, qa_ref, m_ref, acc_ref, lag_ref, *slot_refs = refs
    s_ref, mt_ref, al_ref, p_ref = (slot_refs[i * N_SLOTS:(i + 1) * N_SLOTS] for i in range(4))
    g = pl.program_id(1)
    qt = pl.program_id(2)
    nq = GQA_REP * tq

    @pl.when(qt == 0)
    def _():
        for j in range(n_tbl_tiles):
            key = lax.broadcasted_iota(jnp.int32, (tq, tq), 0) + j * tq
            qry = lax.broadcasted_iota(jnp.int32, (tq, tq), 1) + (n_tbl_tiles - 1) * tq
            dist = qry - key
            for r in range(GQA_REP):
                val = _rel_bias_delta(dist, lambda b, r=r: rb_ref[b, g * GQA_REP + r])
                if window is not None:
                    val = jnp.where(dist >= window, NEG, val)
                tbl_ref[j * tq:(j + 1) * tq, r * tq:(r + 1) * tq] = val

    q = qT_ref[0]
    qa_ref[0:HEAD_DIM, :] = jnp.concatenate([q[r * HEAD_DIM:(r + 1) * HEAD_DIM] for r in range(GQA_REP)], axis=1)
    used = HEAD_DIM
    if selected:
        qa_ref[HEAD_DIM:HEAD_DIM + nsel, :] = jnp.concatenate([selb_ref[0, 0]] * GQA_REP, axis=1)
        used += nsel
    if kd > used:
        qa_ref[used:kd, :] = jnp.zeros((kd - used, nq), BF16)
    m_ref[...] = jnp.full((1, nq), NEG, F32)
    acc_ref[...] = jnp.zeros((V_ROWS, nq), F32)

    def stage_qk(t, slot, tbl_tile):
        k = k_ref[0, 0, pl.ds(pl.multiple_of(t * tq, tq), tq), :]
        s = jnp.dot(k, qa_ref[...], preferred_element_type=F32)
        if tbl_tile is not None:
            s = s + tbl_ref[tbl_tile * tq:(tbl_tile + 1) * tq, :]
        s_ref[slot][...] = s
        mt_ref[slot][...] = jnp.max(s, axis=0, keepdims=True)

    def stage_exp(slot):
        m_old = m_ref[...]
        m_new = jnp.maximum(m_old, mt_ref[slot][...])
        m_ref[...] = m_new
        al_ref[slot][...] = jnp.exp2(m_old - m_new)
        p_ref[slot][...] = jnp.exp2(s_ref[slot][...] - m_new).astype(BF16)

    def stage_pv(t, slot):
        acc_ref[...] = al_ref[slot][...] * acc_ref[...] + jnp.dot(vT_ref[0, 0, t], p_ref[slot][...],
                                                                  preferred_element_type=F32)

    def serial(t, tbl_tile):
        stage_qk(t, 0, tbl_tile)
        stage_exp(0)
        stage_pv(t, 0)

    last = n_tbl_tiles - 1

    def step(t, slot, tbl_tile):
        stage_pv(t - 2, (slot + 1) % N_SLOTS)
        stage_exp((slot + 2) % N_SLOTS)
        stage_qk(t, slot, tbl_tile)

    def stage_lagged(t, slot, tbl_tile, first=False):
        k = k_ref[0, 0, pl.ds(pl.multiple_of(t * tq, tq), tq), :]
        s = jnp.dot(k, qa_ref[...], preferred_element_type=F32)
        if tbl_tile is not None:
            s = s + tbl_ref[tbl_tile * tq:(tbl_tile + 1) * tq, :]
        m = s[0:1, :] if first else m_ref[...]
        mt = jnp.max(s, axis=0, keepdims=True)
        p_ref[slot][...] = jnp.exp2((s - m).astype(BF16))
        m_new = jnp.maximum(m, mt)
        m_ref[...] = m_new
        al_ref[1 - slot][...] = jnp.exp2(m - m_new)
        lag_ref[...] = (mt - m) if first else jnp.maximum(lag_ref[...], mt - m)

    def lagged_step(t, slot, tbl_tile):
        stage_pv(t - 1, 1 - slot)
        stage_lagged(t, slot, tbl_tile)

    if selected:
        n = qt + 1

        @pl.when((n >= 4) & ((n & 1) == 0))
        def _():
            al_ref[1][...] = jnp.ones((1, nq), F32)
            stage_lagged(0, 1, None, first=True)
            lagged_step(1, 0, None)

        @pl.when((n >= 4) & ((n & 1) == 1))
        def _():
            al_ref[0][...] = jnp.ones((1, nq), F32)
            stage_lagged(0, 0, None, first=True)

        @pl.when(n >= 4)
        def _():
            t0 = 2 - (n & 1)

            def body(j, carry):
                t = t0 + 2 * j
                lagged_step(t, 1, None)
                lagged_step(t + 1, 0, None)
                return carry
            lax.fori_loop(0, (n - 2 - t0) >> 1, body, 0)

            lagged_step(n - 2, 1, 0)
            lagged_step(n - 1, 0, 1)
            stage_pv(n - 1, 0)

            @pl.when(jnp.max(lag_ref[...]) > MAX_LAG)
            def _():
                m_ref[...] = jnp.full((1, nq), NEG, F32)
                acc_ref[...] = jnp.zeros((V_ROWS, nq), F32)

                def redo(t, carry):
                    serial(t, None)
                    return carry
                lax.fori_loop(0, n - 2, redo, 0)
                serial(n - 2, 0)
                serial(n - 1, 1)

        @pl.when(n == 3)
        def _():
            serial(0, None)

        @pl.when((n >= 2) & (n < 4))
        def _():
            serial(qt - 1, 0)

        @pl.when(n < 4)
        def _():
            serial(qt, 1)
    else:
        @pl.when(qt >= last)
        def _():
            stage_qk(qt - 2, 0, 0)
            stage_exp(0)
            stage_qk(qt - 1, 1, 1)
            step(qt, 2, 2)
            stage_pv(qt - 1, 1)
            stage_exp(2)
            stage_pv(qt, 2)

        @pl.when(qt == 1)
        def _():
            serial(0, 1)

        @pl.when(qt < last)
        def _():
            serial(qt, 2)

    acc = acc_ref[...]
    o = acc[0:HEAD_DIM] / jnp.maximum(acc[HEAD_DIM:HEAD_DIM + 1], 1e-30)
    for r in range(GQA_REP):
        o_ref[0, r * HEAD_DIM:(r + 1) * HEAD_DIM, :] = (o[:, r * tq:(r + 1) * tq]
                                                        * gs_ref[0, 0, 0, r:r + 1, :]).astype(BF16)


def _out_kernel(oc_ref, os_ref, ow_ref, za_ref, ch_ref, cb_ref, cc_ref, zc_ref, hh_ref, hc_ref,
                x_ref, cw_ref, wo_ref, fw_ref, out_ref, *, tm):
    st = pl.program_id(1)
    f32 = lambda ref: ref[0].astype(F32)
    attn = (f32(oc_ref) + f32(os_ref) + f32(ow_ref)).T * _silu(f32(za_ref))

    u = f32(cc_ref) * f32(ch_ref)
    halo = jnp.where(st == 0, 0.0, f32(hc_ref) * f32(hh_ref))
    row = lax.broadcasted_iota(jnp.int32, u.shape, 0)
    h1, h2 = halo[HALO_ROWS - 1:HALO_ROWS], halo[HALO_ROWS - 2:HALO_ROWS - 1]
    u1 = jnp.where(row == 0, h1, pltpu.roll(u, 1, 0))
    u2 = jnp.where(row == 0, h2, jnp.where(row == 1, h1, pltpu.roll(u, 2, 0)))
    y = cw_ref[0:1] * u2 + cw_ref[1:2] * u1 + cw_ref[2:3] * u
    conv = f32(cb_ref) * y * _silu(f32(zc_ref))

    mixed = jnp.concatenate([attn, conv], axis=1).astype(BF16)
    y2 = x_ref[0] + jnp.dot(mixed, wo_ref[...], preferred_element_type=F32)
    ms = jnp.mean(y2 * y2, axis=-1, keepdims=True)
    out_ref[0] = y2 * lax.rsqrt(ms + EPS) * fw_ref[...]


def _overlap_t(nsel, nch):
    c = np.arange(nch)[None, :] * CMP_STRIDE
    n = np.arange(nsel)[:, None] * SEL_BLOCK
    return ((c < n + SEL_BLOCK) & (c + CMP_BLOCK > n)).astype(np.float32)


def _cparams(sem):
    return pltpu.CompilerParams(dimension_semantics=sem, vmem_limit_bytes=VMEM_LIMIT)


def kernel(x, norm_w, w_in, w_ck1, w_ck2, pe_k, w_cv1, w_cv2, pe_v, conv_w, w_out, rel_bias, final_norm_w):
    b, s, _ = x.shape
    assert norm_w.shape[0] == 1 and s % ROW_TILE == 0 and ROW_TILE % KEY_TILE == 0
    assert WINDOW == 2 * Q_TILE and KEY_TILE == Q_TILE
    tm = ROW_TILE
    nt = s // tm
    nkt = s // KEY_TILE
    kpt = tm // KEY_TILE
    nch = s // CMP_STRIDE
    nsel = s // SEL_BLOCK
    k_top = min(N_SEL, nsel)
    kd = -(-(HEAD_DIM + nsel) // LANES) * LANES
    G, R = N_KV_HEADS, GQA_REP

    w = w_in[0].astype(BF16)
    c0 = D_ATTN + 6 * D_KV
    w_all = jnp.concatenate([w[:, :c0], w[:, c0 + N_GATE:], w[:, c0:c0 + N_GATE],
                             jnp.zeros((D_MODEL, LANES - N_GATE), BF16)], axis=1)
    eye = jnp.eye(G, dtype=F32)

    def expand1(w1):
        w1 = w1.astype(BF16).reshape(2, CMP_STRIDE, HEAD_DIM, CMP_HIDDEN)
        z = jnp.zeros_like(w1)
        rows = [jnp.concatenate([w1 if a == g else z for a in range(G)], axis=-1) for g in range(G)]
        return jnp.stack(rows, axis=2).reshape(2, CMP_STRIDE * D_KV, G * CMP_HIDDEN)

    def expand_pe(pe):
        pe = pe.reshape(2, CMP_STRIDE, 1, HEAD_DIM)
        return jnp.broadcast_to(pe, (2, CMP_STRIDE, G, HEAD_DIM)).reshape(2, 1, CMP_STRIDE * D_KV)

    def expand2(w2):
        return jnp.einsum('hd,ab->ahbd', w2, eye).reshape(G * CMP_HIDDEN, D_KV)

    w1e = jnp.stack([expand1(w_ck1[0]), expand1(w_cv1[0])]).astype(BF16)
    pee = jnp.stack([expand_pe(pe_k[0]), expand_pe(pe_v[0])])
    w2e = jnp.stack([expand2(w_ck2[0]), expand2(w_cv2[0])]).astype(BF16)
    rel_bias = rel_bias.astype(F32)
    smem = pl.BlockSpec(memory_space=pltpu.SMEM)

    n_rest = D_ATTN + 4 * D_CONV
    cw = CMP_STRIDE * D_KV
    qT, kvc, ksel, vselT, kwin, vwinT, gsT, rest = pl.pallas_call(
        partial(_proj_kernel, tm=tm, nsel=nsel, kd=kd),
        grid=(b, nt),
        in_specs=[pl.BlockSpec((1, tm, D_MODEL), lambda i, j: (i, j, 0)),
                  pl.BlockSpec((1, D_MODEL), lambda i, j: (0, 0)),
                  pl.BlockSpec((D_MODEL, _C_END), lambda i, j: (0, 0))],
        out_specs=[pl.BlockSpec((1, D_ATTN, tm), lambda i, j: (i, 0, j)),
                   pl.BlockSpec((2, 1, tm // CMP_STRIDE, cw), lambda i, j: (0, i, j, 0)),
                   pl.BlockSpec((1, G, tm, kd), lambda i, j: (i, 0, j, 0)),
                   pl.BlockSpec((1, G, kpt, V_ROWS, KEY_TILE), lambda i, j: (i, 0, j, 0, 0)),
                   pl.BlockSpec((1, G, tm, LANES), lambda i, j: (i, 0, j, 0)),
                   pl.BlockSpec((1, G, kpt, V_ROWS, KEY_TILE), lambda i, j: (i, 0, j, 0, 0)),
                   pl.BlockSpec((1, N_GATE, tm), lambda i, j: (i, 0, j)),
                   pl.BlockSpec((1, tm, n_rest), lambda i, j: (i, j, 0))],
        out_shape=[jax.ShapeDtypeStruct((b, D_ATTN, s), BF16),
                   jax.ShapeDtypeStruct((2, b, nch, cw), F32),
                   jax.ShapeDtypeStruct((b, G, s, kd), BF16),
                   jax.ShapeDtypeStruct((b, G, nkt, V_ROWS, KEY_TILE), BF16),
                   jax.ShapeDtypeStruct((b, G, s, LANES), BF16),
                   jax.ShapeDtypeStruct((b, G, nkt, V_ROWS, KEY_TILE), BF16),
                   jax.ShapeDtypeStruct((b, N_GATE, s), F32),
                   jax.ShapeDtypeStruct((b, s, n_rest), BF16)],
        scratch_shapes=[pltpu.VMEM((2, tm, D_KV), F32)],
        compiler_params=_cparams(("parallel", "parallel")),
        name="in_proj",
    )(x, norm_w, w_all)
    gs = gsT.reshape(b, 3, G, R, s)

    lhs_rows = V_ROWS + nsel
    lhs_tail = jnp.asarray(np.concatenate([np.ones((V_ROWS - HEAD_DIM, nch), np.float32),
                                           _overlap_t(nsel, nch)], axis=0), BF16)
    kc_aug, c_lhs = pl.pallas_call(
        partial(_compress_kernel, nch=nch),
        grid=(2, b),
        in_specs=[pl.BlockSpec((1, 1, nch, cw), lambda xk, i: (xk, i, 0, 0)),
                  pl.BlockSpec((1, 2, 1, cw), lambda xk, i: (xk, 0, 0, 0)),
                  pl.BlockSpec((1, 2, cw, G * CMP_HIDDEN), lambda xk, i: (xk, 0, 0, 0)),
                  pl.BlockSpec((1, G * CMP_HIDDEN, D_KV), lambda xk, i: (xk, 0, 0)),
                  pl.BlockSpec((lhs_rows - HEAD_DIM, nch), lambda xk, i: (0, 0))],
        out_specs=[pl.BlockSpec((1, 1, G, nch, LANES), lambda xk, i: (xk, i, 0, 0, 0)),
                   pl.BlockSpec((1, 1, G, lhs_rows, nch), lambda xk, i: (xk, i, 0, 0, 0))],
        out_shape=[jax.ShapeDtypeStruct((2, b, G, nch, LANES), BF16),
                   jax.ShapeDtypeStruct((2, b, G, lhs_rows, nch), BF16)],
        compiler_params=_cparams(("parallel", "parallel")),
        name="compress",
    )(kvc, pee, w1e, w2e, lhs_tail)

    tq3 = Q_TILE_CMP
    cpq = tq3 // CMP_STRIDE
    assert nch // 8 <= LANES - HEAD_DIM and k_top > 3 and (nch // 2) % LANES == 0
    ocT, selb = pl.pallas_call(
        partial(_cmp_kernel, nch=nch, nsel=nsel, tq=tq3, k_top=k_top),
        grid=(b, G, s // tq3),
        in_specs=[smem,
                  pl.BlockSpec((1, R * HEAD_DIM, tq3), lambda i, g, t: (i, g, t)),
                  pl.BlockSpec((1, 1, 1, nch, LANES), lambda i, g, t: (0, i, g, 0, 0)),
                  pl.BlockSpec((1, 1, 1, lhs_rows, nch), lambda i, g, t: (1, i, g, 0, 0)),
                  pl.BlockSpec((1, 1, 1, R, tq3), lambda i, g, t: (i, 0, g, 0, t))],
        out_specs=[pl.BlockSpec((1, R * HEAD_DIM, tq3), lambda i, g, t: (i, g, t)),
                   pl.BlockSpec((1, 1, nsel, tq3), lambda i, g, t: (i, g, 0, t))],
        out_shape=[jax.ShapeDtypeStruct((b, D_ATTN, s), BF16),
                   jax.ShapeDtypeStruct((b, G, nsel, s), BF16)],
        scratch_shapes=[pltpu.VMEM((LANES, R * tq3), BF16),
                        pltpu.VMEM((cpq + 16, R * tq3), F32),
                        pltpu.VMEM((lhs_rows, R * tq3), F32)]
                       + [pltpu.VMEM((nch, tq3), F32)] * R,
        compiler_params=_cparams(("parallel", "parallel", "arbitrary")),
        name="cmp_select",
    )(rel_bias, qT, kc_aug, c_lhs, gs)

    tq = Q_TILE
    nq = R * tq

    def flash(selected, k_arr, vT_arr, kdim, n_tbl, window, branch, name):
        in_specs = [smem, pl.BlockSpec((1, R * HEAD_DIM, tq), lambda i, g, t: (i, g, t))]
        args = [rel_bias, qT]
        if selected:
            in_specs.append(pl.BlockSpec((1, 1, nsel, tq), lambda i, g, t: (i, g, 0, t)))
            args.append(selb)
        in_specs += [pl.BlockSpec((1, 1, s, kdim), lambda i, g, t: (i, g, 0, 0)),
                     pl.BlockSpec((1, 1, nkt, V_ROWS, KEY_TILE), lambda i, g, t: (i, g, 0, 0, 0)),
                     pl.BlockSpec((1, 1, 1, R, tq), lambda i, g, t: (i, branch, g, 0, t))]
        args += [k_arr, vT_arr, gs]
        return pl.pallas_call(
            partial(_flash_kernel, tq=tq, nsel=nsel, kd=kdim, n_tbl_tiles=n_tbl, window=window, selected=selected),
            grid=(b, G, s // tq),
            in_specs=in_specs,
            out_specs=pl.BlockSpec((1, R * HEAD_DIM, tq), lambda i, g, t: (i, g, t)),
            out_shape=jax.ShapeDtypeStruct((b, D_ATTN, s), BF16),
            scratch_shapes=[pltpu.VMEM((n_tbl * tq, nq), F32),
                            pltpu.VMEM((kdim, nq), BF16),
                            pltpu.VMEM((1, nq), F32),
                            pltpu.VMEM((V_ROWS, nq), F32),
                            pltpu.VMEM((1, nq), F32)]
                           + [pltpu.VMEM((tq, nq), F32)] * N_SLOTS
                           + [pltpu.VMEM((1, nq), F32)] * N_SLOTS
                           + [pltpu.VMEM((1, nq), F32)] * N_SLOTS
                           + [pltpu.VMEM((tq, nq), BF16)] * N_SLOTS,
            compiler_params=_cparams(("parallel", "parallel", "arbitrary")),
            name=name,
        )(*args)

    osT = flash(True, ksel, vselT, kd, 2, None, 1, "sel_attn")
    owT = flash(False, kwin, vwinT, LANES, 3, WINDOW, 2, "win_attn")

    hb = tm // HALO_ROWS
    col = lambda c: pl.BlockSpec((1, tm, D_CONV), lambda i, j, c=c: (i, j, c))
    halo = lambda c: pl.BlockSpec((1, HALO_ROWS, D_CONV), lambda i, j, c=c: (i, jnp.maximum(j * hb - 1, 0), c))
    oT = pl.BlockSpec((1, D_ATTN, tm), lambda i, j: (i, 0, j))
    out = pl.pallas_call(
        partial(_out_kernel, tm=tm),
        grid=(b, nt),
        in_specs=[oT, oT, oT, col(0), col(1), col(2), col(3), col(4), halo(1), halo(3),
                  pl.BlockSpec((1, tm, D_MODEL), lambda i, j: (i, j, 0)),
                  pl.BlockSpec((3, D_CONV), lambda i, j: (0, 0)),
                  pl.BlockSpec((D_MODEL, D_MODEL), lambda i, j: (0, 0)),
                  pl.BlockSpec((1, D_MODEL), lambda i, j: (0, 0))],
        out_specs=pl.BlockSpec((1, tm, D_MODEL), lambda i, j: (i, j, 0)),
        out_shape=jax.ShapeDtypeStruct((b, s, D_MODEL), F32),
        compiler_params=_cparams(("parallel", "parallel")),
        name="out_proj",
    )(ocT, osT, owT, rest, rest, rest, rest, rest, rest, rest, x, conv_w[0], w_out[0].astype(BF16),
      final_norm_w.reshape(1, D_MODEL))
    return out
```

```python
import math
from functools import partial

import numpy as np
import jax
import jax.numpy as jnp
from jax import lax
from jax.experimental import pallas as pl
from jax.experimental.pallas import tpu as pltpu

D_MODEL = 1024
N_HEADS = 8
HEAD_DIM = 64
N_KV_HEADS = 2
GQA_REP = N_HEADS // N_KV_HEADS
D_ATTN = N_HEADS * HEAD_DIM
D_KV = N_KV_HEADS * HEAD_DIM
D_CONV = D_MODEL - D_ATTN
CMP_BLOCK = 32
CMP_STRIDE = 16
CMP_HIDDEN = 2 * HEAD_DIM
SEL_BLOCK = 64
N_SEL = 16
WINDOW = 512
N_BUCKETS = 32
MAX_DISTANCE = 128
EPS = 1e-6
NEG = -1e30
FORCE_SCORE = 1e3

LOG2E = 1.4426950408889634
LANES = 128
MXU_COLS = 256
V_ROWS = HEAD_DIM + 16
ROW_TILE = 512
KEY_TILE = 256
Q_TILE_CMP = 256
Q_TILE = 256
HALO_ROWS = 16
MAX_LAG = 64.0
N_SLOTS = 2
VMEM_LIMIT = 48 * 1024 * 1024

F32 = jnp.float32
BF16 = jnp.bfloat16

_C_Q = 0
_C_CMP = D_ATTN
_C_SW = _C_CMP + 2 * D_KV
_C_REST = _C_SW + 4 * D_KV
_C_GATE = _C_REST + D_ATTN + 4 * D_CONV
_C_END = _C_GATE + LANES
N_GATE = 3 * N_HEADS


def _bucket_lower_bounds():
    max_exact = N_BUCKETS // 2
    d = np.arange(2 * MAX_DISTANCE, dtype=np.int32)
    nf = np.maximum(d, 1).astype(np.float32)
    large = max_exact + (np.log(nf / np.float32(max_exact)) / np.float32(math.log(MAX_DISTANCE / max_exact))
                         * np.float32(N_BUCKETS - max_exact)).astype(np.int32)
    bucket = np.where(d < max_exact, d, np.minimum(large, N_BUCKETS - 1))
    assert np.all(np.diff(bucket) >= 0) and np.all(bucket[MAX_DISTANCE:] == N_BUCKETS - 1)
    return [int(np.argmax(bucket >= b)) for b in range(N_BUCKETS)]


_BUCKET_LO = _bucket_lower_bounds()


def _rel_bias_delta(dist, bias_of_bucket):
    last = bias_of_bucket(N_BUCKETS - 1)
    val = jnp.full(dist.shape, (bias_of_bucket(0) - last) * LOG2E, F32)
    for b in range(1, N_BUCKETS - 1):
        val = jnp.where(dist >= _BUCKET_LO[b], (bias_of_bucket(b) - last) * LOG2E, val)
    val = jnp.where(dist >= _BUCKET_LO[N_BUCKETS - 1], 0.0, val)
    return jnp.where(dist < 0, NEG, val)


def _silu(z):
    return z * jax.nn.sigmoid(z)


def _proj_kernel(x_ref, nw_ref, w_ref, qT_ref, kvc_ref, ksel_ref, vselT_ref, kwin_ref, vwinT_ref,
                 gsT_ref, rest_ref, kv_scr, *, tm, nsel, kd):
    st = pl.program_id(1)
    x = x_ref[0]
    ms = jnp.mean(x * x, axis=-1, keepdims=True)
    h = (x * lax.rsqrt(ms + EPS) * nw_ref[...]).astype(BF16)

    def mm(lo, hi):
        return jnp.dot(h, w_ref[:, lo:hi], preferred_element_type=F32)

    q = mm(_C_Q, _C_CMP) * (HEAD_DIM ** -0.5 * LOG2E)
    qT_ref[0] = q.T.astype(BF16)

    kvc = mm(_C_CMP, _C_SW)
    for xk in range(2):
        kv_scr[xk] = kvc[:, xk * D_KV:(xk + 1) * D_KV]
        for j in range(CMP_STRIDE):
            kvc_ref[xk, 0, :, j * D_KV:(j + 1) * D_KV] = kv_scr[xk, pl.ds(j, tm // CMP_STRIDE, stride=CMP_STRIDE), :]

    a = mm(_C_SW, _C_REST)
    pos = st * tm + lax.broadcasted_iota(jnp.int32, (tm, nsel), 0)
    col = lax.broadcasted_iota(jnp.int32, (tm, nsel), 1)
    onehot = jnp.where((pos // SEL_BLOCK) == col, 1.0, 0.0).astype(BF16)
    ones = jnp.ones((V_ROWS - HEAD_DIM, tm), BF16)
    zeros_k = jnp.zeros((tm, LANES - HEAD_DIM), BF16)
    vsT = a[:, D_KV:2 * D_KV].T
    vwT = a[:, 3 * D_KV:4 * D_KV].T
    for g in range(N_KV_HEADS):
        lo, hi = g * HEAD_DIM, (g + 1) * HEAD_DIM
        parts = [a[:, lo:hi].astype(BF16), onehot]
        if kd > HEAD_DIM + nsel:
            parts.append(jnp.zeros((tm, kd - HEAD_DIM - nsel), BF16))
        ksel_ref[0, g] = jnp.concatenate(parts, axis=1)
        kwin_ref[0, g] = jnp.concatenate([a[:, 2 * D_KV + lo:2 * D_KV + hi].astype(BF16), zeros_k], axis=1)
        vs = jnp.concatenate([vsT[lo:hi].astype(BF16), ones], axis=0)
        vw = jnp.concatenate([vwT[lo:hi].astype(BF16), ones], axis=0)
        for kt in range(tm // KEY_TILE):
            vselT_ref[0, g, kt] = vs[:, kt * KEY_TILE:(kt + 1) * KEY_TILE]
            vwinT_ref[0, g, kt] = vw[:, kt * KEY_TILE:(kt + 1) * KEY_TILE]

    rest_ref[0] = mm(_C_REST, _C_GATE).astype(BF16)
    gates = jax.nn.sigmoid(mm(_C_GATE, _C_END))
    gsT_ref[0] = gates.T[:N_GATE]


def _compress_kernel(c_ref, pe_ref, w1_ref, w2_ref, lhs_tail_ref, aug_ref, t_ref, *, nch):
    c = c_ref[0, 0]
    a0 = jnp.dot((c + pe_ref[0, 0]).astype(BF16), w1_ref[0, 0], preferred_element_type=F32)
    a1 = jnp.dot((c + pe_ref[0, 1]).astype(BF16), w1_ref[0, 1], preferred_element_type=F32)
    hid = a0 + pltpu.roll(a1, nch - 1, 0)
    out = jnp.dot(_silu(hid).astype(BF16), w2_ref[0], preferred_element_type=F32)
    grp = lax.broadcasted_iota(jnp.int32, (nch, LANES - HEAD_DIM), 0) // 8
    col = lax.broadcasted_iota(jnp.int32, (nch, LANES - HEAD_DIM), 1)
    onehot = jnp.where(grp == col, 1.0, 0.0).astype(BF16)
    out_t = out.T.astype(BF16)
    for g in range(N_KV_HEADS):
        lo, hi = g * HEAD_DIM, (g + 1) * HEAD_DIM
        aug_ref[0, 0, g] = jnp.concatenate([out[:, lo:hi].astype(BF16), onehot], axis=1)
        t_ref[0, 0, g] = jnp.concatenate([out_t[lo:hi], lhs_tail_ref[...]], axis=0)


def _cmp_kernel(rb_ref, qT_ref, kc_ref, lhs_ref, gs_ref, ocT_ref, selb_ref,
                qa_ref, band_ref, acc_ref, s0_ref, s1_ref, s2_ref, s3_ref, *, nch, nsel, tq, k_top):
    g = pl.program_id(1)
    qt = pl.program_id(2)
    s_refs = (s0_ref, s1_ref, s2_ref, s3_ref)
    cpq = tq // CMP_STRIDE
    win = cpq + 16
    lim = cpq * (qt + 1)
    nq = GQA_REP * tq

    @pl.when(qt <= 1)
    def _():
        row = lax.broadcasted_iota(jnp.int32, (win, tq), 0) + jnp.maximum(cpq * qt - 16, 0)
        lane = lax.broadcasted_iota(jnp.int32, (win, tq), 1)
        dist = (qt * tq + lane) - (row * CMP_STRIDE + CMP_BLOCK - 1)
        for r in range(GQA_REP):
            band_ref[:, r * tq:(r + 1) * tq] = _rel_bias_delta(dist, lambda b, r=r: rb_ref[b, g * GQA_REP + r])

    q = qT_ref[0]
    qa_ref[0:HEAD_DIM, :] = jnp.concatenate([q[r * HEAD_DIM:(r + 1) * HEAD_DIM] for r in range(GQA_REP)], axis=1)
    grp = lax.broadcasted_iota(jnp.int32, (LANES - HEAD_DIM, nq), 0)
    qa_ref[HEAD_DIM:LANES, :] = jnp.where(grp * 8 >= lim, NEG, 0.0).astype(BF16)

    def run(rows):
        start = pl.multiple_of(jnp.clip(cpq * qt - 16, 0, rows - win), 8)
        k = kc_ref[0, 0, 0, 0:rows, :]
        lhs = lhs_ref[0, 0, 0, :, 0:rows]
        cols = [slice(r * tq, (r + 1) * tq) for r in range(GQA_REP)]

        def qk(r):
            s_refs[r][0:rows, :] = jnp.dot(k, qa_ref[:, cols[r]], preferred_element_type=F32)

        qk(0)
        for r in range(GQA_REP):
            if r + 1 < GQA_REP:
                qk(r + 1)
            sr = s_refs[r]
            sr[pl.ds(start, win), :] = sr[pl.ds(start, win), :] + band_ref[:, cols[r]]
            s = sr[0:rows, :]
            p = jnp.exp2(s - jnp.max(s, axis=0, keepdims=True)).astype(BF16)
            acc_ref[:, cols[r]] = jnp.dot(lhs, p, preferred_element_type=F32)

    half = nch // 2

    @pl.when(lim <= half)
    def _():
        run(half)

    @pl.when(lim > half)
    def _():
        run(nch)

    l = acc_ref[HEAD_DIM:HEAD_DIM + 1, :]
    t4 = qt * tq + (lax.broadcasted_iota(jnp.int32, l.shape, 1) & (tq - 1))
    inv = jnp.where(t4 >= CMP_BLOCK - 1, 1.0 / jnp.maximum(l, 1e-30), 0.0)
    imp = None
    for r in range(GQA_REP):
        cs = slice(r * tq, (r + 1) * tq)
        ocT_ref[0, r * HEAD_DIM:(r + 1) * HEAD_DIM, :] = (acc_ref[0:HEAD_DIM, cs] * inv[:, cs]
                                                          * gs_ref[0, 0, 0, r:r + 1, :]).astype(BF16)
        part = acc_ref[V_ROWS:V_ROWS + nsel, cs] * inv[:, cs]
        imp = part if imp is None else imp + part

    n = lax.broadcasted_iota(jnp.int32, (nsel, tq), 0)
    blk = (qt * tq + lax.broadcasted_iota(jnp.int32, (nsel, tq), 1)) // SEL_BLOCK
    forced = (n == 0) | (n == blk) | (n == blk - 1)
    candidate = (n <= blk) & jnp.logical_not(forced)
    score = jnp.where(forced, -2.0, jnp.where(n <= blk, imp, -1.0))

    def pick_all(_, score):
        return jnp.where(score == jnp.max(score, axis=0, keepdims=True), -2.0, score)

    taken = lax.fori_loop(0, k_top - 3, pick_all, score) == -2.0
    selb_ref[0, 0] = jnp.where(taken, 0.0, NEG).astype(BF16)
    n_taken = jnp.sum(jnp.where(taken & candidate, 1.0, 0.0), axis=0, keepdims=True)

    @pl.when(jnp.max(n_taken) > k_top - 3)
    def _():
        def pick_one(_, score):
            mx = jnp.max(score, axis=0, keepdims=True)
            idx = jnp.min(jnp.where(score == mx, n, nsel), axis=0, keepdims=True)
            return jnp.where(n == idx, -2.0, score)

        taken = lax.fori_loop(0, k_top - 3, pick_one, score) == -2.0
        selb_ref[0, 0] = jnp.where(taken, 0.0, NEG).astype(BF16)


def _flash_kernel(*refs, tq, nsel, kd, n_tbl_tiles, window, selected):
    if selected:
        rb_ref, qT_ref, selb_ref, k_ref, vT_ref, gs_ref, o_ref, tbl_ref, qa_ref, m_ref, acc_ref, lag_ref, *slot_refs = refs
    else:
        rb_ref, qT_ref, k_ref, vT_ref, gs_ref, o_ref, tbl_ref, qa_ref, m_ref, acc_ref, lag_ref, *slot_refs = refs
    s_ref, mt_ref, al_ref, p_ref = (slot_refs[i * N_SLOTS:(i + 1) * N_SLOTS] for i in range(4))
    g = pl.program_id(1)
    qt = pl.program_id(2)
    nq = GQA_REP * tq

    @pl.when(qt == 0)
    def _():
        for j in range(n_tbl_tiles):
            key = lax.broadcasted_iota(jnp.int32, (tq, tq), 0) + j * tq
            qry = lax.broadcasted_iota(jnp.int32, (tq, tq), 1) + (n_tbl_tiles - 1) * tq
            dist = qry - key
            for r in range(GQA_REP):
                val = _rel_bias_delta(dist, lambda b, r=r: rb_ref[b, g * GQA_REP + r])
                if window is not None:
                    val = jnp.where(dist >= window, NEG, val)
                tbl_ref[j * tq:(j + 1) * tq, r * tq:(r + 1) * tq] = val

    q = qT_ref[0]
    qa_ref[0:HEAD_DIM, :] = jnp.concatenate([q[r * HEAD_DIM:(r + 1) * HEAD_DIM] for r in range(GQA_REP)], axis=1)
    used = HEAD_DIM
    if selected:
        qa_ref[HEAD_DIM:HEAD_DIM + nsel, :] = jnp.concatenate([selb_ref[0, 0]] * GQA_REP, axis=1)
        used += nsel
    if kd > used:
        qa_ref[used:kd, :] = jnp.zeros((kd - used, nq), BF16)
    m_ref[...] = jnp.full((1, nq), NEG, F32)
    acc_ref[...] = jnp.zeros((V_ROWS, nq), F32)

    def stage_qk(t, slot, tbl_tile):
        k = k_ref[0, 0, pl.ds(pl.multiple_of(t * tq, tq), tq), :]
        s = jnp.dot(k, qa_ref[...], preferred_element_type=F32)
        if tbl_tile is not None:
            s = s + tbl_ref[tbl_tile * tq:(tbl_tile + 1) * tq, :]
        s_ref[slot][...] = s
        mt_ref[slot][...] = jnp.max(s, axis=0, keepdims=True)

    def stage_exp(slot):
        m_old = m_ref[...]
        m_new = jnp.maximum(m_old, mt_ref[slot][...])
        m_ref[...] = m_new
        al_ref[slot][...] = jnp.exp2(m_old - m_new)
        p_ref[slot][...] = jnp.exp2(s_ref[slot][...] - m_new).astype(BF16)

    def stage_pv(t, slot):
        acc_ref[...] = al_ref[slot][...] * acc_ref[...] + jnp.dot(vT_ref[0, 0, t], p_ref[slot][...],
                                                                  preferred_element_type=F32)

    def serial(t, tbl_tile):
        stage_qk(t, 0, tbl_tile)
        stage_exp(0)
        stage_pv(t, 0)

    last = n_tbl_tiles - 1

    def stage_lagged(t, slot, tbl_tile, first=False):
        k = k_ref[0, 0, pl.ds(pl.multiple_of(t * tq, tq), tq), :]
        s = jnp.dot(k, qa_ref[...], preferred_element_type=F32)
        if tbl_tile is not None:
            s = s + tbl_ref[tbl_tile * tq:(tbl_tile + 1) * tq, :]
        m = s[0:1, :] if first else m_ref[...]
        mt = jnp.max(s, axis=0, keepdims=True)
        p_ref[slot][...] = jnp.exp2((s - m).astype(BF16))
        m_new = jnp.maximum(m, mt)
        m_ref[...] = m_new
        al_ref[1 - slot][...] = jnp.exp2(m - m_new)
        lag_ref[...] = (mt - m) if first else jnp.maximum(lag_ref[...], mt - m)

    def lagged_step(t, slot, tbl_tile, prev=None):
        stage_pv(t - 1 if prev is None else prev, 1 - slot)
        stage_lagged(t, slot, tbl_tile)

    if selected:
        n = qt + 1

        @pl.when((n >= 4) & ((n & 1) == 0))
        def _():
            al_ref[1][...] = jnp.ones((1, nq), F32)
            stage_lagged(0, 1, None, first=True)
            lagged_step(1, 0, None)

        @pl.when((n >= 4) & ((n & 1) == 1))
        def _():
            al_ref[0][...] = jnp.ones((1, nq), F32)
            stage_lagged(0, 0, None, first=True)

        @pl.when(n >= 4)
        def _():
            t0 = 2 - (n & 1)

            def body(j, carry):
                t = t0 + 2 * j
                lagged_step(t, 1, None)
                lagged_step(t + 1, 0, None)
                return carry
            lax.fori_loop(0, (n - 2 - t0) >> 1, body, 0)

            lagged_step(n - 2, 1, 0)
            lagged_step(n - 1, 0, 1)
            stage_pv(n - 1, 0)

            @pl.when(jnp.max(lag_ref[...]) > MAX_LAG)
            def _():
                m_ref[...] = jnp.full((1, nq), NEG, F32)
                acc_ref[...] = jnp.zeros((V_ROWS, nq), F32)

                def redo(t, carry):
                    serial(t, None)
                    return carry
                lax.fori_loop(0, n - 2, redo, 0)
                serial(n - 2, 0)
                serial(n - 1, 1)

        @pl.when(n == 3)
        def _():
            serial(0, None)

        @pl.when((n >= 2) & (n < 4))
        def _():
            serial(qt - 1, 0)

        @pl.when(n < 4)
        def _():
            serial(qt, 1)
    else:
        @pl.when(qt >= last)
        def _():
            al_ref[0][...] = jnp.ones((1, nq), F32)
            stage_lagged(qt, 0, 2, first=True)
            lagged_step(qt - 1, 1, 1, prev=qt)
            lagged_step(qt - 2, 0, 0, prev=qt - 1)
            stage_pv(qt - 2, 0)

            @pl.when(jnp.max(lag_ref[...]) > MAX_LAG)
            def _():
                m_ref[...] = jnp.full((1, nq), NEG, F32)
                acc_ref[...] = jnp.zeros((V_ROWS, nq), F32)
                for j in range(n_tbl_tiles):
                    serial(qt - last + j, j)

        @pl.when(qt == 1)
        def _():
            serial(0, 1)

        @pl.when(qt < last)
        def _():
            serial(qt, 2)

    acc = acc_ref[...]
    o = acc[0:HEAD_DIM] / jnp.maximum(acc[HEAD_DIM:HEAD_DIM + 1], 1e-30)
    for r in range(GQA_REP):
        o_ref[0, r * HEAD_DIM:(r + 1) * HEAD_DIM, :] = (o[:, r * tq:(r + 1) * tq]
                                                        * gs_ref[0, 0, 0, r:r + 1, :]).astype(BF16)


def _out_kernel(oc_ref, os_ref, ow_ref, za_ref, ch_ref, cb_ref, cc_ref, zc_ref, hh_ref, hc_ref,
                x_ref, cw_ref, wo_ref, fw_ref, out_ref, *, tm):
    st = pl.program_id(1)
    f32 = lambda ref: ref[0].astype(F32)
    attn = (f32(oc_ref) + f32(os_ref) + f32(ow_ref)).T * _silu(f32(za_ref))

    u = f32(cc_ref) * f32(ch_ref)
    halo = jnp.where(st == 0, 0.0, f32(hc_ref) * f32(hh_ref))
    row = lax.broadcasted_iota(jnp.int32, u.shape, 0)
    h1, h2 = halo[HALO_ROWS - 1:HALO_ROWS], halo[HALO_ROWS - 2:HALO_ROWS - 1]
    u1 = jnp.where(row == 0, h1, pltpu.roll(u, 1, 0))
    u2 = jnp.where(row == 0, h2, jnp.where(row == 1, h1, pltpu.roll(u, 2, 0)))
    y = cw_ref[0:1] * u2 + cw_ref[1:2] * u1 + cw_ref[2:3] * u
    conv = f32(cb_ref) * y * _silu(f32(zc_ref))

    mixed = jnp.concatenate([attn, conv], axis=1).astype(BF16)
    y2 = x_ref[0] + jnp.dot(mixed, wo_ref[...], preferred_element_type=F32)
    ms = jnp.mean(y2 * y2, axis=-1, keepdims=True)
    out_ref[0] = y2 * lax.rsqrt(ms + EPS) * fw_ref[...]


def _overlap_t(nsel, nch):
    c = np.arange(nch)[None, :] * CMP_STRIDE
    n = np.arange(nsel)[:, None] * SEL_BLOCK
    return ((c < n + SEL_BLOCK) & (c + CMP_BLOCK > n)).astype(np.float32)


def _cparams(sem):
    return pltpu.CompilerParams(dimension_semantics=sem, vmem_limit_bytes=VMEM_LIMIT)


def kernel(x, norm_w, w_in, w_ck1, w_ck2, pe_k, w_cv1, w_cv2, pe_v, conv_w, w_out, rel_bias, final_norm_w):
    b, s, _ = x.shape
    assert norm_w.shape[0] == 1 and s % ROW_TILE == 0 and ROW_TILE % KEY_TILE == 0
    assert WINDOW == 2 * Q_TILE and KEY_TILE == Q_TILE
    tm = ROW_TILE
    nt = s // tm
    nkt = s // KEY_TILE
    kpt = tm // KEY_TILE
    nch = s // CMP_STRIDE
    nsel = s // SEL_BLOCK
    k_top = min(N_SEL, nsel)
    kd = -(-(HEAD_DIM + nsel) // LANES) * LANES
    G, R = N_KV_HEADS, GQA_REP

    w = w_in[0].astype(BF16)
    c0 = D_ATTN + 6 * D_KV
    w_all = jnp.concatenate([w[:, :c0], w[:, c0 + N_GATE:], w[:, c0:c0 + N_GATE],
                             jnp.zeros((D_MODEL, LANES - N_GATE), BF16)], axis=1)
    eye = jnp.eye(G, dtype=F32)

    def expand1(w1):
        w1 = w1.astype(BF16).reshape(2, CMP_STRIDE, HEAD_DIM, CMP_HIDDEN)
        z = jnp.zeros_like(w1)
        rows = [jnp.concatenate([w1 if a == g else z for a in range(G)], axis=-1) for g in range(G)]
        return jnp.stack(rows, axis=2).reshape(2, CMP_STRIDE * D_KV, G * CMP_HIDDEN)

    def expand_pe(pe):
        pe = pe.reshape(2, CMP_STRIDE, 1, HEAD_DIM)
        return jnp.broadcast_to(pe, (2, CMP_STRIDE, G, HEAD_DIM)).reshape(2, 1, CMP_STRIDE * D_KV)

    def expand2(w2):
        return jnp.einsum('hd,ab->ahbd', w2, eye).reshape(G * CMP_HIDDEN, D_KV)

    w1e = jnp.stack([expand1(w_ck1[0]), expand1(w_cv1[0])]).astype(BF16)
    pee = jnp.stack([expand_pe(pe_k[0]), expand_pe(pe_v[0])])
    w2e = jnp.stack([expand2(w_ck2[0]), expand2(w_cv2[0])]).astype(BF16)
    rel_bias = rel_bias.astype(F32)
    smem = pl.BlockSpec(memory_space=pltpu.SMEM)

    n_rest = D_ATTN + 4 * D_CONV
    cw = CMP_STRIDE * D_KV
    qT, kvc, ksel, vselT, kwin, vwinT, gsT, rest = pl.pallas_call(
        partial(_proj_kernel, tm=tm, nsel=nsel, kd=kd),
        grid=(b, nt),
        in_specs=[pl.BlockSpec((1, tm, D_MODEL), lambda i, j: (i, j, 0)),
                  pl.BlockSpec((1, D_MODEL), lambda i, j: (0, 0)),
                  pl.BlockSpec((D_MODEL, _C_END), lambda i, j: (0, 0))],
        out_specs=[pl.BlockSpec((1, D_ATTN, tm), lambda i, j: (i, 0, j)),
                   pl.BlockSpec((2, 1, tm // CMP_STRIDE, cw), lambda i, j: (0, i, j, 0)),
                   pl.BlockSpec((1, G, tm, kd), lambda i, j: (i, 0, j, 0)),
                   pl.BlockSpec((1, G, kpt, V_ROWS, KEY_TILE), lambda i, j: (i, 0, j, 0, 0)),
                   pl.BlockSpec((1, G, tm, LANES), lambda i, j: (i, 0, j, 0)),
                   pl.BlockSpec((1, G, kpt, V_ROWS, KEY_TILE), lambda i, j: (i, 0, j, 0, 0)),
                   pl.BlockSpec((1, N_GATE, tm), lambda i, j: (i, 0, j)),
                   pl.BlockSpec((1, tm, n_rest), lambda i, j: (i, j, 0))],
        out_shape=[jax.ShapeDtypeStruct((b, D_ATTN, s), BF16),
                   jax.ShapeDtypeStruct((2, b, nch, cw), F32),
                   jax.ShapeDtypeStruct((b, G, s, kd), BF16),
                   jax.ShapeDtypeStruct((b, G, nkt, V_ROWS, KEY_TILE), BF16),
                   jax.ShapeDtypeStruct((b, G, s, LANES), BF16),
                   jax.ShapeDtypeStruct((b, G, nkt, V_ROWS, KEY_TILE), BF16),
                   jax.ShapeDtypeStruct((b, N_GATE, s), F32),
                   jax.ShapeDtypeStruct((b, s, n_rest), BF16)],
        scratch_shapes=[pltpu.VMEM((2, tm, D_KV), F32)],
        compiler_params=_cparams(("parallel", "parallel")),
        name="in_proj",
    )(x, norm_w, w_all)
    gs = gsT.reshape(b, 3, G, R, s)

    lhs_rows = V_ROWS + nsel
    lhs_tail = jnp.asarray(np.concatenate([np.ones((V_ROWS - HEAD_DIM, nch), np.float32),
                                           _overlap_t(nsel, nch)], axis=0), BF16)
    kc_aug, c_lhs = pl.pallas_call(
        partial(_compress_kernel, nch=nch),
        grid=(2, b),
        in_specs=[pl.BlockSpec((1, 1, nch, cw), lambda xk, i: (xk, i, 0, 0)),
                  pl.BlockSpec((1, 2, 1, cw), lambda xk, i: (xk, 0, 0, 0)),
                  pl.BlockSpec((1, 2, cw, G * CMP_HIDDEN), lambda xk, i: (xk, 0, 0, 0)),
                  pl.BlockSpec((1, G * CMP_HIDDEN, D_KV), lambda xk, i: (xk, 0, 0)),
                  pl.BlockSpec((lhs_rows - HEAD_DIM, nch), lambda xk, i: (0, 0))],
        out_specs=[pl.BlockSpec((1, 1, G, nch, LANES), lambda xk, i: (xk, i, 0, 0, 0)),
                   pl.BlockSpec((1, 1, G, lhs_rows, nch), lambda xk, i: (xk, i, 0, 0, 0))],
        out_shape=[jax.ShapeDtypeStruct((2, b, G, nch, LANES), BF16),
                   jax.ShapeDtypeStruct((2, b, G, lhs_rows, nch), BF16)],
        compiler_params=_cparams(("parallel", "parallel")),
        name="compress",
    )(kvc, pee, w1e, w2e, lhs_tail)

    tq3 = Q_TILE_CMP
    cpq = tq3 // CMP_STRIDE
    assert nch // 8 <= LANES - HEAD_DIM and k_top > 3 and (nch // 2) % LANES == 0
    ocT, selb = pl.pallas_call(
        partial(_cmp_kernel, nch=nch, nsel=nsel, tq=tq3, k_top=k_top),
        grid=(b, G, s // tq3),
        in_specs=[smem,
                  pl.BlockSpec((1, R * HEAD_DIM, tq3), lambda i, g, t: (i, g, t)),
                  pl.BlockSpec((1, 1, 1, nch, LANES), lambda i, g, t: (0, i, g, 0, 0)),
                  pl.BlockSpec((1, 1, 1, lhs_rows, nch), lambda i, g, t: (1, i, g, 0, 0)),
                  pl.BlockSpec((1, 1, 1, R, tq3), lambda i, g, t: (i, 0, g, 0, t))],
        out_specs=[pl.BlockSpec((1, R * HEAD_DIM, tq3), lambda i, g, t: (i, g, t)),
                   pl.BlockSpec((1, 1, nsel, tq3), lambda i, g, t: (i, g, 0, t))],
        out_shape=[jax.ShapeDtypeStruct((b, D_ATTN, s), BF16),
                   jax.ShapeDtypeStruct((b, G, nsel, s), BF16)],
        scratch_shapes=[pltpu.VMEM((LANES, R * tq3), BF16),
                        pltpu.VMEM((cpq + 16, R * tq3), F32),
                        pltpu.VMEM((lhs_rows, R * tq3), F32)]
                       + [pltpu.VMEM((nch, tq3), F32)] * R,
        compiler_params=_cparams(("parallel", "parallel", "arbitrary")),
        name="cmp_select",
    )(rel_bias, qT, kc_aug, c_lhs, gs)

    tq = Q_TILE
    nq = R * tq

    def flash(selected, k_arr, vT_arr, kdim, n_tbl, window, branch, name):
        in_specs = [smem, pl.BlockSpec((1, R * HEAD_DIM, tq), lambda i, g, t: (i, g, t))]
        args = [rel_bias, qT]
        if selected:
            in_specs.append(pl.BlockSpec((1, 1, nsel, tq), lambda i, g, t: (i, g, 0, t)))
            args.append(selb)
        in_specs += [pl.BlockSpec((1, 1, s, kdim), lambda i, g, t: (i, g, 0, 0)),
                     pl.BlockSpec((1, 1, nkt, V_ROWS, KEY_TILE), lambda i, g, t: (i, g, 0, 0, 0)),
                     pl.BlockSpec((1, 1, 1, R, tq), lambda i, g, t: (i, branch, g, 0, t))]
        args += [k_arr, vT_arr, gs]
        return pl.pallas_call(
            partial(_flash_kernel, tq=tq, nsel=nsel, kd=kdim, n_tbl_tiles=n_tbl, window=window, selected=selected),
            grid=(b, G, s // tq),
            in_specs=in_specs,
            out_specs=pl.BlockSpec((1, R * HEAD_DIM, tq), lambda i, g, t: (i, g, t)),
            out_shape=jax.ShapeDtypeStruct((b, D_ATTN, s), BF16),
            scratch_shapes=[pltpu.VMEM((n_tbl * tq, nq), F32),
                            pltpu.VMEM((kdim, nq), BF16),
                            pltpu.VMEM((1, nq), F32),
                            pltpu.VMEM((V_ROWS, nq), F32),
                            pltpu.VMEM((1, nq), F32)]
                           + [pltpu.VMEM((tq, nq), F32)] * N_SLOTS
                           + [pltpu.VMEM((1, nq), F32)] * N_SLOTS
                           + [pltpu.VMEM((1, nq), F32)] * N_SLOTS
                           + [pltpu.VMEM((tq, nq), BF16)] * N_SLOTS,
            compiler_params=_cparams(("parallel", "parallel", "arbitrary")),
            name=name,
        )(*args)

    osT = flash(True, ksel, vselT, kd, 2, None, 1, "sel_attn")
    owT = flash(False, kwin, vwinT, LANES, 3, WINDOW, 2, "win_attn")

    hb = tm // HALO_ROWS
    col = lambda c: pl.BlockSpec((1, tm, D_CONV), lambda i, j, c=c: (i, j, c))
    halo = lambda c: pl.BlockSpec((1, HALO_ROWS, D_CONV), lambda i, j, c=c: (i, jnp.maximum(j * hb - 1, 0), c))
    oT = pl.BlockSpec((1, D_ATTN, tm), lambda i, j: (i, 0, j))
    out = pl.pallas_call(
        partial(_out_kernel, tm=tm),
        grid=(b, nt),
        in_specs=[oT, oT, oT, col(0), col(1), col(2), col(3), col(4), halo(1), halo(3),
                  pl.BlockSpec((1, tm, D_MODEL), lambda i, j: (i, j, 0)),
                  pl.BlockSpec((3, D_CONV), lambda i, j: (0, 0)),
                  pl.BlockSpec((D_MODEL, D_MODEL), lambda i, j: (0, 0)),
                  pl.BlockSpec((1, D_MODEL), lambda i, j: (0, 0))],
        out_specs=pl.BlockSpec((1, tm, D_MODEL), lambda i, j: (i, j, 0)),
        out_shape=jax.ShapeDtypeStruct((b, s, D_MODEL), F32),
        compiler_params=_cparams(("parallel", "parallel")),
        name="out_proj",
    )(ocT, osT, owT, rest, rest, rest, rest, rest, rest, rest, x, conv_w[0], w_out[0].astype(BF16),
      final_norm_w.reshape(1, D_MODEL))
    return out
```

```python
import math
from functools import partial

import numpy as np
import jax
import jax.numpy as jnp
from jax import lax
from jax.experimental import pallas as pl
from jax.experimental.pallas import tpu as pltpu

D_MODEL = 1024
N_HEADS = 8
HEAD_DIM = 64
N_KV_HEADS = 2
GQA_REP = N_HEADS // N_KV_HEADS
D_ATTN = N_HEADS * HEAD_DIM
D_KV = N_KV_HEADS * HEAD_DIM
D_CONV = D_MODEL - D_ATTN
CMP_BLOCK = 32
CMP_STRIDE = 16
CMP_HIDDEN = 2 * HEAD_DIM
SEL_BLOCK = 64
N_SEL = 16
WINDOW = 512
N_BUCKETS = 32
MAX_DISTANCE = 128
EPS = 1e-6
NEG = -1e30
FORCE_SCORE = 1e3

LOG2E = 1.4426950408889634
LANES = 128
MXU_COLS = 256
V_ROWS = HEAD_DIM + 16
ROW_TILE = 512
KEY_TILE = 256
Q_TILE_CMP = 256
Q_TILE = 256
HALO_ROWS = 16
MAX_LAG = 64.0
N_SLOTS = 2
VMEM_LIMIT = 48 * 1024 * 1024

F32 = jnp.float32
BF16 = jnp.bfloat16

_C_Q = 0
_C_CMP = D_ATTN
_C_SW = _C_CMP + 2 * D_KV
_C_REST = _C_SW + 4 * D_KV
_C_GATE = _C_REST + D_ATTN + 4 * D_CONV
_C_END = _C_GATE + LANES
N_GATE = 3 * N_HEADS


def _bucket_lower_bounds():
    max_exact = N_BUCKETS // 2
    d = np.arange(2 * MAX_DISTANCE, dtype=np.int32)
    nf = np.maximum(d, 1).astype(np.float32)
    large = max_exact + (np.log(nf / np.float32(max_exact)) / np.float32(math.log(MAX_DISTANCE / max_exact))
                         * np.float32(N_BUCKETS - max_exact)).astype(np.int32)
    bucket = np.where(d < max_exact, d, np.minimum(large, N_BUCKETS - 1))
    assert np.all(np.diff(bucket) >= 0) and np.all(bucket[MAX_DISTANCE:] == N_BUCKETS - 1)
    return [int(np.argmax(bucket >= b)) for b in range(N_BUCKETS)]


_BUCKET_LO = _bucket_lower_bounds()


def _rel_bias_delta(dist, bias_of_bucket):
    last = bias_of_bucket(N_BUCKETS - 1)
    val = jnp.full(dist.shape, (bias_of_bucket(0) - last) * LOG2E, F32)
    for b in range(1, N_BUCKETS - 1):
        val = jnp.where(dist >= _BUCKET_LO[b], (bias_of_bucket(b) - last) * LOG2E, val)
    val = jnp.where(dist >= _BUCKET_LO[N_BUCKETS - 1], 0.0, val)
    return jnp.where(dist < 0, NEG, val)


def _silu(z):
    return z * jax.nn.sigmoid(z)


def _proj_kernel(x_ref, nw_ref, w_ref, qT_ref, kvc_ref, ksel_ref, vselT_ref, kwin_ref, vwinT_ref,
                 gsT_ref, rest_ref, kv_scr, *, tm, nsel, kd):
    st = pl.program_id(1)
    x = x_ref[0]
    ms = jnp.mean(x * x, axis=-1, keepdims=True)
    h = (x * lax.rsqrt(ms + EPS) * nw_ref[...]).astype(BF16)

    def mm(lo, hi):
        return jnp.dot(h, w_ref[:, lo:hi], preferred_element_type=F32)

    q = mm(_C_Q, _C_CMP) * (HEAD_DIM ** -0.5 * LOG2E)
    qT_ref[0] = q.T.astype(BF16)

    kvc = mm(_C_CMP, _C_SW)
    for xk in range(2):
        kv_scr[xk] = kvc[:, xk * D_KV:(xk + 1) * D_KV]
        for j in range(CMP_STRIDE):
            kvc_ref[xk, 0, :, j * D_KV:(j + 1) * D_KV] = kv_scr[xk, pl.ds(j, tm // CMP_STRIDE, stride=CMP_STRIDE), :]

    a = mm(_C_SW, _C_REST)
    pos = st * tm + lax.broadcasted_iota(jnp.int32, (tm, nsel), 0)
    col = lax.broadcasted_iota(jnp.int32, (tm, nsel), 1)
    onehot = jnp.where((pos // SEL_BLOCK) == col, 1.0, 0.0).astype(BF16)
    ones = jnp.ones((V_ROWS - HEAD_DIM, tm), BF16)
    zeros_k = jnp.zeros((tm, LANES - HEAD_DIM), BF16)
    vsT = a[:, D_KV:2 * D_KV].T
    vwT = a[:, 3 * D_KV:4 * D_KV].T
    for g in range(N_KV_HEADS):
        lo, hi = g * HEAD_DIM, (g + 1) * HEAD_DIM
        parts = [a[:, lo:hi].astype(BF16), onehot]
        if kd > HEAD_DIM + nsel:
            parts.append(jnp.zeros((tm, kd - HEAD_DIM - nsel), BF16))
        ksel_ref[0, g] = jnp.concatenate(parts, axis=1)
        kwin_ref[0, g] = jnp.concatenate([a[:, 2 * D_KV + lo:2 * D_KV + hi].astype(BF16), zeros_k], axis=1)
        vs = jnp.concatenate([vsT[lo:hi].astype(BF16), ones], axis=0)
        vw = jnp.concatenate([vwT[lo:hi].astype(BF16), ones], axis=0)
        for kt in range(tm // KEY_TILE):
            vselT_ref[0, g, kt] = vs[:, kt * KEY_TILE:(kt + 1) * KEY_TILE]
            vwinT_ref[0, g, kt] = vw[:, kt * KEY_TILE:(kt + 1) * KEY_TILE]

    rest_ref[0] = mm(_C_REST, _C_GATE).astype(BF16)
    gates = jax.nn.sigmoid(mm(_C_GATE, _C_END))
    gsT_ref[0] = gates.T[:N_GATE]


def _compress_kernel(c_ref, pe_ref, w1_ref, w2_ref, lhs_tail_ref, aug_ref, t_ref, *, nch):
    c = c_ref[0, 0]
    a0 = jnp.dot((c + pe_ref[0, 0]).astype(BF16), w1_ref[0, 0], preferred_element_type=F32)
    a1 = jnp.dot((c + pe_ref[0, 1]).astype(BF16), w1_ref[0, 1], preferred_element_type=F32)
    hid = a0 + pltpu.roll(a1, nch - 1, 0)
    out = jnp.dot(_silu(hid).astype(BF16), w2_ref[0], preferred_element_type=F32)
    grp = lax.broadcasted_iota(jnp.int32, (nch, LANES - HEAD_DIM), 0) // 8
    col = lax.broadcasted_iota(jnp.int32, (nch, LANES - HEAD_DIM), 1)
    onehot = jnp.where(grp == col, 1.0, 0.0).astype(BF16)
    out_t = out.T.astype(BF16)
    for g in range(N_KV_HEADS):
        lo, hi = g * HEAD_DIM, (g + 1) * HEAD_DIM
        aug_ref[0, 0, g] = jnp.concatenate([out[:, lo:hi].astype(BF16), onehot], axis=1)
        t_ref[0, 0, g] = jnp.concatenate([out_t[lo:hi], lhs_tail_ref[...]], axis=0)


def _cmp_kernel(rb_ref, qT_ref, kc_ref, lhs_ref, gs_ref, ocT_ref, selb_ref,
                qa_ref, band_ref, acc_ref, s0_ref, s1_ref, s2_ref, s3_ref, *, nch, nsel, tq, k_top):
    g = pl.program_id(1)
    qt = pl.program_id(2)
    s_refs = (s0_ref, s1_ref, s2_ref, s3_ref)
    cpq = tq // CMP_STRIDE
    win = cpq + 16
    lim = cpq * (qt + 1)
    nq = GQA_REP * tq

    @pl.when(qt <= 1)
    def _():
        row = lax.broadcasted_iota(jnp.int32, (win, tq), 0) + jnp.maximum(cpq * qt - 16, 0)
        lane = lax.broadcasted_iota(jnp.int32, (win, tq), 1)
        dist = (qt * tq + lane) - (row * CMP_STRIDE + CMP_BLOCK - 1)
        for r in range(GQA_REP):
            band_ref[:, r * tq:(r + 1) * tq] = _rel_bias_delta(dist, lambda b, r=r: rb_ref[b, g * GQA_REP + r])

    q = qT_ref[0]
    qa_ref[0:HEAD_DIM, :] = jnp.concatenate([q[r * HEAD_DIM:(r + 1) * HEAD_DIM] for r in range(GQA_REP)], axis=1)
    grp = lax.broadcasted_iota(jnp.int32, (LANES - HEAD_DIM, nq), 0)
    qa_ref[HEAD_DIM:LANES, :] = jnp.where(grp * 8 >= lim, NEG, 0.0).astype(BF16)

    def run(rows):
        start = pl.multiple_of(jnp.clip(cpq * qt - 16, 0, rows - win), 8)
        k = kc_ref[0, 0, 0, 0:rows, :]
        lhs = lhs_ref[0, 0, 0, :, 0:rows]
        cols = [slice(r * tq, (r + 1) * tq) for r in range(GQA_REP)]

        def qk(r):
            s_refs[r][0:rows, :] = jnp.dot(k, qa_ref[:, cols[r]], preferred_element_type=F32)

        qk(0)
        for r in range(GQA_REP):
            if r + 1 < GQA_REP:
                qk(r + 1)
            sr = s_refs[r]
            sr[pl.ds(start, win), :] = sr[pl.ds(start, win), :] + band_ref[:, cols[r]]
            s = sr[0:rows, :]
            p = jnp.exp2(s - jnp.max(s, axis=0, keepdims=True)).astype(BF16)
            acc_ref[:, cols[r]] = jnp.dot(lhs, p, preferred_element_type=F32)

    half = nch // 2

    @pl.when(lim <= half)
    def _():
        run(half)

    @pl.when(lim > half)
    def _():
        run(nch)

    l = acc_ref[HEAD_DIM:HEAD_DIM + 1, :]
    t4 = qt * tq + (lax.broadcasted_iota(jnp.int32, l.shape, 1) & (tq - 1))
    inv = jnp.where(t4 >= CMP_BLOCK - 1, 1.0 / jnp.maximum(l, 1e-30), 0.0)
    imp = None
    for r in range(GQA_REP):
        cs = slice(r * tq, (r + 1) * tq)
        ocT_ref[0, r * HEAD_DIM:(r + 1) * HEAD_DIM, :] = (acc_ref[0:HEAD_DIM, cs] * inv[:, cs]
                                                          * gs_ref[0, 0, 0, r:r + 1, :]).astype(BF16)
        part = acc_ref[V_ROWS:V_ROWS + nsel, cs] * inv[:, cs]
        imp = part if imp is None else imp + part

    n = lax.broadcasted_iota(jnp.int32, (nsel, tq), 0)
    blk = (qt * tq + lax.broadcasted_iota(jnp.int32, (nsel, tq), 1)) // SEL_BLOCK
    forced = (n == 0) | (n == blk) | (n == blk - 1)
    candidate = (n <= blk) & jnp.logical_not(forced)
    score = jnp.where(forced, -2.0, jnp.where(n <= blk, imp, -1.0))

    def pick_all(_, score):
        return jnp.where(score == jnp.max(score, axis=0, keepdims=True), -2.0, score)

    taken = lax.fori_loop(0, k_top - 3, pick_all, score) == -2.0
    selb_ref[0, 0] = jnp.where(taken, 0.0, NEG).astype(BF16)
    n_taken = jnp.sum(jnp.where(taken & candidate, 1.0, 0.0), axis=0, keepdims=True)

    @pl.when(jnp.max(n_taken) > k_top - 3)
    def _():
        def pick_one(_, score):
            mx = jnp.max(score, axis=0, keepdims=True)
            idx = jnp.min(jnp.where(score == mx, n, nsel), axis=0, keepdims=True)
            return jnp.where(n == idx, -2.0, score)

        taken = lax.fori_loop(0, k_top - 3, pick_one, score) == -2.0
        selb_ref[0, 0] = jnp.where(taken, 0.0, NEG).astype(BF16)


def _flash_kernel(*refs, tq, nsel, kd, n_tbl_tiles, window, selected):
    if selected:
        rb_ref, qT_ref, selb_ref, k_ref, vT_ref, gs_ref, o_ref, tbl_ref, qa_ref, m_ref, acc_ref, lag_ref, *slot_refs = refs
    else:
        rb_ref, qT_ref, k_ref, vT_ref, gs_ref, o_ref, tbl_ref, qa_ref, m_ref, acc_ref, lag_ref, *slot_refs = refs
    s_ref, mt_ref, al_ref, p_ref = (slot_refs[i * N_SLOTS:(i + 1) * N_SLOTS] for i in range(4))
    g = pl.program_id(1)
    qt = pl.program_id(2)
    nq = GQA_REP * tq

    @pl.when(qt == 0)
    def _():
        for j in range(n_tbl_tiles):
            key = lax.broadcasted_iota(jnp.int32, (tq, tq), 0) + j * tq
            qry = lax.broadcasted_iota(jnp.int32, (tq, tq), 1) + (n_tbl_tiles - 1) * tq
            dist = qry - key
            for r in range(GQA_REP):
                val = _rel_bias_delta(dist, lambda b, r=r: rb_ref[b, g * GQA_REP + r])
                if window is not None:
                    val = jnp.where(dist >= window, NEG, val)
                tbl_ref[j * tq:(j + 1) * tq, r * tq:(r + 1) * tq] = val

    q = qT_ref[0]
    qa_ref[0:HEAD_DIM, :] = jnp.concatenate([q[r * HEAD_DIM:(r + 1) * HEAD_DIM] for r in range(GQA_REP)], axis=1)
    used = HEAD_DIM
    if selected:
        qa_ref[HEAD_DIM:HEAD_DIM + nsel, :] = jnp.concatenate([selb_ref[0, 0]] * GQA_REP, axis=1)
        used += nsel
    if kd > used:
        qa_ref[used:kd, :] = jnp.zeros((kd - used, nq), BF16)
    m_ref[...] = jnp.full((1, nq), NEG, F32)
    acc_ref[...] = jnp.zeros((V_ROWS, nq), F32)

    def stage_qk(t, slot, tbl_tile):
        k = k_ref[0, 0, pl.ds(pl.multiple_of(t * tq, tq), tq), :]
        s = jnp.dot(k, qa_ref[...], preferred_element_type=F32)
        if tbl_tile is not None:
            s = s + tbl_ref[tbl_tile * tq:(tbl_tile + 1) * tq, :]
        s_ref[slot][...] = s
        mt_ref[slot][...] = jnp.max(s, axis=0, keepdims=True)

    def stage_exp(slot):
        m_old = m_ref[...]
        m_new = jnp.maximum(m_old, mt_ref[slot][...])
        m_ref[...] = m_new
        al_ref[slot][...] = jnp.exp2(m_old - m_new)
        p_ref[slot][...] = jnp.exp2(s_ref[slot][...] - m_new).astype(BF16)

    def stage_pv(t, slot):
        acc_ref[...] = al_ref[slot][...] * acc_ref[...] + jnp.dot(vT_ref[0, 0, t], p_ref[slot][...],
                                                                  preferred_element_type=F32)

    def serial(t, tbl_tile):
        stage_qk(t, 0, tbl_tile)
        stage_exp(0)
        stage_pv(t, 0)

    last = n_tbl_tiles - 1

    def stage_lagged(t, slot, tbl_tile, first=False):
        k = k_ref[0, 0, pl.ds(pl.multiple_of(t * tq, tq), tq), :]
        s = jnp.dot(k, qa_ref[...], preferred_element_type=F32)
        if tbl_tile is not None:
            s = s + tbl_ref[tbl_tile * tq:(tbl_tile + 1) * tq, :]
        m = s[0:1, :] if first else m_ref[...]
        mt = jnp.max(s, axis=0, keepdims=True)
        p_ref[slot][...] = jnp.exp2((s - m).astype(BF16))
        m_new = jnp.maximum(m, mt)
        m_ref[...] = m_new
        al_ref[slot][...] = jnp.exp2(m - m_new)
        lag_ref[...] = (mt - m) if first else jnp.maximum(lag_ref[...], mt - m)

    def stage_pv_lagged(t, slot):
        acc_ref[...] = (acc_ref[...] + jnp.dot(vT_ref[0, 0, t], p_ref[slot][...],
                                               preferred_element_type=F32)) * al_ref[slot][...]

    def lagged_step(t, slot, tbl_tile, prev=None):
        stage_lagged(t, slot, tbl_tile)
        stage_pv_lagged(t - 1 if prev is None else prev, 1 - slot)

    if selected:
        n = qt + 1

        @pl.when((n >= 4) & ((n & 1) == 0))
        def _():
            stage_lagged(0, 1, None, first=True)
            lagged_step(1, 0, None)

        @pl.when((n >= 4) & ((n & 1) == 1))
        def _():
            stage_lagged(0, 0, None, first=True)

        @pl.when(n >= 4)
        def _():
            t0 = 2 - (n & 1)

            pairs = (n - 2 - t0) >> 1

            def pair(t):
                lagged_step(t, 1, None)
                lagged_step(t + 1, 0, None)

            def body(j, carry):
                pair(t0 + 4 * j)
                pair(t0 + 4 * j + 2)
                return carry
            lax.fori_loop(0, pairs >> 1, body, 0)

            @pl.when((pairs & 1) == 1)
            def _():
                pair(n - 4)

        @pl.when(n >= 4)
        def _():
            lagged_step(n - 2, 1, 0)
            lagged_step(n - 1, 0, 1)
            stage_pv_lagged(n - 1, 0)

            @pl.when(jnp.max(lag_ref[...]) > MAX_LAG)
            def _():
                m_ref[...] = jnp.full((1, nq), NEG, F32)
                acc_ref[...] = jnp.zeros((V_ROWS, nq), F32)

                def redo(t, carry):
                    serial(t, None)
                    return carry
                lax.fori_loop(0, n - 2, redo, 0)
                serial(n - 2, 0)
                serial(n - 1, 1)

        @pl.when(n == 3)
        def _():
            serial(0, None)

        @pl.when((n >= 2) & (n < 4))
        def _():
            serial(qt - 1, 0)

        @pl.when(n < 4)
        def _():
            serial(qt, 1)
    else:
        @pl.when(qt >= last)
        def _():
            stage_lagged(qt, 0, 2, first=True)
            lagged_step(qt - 1, 1, 1, prev=qt)
            lagged_step(qt - 2, 0, 0, prev=qt - 1)
            stage_pv_lagged(qt - 2, 0)

            @pl.when(jnp.max(lag_ref[...]) > MAX_LAG)
            def _():
                m_ref[...] = jnp.full((1, nq), NEG, F32)
                acc_ref[...] = jnp.zeros((V_ROWS, nq), F32)
                for j in range(n_tbl_tiles):
                    serial(qt - last + j, j)

        @pl.when(qt == 1)
        def _():
            serial(0, 1)

        @pl.when(qt < last)
        def _():
            serial(qt, 2)

    acc = acc_ref[...]
    o = acc[0:HEAD_DIM] / jnp.maximum(acc[HEAD_DIM:HEAD_DIM + 1], 1e-30)
    for r in range(GQA_REP):
        o_ref[0, r * HEAD_DIM:(r + 1) * HEAD_DIM, :] = (o[:, r * tq:(r + 1) * tq]
                                                        * gs_ref[0, 0, 0, r:r + 1, :]).astype(BF16)


def _out_kernel(oc_ref, os_ref, ow_ref, za_ref, ch_ref, cb_ref, cc_ref, zc_ref, hh_ref, hc_ref,
                x_ref, cw_ref, wo_ref, fw_ref, out_ref, *, tm):
    st = pl.program_id(1)
    f32 = lambda ref: ref[0].astype(F32)
    attn = (f32(oc_ref) + f32(os_ref) + f32(ow_ref)).T * _silu(f32(za_ref))

    u = f32(cc_ref) * f32(ch_ref)
    halo = jnp.where(st == 0, 0.0, f32(hc_ref) * f32(hh_ref))
    row = lax.broadcasted_iota(jnp.int32, u.shape, 0)
    h1, h2 = halo[HALO_ROWS - 1:HALO_ROWS], halo[HALO_ROWS - 2:HALO_ROWS - 1]
    u1 = jnp.where(row == 0, h1, pltpu.roll(u, 1, 0))
    u2 = jnp.where(row == 0, h2, jnp.where(row == 1, h1, pltpu.roll(u, 2, 0)))
    y = cw_ref[0:1] * u2 + cw_ref[1:2] * u1 + cw_ref[2:3] * u
    conv = f32(cb_ref) * y * _silu(f32(zc_ref))

    mixed = jnp.concatenate([attn, conv], axis=1).astype(BF16)
    y2 = x_ref[0] + jnp.dot(mixed, wo_ref[...], preferred_element_type=F32)
    ms = jnp.mean(y2 * y2, axis=-1, keepdims=True)
    out_ref[0] = y2 * lax.rsqrt(ms + EPS) * fw_ref[...]


def _overlap_t(nsel, nch):
    c = np.arange(nch)[None, :] * CMP_STRIDE
    n = np.arange(nsel)[:, None] * SEL_BLOCK
    return ((c < n + SEL_BLOCK) & (c + CMP_BLOCK > n)).astype(np.float32)


def _cparams(sem):
    return pltpu.CompilerParams(dimension_semantics=sem, vmem_limit_bytes=VMEM_LIMIT)


def kernel(x, norm_w, w_in, w_ck1, w_ck2, pe_k, w_cv1, w_cv2, pe_v, conv_w, w_out, rel_bias, final_norm_w):
    b, s, _ = x.shape
    assert norm_w.shape[0] == 1 and s % ROW_TILE == 0 and ROW_TILE % KEY_TILE == 0
    assert WINDOW == 2 * Q_TILE and KEY_TILE == Q_TILE
    tm = ROW_TILE
    nt = s // tm
    nkt = s // KEY_TILE
    kpt = tm // KEY_TILE
    nch = s // CMP_STRIDE
    nsel = s // SEL_BLOCK
    k_top = min(N_SEL, nsel)
    kd = -(-(HEAD_DIM + nsel) // LANES) * LANES
    G, R = N_KV_HEADS, GQA_REP

    w = w_in[0].astype(BF16)
    c0 = D_ATTN + 6 * D_KV
    w_all = jnp.concatenate([w[:, :c0], w[:, c0 + N_GATE:], w[:, c0:c0 + N_GATE],
                             jnp.zeros((D_MODEL, LANES - N_GATE), BF16)], axis=1)
    eye = jnp.eye(G, dtype=F32)

    def expand1(w1):
        w1 = w1.astype(BF16).reshape(2, CMP_STRIDE, HEAD_DIM, CMP_HIDDEN)
        z = jnp.zeros_like(w1)
        rows = [jnp.concatenate([w1 if a == g else z for a in range(G)], axis=-1) for g in range(G)]
        return jnp.stack(rows, axis=2).reshape(2, CMP_STRIDE * D_KV, G * CMP_HIDDEN)

    def expand_pe(pe):
        pe = pe.reshape(2, CMP_STRIDE, 1, HEAD_DIM)
        return jnp.broadcast_to(pe, (2, CMP_STRIDE, G, HEAD_DIM)).reshape(2, 1, CMP_STRIDE * D_KV)

    def expand2(w2):
        return jnp.einsum('hd,ab->ahbd', w2, eye).reshape(G * CMP_HIDDEN, D_KV)

    w1e = jnp.stack([expand1(w_ck1[0]), expand1(w_cv1[0])]).astype(BF16)
    pee = jnp.stack([expand_pe(pe_k[0]), expand_pe(pe_v[0])])
    w2e = jnp.stack([expand2(w_ck2[0]), expand2(w_cv2[0])]).astype(BF16)
    rel_bias = rel_bias.astype(F32)
    smem = pl.BlockSpec(memory_space=pltpu.SMEM)

    n_rest = D_ATTN + 4 * D_CONV
    cw = CMP_STRIDE * D_KV
    qT, kvc, ksel, vselT, kwin, vwinT, gsT, rest = pl.pallas_call(
        partial(_proj_kernel, tm=tm, nsel=nsel, kd=kd),
        grid=(b, nt),
        in_specs=[pl.BlockSpec((1, tm, D_MODEL), lambda i, j: (i, j, 0)),
                  pl.BlockSpec((1, D_MODEL), lambda i, j: (0, 0)),
                  pl.BlockSpec((D_MODEL, _C_END), lambda i, j: (0, 0))],
        out_specs=[pl.BlockSpec((1, D_ATTN, tm), lambda i, j: (i, 0, j)),
                   pl.BlockSpec((2, 1, tm // CMP_STRIDE, cw), lambda i, j: (0, i, j, 0)),
                   pl.BlockSpec((1, G, tm, kd), lambda i, j: (i, 0, j, 0)),
                   pl.BlockSpec((1, G, kpt, V_ROWS, KEY_TILE), lambda i, j: (i, 0, j, 0, 0)),
                   pl.BlockSpec((1, G, tm, LANES), lambda i, j: (i, 0, j, 0)),
                   pl.BlockSpec((1, G, kpt, V_ROWS, KEY_TILE), lambda i, j: (i, 0, j, 0, 0)),
                   pl.BlockSpec((1, N_GATE, tm), lambda i, j: (i, 0, j)),
                   pl.BlockSpec((1, tm, n_rest), lambda i, j: (i, j, 0))],
        out_shape=[jax.ShapeDtypeStruct((b, D_ATTN, s), BF16),
                   jax.ShapeDtypeStruct((2, b, nch, cw), F32),
                   jax.ShapeDtypeStruct((b, G, s, kd), BF16),
                   jax.ShapeDtypeStruct((b, G, nkt, V_ROWS, KEY_TILE), BF16),
                   jax.ShapeDtypeStruct((b, G, s, LANES), BF16),
                   jax.ShapeDtypeStruct((b, G, nkt, V_ROWS, KEY_TILE), BF16),
                   jax.ShapeDtypeStruct((b, N_GATE, s), F32),
                   jax.ShapeDtypeStruct((b, s, n_rest), BF16)],
        scratch_shapes=[pltpu.VMEM((2, tm, D_KV), F32)],
        compiler_params=_cparams(("parallel", "parallel")),
        name="in_proj",
    )(x, norm_w, w_all)
    gs = gsT.reshape(b, 3, G, R, s)

    lhs_rows = V_ROWS + nsel
    lhs_tail = jnp.asarray(np.concatenate([np.ones((V_ROWS - HEAD_DIM, nch), np.float32),
                                           _overlap_t(nsel, nch)], axis=0), BF16)
    kc_aug, c_lhs = pl.pallas_call(
        partial(_compress_kernel, nch=nch),
        grid=(2, b),
        in_specs=[pl.BlockSpec((1, 1, nch, cw), lambda xk, i: (xk, i, 0, 0)),
                  pl.BlockSpec((1, 2, 1, cw), lambda xk, i: (xk, 0, 0, 0)),
                  pl.BlockSpec((1, 2, cw, G * CMP_HIDDEN), lambda xk, i: (xk, 0, 0, 0)),
                  pl.BlockSpec((1, G * CMP_HIDDEN, D_KV), lambda xk, i: (xk, 0, 0)),
                  pl.BlockSpec((lhs_rows - HEAD_DIM, nch), lambda xk, i: (0, 0))],
        out_specs=[pl.BlockSpec((1, 1, G, nch, LANES), lambda xk, i: (xk, i, 0, 0, 0)),
                   pl.BlockSpec((1, 1, G, lhs_rows, nch), lambda xk, i: (xk, i, 0, 0, 0))],
        out_shape=[jax.ShapeDtypeStruct((2, b, G, nch, LANES), BF16),
                   jax.ShapeDtypeStruct((2, b, G, lhs_rows, nch), BF16)],
        compiler_params=_cparams(("parallel", "parallel")),
        name="compress",
    )(kvc, pee, w1e, w2e, lhs_tail)

    tq3 = Q_TILE_CMP
    cpq = tq3 // CMP_STRIDE
    assert nch // 8 <= LANES - HEAD_DIM and k_top > 3 and (nch // 2) % LANES == 0
    ocT, selb = pl.pallas_call(
        partial(_cmp_kernel, nch=nch, nsel=nsel, tq=tq3, k_top=k_top),
        grid=(b, G, s // tq3),
        in_specs=[smem,
                  pl.BlockSpec((1, R * HEAD_DIM, tq3), lambda i, g, t: (i, g, t)),
                  pl.BlockSpec((1, 1, 1, nch, LANES), lambda i, g, t: (0, i, g, 0, 0)),
                  pl.BlockSpec((1, 1, 1, lhs_rows, nch), lambda i, g, t: (1, i, g, 0, 0)),
                  pl.BlockSpec((1, 1, 1, R, tq3), lambda i, g, t: (i, 0, g, 0, t))],
        out_specs=[pl.BlockSpec((1, R * HEAD_DIM, tq3), lambda i, g, t: (i, g, t)),
                   pl.BlockSpec((1, 1, nsel, tq3), lambda i, g, t: (i, g, 0, t))],
        out_shape=[jax.ShapeDtypeStruct((b, D_ATTN, s), BF16),
                   jax.ShapeDtypeStruct((b, G, nsel, s), BF16)],
        scratch_shapes=[pltpu.VMEM((LANES, R * tq3), BF16),
                        pltpu.VMEM((cpq + 16, R * tq3), F32),
                        pltpu.VMEM((lhs_rows, R * tq3), F32)]
                       + [pltpu.VMEM((nch, tq3), F32)] * R,
        compiler_params=_cparams(("parallel", "parallel", "arbitrary")),
        name="cmp_select",
    )(rel_bias, qT, kc_aug, c_lhs, gs)

    tq = Q_TILE
    nq = R * tq

    def flash(selected, k_arr, vT_arr, kdim, n_tbl, window, branch, name):
        in_specs = [smem, pl.BlockSpec((1, R * HEAD_DIM, tq), lambda i, g, t: (i, g, t))]
        args = [rel_bias, qT]
        if selected:
            in_specs.append(pl.BlockSpec((1, 1, nsel, tq), lambda i, g, t: (i, g, 0, t)))
            args.append(selb)
        in_specs += [pl.BlockSpec((1, 1, s, kdim), lambda i, g, t: (i, g, 0, 0)),
                     pl.BlockSpec((1, 1, nkt, V_ROWS, KEY_TILE), lambda i, g, t: (i, g, 0, 0, 0)),
                     pl.BlockSpec((1, 1, 1, R, tq), lambda i, g, t: (i, branch, g, 0, t))]
        args += [k_arr, vT_arr, gs]
        return pl.pallas_call(
            partial(_flash_kernel, tq=tq, nsel=nsel, kd=kdim, n_tbl_tiles=n_tbl, window=window, selected=selected),
            grid=(b, G, s // tq),
            in_specs=in_specs,
            out_specs=pl.BlockSpec((1, R * HEAD_DIM, tq), lambda i, g, t: (i, g, t)),
            out_shape=jax.ShapeDtypeStruct((b, D_ATTN, s), BF16),
            scratch_shapes=[pltpu.VMEM((n_tbl * tq, nq), F32),
                            pltpu.VMEM((kdim, nq), BF16),
                            pltpu.VMEM((1, nq), F32),
                            pltpu.VMEM((V_ROWS, nq), F32),
                            pltpu.VMEM((1, nq), F32)]
                           + [pltpu.VMEM((tq, nq), F32)] * N_SLOTS
                           + [pltpu.VMEM((1, nq), F32)] * N_SLOTS
                           + [pltpu.VMEM((1, nq), F32)] * N_SLOTS
                           + [pltpu.VMEM((tq, nq), BF16)] * N_SLOTS,
            compiler_params=_cparams(("parallel", "parallel", "arbitrary")),
            name=name,
        )(*args)

    osT = flash(True, ksel, vselT, kd, 2, None, 1, "sel_attn")
    owT = flash(False, kwin, vwinT, LANES, 3, WINDOW, 2, "win_attn")

    hb = tm // HALO_ROWS
    col = lambda c: pl.BlockSpec((1, tm, D_CONV), lambda i, j, c=c: (i, j, c))
    halo = lambda c: pl.BlockSpec((1, HALO_ROWS, D_CONV), lambda i, j, c=c: (i, jnp.maximum(j * hb - 1, 0), c))
    oT = pl.BlockSpec((1, D_ATTN, tm), lambda i, j: (i, 0, j))
    out = pl.pallas_call(
        partial(_out_kernel, tm=tm),
        grid=(b, nt),
        in_specs=[oT, oT, oT, col(0), col(1), col(2), col(3), col(4), halo(1), halo(3),
                  pl.BlockSpec((1, tm, D_MODEL), lambda i, j: (i, j, 0)),
                  pl.BlockSpec((3, D_CONV), lambda i, j: (0, 0)),
                  pl.BlockSpec((D_MODEL, D_MODEL), lambda i, j: (0, 0)),
                  pl.BlockSpec((1, D_MODEL), lambda i, j: (0, 0))],
        out_specs=pl.BlockSpec((1, tm, D_MODEL), lambda i, j: (i, j, 0)),
        out_shape=jax.ShapeDtypeStruct((b, s, D_MODEL), F32),
        compiler_params=_cparams(("parallel", "parallel")),
        name="out_proj",
    )(ocT, osT, owT, rest, rest, rest, rest, rest, rest, rest, x, conv_w[0], w_out[0].astype(BF16),
      final_norm_w.reshape(1, D_MODEL))
    return out
```

```python
import math
from functools import partial

import numpy as np
import jax
import jax.numpy as jnp
from jax import lax
from jax.experimental import pallas as pl
from jax.experimental.pallas import tpu as pltpu

D_MODEL = 1024
N_HEADS = 8
HEAD_DIM = 64
N_KV_HEADS = 2
GQA_REP = N_HEADS // N_KV_HEADS
D_ATTN = N_HEADS * HEAD_DIM
D_KV = N_KV_HEADS * HEAD_DIM
D_CONV = D_MODEL - D_ATTN
CMP_BLOCK = 32
CMP_STRIDE = 16
CMP_HIDDEN = 2 * HEAD_DIM
SEL_BLOCK = 64
N_SEL = 16
WINDOW = 512
N_BUCKETS = 32
MAX_DISTANCE = 128
EPS = 1e-6
NEG = -1e30
FORCE_SCORE = 1e3

LOG2E = 1.4426950408889634
LANES = 128
MXU_COLS = 256
V_ROWS = HEAD_DIM + 16
ROW_TILE = 512
KEY_TILE = 256
Q_TILE_CMP = 512
Q_TILE = 256
HALO_ROWS = 16
MAX_LAG = 64.0
N_SLOTS = 2
VMEM_LIMIT = 48 * 1024 * 1024

F32 = jnp.float32
BF16 = jnp.bfloat16

_C_Q = 0
_C_CMP = D_ATTN
_C_SW = _C_CMP + 2 * D_KV
_C_REST = _C_SW + 4 * D_KV
_C_GATE = _C_REST + D_ATTN + 4 * D_CONV
_C_END = _C_GATE + LANES
N_GATE = 3 * N_HEADS


def _bucket_lower_bounds():
    max_exact = N_BUCKETS // 2
    d = np.arange(2 * MAX_DISTANCE, dtype=np.int32)
    nf = np.maximum(d, 1).astype(np.float32)
    large = max_exact + (np.log(nf / np.float32(max_exact)) / np.float32(math.log(MAX_DISTANCE / max_exact))
                         * np.float32(N_BUCKETS - max_exact)).astype(np.int32)
    bucket = np.where(d < max_exact, d, np.minimum(large, N_BUCKETS - 1))
    assert np.all(np.diff(bucket) >= 0) and np.all(bucket[MAX_DISTANCE:] == N_BUCKETS - 1)
    return [int(np.argmax(bucket >= b)) for b in range(N_BUCKETS)]


_BUCKET_LO = _bucket_lower_bounds()


def _rel_bias_delta(dist, bias_of_bucket):
    last = bias_of_bucket(N_BUCKETS - 1)
    val = jnp.full(dist.shape, (bias_of_bucket(0) - last) * LOG2E, F32)
    for b in range(1, N_BUCKETS - 1):
        val = jnp.where(dist >= _BUCKET_LO[b], (bias_of_bucket(b) - last) * LOG2E, val)
    val = jnp.where(dist >= _BUCKET_LO[N_BUCKETS - 1], 0.0, val)
    return jnp.where(dist < 0, NEG, val)


def _silu(z):
    return z * jax.nn.sigmoid(z)


def _proj_kernel(x_ref, nw_ref, w_ref, qT_ref, kvc_ref, ksel_ref, vselT_ref, kwin_ref, vwinT_ref,
                 gsT_ref, rest_ref, kv_scr, *, tm, nsel, kd):
    st = pl.program_id(1)
    x = x_ref[0]
    ms = jnp.mean(x * x, axis=-1, keepdims=True)
    h = (x * lax.rsqrt(ms + EPS) * nw_ref[...]).astype(BF16)

    def mm(lo, hi):
        return jnp.dot(h, w_ref[:, lo:hi], preferred_element_type=F32)

    q = mm(_C_Q, _C_CMP) * (HEAD_DIM ** -0.5 * LOG2E)
    qT_ref[0] = q.T.astype(BF16)

    kvc = mm(_C_CMP, _C_SW)
    for xk in range(2):
        kv_scr[xk] = kvc[:, xk * D_KV:(xk + 1) * D_KV]
        for j in range(CMP_STRIDE):
            kvc_ref[xk, 0, :, j * D_KV:(j + 1) * D_KV] = kv_scr[xk, pl.ds(j, tm // CMP_STRIDE, stride=CMP_STRIDE), :]

    a = mm(_C_SW, _C_REST)
    pos = st * tm + lax.broadcasted_iota(jnp.int32, (tm, nsel), 0)
    col = lax.broadcasted_iota(jnp.int32, (tm, nsel), 1)
    onehot = jnp.where((pos // SEL_BLOCK) == col, 1.0, 0.0).astype(BF16)
    ones = jnp.ones((V_ROWS - HEAD_DIM, tm), BF16)
    zeros_k = jnp.zeros((tm, LANES - HEAD_DIM), BF16)
    vsT = a[:, D_KV:2 * D_KV].T
    vwT = a[:, 3 * D_KV:4 * D_KV].T
    for g in range(N_KV_HEADS):
        lo, hi = g * HEAD_DIM, (g + 1) * HEAD_DIM
        parts = [a[:, lo:hi].astype(BF16), onehot]
        if kd > HEAD_DIM + nsel:
            parts.append(jnp.zeros((tm, kd - HEAD_DIM - nsel), BF16))
        ksel_ref[0, g] = jnp.concatenate(parts, axis=1)
        kwin_ref[0, g] = jnp.concatenate([a[:, 2 * D_KV + lo:2 * D_KV + hi].astype(BF16), zeros_k], axis=1)
        vs = jnp.concatenate([vsT[lo:hi].astype(BF16), ones], axis=0)
        vw = jnp.concatenate([vwT[lo:hi].astype(BF16), ones], axis=0)
        for kt in range(tm // KEY_TILE):
            vselT_ref[0, g, kt] = vs[:, kt * KEY_TILE:(kt + 1) * KEY_TILE]
            vwinT_ref[0, g, kt] = vw[:, kt * KEY_TILE:(kt + 1) * KEY_TILE]

    rest_ref[0] = mm(_C_REST, _C_GATE).astype(BF16)
    gates = jax.nn.sigmoid(mm(_C_GATE, _C_END))
    gsT_ref[0] = gates.T[:N_GATE]


def _compress_kernel(c_ref, pe_ref, w1_ref, w2_ref, lhs_tail_ref, aug_ref, t_ref, *, nch):
    c = c_ref[0, 0]
    a0 = jnp.dot((c + pe_ref[0, 0]).astype(BF16), w1_ref[0, 0], preferred_element_type=F32)
    a1 = jnp.dot((c + pe_ref[0, 1]).astype(BF16), w1_ref[0, 1], preferred_element_type=F32)
    hid = a0 + pltpu.roll(a1, nch - 1, 0)
    out = jnp.dot(_silu(hid).astype(BF16), w2_ref[0], preferred_element_type=F32)
    grp = lax.broadcasted_iota(jnp.int32, (nch, LANES - HEAD_DIM), 0) // 8
    col = lax.broadcasted_iota(jnp.int32, (nch, LANES - HEAD_DIM), 1)
    onehot = jnp.where(grp == col, 1.0, 0.0).astype(BF16)
    out_t = out.T.astype(BF16)
    for g in range(N_KV_HEADS):
        lo, hi = g * HEAD_DIM, (g + 1) * HEAD_DIM
        aug_ref[0, 0, g] = jnp.concatenate([out[:, lo:hi].astype(BF16), onehot], axis=1)
        t_ref[0, 0, g] = jnp.concatenate([out_t[lo:hi], lhs_tail_ref[...]], axis=0)


def _cmp_kernel(rb_ref, qT_ref, kc_ref, lhs_ref, gs_ref, ocT_ref, selb_ref,
                qa_ref, band_ref, acc_ref, s0_ref, s1_ref, s2_ref, s3_ref, *, nch, nsel, tq, k_top):
    g = pl.program_id(1)
    qt = pl.program_id(2)
    s_refs = (s0_ref, s1_ref, s2_ref, s3_ref)
    cpq = tq // CMP_STRIDE
    win = cpq + 16
    lim = cpq * (qt + 1)
    nq = GQA_REP * tq

    @pl.when(qt <= 1)
    def _():
        row = lax.broadcasted_iota(jnp.int32, (win, tq), 0) + jnp.maximum(cpq * qt - 16, 0)
        lane = lax.broadcasted_iota(jnp.int32, (win, tq), 1)
        dist = (qt * tq + lane) - (row * CMP_STRIDE + CMP_BLOCK - 1)
        for r in range(GQA_REP):
            band_ref[:, r * tq:(r + 1) * tq] = _rel_bias_delta(dist, lambda b, r=r: rb_ref[b, g * GQA_REP + r])

    q = qT_ref[0]
    qa_ref[0:HEAD_DIM, :] = jnp.concatenate([q[r * HEAD_DIM:(r + 1) * HEAD_DIM] for r in range(GQA_REP)], axis=1)
    grp = lax.broadcasted_iota(jnp.int32, (LANES - HEAD_DIM, nq), 0)
    qa_ref[HEAD_DIM:LANES, :] = jnp.where(grp * 8 >= lim, NEG, 0.0).astype(BF16)

    def run(rows):
        start = pl.multiple_of(jnp.clip(cpq * qt - 16, 0, rows - win), 8)
        k = kc_ref[0, 0, 0, 0:rows, :]
        lhs = lhs_ref[0, 0, 0, :, 0:rows]
        cols = [slice(r * tq, (r + 1) * tq) for r in range(GQA_REP)]

        def qk(r):
            s_refs[r][0:rows, :] = jnp.dot(k, qa_ref[:, cols[r]], preferred_element_type=F32)

        qk(0)
        for r in range(GQA_REP):
            if r + 1 < GQA_REP:
                qk(r + 1)
            sr = s_refs[r]
            sr[pl.ds(start, win), :] = sr[pl.ds(start, win), :] + band_ref[:, cols[r]]
            s = sr[0:rows, :]
            p = jnp.exp2(s - jnp.max(s, axis=0, keepdims=True)).astype(BF16)
            acc_ref[:, cols[r]] = jnp.dot(lhs, p, preferred_element_type=F32)

    half = nch // 2

    @pl.when(lim <= half)
    def _():
        run(half)

    @pl.when(lim > half)
    def _():
        run(nch)

    l = acc_ref[HEAD_DIM:HEAD_DIM + 1, :]
    t4 = qt * tq + (lax.broadcasted_iota(jnp.int32, l.shape, 1) & (tq - 1))
    inv = jnp.where(t4 >= CMP_BLOCK - 1, 1.0 / jnp.maximum(l, 1e-30), 0.0)
    imp = None
    for r in range(GQA_REP):
        cs = slice(r * tq, (r + 1) * tq)
        ocT_ref[0, r * HEAD_DIM:(r + 1) * HEAD_DIM, :] = (acc_ref[0:HEAD_DIM, cs] * inv[:, cs]
                                                          * gs_ref[0, 0, 0, r:r + 1, :]).astype(BF16)
        part = acc_ref[V_ROWS:V_ROWS + nsel, cs] * inv[:, cs]
        imp = part if imp is None else imp + part

    n = lax.broadcasted_iota(jnp.int32, (nsel, tq), 0)
    blk = (qt * tq + lax.broadcasted_iota(jnp.int32, (nsel, tq), 1)) // SEL_BLOCK
    forced = (n == 0) | (n == blk) | (n == blk - 1)
    candidate = (n <= blk) & jnp.logical_not(forced)
    score = jnp.where(forced, -2.0, jnp.where(n <= blk, imp, -1.0))

    def pick_all(_, score):
        return jnp.where(score == jnp.max(score, axis=0, keepdims=True), -2.0, score)

    taken = lax.fori_loop(0, k_top - 3, pick_all, score) == -2.0
    selb_ref[0, 0] = jnp.where(taken, 0.0, NEG).astype(BF16)
    n_taken = jnp.sum(jnp.where(taken & candidate, 1.0, 0.0), axis=0, keepdims=True)

    @pl.when(jnp.max(n_taken) > k_top - 3)
    def _():
        def pick_one(_, score):
            mx = jnp.max(score, axis=0, keepdims=True)
            idx = jnp.min(jnp.where(score == mx, n, nsel), axis=0, keepdims=True)
            return jnp.where(n == idx, -2.0, score)

        taken = lax.fori_loop(0, k_top - 3, pick_one, score) == -2.0
        selb_ref[0, 0] = jnp.where(taken, 0.0, NEG).astype(BF16)


def _flash_kernel(*refs, tq, nsel, kd, n_tbl_tiles, window, selected):
    if selected:
        rb_ref, qT_ref, selb_ref, k_ref, vT_ref, gs_ref, o_ref, tbl_ref, qa_ref, m_ref, acc_ref, lag_ref, *slot_refs = refs
    else:
        rb_ref, qT_ref, k_ref, vT_ref, gs_ref, o_ref, tbl_ref, qa_ref, m_ref, acc_ref, lag_ref, *slot_refs = refs
    s_ref, mt_ref, al_ref, p_ref = (slot_refs[i * N_SLOTS:(i + 1) * N_SLOTS] for i in range(4))
    g = pl.program_id(1)
    qt = pl.program_id(2)
    nq = GQA_REP * tq

    @pl.when(qt == 0)
    def _():
        for j in range(n_tbl_tiles):
            key = lax.broadcasted_iota(jnp.int32, (tq, tq), 0) + j * tq
            qry = lax.broadcasted_iota(jnp.int32, (tq, tq), 1) + (n_tbl_tiles - 1) * tq
            dist = qry - key
            for r in range(GQA_REP):
                val = _rel_bias_delta(dist, lambda b, r=r: rb_ref[b, g * GQA_REP + r])
                if window is not None:
                    val = jnp.where(dist >= window, NEG, val)
                tbl_ref[j * tq:(j + 1) * tq, r * tq:(r + 1) * tq] = val

    q = qT_ref[0]
    qa_ref[0:HEAD_DIM, :] = jnp.concatenate([q[r * HEAD_DIM:(r + 1) * HEAD_DIM] for r in range(GQA_REP)], axis=1)
    used = HEAD_DIM
    if selected:
        qa_ref[HEAD_DIM:HEAD_DIM + nsel, :] = jnp.concatenate([selb_ref[0, 0]] * GQA_REP, axis=1)
        used += nsel
    if kd > used:
        qa_ref[used:kd, :] = jnp.zeros((kd - used, nq), BF16)
    m_ref[...] = jnp.full((1, nq), NEG, F32)
    acc_ref[...] = jnp.zeros((V_ROWS, nq), F32)

    def stage_qk(t, slot, tbl_tile):
        k = k_ref[0, 0, pl.ds(pl.multiple_of(t * tq, tq), tq), :]
        s = jnp.dot(k, qa_ref[...], preferred_element_type=F32)
        if tbl_tile is not None:
            s = s + tbl_ref[tbl_tile * tq:(tbl_tile + 1) * tq, :]
        s_ref[slot][...] = s
        mt_ref[slot][...] = jnp.max(s, axis=0, keepdims=True)

    def stage_exp(slot):
        m_old = m_ref[...]
        m_new = jnp.maximum(m_old, mt_ref[slot][...])
        m_ref[...] = m_new
        al_ref[slot][...] = jnp.exp2(m_old - m_new)
        p_ref[slot][...] = jnp.exp2(s_ref[slot][...] - m_new).astype(BF16)

    def stage_pv(t, slot):
        acc_ref[...] = al_ref[slot][...] * acc_ref[...] + jnp.dot(vT_ref[0, 0, t], p_ref[slot][...],
                                                                  preferred_element_type=F32)

    def serial(t, tbl_tile):
        stage_qk(t, 0, tbl_tile)
        stage_exp(0)
        stage_pv(t, 0)

    last = n_tbl_tiles - 1

    def stage_lagged(t, slot, tbl_tile, first=False):
        k = k_ref[0, 0, pl.ds(pl.multiple_of(t * tq, tq), tq), :]
        s = jnp.dot(k, qa_ref[...], preferred_element_type=F32)
        if tbl_tile is not None:
            s = s + tbl_ref[tbl_tile * tq:(tbl_tile + 1) * tq, :]
        m = s[0:1, :] if first else m_ref[...]
        mt = jnp.max(s, axis=0, keepdims=True)
        p_ref[slot][...] = jnp.exp2((s - m).astype(BF16))
        m_new = jnp.maximum(m, mt)
        m_ref[...] = m_new
        al_ref[slot][...] = jnp.exp2(m - m_new)
        lag_ref[...] = (mt - m) if first else jnp.maximum(lag_ref[...], mt - m)

    def stage_pv_lagged(t, slot):
        acc_ref[...] = (acc_ref[...] + jnp.dot(vT_ref[0, 0, t], p_ref[slot][...],
                                               preferred_element_type=F32)) * al_ref[slot][...]

    def lagged_step(t, slot, tbl_tile, prev=None):
        stage_lagged(t, slot, tbl_tile)
        stage_pv_lagged(t - 1 if prev is None else prev, 1 - slot)

    if selected:
        n = qt + 1

        @pl.when((n >= 4) & ((n & 1) == 0))
        def _():
            stage_lagged(0, 1, None, first=True)
            lagged_step(1, 0, None)

        @pl.when((n >= 4) & ((n & 1) == 1))
        def _():
            stage_lagged(0, 0, None, first=True)

        @pl.when(n >= 4)
        def _():
            t0 = 2 - (n & 1)

            pairs = (n - 2 - t0) >> 1

            def pair(t):
                lagged_step(t, 1, None)
                lagged_step(t + 1, 0, None)

            def body(j, carry):
                pair(t0 + 4 * j)
                pair(t0 + 4 * j + 2)
                return carry
            lax.fori_loop(0, pairs >> 1, body, 0)

            @pl.when((pairs & 1) == 1)
            def _():
                pair(n - 4)

        @pl.when(n >= 4)
        def _():
            lagged_step(n - 2, 1, 0)
            lagged_step(n - 1, 0, 1)
            stage_pv_lagged(n - 1, 0)

            @pl.when(jnp.max(lag_ref[...]) > MAX_LAG)
            def _():
                m_ref[...] = jnp.full((1, nq), NEG, F32)
                acc_ref[...] = jnp.zeros((V_ROWS, nq), F32)

                def redo(t, carry):
                    serial(t, None)
                    return carry
                lax.fori_loop(0, n - 2, redo, 0)
                serial(n - 2, 0)
                serial(n - 1, 1)

        @pl.when(n == 3)
        def _():
            serial(0, None)

        @pl.when((n >= 2) & (n < 4))
        def _():
            serial(qt - 1, 0)

        @pl.when(n < 4)
        def _():
            serial(qt, 1)
    else:
        @pl.when(qt >= last)
        def _():
            stage_lagged(qt, 0, 2, first=True)
            lagged_step(qt - 1, 1, 1, prev=qt)
            lagged_step(qt - 2, 0, 0, prev=qt - 1)
            stage_pv_lagged(qt - 2, 0)

            @pl.when(jnp.max(lag_ref[...]) > MAX_LAG)
            def _():
                m_ref[...] = jnp.full((1, nq), NEG, F32)
                acc_ref[...] = jnp.zeros((V_ROWS, nq), F32)
                for j in range(n_tbl_tiles):
                    serial(qt - last + j, j)

        @pl.when(qt == 1)
        def _():
            serial(0, 1)

        @pl.when(qt < last)
        def _():
            serial(qt, 2)

    acc = acc_ref[...]
    o = acc[0:HEAD_DIM] / jnp.maximum(acc[HEAD_DIM:HEAD_DIM + 1], 1e-30)
    for r in range(GQA_REP):
        o_ref[0, r * HEAD_DIM:(r + 1) * HEAD_DIM, :] = (o[:, r * tq:(r + 1) * tq]
                                                        * gs_ref[0, 0, 0, r:r + 1, :]).astype(BF16)


SEL_TBL_TILES = 2
WIN_TBL_TILES = 3


def _branches_kernel(rb_ref, qT_ref, selb_ref, ksel_ref, vselT_ref, kwin_ref, vwinT_ref, gs_sel_ref, gs_win_ref,
                     osel_ref, owin_ref, tbl_ref, qa_ref, *state, tq, nsel, kd):
    win_tbl = tbl_ref.at[SEL_TBL_TILES * tq:(SEL_TBL_TILES + WIN_TBL_TILES) * tq]
    _flash_kernel(rb_ref, qT_ref, kwin_ref, vwinT_ref, gs_win_ref, owin_ref, win_tbl, qa_ref.at[0:LANES], *state,
                  tq=tq, nsel=nsel, kd=LANES, n_tbl_tiles=WIN_TBL_TILES, window=WINDOW, selected=False)
    _flash_kernel(rb_ref, qT_ref, selb_ref, ksel_ref, vselT_ref, gs_sel_ref, osel_ref,
                  tbl_ref.at[0:SEL_TBL_TILES * tq], qa_ref, *state,
                  tq=tq, nsel=nsel, kd=kd, n_tbl_tiles=SEL_TBL_TILES, window=None, selected=True)


def _out_kernel(oc_ref, os_ref, ow_ref, za_ref, ch_ref, cb_ref, cc_ref, zc_ref, hh_ref, hc_ref,
                x_ref, cw_ref, wo_ref, fw_ref, out_ref, *, tm):
    st = pl.program_id(1)
    f32 = lambda ref: ref[0].astype(F32)
    attn = (f32(oc_ref) + f32(os_ref) + f32(ow_ref)).T * _silu(f32(za_ref))

    u = f32(cc_ref) * f32(ch_ref)
    halo = jnp.where(st == 0, 0.0, f32(hc_ref) * f32(hh_ref))
    row = lax.broadcasted_iota(jnp.int32, u.shape, 0)
    h1, h2 = halo[HALO_ROWS - 1:HALO_ROWS], halo[HALO_ROWS - 2:HALO_ROWS - 1]
    u1 = jnp.where(row == 0, h1, pltpu.roll(u, 1, 0))
    u2 = jnp.where(row == 0, h2, jnp.where(row == 1, h1, pltpu.roll(u, 2, 0)))
    y = cw_ref[0:1] * u2 + cw_ref[1:2] * u1 + cw_ref[2:3] * u
    conv = f32(cb_ref) * y * _silu(f32(zc_ref))

    mixed = jnp.concatenate([attn, conv], axis=1).astype(BF16)
    y2 = x_ref[0] + jnp.dot(mixed, wo_ref[...], preferred_element_type=F32)
    ms = jnp.mean(y2 * y2, axis=-1, keepdims=True)
    out_ref[0] = y2 * lax.rsqrt(ms + EPS) * fw_ref[...]


def _overlap_t(nsel, nch):
    c = np.arange(nch)[None, :] * CMP_STRIDE
    n = np.arange(nsel)[:, None] * SEL_BLOCK
    return ((c < n + SEL_BLOCK) & (c + CMP_BLOCK > n)).astype(np.float32)


def _cparams(sem):
    return pltpu.CompilerParams(dimension_semantics=sem, vmem_limit_bytes=VMEM_LIMIT)


def kernel(x, norm_w, w_in, w_ck1, w_ck2, pe_k, w_cv1, w_cv2, pe_v, conv_w, w_out, rel_bias, final_norm_w):
    b, s, _ = x.shape
    assert norm_w.shape[0] == 1 and s % ROW_TILE == 0 and ROW_TILE % KEY_TILE == 0
    assert WINDOW == 2 * Q_TILE and KEY_TILE == Q_TILE
    tm = ROW_TILE
    nt = s // tm
    nkt = s // KEY_TILE
    kpt = tm // KEY_TILE
    nch = s // CMP_STRIDE
    nsel = s // SEL_BLOCK
    k_top = min(N_SEL, nsel)
    kd = -(-(HEAD_DIM + nsel) // LANES) * LANES
    G, R = N_KV_HEADS, GQA_REP

    w = w_in[0].astype(BF16)
    c0 = D_ATTN + 6 * D_KV
    w_all = jnp.concatenate([w[:, :c0], w[:, c0 + N_GATE:], w[:, c0:c0 + N_GATE],
                             jnp.zeros((D_MODEL, LANES - N_GATE), BF16)], axis=1)
    eye = jnp.eye(G, dtype=F32)

    def expand1(w1):
        w1 = w1.astype(BF16).reshape(2, CMP_STRIDE, HEAD_DIM, CMP_HIDDEN)
        z = jnp.zeros_like(w1)
        rows = [jnp.concatenate([w1 if a == g else z for a in range(G)], axis=-1) for g in range(G)]
        return jnp.stack(rows, axis=2).reshape(2, CMP_STRIDE * D_KV, G * CMP_HIDDEN)

    def expand_pe(pe):
        pe = pe.reshape(2, CMP_STRIDE, 1, HEAD_DIM)
        return jnp.broadcast_to(pe, (2, CMP_STRIDE, G, HEAD_DIM)).reshape(2, 1, CMP_STRIDE * D_KV)

    def expand2(w2):
        return jnp.einsum('hd,ab->ahbd', w2, eye).reshape(G * CMP_HIDDEN, D_KV)

    w1e = jnp.stack([expand1(w_ck1[0]), expand1(w_cv1[0])]).astype(BF16)
    pee = jnp.stack([expand_pe(pe_k[0]), expand_pe(pe_v[0])])
    w2e = jnp.stack([expand2(w_ck2[0]), expand2(w_cv2[0])]).astype(BF16)
    rel_bias = rel_bias.astype(F32)
    smem = pl.BlockSpec(memory_space=pltpu.SMEM)

    n_rest = D_ATTN + 4 * D_CONV
    cw = CMP_STRIDE * D_KV
    qT, kvc, ksel, vselT, kwin, vwinT, gsT, rest = pl.pallas_call(
        partial(_proj_kernel, tm=tm, nsel=nsel, kd=kd),
        grid=(b, nt),
        in_specs=[pl.BlockSpec((1, tm, D_MODEL), lambda i, j: (i, j, 0)),
                  pl.BlockSpec((1, D_MODEL), lambda i, j: (0, 0)),
                  pl.BlockSpec((D_MODEL, _C_END), lambda i, j: (0, 0))],
        out_specs=[pl.BlockSpec((1, D_ATTN, tm), lambda i, j: (i, 0, j)),
                   pl.BlockSpec((2, 1, tm // CMP_STRIDE, cw), lambda i, j: (0, i, j, 0)),
                   pl.BlockSpec((1, G, tm, kd), lambda i, j: (i, 0, j, 0)),
                   pl.BlockSpec((1, G, kpt, V_ROWS, KEY_TILE), lambda i, j: (i, 0, j, 0, 0)),
                   pl.BlockSpec((1, G, tm, LANES), lambda i, j: (i, 0, j, 0)),
                   pl.BlockSpec((1, G, kpt, V_ROWS, KEY_TILE), lambda i, j: (i, 0, j, 0, 0)),
                   pl.BlockSpec((1, N_GATE, tm), lambda i, j: (i, 0, j)),
                   pl.BlockSpec((1, tm, n_rest), lambda i, j: (i, j, 0))],
        out_shape=[jax.ShapeDtypeStruct((b, D_ATTN, s), BF16),
                   jax.ShapeDtypeStruct((2, b, nch, cw), F32),
                   jax.ShapeDtypeStruct((b, G, s, kd), BF16),
                   jax.ShapeDtypeStruct((b, G, nkt, V_ROWS, KEY_TILE), BF16),
                   jax.ShapeDtypeStruct((b, G, s, LANES), BF16),
                   jax.ShapeDtypeStruct((b, G, nkt, V_ROWS, KEY_TILE), BF16),
                   jax.ShapeDtypeStruct((b, N_GATE, s), F32),
                   jax.ShapeDtypeStruct((b, s, n_rest), BF16)],
        scratch_shapes=[pltpu.VMEM((2, tm, D_KV), F32)],
        compiler_params=_cparams(("parallel", "parallel")),
        name="in_proj",
    )(x, norm_w, w_all)
    gs = gsT.reshape(b, 3, G, R, s)

    lhs_rows = V_ROWS + nsel
    lhs_tail = jnp.asarray(np.concatenate([np.ones((V_ROWS - HEAD_DIM, nch), np.float32),
                                           _overlap_t(nsel, nch)], axis=0), BF16)
    kc_aug, c_lhs = pl.pallas_call(
        partial(_compress_kernel, nch=nch),
        grid=(2, b),
        in_specs=[pl.BlockSpec((1, 1, nch, cw), lambda xk, i: (xk, i, 0, 0)),
                  pl.BlockSpec((1, 2, 1, cw), lambda xk, i: (xk, 0, 0, 0)),
                  pl.BlockSpec((1, 2, cw, G * CMP_HIDDEN), lambda xk, i: (xk, 0, 0, 0)),
                  pl.BlockSpec((1, G * CMP_HIDDEN, D_KV), lambda xk, i: (xk, 0, 0)),
                  pl.BlockSpec((lhs_rows - HEAD_DIM, nch), lambda xk, i: (0, 0))],
        out_specs=[pl.BlockSpec((1, 1, G, nch, LANES), lambda xk, i: (xk, i, 0, 0, 0)),
                   pl.BlockSpec((1, 1, G, lhs_rows, nch), lambda xk, i: (xk, i, 0, 0, 0))],
        out_shape=[jax.ShapeDtypeStruct((2, b, G, nch, LANES), BF16),
                   jax.ShapeDtypeStruct((2, b, G, lhs_rows, nch), BF16)],
        compiler_params=_cparams(("parallel", "parallel")),
        name="compress",
    )(kvc, pee, w1e, w2e, lhs_tail)

    tq3 = Q_TILE_CMP
    cpq = tq3 // CMP_STRIDE
    assert nch // 8 <= LANES - HEAD_DIM and k_top > 3 and (nch // 2) % LANES == 0
    ocT, selb = pl.pallas_call(
        partial(_cmp_kernel, nch=nch, nsel=nsel, tq=tq3, k_top=k_top),
        grid=(b, G, s // tq3),
        in_specs=[smem,
                  pl.BlockSpec((1, R * HEAD_DIM, tq3), lambda i, g, t: (i, g, t)),
                  pl.BlockSpec((1, 1, 1, nch, LANES), lambda i, g, t: (0, i, g, 0, 0)),
                  pl.BlockSpec((1, 1, 1, lhs_rows, nch), lambda i, g, t: (1, i, g, 0, 0)),
                  pl.BlockSpec((1, 1, 1, R, tq3), lambda i, g, t: (i, 0, g, 0, t))],
        out_specs=[pl.BlockSpec((1, R * HEAD_DIM, tq3), lambda i, g, t: (i, g, t)),
                   pl.BlockSpec((1, 1, nsel, tq3), lambda i, g, t: (i, g, 0, t))],
        out_shape=[jax.ShapeDtypeStruct((b, D_ATTN, s), BF16),
                   jax.ShapeDtypeStruct((b, G, nsel, s), BF16)],
        scratch_shapes=[pltpu.VMEM((LANES, R * tq3), BF16),
                        pltpu.VMEM((cpq + 16, R * tq3), F32),
                        pltpu.VMEM((lhs_rows, R * tq3), F32)]
                       + [pltpu.VMEM((nch, tq3), F32)] * R,
        compiler_params=_cparams(("parallel", "parallel", "arbitrary")),
        name="cmp_select",
    )(rel_bias, qT, kc_aug, c_lhs, gs)

    tq = Q_TILE
    nq = R * tq

    o_spec = pl.BlockSpec((1, R * HEAD_DIM, tq), lambda i, g, t: (i, g, t))
    k_spec = lambda kdim: pl.BlockSpec((1, 1, s, kdim), lambda i, g, t: (i, g, 0, 0))
    v_spec = pl.BlockSpec((1, 1, nkt, V_ROWS, KEY_TILE), lambda i, g, t: (i, g, 0, 0, 0))
    gs_spec = lambda branch: pl.BlockSpec((1, 1, 1, R, tq), lambda i, g, t: (i, branch, g, 0, t))
    osT, owT = pl.pallas_call(
        partial(_branches_kernel, tq=tq, nsel=nsel, kd=kd),
        grid=(b, G, s // tq),
        in_specs=[smem, o_spec, pl.BlockSpec((1, 1, nsel, tq), lambda i, g, t: (i, g, 0, t)),
                  k_spec(kd), v_spec, k_spec(LANES), v_spec, gs_spec(1), gs_spec(2)],
        out_specs=[o_spec, o_spec],
        out_shape=[jax.ShapeDtypeStruct((b, D_ATTN, s), BF16)] * 2,
        scratch_shapes=[pltpu.VMEM(((SEL_TBL_TILES + WIN_TBL_TILES) * tq, nq), F32),
                        pltpu.VMEM((kd, nq), BF16),
                        pltpu.VMEM((1, nq), F32),
                        pltpu.VMEM((V_ROWS, nq), F32),
                        pltpu.VMEM((1, nq), F32)]
                       + [pltpu.VMEM((tq, nq), F32)] * N_SLOTS
                       + [pltpu.VMEM((1, nq), F32)] * N_SLOTS
                       + [pltpu.VMEM((1, nq), F32)] * N_SLOTS
                       + [pltpu.VMEM((tq, nq), BF16)] * N_SLOTS,
        compiler_params=_cparams(("parallel", "parallel", "arbitrary")),
        name="sel_win_attn",
    )(rel_bias, qT, selb, ksel, vselT, kwin, vwinT, gs, gs)

    hb = tm // HALO_ROWS
    col = lambda c: pl.BlockSpec((1, tm, D_CONV), lambda i, j, c=c: (i, j, c))
    halo = lambda c: pl.BlockSpec((1, HALO_ROWS, D_CONV), lambda i, j, c=c: (i, jnp.maximum(j * hb - 1, 0), c))
    oT = pl.BlockSpec((1, D_ATTN, tm), lambda i, j: (i, 0, j))
    out = pl.pallas_call(
        partial(_out_kernel, tm=tm),
        grid=(b, nt),
        in_specs=[oT, oT, oT, col(0), col(1), col(2), col(3), col(4), halo(1), halo(3),
                  pl.BlockSpec((1, tm, D_MODEL), lambda i, j: (i, j, 0)),
                  pl.BlockSpec((3, D_CONV), lambda i, j: (0, 0)),
                  pl.BlockSpec((D_MODEL, D_MODEL), lambda i, j: (0, 0)),
                  pl.BlockSpec((1, D_MODEL), lambda i, j: (0, 0))],
        out_specs=pl.BlockSpec((1, tm, D_MODEL), lambda i, j: (i, j, 0)),
        out_shape=jax.ShapeDtypeStruct((b, s, D_MODEL), F32),
        compiler_params=_cparams(("parallel", "parallel")),
        name="out_proj",
    )(ocT, osT, owT, rest, rest, rest, rest, rest, rest, rest, x, conv_w[0], w_out[0].astype(BF16),
      final_norm_w.reshape(1, D_MODEL))
    return out
```

```python
import math
from functools import partial

import numpy as np
import jax
import jax.numpy as jnp
from jax import lax
from jax.experimental import pallas as pl
from jax.experimental.pallas import tpu as pltpu

D_MODEL = 1024
N_HEADS = 8
HEAD_DIM = 64
N_KV_HEADS = 2
GQA_REP = N_HEADS // N_KV_HEADS
D_ATTN = N_HEADS * HEAD_DIM
D_KV = N_KV_HEADS * HEAD_DIM
D_CONV = D_MODEL - D_ATTN
CMP_BLOCK = 32
CMP_STRIDE = 16
CMP_HIDDEN = 2 * HEAD_DIM
SEL_BLOCK = 64
N_SEL = 16
WINDOW = 512
N_BUCKETS = 32
MAX_DISTANCE = 128
EPS = 1e-6
NEG = -1e30
FORCE_SCORE = 1e3

LOG2E = 1.4426950408889634
LANES = 128
MXU_COLS = 256
V_ROWS = HEAD_DIM + 16
ROW_TILE = 512
KEY_TILE = 256
Q_TILE_CMP = 512
Q_TILE = 256
HALO_ROWS = 16
MAX_LAG = 64.0
N_SLOTS = 2
VMEM_LIMIT = 48 * 1024 * 1024

F32 = jnp.float32
BF16 = jnp.bfloat16

_C_Q = 0
_C_CMP = D_ATTN
_C_SW = _C_CMP + 2 * D_KV
_C_REST = _C_SW + 4 * D_KV
_C_GATE = _C_REST + D_ATTN + 4 * D_CONV
_C_END = _C_GATE + LANES
N_GATE = 3 * N_HEADS


def _bucket_lower_bounds():
    max_exact = N_BUCKETS // 2
    d = np.arange(2 * MAX_DISTANCE, dtype=np.int32)
    nf = np.maximum(d, 1).astype(np.float32)
    large = max_exact + (np.log(nf / np.float32(max_exact)) / np.float32(math.log(MAX_DISTANCE / max_exact))
                         * np.float32(N_BUCKETS - max_exact)).astype(np.int32)
    bucket = np.where(d < max_exact, d, np.minimum(large, N_BUCKETS - 1))
    assert np.all(np.diff(bucket) >= 0) and np.all(bucket[MAX_DISTANCE:] == N_BUCKETS - 1)
    return [int(np.argmax(bucket >= b)) for b in range(N_BUCKETS)]


_BUCKET_LO = _bucket_lower_bounds()


def _rel_bias_delta(dist, bias_of_bucket):
    last = bias_of_bucket(N_BUCKETS - 1)
    val = jnp.full(dist.shape, (bias_of_bucket(0) - last) * LOG2E, F32)
    for b in range(1, N_BUCKETS - 1):
        val = jnp.where(dist >= _BUCKET_LO[b], (bias_of_bucket(b) - last) * LOG2E, val)
    val = jnp.where(dist >= _BUCKET_LO[N_BUCKETS - 1], 0.0, val)
    return jnp.where(dist < 0, NEG, val)


def _silu(z):
    return z * jax.nn.sigmoid(z)


def _proj_kernel(x_ref, nw_ref, wT_ref, qT_ref, kvc_ref, ksel_ref, vselT_ref, kwin_ref, vwinT_ref,
                 gsT_ref, rest_ref, kv_scr, w_scr, *, tm, nsel, kd):
    st = pl.program_id(1)

    @pl.when((pl.program_id(0) == 0) & (st == 0))
    def _():
        c0 = _C_REST
        chunk = 256
        for dst, src, rows in ((0, 0, c0), (c0, c0 + N_GATE, _C_GATE - c0)):
            for r in range(0, rows, chunk):
                w_scr[dst + r:dst + r + chunk, :] = wT_ref[src + r:src + r + chunk, :].astype(BF16)
        w_scr[_C_GATE:_C_END, :] = jnp.concatenate(
            [wT_ref[c0:c0 + N_GATE, :], jnp.zeros((LANES - N_GATE, D_MODEL), F32)], axis=0).astype(BF16)

    x = x_ref[0]
    ms = jnp.mean(x * x, axis=-1, keepdims=True)
    h = (x * lax.rsqrt(ms + EPS) * nw_ref[...]).astype(BF16)

    def mm(lo, hi):
        return lax.dot_general(h, w_scr[lo:hi, :], (((1,), (1,)), ((), ())), preferred_element_type=F32)

    q = mm(_C_Q, _C_CMP) * (HEAD_DIM ** -0.5 * LOG2E)
    qT_ref[0] = q.T.astype(BF16)

    kvc = mm(_C_CMP, _C_SW)
    for xk in range(2):
        kv_scr[xk] = kvc[:, xk * D_KV:(xk + 1) * D_KV]
        for j in range(CMP_STRIDE):
            kvc_ref[xk, 0, :, j * D_KV:(j + 1) * D_KV] = kv_scr[xk, pl.ds(j, tm // CMP_STRIDE, stride=CMP_STRIDE), :]

    a = mm(_C_SW, _C_REST)
    pos = st * tm + lax.broadcasted_iota(jnp.int32, (tm, nsel), 0)
    col = lax.broadcasted_iota(jnp.int32, (tm, nsel), 1)
    onehot = jnp.where((pos // SEL_BLOCK) == col, 1.0, 0.0).astype(BF16)
    ones = jnp.ones((V_ROWS - HEAD_DIM, tm), BF16)
    zeros_k = jnp.zeros((tm, LANES - HEAD_DIM), BF16)
    vsT = a[:, D_KV:2 * D_KV].T
    vwT = a[:, 3 * D_KV:4 * D_KV].T
    for g in range(N_KV_HEADS):
        lo, hi = g * HEAD_DIM, (g + 1) * HEAD_DIM
        parts = [a[:, lo:hi].astype(BF16), onehot]
        if kd > HEAD_DIM + nsel:
            parts.append(jnp.zeros((tm, kd - HEAD_DIM - nsel), BF16))
        ksel_ref[0, g] = jnp.concatenate(parts, axis=1)
        kwin_ref[0, g] = jnp.concatenate([a[:, 2 * D_KV + lo:2 * D_KV + hi].astype(BF16), zeros_k], axis=1)
        vs = jnp.concatenate([vsT[lo:hi].astype(BF16), ones], axis=0)
        vw = jnp.concatenate([vwT[lo:hi].astype(BF16), ones], axis=0)
        for kt in range(tm // KEY_TILE):
            vselT_ref[0, g, kt] = vs[:, kt * KEY_TILE:(kt + 1) * KEY_TILE]
            vwinT_ref[0, g, kt] = vw[:, kt * KEY_TILE:(kt + 1) * KEY_TILE]

    rest_ref[0] = mm(_C_REST, _C_GATE).astype(BF16)
    gates = jax.nn.sigmoid(mm(_C_GATE, _C_END))
    gsT_ref[0] = gates.T[:N_GATE]


def _compress_kernel(c_ref, pe_ref, w1_ref, w2_ref, lhs_tail_ref, aug_ref, t_ref, *, nch):
    c = c_ref[0, 0]
    a0 = jnp.dot((c + pe_ref[0, 0]).astype(BF16), w1_ref[0, 0], preferred_element_type=F32)
    a1 = jnp.dot((c + pe_ref[0, 1]).astype(BF16), w1_ref[0, 1], preferred_element_type=F32)
    hid = a0 + pltpu.roll(a1, nch - 1, 0)
    out = jnp.dot(_silu(hid).astype(BF16), w2_ref[0], preferred_element_type=F32)
    grp = lax.broadcasted_iota(jnp.int32, (nch, LANES - HEAD_DIM), 0) // 8
    col = lax.broadcasted_iota(jnp.int32, (nch, LANES - HEAD_DIM), 1)
    onehot = jnp.where(grp == col, 1.0, 0.0).astype(BF16)
    out_t = out.T.astype(BF16)
    for g in range(N_KV_HEADS):
        lo, hi = g * HEAD_DIM, (g + 1) * HEAD_DIM
        aug_ref[0, 0, g] = jnp.concatenate([out[:, lo:hi].astype(BF16), onehot], axis=1)
        t_ref[0, 0, g] = jnp.concatenate([out_t[lo:hi], lhs_tail_ref[...]], axis=0)


def _cmp_kernel(rb_ref, qT_ref, kc_ref, lhs_ref, gs_ref, ocT_ref, selb_ref,
                qa_ref, band_ref, acc_ref, s0_ref, s1_ref, s2_ref, s3_ref, *, nch, nsel, tq, k_top):
    g = pl.program_id(1)
    qt = pl.program_id(2)
    s_refs = (s0_ref, s1_ref, s2_ref, s3_ref)
    cpq = tq // CMP_STRIDE
    win = cpq + 16
    lim = cpq * (qt + 1)
    nq = GQA_REP * tq

    @pl.when(qt <= 1)
    def _():
        row = lax.broadcasted_iota(jnp.int32, (win, tq), 0) + jnp.maximum(cpq * qt - 16, 0)
        lane = lax.broadcasted_iota(jnp.int32, (win, tq), 1)
        dist = (qt * tq + lane) - (row * CMP_STRIDE + CMP_BLOCK - 1)
        for r in range(GQA_REP):
            band_ref[:, r * tq:(r + 1) * tq] = _rel_bias_delta(dist, lambda b, r=r: rb_ref[b, g * GQA_REP + r])

    q = qT_ref[0]
    qa_ref[0:HEAD_DIM, :] = jnp.concatenate([q[r * HEAD_DIM:(r + 1) * HEAD_DIM] for r in range(GQA_REP)], axis=1)
    grp = lax.broadcasted_iota(jnp.int32, (LANES - HEAD_DIM, nq), 0)
    qa_ref[HEAD_DIM:LANES, :] = jnp.where(grp * 8 >= lim, NEG, 0.0).astype(BF16)

    def run(rows):
        start = pl.multiple_of(jnp.clip(cpq * qt - 16, 0, rows - win), 8)
        k = kc_ref[0, 0, 0, 0:rows, :]
        lhs = lhs_ref[0, 0, 0, :, 0:rows]
        cols = [slice(r * tq, (r + 1) * tq) for r in range(GQA_REP)]

        def qk(r):
            s_refs[r][0:rows, :] = jnp.dot(k, qa_ref[:, cols[r]], preferred_element_type=F32)

        qk(0)
        for r in range(GQA_REP):
            if r + 1 < GQA_REP:
                qk(r + 1)
            sr = s_refs[r]
            sr[pl.ds(start, win), :] = sr[pl.ds(start, win), :] + band_ref[:, cols[r]]
            s = sr[0:rows, :]
            p = jnp.exp2(s - jnp.max(s, axis=0, keepdims=True)).astype(BF16)
            acc_ref[:, cols[r]] = jnp.dot(lhs, p, preferred_element_type=F32)

    half = nch // 2

    @pl.when(lim <= half)
    def _():
        run(half)

    @pl.when(lim > half)
    def _():
        run(nch)

    l = acc_ref[HEAD_DIM:HEAD_DIM + 1, :]
    t4 = qt * tq + (lax.broadcasted_iota(jnp.int32, l.shape, 1) & (tq - 1))
    inv = jnp.where(t4 >= CMP_BLOCK - 1, 1.0 / jnp.maximum(l, 1e-30), 0.0)
    imp = None
    for r in range(GQA_REP):
        cs = slice(r * tq, (r + 1) * tq)
        ocT_ref[0, r * HEAD_DIM:(r + 1) * HEAD_DIM, :] = (acc_ref[0:HEAD_DIM, cs] * inv[:, cs]
                                                          * gs_ref[0, 0, 0, r:r + 1, :]).astype(BF16)
        part = acc_ref[V_ROWS:V_ROWS + nsel, cs] * inv[:, cs]
        imp = part if imp is None else imp + part

    n = lax.broadcasted_iota(jnp.int32, (nsel, tq), 0)
    blk = (qt * tq + lax.broadcasted_iota(jnp.int32, (nsel, tq), 1)) // SEL_BLOCK
    forced = (n == 0) | (n == blk) | (n == blk - 1)
    candidate = (n <= blk) & jnp.logical_not(forced)
    score = jnp.where(forced, -2.0, jnp.where(n <= blk, imp, -1.0))

    def pick_all(_, score):
        return jnp.where(score == jnp.max(score, axis=0, keepdims=True), -2.0, score)

    taken = lax.fori_loop(0, k_top - 3, pick_all, score) == -2.0
    selb_ref[0, 0] = jnp.where(taken, 0.0, NEG).astype(BF16)
    n_taken = jnp.sum(jnp.where(taken & candidate, 1.0, 0.0), axis=0, keepdims=True)

    @pl.when(jnp.max(n_taken) > k_top - 3)
    def _():
        def pick_one(_, score):
            mx = jnp.max(score, axis=0, keepdims=True)
            idx = jnp.min(jnp.where(score == mx, n, nsel), axis=0, keepdims=True)
            return jnp.where(n == idx, -2.0, score)

        taken = lax.fori_loop(0, k_top - 3, pick_one, score) == -2.0
        selb_ref[0, 0] = jnp.where(taken, 0.0, NEG).astype(BF16)


def _flash_kernel(*refs, tq, nsel, kd, n_tbl_tiles, window, selected, add_ref=None):
    if selected:
        rb_ref, qT_ref, selb_ref, k_ref, vT_ref, gs_ref, o_ref, tbl_ref, qa_ref, m_ref, acc_ref, lag_ref, *slot_refs = refs
    else:
        rb_ref, qT_ref, k_ref, vT_ref, gs_ref, o_ref, tbl_ref, qa_ref, m_ref, acc_ref, lag_ref, *slot_refs = refs
    s_ref, mt_ref, al_ref, p_ref = (slot_refs[i * N_SLOTS:(i + 1) * N_SLOTS] for i in range(4))
    g = pl.program_id(1)
    qt = pl.program_id(2)
    nq = GQA_REP * tq

    @pl.when(qt == 0)
    def _():
        for j in range(n_tbl_tiles):
            key = lax.broadcasted_iota(jnp.int32, (tq, tq), 0) + j * tq
            qry = lax.broadcasted_iota(jnp.int32, (tq, tq), 1) + (n_tbl_tiles - 1) * tq
            dist = qry - key
            for r in range(GQA_REP):
                val = _rel_bias_delta(dist, lambda b, r=r: rb_ref[b, g * GQA_REP + r])
                if window is not None:
                    val = jnp.where(dist >= window, NEG, val)
                tbl_ref[j * tq:(j + 1) * tq, r * tq:(r + 1) * tq] = val

    q = qT_ref[0]
    qa_ref[0:HEAD_DIM, :] = jnp.concatenate([q[r * HEAD_DIM:(r + 1) * HEAD_DIM] for r in range(GQA_REP)], axis=1)
    used = HEAD_DIM
    if selected:
        qa_ref[HEAD_DIM:HEAD_DIM + nsel, :] = jnp.concatenate([selb_ref[0, 0]] * GQA_REP, axis=1)
        used += nsel
    if kd > used:
        qa_ref[used:kd, :] = jnp.zeros((kd - used, nq), BF16)
    m_ref[...] = jnp.full((1, nq), NEG, F32)
    acc_ref[...] = jnp.zeros((V_ROWS, nq), F32)

    def stage_qk(t, slot, tbl_tile):
        k = k_ref[0, 0, pl.ds(pl.multiple_of(t * tq, tq), tq), :]
        s = jnp.dot(k, qa_ref[...], preferred_element_type=F32)
        if tbl_tile is not None:
            s = s + tbl_ref[tbl_tile * tq:(tbl_tile + 1) * tq, :]
        s_ref[slot][...] = s
        mt_ref[slot][...] = jnp.max(s, axis=0, keepdims=True)

    def stage_exp(slot):
        m_old = m_ref[...]
        m_new = jnp.maximum(m_old, mt_ref[slot][...])
        m_ref[...] = m_new
        al_ref[slot][...] = jnp.exp2(m_old - m_new)
        p_ref[slot][...] = jnp.exp2(s_ref[slot][...] - m_new).astype(BF16)

    def stage_pv(t, slot):
        acc_ref[...] = al_ref[slot][...] * acc_ref[...] + jnp.dot(vT_ref[0, 0, t], p_ref[slot][...],
                                                                  preferred_element_type=F32)

    def serial(t, tbl_tile):
        stage_qk(t, 0, tbl_tile)
        stage_exp(0)
        stage_pv(t, 0)

    last = n_tbl_tiles - 1

    def stage_lagged(t, slot, tbl_tile, first=False):
        k = k_ref[0, 0, pl.ds(pl.multiple_of(t * tq, tq), tq), :]
        s = jnp.dot(k, qa_ref[...], preferred_element_type=F32)
        if tbl_tile is not None:
            s = s + tbl_ref[tbl_tile * tq:(tbl_tile + 1) * tq, :]
        m = s[0:1, :] if first else m_ref[...]
        mt = jnp.max(s, axis=0, keepdims=True)
        p_ref[slot][...] = jnp.exp2((s - m).astype(BF16))
        m_new = jnp.maximum(m, mt)
        m_ref[...] = m_new
        al_ref[slot][...] = jnp.exp2(m - m_new)
        lag_ref[...] = (mt - m) if first else jnp.maximum(lag_ref[...], mt - m)

    def stage_pv_lagged(t, slot):
        acc_ref[...] = (acc_ref[...] + jnp.dot(vT_ref[0, 0, t], p_ref[slot][...],
                                               preferred_element_type=F32)) * al_ref[slot][...]

    def lagged_step(t, slot, tbl_tile, prev=None):
        stage_lagged(t, slot, tbl_tile)
        stage_pv_lagged(t - 1 if prev is None else prev, 1 - slot)

    if selected:
        n = qt + 1

        @pl.when((n >= 4) & ((n & 1) == 0))
        def _():
            stage_lagged(0, 1, None, first=True)
            lagged_step(1, 0, None)

        @pl.when((n >= 4) & ((n & 1) == 1))
        def _():
            stage_lagged(0, 0, None, first=True)

        @pl.when(n >= 4)
        def _():
            t0 = 2 - (n & 1)

            pairs = (n - 2 - t0) >> 1

            def pair(t):
                lagged_step(t, 1, None)
                lagged_step(t + 1, 0, None)

            def body(j, carry):
                pair(t0 + 4 * j)
                pair(t0 + 4 * j + 2)
                return carry
            lax.fori_loop(0, pairs >> 1, body, 0)

            @pl.when((pairs & 1) == 1)
            def _():
                pair(n - 4)

        @pl.when(n >= 4)
        def _():
            lagged_step(n - 2, 1, 0)
            lagged_step(n - 1, 0, 1)
            stage_pv_lagged(n - 1, 0)

            @pl.when(jnp.max(lag_ref[...]) > MAX_LAG)
            def _():
                m_ref[...] = jnp.full((1, nq), NEG, F32)
                acc_ref[...] = jnp.zeros((V_ROWS, nq), F32)

                def redo(t, carry):
                    serial(t, None)
                    return carry
                lax.fori_loop(0, n - 2, redo, 0)
                serial(n - 2, 0)
                serial(n - 1, 1)

        @pl.when(n == 3)
        def _():
            serial(0, None)

        @pl.when((n >= 2) & (n < 4))
        def _():
            serial(qt - 1, 0)

        @pl.when(n < 4)
        def _():
            serial(qt, 1)
    else:
        @pl.when(qt >= last)
        def _():
            stage_lagged(qt, 0, 2, first=True)
            lagged_step(qt - 1, 1, 1, prev=qt)
            lagged_step(qt - 2, 0, 0, prev=qt - 1)
            stage_pv_lagged(qt - 2, 0)

            @pl.when(jnp.max(lag_ref[...]) > MAX_LAG)
            def _():
                m_ref[...] = jnp.full((1, nq), NEG, F32)
                acc_ref[...] = jnp.zeros((V_ROWS, nq), F32)
                for j in range(n_tbl_tiles):
                    serial(qt - last + j, j)

        @pl.when(qt == 1)
        def _():
            serial(0, 1)

        @pl.when(qt < last)
        def _():
            serial(qt, 2)

    acc = acc_ref[...]
    o = acc[0:HEAD_DIM] / jnp.maximum(acc[HEAD_DIM:HEAD_DIM + 1], 1e-30)
    for r in range(GQA_REP):
        rows = slice(r * HEAD_DIM, (r + 1) * HEAD_DIM)
        val = o[:, r * tq:(r + 1) * tq] * gs_ref[0, 0, 0, r:r + 1, :]
        if add_ref is not None:
            val = val + add_ref[0, rows, :]
        o_ref[0, rows, :] = val.astype(o_ref.dtype)


SEL_TBL_TILES = 2
WIN_TBL_TILES = 3


def _branches_kernel(rb_ref, qT_ref, selb_ref, ksel_ref, vselT_ref, kwin_ref, vwinT_ref, gs_sel_ref, gs_win_ref,
                     o_ref, owin_ref, tbl_ref, qa_ref, *state, tq, nsel, kd):
    win_tbl = tbl_ref.at[SEL_TBL_TILES * tq:(SEL_TBL_TILES + WIN_TBL_TILES) * tq]
    _flash_kernel(rb_ref, qT_ref, kwin_ref, vwinT_ref, gs_win_ref, owin_ref, win_tbl, qa_ref.at[0:LANES], *state,
                  tq=tq, nsel=nsel, kd=LANES, n_tbl_tiles=WIN_TBL_TILES, window=WINDOW, selected=False)
    _flash_kernel(rb_ref, qT_ref, selb_ref, ksel_ref, vselT_ref, gs_sel_ref, o_ref,
                  tbl_ref.at[0:SEL_TBL_TILES * tq], qa_ref, *state,
                  tq=tq, nsel=nsel, kd=kd, n_tbl_tiles=SEL_TBL_TILES, window=None, selected=True, add_ref=owin_ref)


def _out_kernel(oc_ref, osw_ref, za_ref, ch_ref, cb_ref, cc_ref, zc_ref, hh_ref, hc_ref,
                x_ref, cw_ref, wo_ref, fw_ref, out_ref, *, tm):
    st = pl.program_id(1)
    f32 = lambda ref: ref[0].astype(F32)
    attn = (f32(oc_ref) + f32(osw_ref)).T * _silu(f32(za_ref))

    u = f32(cc_ref) * f32(ch_ref)
    halo = jnp.where(st == 0, 0.0, f32(hc_ref) * f32(hh_ref))
    row = lax.broadcasted_iota(jnp.int32, u.shape, 0)
    h1, h2 = halo[HALO_ROWS - 1:HALO_ROWS], halo[HALO_ROWS - 2:HALO_ROWS - 1]
    u1 = jnp.where(row == 0, h1, pltpu.roll(u, 1, 0))
    u2 = jnp.where(row == 0, h2, jnp.where(row == 1, h1, pltpu.roll(u, 2, 0)))
    y = cw_ref[0:1] * u2 + cw_ref[1:2] * u1 + cw_ref[2:3] * u
    conv = f32(cb_ref) * y * _silu(f32(zc_ref))

    mixed = jnp.concatenate([attn, conv], axis=1).astype(BF16)
    y2 = x_ref[0] + jnp.dot(mixed, wo_ref[...], preferred_element_type=F32)
    ms = jnp.mean(y2 * y2, axis=-1, keepdims=True)
    out_ref[0] = y2 * lax.rsqrt(ms + EPS) * fw_ref[...]


def _overlap_t(nsel, nch):
    c = np.arange(nch)[None, :] * CMP_STRIDE
    n = np.arange(nsel)[:, None] * SEL_BLOCK
    return ((c < n + SEL_BLOCK) & (c + CMP_BLOCK > n)).astype(np.float32)


def _cparams(sem):
    return pltpu.CompilerParams(dimension_semantics=sem, vmem_limit_bytes=VMEM_LIMIT)


def kernel(x, norm_w, w_in, w_ck1, w_ck2, pe_k, w_cv1, w_cv2, pe_v, conv_w, w_out, rel_bias, final_norm_w):
    b, s, _ = x.shape
    assert norm_w.shape[0] == 1 and s % ROW_TILE == 0 and ROW_TILE % KEY_TILE == 0
    assert WINDOW == 2 * Q_TILE and KEY_TILE == Q_TILE
    tm = ROW_TILE
    nt = s // tm
    nkt = s // KEY_TILE
    kpt = tm // KEY_TILE
    nch = s // CMP_STRIDE
    nsel = s // SEL_BLOCK
    k_top = min(N_SEL, nsel)
    kd = -(-(HEAD_DIM + nsel) // LANES) * LANES
    G, R = N_KV_HEADS, GQA_REP

    w_in_t = jnp.swapaxes(w_in[0], 0, 1)
    assert w_in_t.shape[0] == _C_GATE + N_GATE
    eye = jnp.eye(G, dtype=F32)

    def expand1(w1):
        w1 = w1.astype(BF16).reshape(2, CMP_STRIDE, HEAD_DIM, CMP_HIDDEN)
        z = jnp.zeros_like(w1)
        rows = [jnp.concatenate([w1 if a == g else z for a in range(G)], axis=-1) for g in range(G)]
        return jnp.stack(rows, axis=2).reshape(2, CMP_STRIDE * D_KV, G * CMP_HIDDEN)

    def expand_pe(pe):
        pe = pe.reshape(2, CMP_STRIDE, 1, HEAD_DIM)
        return jnp.broadcast_to(pe, (2, CMP_STRIDE, G, HEAD_DIM)).reshape(2, 1, CMP_STRIDE * D_KV)

    def expand2(w2):
        return jnp.einsum('hd,ab->ahbd', w2, eye).reshape(G * CMP_HIDDEN, D_KV)

    w1e = jnp.stack([expand1(w_ck1[0]), expand1(w_cv1[0])]).astype(BF16)
    pee = jnp.stack([expand_pe(pe_k[0]), expand_pe(pe_v[0])])
    w2e = jnp.stack([expand2(w_ck2[0]), expand2(w_cv2[0])]).astype(BF16)
    rel_bias = rel_bias.astype(F32)
    smem = pl.BlockSpec(memory_space=pltpu.SMEM)

    n_rest = D_ATTN + 4 * D_CONV
    cw = CMP_STRIDE * D_KV
    qT, kvc, ksel, vselT, kwin, vwinT, gsT, rest = pl.pallas_call(
        partial(_proj_kernel, tm=tm, nsel=nsel, kd=kd),
        grid=(b, nt),
        in_specs=[pl.BlockSpec((1, tm, D_MODEL), lambda i, j: (i, j, 0)),
                  pl.BlockSpec((1, D_MODEL), lambda i, j: (0, 0)),
                  pl.BlockSpec((_C_GATE + N_GATE, D_MODEL), lambda i, j: (0, 0), pipeline_mode=pl.Buffered(1))],
        out_specs=[pl.BlockSpec((1, D_ATTN, tm), lambda i, j: (i, 0, j)),
                   pl.BlockSpec((2, 1, tm // CMP_STRIDE, cw), lambda i, j: (0, i, j, 0)),
                   pl.BlockSpec((1, G, tm, kd), lambda i, j: (i, 0, j, 0)),
                   pl.BlockSpec((1, G, kpt, V_ROWS, KEY_TILE), lambda i, j: (i, 0, j, 0, 0)),
                   pl.BlockSpec((1, G, tm, LANES), lambda i, j: (i, 0, j, 0)),
                   pl.BlockSpec((1, G, kpt, V_ROWS, KEY_TILE), lambda i, j: (i, 0, j, 0, 0)),
                   pl.BlockSpec((1, N_GATE, tm), lambda i, j: (i, 0, j)),
                   pl.BlockSpec((1, tm, n_rest), lambda i, j: (i, j, 0))],
        out_shape=[jax.ShapeDtypeStruct((b, D_ATTN, s), BF16),
                   jax.ShapeDtypeStruct((2, b, nch, cw), F32),
                   jax.ShapeDtypeStruct((b, G, s, kd), BF16),
                   jax.ShapeDtypeStruct((b, G, nkt, V_ROWS, KEY_TILE), BF16),
                   jax.ShapeDtypeStruct((b, G, s, LANES), BF16),
                   jax.ShapeDtypeStruct((b, G, nkt, V_ROWS, KEY_TILE), BF16),
                   jax.ShapeDtypeStruct((b, N_GATE, s), F32),
                   jax.ShapeDtypeStruct((b, s, n_rest), BF16)],
        scratch_shapes=[pltpu.VMEM((2, tm, D_KV), F32),
                        pltpu.VMEM((_C_END, D_MODEL), BF16)],
        compiler_params=_cparams(("arbitrary", "arbitrary")),
        name="in_proj",
    )(x, norm_w, w_in_t)
    gs = gsT.reshape(b, 3, G, R, s)

    lhs_rows = V_ROWS + nsel
    lhs_tail = jnp.asarray(np.concatenate([np.ones((V_ROWS - HEAD_DIM, nch), np.float32),
                                           _overlap_t(nsel, nch)], axis=0), BF16)
    kc_aug, c_lhs = pl.pallas_call(
        partial(_compress_kernel, nch=nch),
        grid=(2, b),
        in_specs=[pl.BlockSpec((1, 1, nch, cw), lambda xk, i: (xk, i, 0, 0)),
                  pl.BlockSpec((1, 2, 1, cw), lambda xk, i: (xk, 0, 0, 0)),
                  pl.BlockSpec((1, 2, cw, G * CMP_HIDDEN), lambda xk, i: (xk, 0, 0, 0)),
                  pl.BlockSpec((1, G * CMP_HIDDEN, D_KV), lambda xk, i: (xk, 0, 0)),
                  pl.BlockSpec((lhs_rows - HEAD_DIM, nch), lambda xk, i: (0, 0))],
        out_specs=[pl.BlockSpec((1, 1, G, nch, LANES), lambda xk, i: (xk, i, 0, 0, 0)),
                   pl.BlockSpec((1, 1, G, lhs_rows, nch), lambda xk, i: (xk, i, 0, 0, 0))],
        out_shape=[jax.ShapeDtypeStruct((2, b, G, nch, LANES), BF16),
                   jax.ShapeDtypeStruct((2, b, G, lhs_rows, nch), BF16)],
        compiler_params=_cparams(("parallel", "parallel")),
        name="compress",
    )(kvc, pee, w1e, w2e, lhs_tail)

    tq3 = Q_TILE_CMP
    cpq = tq3 // CMP_STRIDE
    assert nch // 8 <= LANES - HEAD_DIM and k_top > 3 and (nch // 2) % LANES == 0
    ocT, selb = pl.pallas_call(
        partial(_cmp_kernel, nch=nch, nsel=nsel, tq=tq3, k_top=k_top),
        grid=(b, G, s // tq3),
        in_specs=[smem,
                  pl.BlockSpec((1, R * HEAD_DIM, tq3), lambda i, g, t: (i, g, t)),
                  pl.BlockSpec((1, 1, 1, nch, LANES), lambda i, g, t: (0, i, g, 0, 0)),
                  pl.BlockSpec((1, 1, 1, lhs_rows, nch), lambda i, g, t: (1, i, g, 0, 0)),
                  pl.BlockSpec((1, 1, 1, R, tq3), lambda i, g, t: (i, 0, g, 0, t))],
        out_specs=[pl.BlockSpec((1, R * HEAD_DIM, tq3), lambda i, g, t: (i, g, t)),
                   pl.BlockSpec((1, 1, nsel, tq3), lambda i, g, t: (i, g, 0, t))],
        out_shape=[jax.ShapeDtypeStruct((b, D_ATTN, s), BF16),
                   jax.ShapeDtypeStruct((b, G, nsel, s), BF16)],
        scratch_shapes=[pltpu.VMEM((LANES, R * tq3), BF16),
                        pltpu.VMEM((cpq + 16, R * tq3), F32),
                        pltpu.VMEM((lhs_rows, R * tq3), F32)]
                       + [pltpu.VMEM((nch, tq3), F32)] * R,
        compiler_params=_cparams(("parallel", "parallel", "arbitrary")),
        name="cmp_select",
    )(rel_bias, qT, kc_aug, c_lhs, gs)

    tq = Q_TILE
    nq = R * tq

    o_spec = pl.BlockSpec((1, R * HEAD_DIM, tq), lambda i, g, t: (i, g, t))
    k_spec = lambda kdim: pl.BlockSpec((1, 1, s, kdim), lambda i, g, t: (i, g, 0, 0))
    v_spec = pl.BlockSpec((1, 1, nkt, V_ROWS, KEY_TILE), lambda i, g, t: (i, g, 0, 0, 0))
    gs_spec = lambda branch: pl.BlockSpec((1, 1, 1, R, tq), lambda i, g, t: (i, branch, g, 0, t))
    oswT = pl.pallas_call(
        partial(_branches_kernel, tq=tq, nsel=nsel, kd=kd),
        grid=(b, G, s // tq),
        in_specs=[smem, o_spec, pl.BlockSpec((1, 1, nsel, tq), lambda i, g, t: (i, g, 0, t)),
                  k_spec(kd), v_spec, k_spec(LANES), v_spec, gs_spec(1), gs_spec(2)],
        out_specs=o_spec,
        out_shape=jax.ShapeDtypeStruct((b, D_ATTN, s), BF16),
        scratch_shapes=[pltpu.VMEM((1, R * HEAD_DIM, tq), F32),
                        pltpu.VMEM(((SEL_TBL_TILES + WIN_TBL_TILES) * tq, nq), F32),
                        pltpu.VMEM((kd, nq), BF16),
                        pltpu.VMEM((1, nq), F32),
                        pltpu.VMEM((V_ROWS, nq), F32),
                        pltpu.VMEM((1, nq), F32)]
                       + [pltpu.VMEM((tq, nq), F32)] * N_SLOTS
                       + [pltpu.VMEM((1, nq), F32)] * N_SLOTS
                       + [pltpu.VMEM((1, nq), F32)] * N_SLOTS
                       + [pltpu.VMEM((tq, nq), BF16)] * N_SLOTS,
        compiler_params=_cparams(("parallel", "parallel", "arbitrary")),
        name="sel_win_attn",
    )(rel_bias, qT, selb, ksel, vselT, kwin, vwinT, gs, gs)

    hb = tm // HALO_ROWS
    col = lambda c: pl.BlockSpec((1, tm, D_CONV), lambda i, j, c=c: (i, j, c))
    halo = lambda c: pl.BlockSpec((1, HALO_ROWS, D_CONV), lambda i, j, c=c: (i, jnp.maximum(j * hb - 1, 0), c))
    oT = pl.BlockSpec((1, D_ATTN, tm), lambda i, j: (i, 0, j))
    out = pl.pallas_call(
        partial(_out_kernel, tm=tm),
        grid=(b, nt),
        in_specs=[oT, oT, col(0), col(1), col(2), col(3), col(4), halo(1), halo(3),
                  pl.BlockSpec((1, tm, D_MODEL), lambda i, j: (i, j, 0)),
                  pl.BlockSpec((3, D_CONV), lambda i, j: (0, 0)),
                  pl.BlockSpec((D_MODEL, D_MODEL), lambda i, j: (0, 0)),
                  pl.BlockSpec((1, D_MODEL), lambda i, j: (0, 0))],
        out_specs=pl.BlockSpec((1, tm, D_MODEL), lambda i, j: (i, j, 0)),
        out_shape=jax.ShapeDtypeStruct((b, s, D_MODEL), F32),
        compiler_params=_cparams(("parallel", "parallel")),
        name="out_proj",
    )(ocT, oswT, rest, rest, rest, rest, rest, rest, rest, x, conv_w[0], w_out[0].astype(BF16),
      final_norm_w.reshape(1, D_MODEL))
    return out
```

```python
import math
from functools import partial

import numpy as np
import jax
import jax.numpy as jnp
from jax import lax
from jax.experimental import pallas as pl
from jax.experimental.pallas import tpu as pltpu

D_MODEL = 1024
N_HEADS = 8
HEAD_DIM = 64
N_KV_HEADS = 2
GQA_REP = N_HEADS // N_KV_HEADS
D_ATTN = N_HEADS * HEAD_DIM
D_KV = N_KV_HEADS * HEAD_DIM
D_CONV = D_MODEL - D_ATTN
CMP_BLOCK = 32
CMP_STRIDE = 16
CMP_HIDDEN = 2 * HEAD_DIM
SEL_BLOCK = 64
N_SEL = 16
WINDOW = 512
N_BUCKETS = 32
MAX_DISTANCE = 128
EPS = 1e-6
NEG = -1e30
FORCE_SCORE = 1e3

LOG2E = 1.4426950408889634
LANES = 128
MXU_COLS = 256
V_ROWS = HEAD_DIM + 16
ROW_TILE = 512
KEY_TILE = 256
Q_TILE_CMP = 512
Q_TILE = 256
HALO_ROWS = 16
MAX_LAG = 64.0
N_SLOTS = 2
VMEM_LIMIT = 48 * 1024 * 1024

F32 = jnp.float32
BF16 = jnp.bfloat16

_C_Q = 0
_C_CMP = D_ATTN
_C_SW = _C_CMP + 2 * D_KV
_C_REST = _C_SW + 4 * D_KV
_C_GATE = _C_REST + D_ATTN + 4 * D_CONV
_C_END = _C_GATE + LANES
N_GATE = 3 * N_HEADS


def _bucket_lower_bounds():
    max_exact = N_BUCKETS // 2
    d = np.arange(2 * MAX_DISTANCE, dtype=np.int32)
    nf = np.maximum(d, 1).astype(np.float32)
    large = max_exact + (np.log(nf / np.float32(max_exact)) / np.float32(math.log(MAX_DISTANCE / max_exact))
                         * np.float32(N_BUCKETS - max_exact)).astype(np.int32)
    bucket = np.where(d < max_exact, d, np.minimum(large, N_BUCKETS - 1))
    assert np.all(np.diff(bucket) >= 0) and np.all(bucket[MAX_DISTANCE:] == N_BUCKETS - 1)
    return [int(np.argmax(bucket >= b)) for b in range(N_BUCKETS)]


_BUCKET_LO = _bucket_lower_bounds()


def _rel_bias_delta(dist, bias_of_bucket):
    last = bias_of_bucket(N_BUCKETS - 1)
    val = jnp.full(dist.shape, (bias_of_bucket(0) - last) * LOG2E, F32)
    for b in range(1, N_BUCKETS - 1):
        val = jnp.where(dist >= _BUCKET_LO[b], (bias_of_bucket(b) - last) * LOG2E, val)
    val = jnp.where(dist >= _BUCKET_LO[N_BUCKETS - 1], 0.0, val)
    return jnp.where(dist < 0, NEG, val)


def _silu(z):
    return z * (1.0 / (1.0 + jnp.exp2(z * -LOG2E)))


def _proj_kernel(x_ref, nw_ref, wT_ref, qT_ref, kvc_ref, ksel_ref, vselT_ref, kwin_ref, vwinT_ref,
                 gsT_ref, rest_ref, kv_scr, w_scr, *, tm, nsel, kd):
    st = pl.program_id(1)

    @pl.when((pl.program_id(0) == 0) & (st == 0))
    def _():
        c0 = _C_REST
        chunk = 256
        for dst, src, rows in ((0, 0, c0), (c0, c0 + N_GATE, _C_GATE - c0)):
            for r in range(0, rows, chunk):
                w_scr[dst + r:dst + r + chunk, :] = wT_ref[src + r:src + r + chunk, :].astype(BF16)
        w_scr[_C_GATE:_C_END, :] = jnp.concatenate(
            [wT_ref[c0:c0 + N_GATE, :], jnp.zeros((LANES - N_GATE, D_MODEL), F32)], axis=0).astype(BF16)

    x = x_ref[0]
    ms = jnp.mean(x * x, axis=-1, keepdims=True)
    h = (x * lax.rsqrt(ms + EPS) * nw_ref[...]).astype(BF16)

    def mm(lo, hi):
        return lax.dot_general(h, w_scr[lo:hi, :], (((1,), (1,)), ((), ())), preferred_element_type=F32)

    q = mm(_C_Q, _C_CMP) * (HEAD_DIM ** -0.5 * LOG2E)
    qT_ref[0] = q.T.astype(BF16)

    kvc = mm(_C_CMP, _C_SW)
    for xk in range(2):
        kv_scr[xk] = kvc[:, xk * D_KV:(xk + 1) * D_KV]
        for j in range(CMP_STRIDE):
            kvc_ref[xk, 0, :, j * D_KV:(j + 1) * D_KV] = kv_scr[xk, pl.ds(j, tm // CMP_STRIDE, stride=CMP_STRIDE), :]

    a = mm(_C_SW, _C_REST)
    pos = st * tm + lax.broadcasted_iota(jnp.int32, (tm, nsel), 0)
    col = lax.broadcasted_iota(jnp.int32, (tm, nsel), 1)
    onehot = jnp.where((pos // SEL_BLOCK) == col, 1.0, 0.0).astype(BF16)
    ones = jnp.ones((V_ROWS - HEAD_DIM, tm), BF16)
    zeros_k = jnp.zeros((tm, LANES - HEAD_DIM), BF16)
    vsT = a[:, D_KV:2 * D_KV].T
    vwT = a[:, 3 * D_KV:4 * D_KV].T
    for g in range(N_KV_HEADS):
        lo, hi = g * HEAD_DIM, (g + 1) * HEAD_DIM
        parts = [a[:, lo:hi].astype(BF16), onehot]
        if kd > HEAD_DIM + nsel:
            parts.append(jnp.zeros((tm, kd - HEAD_DIM - nsel), BF16))
        ksel_ref[0, g] = jnp.concatenate(parts, axis=1)
        kwin_ref[0, g] = jnp.concatenate([a[:, 2 * D_KV + lo:2 * D_KV + hi].astype(BF16), zeros_k], axis=1)
        vs = jnp.concatenate([vsT[lo:hi].astype(BF16), ones], axis=0)
        vw = jnp.concatenate([vwT[lo:hi].astype(BF16), ones], axis=0)
        for kt in range(tm // KEY_TILE):
            vselT_ref[0, g, kt] = vs[:, kt * KEY_TILE:(kt + 1) * KEY_TILE]
            vwinT_ref[0, g, kt] = vw[:, kt * KEY_TILE:(kt + 1) * KEY_TILE]

    rest_ref[0] = mm(_C_REST, _C_GATE).astype(BF16)
    gates = jax.nn.sigmoid(mm(_C_GATE, _C_END))
    gsT_ref[0] = gates.T[:N_GATE]


def _compress_kernel(c_ref, pe_ref, w1_ref, w2_ref, lhs_tail_ref, aug_ref, t_ref, *, nch):
    c = c_ref[0, 0]
    a0 = jnp.dot((c + pe_ref[0, 0]).astype(BF16), w1_ref[0, 0], preferred_element_type=F32)
    a1 = jnp.dot((c + pe_ref[0, 1]).astype(BF16), w1_ref[0, 1], preferred_element_type=F32)
    hid = a0 + pltpu.roll(a1, nch - 1, 0)
    out = jnp.dot(_silu(hid).astype(BF16), w2_ref[0], preferred_element_type=F32)
    grp = lax.broadcasted_iota(jnp.int32, (nch, LANES - HEAD_DIM), 0) // 8
    col = lax.broadcasted_iota(jnp.int32, (nch, LANES - HEAD_DIM), 1)
    onehot = jnp.where(grp == col, 1.0, 0.0).astype(BF16)
    out_t = out.T.astype(BF16)
    for g in range(N_KV_HEADS):
        lo, hi = g * HEAD_DIM, (g + 1) * HEAD_DIM
        aug_ref[0, 0, g] = jnp.concatenate([out[:, lo:hi].astype(BF16), onehot], axis=1)
        t_ref[0, 0, g] = jnp.concatenate([out_t[lo:hi], lhs_tail_ref[...]], axis=0)


def _cmp_kernel(rb_ref, qT_ref, kc_ref, lhs_ref, gs_ref, ocT_ref, selb_ref,
                qa_ref, band_ref, acc_ref, s0_ref, s1_ref, s2_ref, s3_ref, *, nch, nsel, tq, k_top):
    g = pl.program_id(1)
    qt = pl.program_id(2)
    s_refs = (s0_ref, s1_ref, s2_ref, s3_ref)
    cpq = tq // CMP_STRIDE
    win = cpq + 16
    lim = cpq * (qt + 1)
    nq = GQA_REP * tq

    @pl.when(qt <= 1)
    def _():
        row = lax.broadcasted_iota(jnp.int32, (win, tq), 0) + jnp.maximum(cpq * qt - 16, 0)
        lane = lax.broadcasted_iota(jnp.int32, (win, tq), 1)
        dist = (qt * tq + lane) - (row * CMP_STRIDE + CMP_BLOCK - 1)
        for r in range(GQA_REP):
            band_ref[:, r * tq:(r + 1) * tq] = _rel_bias_delta(dist, lambda b, r=r: rb_ref[b, g * GQA_REP + r])

    q = qT_ref[0]
    qa_ref[0:HEAD_DIM, :] = jnp.concatenate([q[r * HEAD_DIM:(r + 1) * HEAD_DIM] for r in range(GQA_REP)], axis=1)
    grp = lax.broadcasted_iota(jnp.int32, (LANES - HEAD_DIM, nq), 0)
    qa_ref[HEAD_DIM:LANES, :] = jnp.where(grp * 8 >= lim, NEG, 0.0).astype(BF16)

    def run(rows):
        start = pl.multiple_of(jnp.clip(cpq * qt - 16, 0, rows - win), 8)
        k = kc_ref[0, 0, 0, 0:rows, :]
        lhs = lhs_ref[0, 0, 0, :, 0:rows]
        cols = [slice(r * tq, (r + 1) * tq) for r in range(GQA_REP)]

        def qk(r):
            s_refs[r][0:rows, :] = jnp.dot(k, qa_ref[:, cols[r]], preferred_element_type=F32)

        ahead = 3
        for r in range(ahead):
            qk(r)
        for r in range(GQA_REP):
            if r + ahead < GQA_REP:
                qk(r + ahead)
            sr = s_refs[r]
            sr[pl.ds(start, win), :] = sr[pl.ds(start, win), :] + band_ref[:, cols[r]]
            s = sr[0:rows, :]
            p = jnp.exp2(s - jnp.max(s, axis=0, keepdims=True)).astype(BF16)
            acc_ref[:, cols[r]] = jnp.dot(lhs, p, preferred_element_type=F32)

    half = nch // 2

    @pl.when(lim <= half)
    def _():
        run(half)

    @pl.when(lim > half)
    def _():
        run(nch)

    l = acc_ref[HEAD_DIM:HEAD_DIM + 1, :]
    t4 = qt * tq + (lax.broadcasted_iota(jnp.int32, l.shape, 1) & (tq - 1))
    inv = jnp.where(t4 >= CMP_BLOCK - 1, 1.0 / jnp.maximum(l, 1e-30), 0.0)
    imp = None
    for r in range(GQA_REP):
        cs = slice(r * tq, (r + 1) * tq)
        ocT_ref[0, r * HEAD_DIM:(r + 1) * HEAD_DIM, :] = (acc_ref[0:HEAD_DIM, cs] * inv[:, cs]
                                                          * gs_ref[0, 0, 0, r:r + 1, :]).astype(BF16)
        part = acc_ref[V_ROWS:V_ROWS + nsel, cs] * inv[:, cs]
        imp = part if imp is None else imp + part

    n = lax.broadcasted_iota(jnp.int32, (nsel, tq), 0)
    blk = (qt * tq + lax.broadcasted_iota(jnp.int32, (nsel, tq), 1)) // SEL_BLOCK
    forced = (n == 0) | (n == blk) | (n == blk - 1)
    candidate = (n <= blk) & jnp.logical_not(forced)
    score = jnp.where(forced, -2.0, jnp.where(n <= blk, imp, -1.0))

    def pick_all(_, score):
        return jnp.where(score == jnp.max(score, axis=0, keepdims=True), -2.0, score)

    taken = lax.fori_loop(0, k_top - 3, pick_all, score) == -2.0
    selb_ref[0, 0] = jnp.where(taken, 0.0, NEG).astype(BF16)
    n_taken = jnp.sum(jnp.where(taken & candidate, 1.0, 0.0), axis=0, keepdims=True)

    @pl.when(jnp.max(n_taken) > k_top - 3)
    def _():
        def pick_one(_, score):
            mx = jnp.max(score, axis=0, keepdims=True)
            idx = jnp.min(jnp.where(score == mx, n, nsel), axis=0, keepdims=True)
            return jnp.where(n == idx, -2.0, score)

        taken = lax.fori_loop(0, k_top - 3, pick_one, score) == -2.0
        selb_ref[0, 0] = jnp.where(taken, 0.0, NEG).astype(BF16)


def _flash_kernel(*refs, tq, nsel, kd, n_tbl_tiles, window, selected, add_ref=None):
    if selected:
        rb_ref, qT_ref, selb_ref, k_ref, vT_ref, gs_ref, o_ref, tbl_ref, qa_ref, m_ref, acc_ref, lag_ref, *slot_refs = refs
    else:
        rb_ref, qT_ref, k_ref, vT_ref, gs_ref, o_ref, tbl_ref, qa_ref, m_ref, acc_ref, lag_ref, *slot_refs = refs
    s_ref, mt_ref, al_ref, p_ref = (slot_refs[i * N_SLOTS:(i + 1) * N_SLOTS] for i in range(4))
    g = pl.program_id(1)
    qt = pl.program_id(2)
    nq = GQA_REP * tq

    @pl.when(qt == 0)
    def _():
        for j in range(n_tbl_tiles):
            key = lax.broadcasted_iota(jnp.int32, (tq, tq), 0) + j * tq
            qry = lax.broadcasted_iota(jnp.int32, (tq, tq), 1) + (n_tbl_tiles - 1) * tq
            dist = qry - key
            for r in range(GQA_REP):
                val = _rel_bias_delta(dist, lambda b, r=r: rb_ref[b, g * GQA_REP + r])
                if window is not None:
                    val = jnp.where(dist >= window, NEG, val)
                tbl_ref[j * tq:(j + 1) * tq, r * tq:(r + 1) * tq] = val

    q = qT_ref[0]
    qa_ref[0:HEAD_DIM, :] = jnp.concatenate([q[r * HEAD_DIM:(r + 1) * HEAD_DIM] for r in range(GQA_REP)], axis=1)
    used = HEAD_DIM
    if selected:
        qa_ref[HEAD_DIM:HEAD_DIM + nsel, :] = jnp.concatenate([selb_ref[0, 0]] * GQA_REP, axis=1)
        used += nsel
    if kd > used:
        qa_ref[used:kd, :] = jnp.zeros((kd - used, nq), BF16)
    m_ref[...] = jnp.full((1, nq), NEG, F32)
    acc_ref[...] = jnp.zeros((V_ROWS, nq), F32)

    def stage_qk(t, slot, tbl_tile):
        k = k_ref[0, 0, pl.ds(pl.multiple_of(t * tq, tq), tq), :]
        s = jnp.dot(k, qa_ref[...], preferred_element_type=F32)
        if tbl_tile is not None:
            s = s + tbl_ref[tbl_tile * tq:(tbl_tile + 1) * tq, :]
        s_ref[slot][...] = s
        mt_ref[slot][...] = jnp.max(s, axis=0, keepdims=True)

    def stage_exp(slot):
        m_old = m_ref[...]
        m_new = jnp.maximum(m_old, mt_ref[slot][...])
        m_ref[...] = m_new
        al_ref[slot][...] = jnp.exp2(m_old - m_new)
        p_ref[slot][...] = jnp.exp2(s_ref[slot][...] - m_new).astype(BF16)

    def stage_pv(t, slot):
        acc_ref[...] = al_ref[slot][...] * acc_ref[...] + jnp.dot(vT_ref[0, 0, t], p_ref[slot][...],
                                                                  preferred_element_type=F32)

    def serial(t, tbl_tile):
        stage_qk(t, 0, tbl_tile)
        stage_exp(0)
        stage_pv(t, 0)

    last = n_tbl_tiles - 1

    def stage_lagged(t, slot, tbl_tile, first=False):
        k = k_ref[0, 0, pl.ds(pl.multiple_of(t * tq, tq), tq), :]
        s = jnp.dot(k, qa_ref[...], preferred_element_type=F32)
        if tbl_tile is not None:
            s = s + tbl_ref[tbl_tile * tq:(tbl_tile + 1) * tq, :]
        m = s[0:1, :] if first else m_ref[...]
        mt = jnp.max(s, axis=0, keepdims=True)
        p_ref[slot][...] = jnp.exp2((s - m).astype(BF16))
        m_new = jnp.maximum(m, mt)
        m_ref[...] = m_new
        al_ref[slot][...] = jnp.exp2(m - m_new)
        lag_ref[...] = (mt - m) if first else jnp.maximum(lag_ref[...], mt - m)

    def stage_pv_lagged(t, slot):
        acc_ref[...] = (acc_ref[...] + jnp.dot(vT_ref[0, 0, t], p_ref[slot][...],
                                               preferred_element_type=F32)) * al_ref[slot][...]

    def lagged_step(t, slot, tbl_tile, prev=None):
        stage_lagged(t, slot, tbl_tile)
        stage_pv_lagged(t - 1 if prev is None else prev, 1 - slot)

    if selected:
        n = qt + 1

        @pl.when((n >= 4) & ((n & 1) == 0))
        def _():
            stage_lagged(0, 1, None, first=True)
            lagged_step(1, 0, None)

        @pl.when((n >= 4) & ((n & 1) == 1))
        def _():
            stage_lagged(0, 0, None, first=True)

        @pl.when(n >= 4)
        def _():
            t0 = 2 - (n & 1)

            pairs = (n - 2 - t0) >> 1

            def pair(t):
                lagged_step(t, 1, None)
                lagged_step(t + 1, 0, None)

            def body(j, carry):
                pair(t0 + 4 * j)
                pair(t0 + 4 * j + 2)
                return carry
            lax.fori_loop(0, pairs >> 1, body, 0)

            @pl.when((pairs & 1) == 1)
            def _():
                pair(n - 4)

        @pl.when(n >= 4)
        def _():
            lagged_step(n - 2, 1, 0)
            lagged_step(n - 1, 0, 1)
            stage_pv_lagged(n - 1, 0)

            @pl.when(jnp.max(lag_ref[...]) > MAX_LAG)
            def _():
                m_ref[...] = jnp.full((1, nq), NEG, F32)
                acc_ref[...] = jnp.zeros((V_ROWS, nq), F32)

                def redo(t, carry):
                    serial(t, None)
                    return carry
                lax.fori_loop(0, n - 2, redo, 0)
                serial(n - 2, 0)
                serial(n - 1, 1)

        @pl.when(n == 3)
        def _():
            serial(0, None)

        @pl.when((n >= 2) & (n < 4))
        def _():
            serial(qt - 1, 0)

        @pl.when(n < 4)
        def _():
            serial(qt, 1)
    else:
        @pl.when(qt >= last)
        def _():
            stage_lagged(qt, 0, 2, first=True)
            lagged_step(qt - 1, 1, 1, prev=qt)
            lagged_step(qt - 2, 0, 0, prev=qt - 1)
            stage_pv_lagged(qt - 2, 0)

            @pl.when(jnp.max(lag_ref[...]) > MAX_LAG)
            def _():
                m_ref[...] = jnp.full((1, nq), NEG, F32)
                acc_ref[...] = jnp.zeros((V_ROWS, nq), F32)
                for j in range(n_tbl_tiles):
                    serial(qt - last + j, j)

        @pl.when(qt == 1)
        def _():
            serial(0, 1)

        @pl.when(qt < last)
        def _():
            serial(qt, 2)

    acc = acc_ref[...]
    o = acc[0:HEAD_DIM] / jnp.maximum(acc[HEAD_DIM:HEAD_DIM + 1], 1e-30)
    for r in range(GQA_REP):
        rows = slice(r * HEAD_DIM, (r + 1) * HEAD_DIM)
        val = o[:, r * tq:(r + 1) * tq] * gs_ref[0, 0, 0, r:r + 1, :]
        if add_ref is not None:
            val = val + add_ref[0, rows, :]
        o_ref[0, rows, :] = val.astype(o_ref.dtype)


SEL_TBL_TILES = 2
WIN_TBL_TILES = 3


def _branches_kernel(rb_ref, qT_ref, selb_ref, ksel_ref, vselT_ref, kwin_ref, vwinT_ref, gs_sel_ref, gs_win_ref,
                     o_ref, owin_ref, tbl_ref, qa_ref, *state, tq, nsel, kd):
    win_tbl = tbl_ref.at[SEL_TBL_TILES * tq:(SEL_TBL_TILES + WIN_TBL_TILES) * tq]
    _flash_kernel(rb_ref, qT_ref, kwin_ref, vwinT_ref, gs_win_ref, owin_ref, win_tbl, qa_ref.at[0:LANES], *state,
                  tq=tq, nsel=nsel, kd=LANES, n_tbl_tiles=WIN_TBL_TILES, window=WINDOW, selected=False)
    _flash_kernel(rb_ref, qT_ref, selb_ref, ksel_ref, vselT_ref, gs_sel_ref, o_ref,
                  tbl_ref.at[0:SEL_TBL_TILES * tq], qa_ref, *state,
                  tq=tq, nsel=nsel, kd=kd, n_tbl_tiles=SEL_TBL_TILES, window=None, selected=True, add_ref=owin_ref)


def _out_kernel(oc_ref, osw_ref, za_ref, ch_ref, cb_ref, cc_ref, zc_ref, hh_ref, hc_ref,
                x_ref, cw_ref, wo_ref, fw_ref, out_ref, *, tm):
    st = pl.program_id(1)
    f32 = lambda ref: ref[0].astype(F32)
    attn = (f32(oc_ref) + f32(osw_ref)).T * _silu(f32(za_ref))

    u = f32(cc_ref) * f32(ch_ref)
    halo = jnp.where(st == 0, 0.0, f32(hc_ref) * f32(hh_ref))
    row = lax.broadcasted_iota(jnp.int32, u.shape, 0)
    h1, h2 = halo[HALO_ROWS - 1:HALO_ROWS], halo[HALO_ROWS - 2:HALO_ROWS - 1]
    u1 = jnp.where(row == 0, h1, pltpu.roll(u, 1, 0))
    u2 = jnp.where(row == 0, h2, jnp.where(row == 1, h1, pltpu.roll(u, 2, 0)))
    y = cw_ref[0:1] * u2 + cw_ref[1:2] * u1 + cw_ref[2:3] * u
    conv = f32(cb_ref) * y * _silu(f32(zc_ref))

    mixed = jnp.concatenate([attn, conv], axis=1).astype(BF16)
    y2 = x_ref[0] + jnp.dot(mixed, wo_ref[...], preferred_element_type=F32)
    ms = jnp.mean(y2 * y2, axis=-1, keepdims=True)
    out_ref[0] = y2 * lax.rsqrt(ms + EPS) * fw_ref[...]


def _overlap_t(nsel, nch):
    c = np.arange(nch)[None, :] * CMP_STRIDE
    n = np.arange(nsel)[:, None] * SEL_BLOCK
    return ((c < n + SEL_BLOCK) & (c + CMP_BLOCK > n)).astype(np.float32)


def _cparams(sem):
    return pltpu.CompilerParams(dimension_semantics=sem, vmem_limit_bytes=VMEM_LIMIT)


def kernel(x, norm_w, w_in, w_ck1, w_ck2, pe_k, w_cv1, w_cv2, pe_v, conv_w, w_out, rel_bias, final_norm_w):
    b, s, _ = x.shape
    assert norm_w.shape[0] == 1 and s % ROW_TILE == 0 and ROW_TILE % KEY_TILE == 0
    assert WINDOW == 2 * Q_TILE and KEY_TILE == Q_TILE
    tm = ROW_TILE
    nt = s // tm
    nkt = s // KEY_TILE
    kpt = tm // KEY_TILE
    nch = s // CMP_STRIDE
    nsel = s // SEL_BLOCK
    k_top = min(N_SEL, nsel)
    kd = -(-(HEAD_DIM + nsel) // LANES) * LANES
    G, R = N_KV_HEADS, GQA_REP

    w_in_t = jnp.swapaxes(w_in[0], 0, 1)
    assert w_in_t.shape[0] == _C_GATE + N_GATE
    eye = jnp.eye(G, dtype=F32)

    def expand1(w1):
        w1 = w1.astype(BF16).reshape(2, CMP_STRIDE, HEAD_DIM, CMP_HIDDEN)
        z = jnp.zeros_like(w1)
        rows = [jnp.concatenate([w1 if a == g else z for a in range(G)], axis=-1) for g in range(G)]
        return jnp.stack(rows, axis=2).reshape(2, CMP_STRIDE * D_KV, G * CMP_HIDDEN)

    def expand_pe(pe):
        pe = pe.reshape(2, CMP_STRIDE, 1, HEAD_DIM)
        return jnp.broadcast_to(pe, (2, CMP_STRIDE, G, HEAD_DIM)).reshape(2, 1, CMP_STRIDE * D_KV)

    def expand2(w2):
        return jnp.einsum('hd,ab->ahbd', w2, eye).reshape(G * CMP_HIDDEN, D_KV)

    w1e = jnp.stack([expand1(w_ck1[0]), expand1(w_cv1[0])]).astype(BF16)
    pee = jnp.stack([expand_pe(pe_k[0]), expand_pe(pe_v[0])])
    w2e = jnp.stack([expand2(w_ck2[0]), expand2(w_cv2[0])]).astype(BF16)
    rel_bias = rel_bias.astype(F32)
    smem = pl.BlockSpec(memory_space=pltpu.SMEM)

    n_rest = D_ATTN + 4 * D_CONV
    cw = CMP_STRIDE * D_KV
    qT, kvc, ksel, vselT, kwin, vwinT, gsT, rest = pl.pallas_call(
        partial(_proj_kernel, tm=tm, nsel=nsel, kd=kd),
        grid=(b, nt),
        in_specs=[pl.BlockSpec((1, tm, D_MODEL), lambda i, j: (i, j, 0)),
                  pl.BlockSpec((1, D_MODEL), lambda i, j: (0, 0)),
                  pl.BlockSpec((_C_GATE + N_GATE, D_MODEL), lambda i, j: (0, 0), pipeline_mode=pl.Buffered(1))],
        out_specs=[pl.BlockSpec((1, D_ATTN, tm), lambda i, j: (i, 0, j)),
                   pl.BlockSpec((2, 1, tm // CMP_STRIDE, cw), lambda i, j: (0, i, j, 0)),
                   pl.BlockSpec((1, G, tm, kd), lambda i, j: (i, 0, j, 0)),
                   pl.BlockSpec((1, G, kpt, V_ROWS, KEY_TILE), lambda i, j: (i, 0, j, 0, 0)),
                   pl.BlockSpec((1, G, tm, LANES), lambda i, j: (i, 0, j, 0)),
                   pl.BlockSpec((1, G, kpt, V_ROWS, KEY_TILE), lambda i, j: (i, 0, j, 0, 0)),
                   pl.BlockSpec((1, N_GATE, tm), lambda i, j: (i, 0, j)),
                   pl.BlockSpec((1, tm, n_rest), lambda i, j: (i, j, 0))],
        out_shape=[jax.ShapeDtypeStruct((b, D_ATTN, s), BF16),
                   jax.ShapeDtypeStruct((2, b, nch, cw), F32),
                   jax.ShapeDtypeStruct((b, G, s, kd), BF16),
                   jax.ShapeDtypeStruct((b, G, nkt, V_ROWS, KEY_TILE), BF16),
                   jax.ShapeDtypeStruct((b, G, s, LANES), BF16),
                   jax.ShapeDtypeStruct((b, G, nkt, V_ROWS, KEY_TILE), BF16),
                   jax.ShapeDtypeStruct((b, N_GATE, s), F32),
                   jax.ShapeDtypeStruct((b, s, n_rest), BF16)],
        scratch_shapes=[pltpu.VMEM((2, tm, D_KV), F32),
                        pltpu.VMEM((_C_END, D_MODEL), BF16)],
        compiler_params=_cparams(("arbitrary", "arbitrary")),
        name="in_proj",
    )(x, norm_w, w_in_t)
    gs = gsT.reshape(b, 3, G, R, s)

    lhs_rows = V_ROWS + nsel
    lhs_tail = jnp.asarray(np.concatenate([np.ones((V_ROWS - HEAD_DIM, nch), np.float32),
                                           _overlap_t(nsel, nch)], axis=0), BF16)
    kc_aug, c_lhs = pl.pallas_call(
        partial(_compress_kernel, nch=nch),
        grid=(2, b),
        in_specs=[pl.BlockSpec((1, 1, nch, cw), lambda xk, i: (xk, i, 0, 0)),
                  pl.BlockSpec((1, 2, 1, cw), lambda xk, i: (xk, 0, 0, 0)),
                  pl.BlockSpec((1, 2, cw, G * CMP_HIDDEN), lambda xk, i: (xk, 0, 0, 0)),
                  pl.BlockSpec((1, G * CMP_HIDDEN, D_KV), lambda xk, i: (xk, 0, 0)),
                  pl.BlockSpec((lhs_rows - HEAD_DIM, nch), lambda xk, i: (0, 0))],
        out_specs=[pl.BlockSpec((1, 1, G, nch, LANES), lambda xk, i: (xk, i, 0, 0, 0)),
                   pl.BlockSpec((1, 1, G, lhs_rows, nch), lambda xk, i: (xk, i, 0, 0, 0))],
        out_shape=[jax.ShapeDtypeStruct((2, b, G, nch, LANES), BF16),
                   jax.ShapeDtypeStruct((2, b, G, lhs_rows, nch), BF16)],
        compiler_params=_cparams(("parallel", "parallel")),
        name="compress",
    )(kvc, pee, w1e, w2e, lhs_tail)

    tq3 = Q_TILE_CMP
    cpq = tq3 // CMP_STRIDE
    assert nch // 8 <= LANES - HEAD_DIM and k_top > 3 and (nch // 2) % LANES == 0
    ocT, selb = pl.pallas_call(
        partial(_cmp_kernel, nch=nch, nsel=nsel, tq=tq3, k_top=k_top),
        grid=(b, G, s // tq3),
        in_specs=[smem,
                  pl.BlockSpec((1, R * HEAD_DIM, tq3), lambda i, g, t: (i, g, t)),
                  pl.BlockSpec((1, 1, 1, nch, LANES), lambda i, g, t: (0, i, g, 0, 0)),
                  pl.BlockSpec((1, 1, 1, lhs_rows, nch), lambda i, g, t: (1, i, g, 0, 0)),
                  pl.BlockSpec((1, 1, 1, R, tq3), lambda i, g, t: (i, 0, g, 0, t))],
        out_specs=[pl.BlockSpec((1, R * HEAD_DIM, tq3), lambda i, g, t: (i, g, t)),
                   pl.BlockSpec((1, 1, nsel, tq3), lambda i, g, t: (i, g, 0, t))],
        out_shape=[jax.ShapeDtypeStruct((b, D_ATTN, s), BF16),
                   jax.ShapeDtypeStruct((b, G, nsel, s), BF16)],
        scratch_shapes=[pltpu.VMEM((LANES, R * tq3), BF16),
                        pltpu.VMEM((cpq + 16, R * tq3), F32),
                        pltpu.VMEM((lhs_rows, R * tq3), F32)]
                       + [pltpu.VMEM((nch, tq3), F32)] * R,
        compiler_params=_cparams(("parallel", "parallel", "arbitrary")),
        name="cmp_select",
    )(rel_bias, qT, kc_aug, c_lhs, gs)

    tq = Q_TILE
    nq = R * tq

    o_spec = pl.BlockSpec((1, R * HEAD_DIM, tq), lambda i, g, t: (i, g, t))
    k_spec = lambda kdim: pl.BlockSpec((1, 1, s, kdim), lambda i, g, t: (i, g, 0, 0))
    v_spec = pl.BlockSpec((1, 1, nkt, V_ROWS, KEY_TILE), lambda i, g, t: (i, g, 0, 0, 0))
    gs_spec = lambda branch: pl.BlockSpec((1, 1, 1, R, tq), lambda i, g, t: (i, branch, g, 0, t))
    oswT = pl.pallas_call(
        partial(_branches_kernel, tq=tq, nsel=nsel, kd=kd),
        grid=(b, G, s // tq),
        in_specs=[smem, o_spec, pl.BlockSpec((1, 1, nsel, tq), lambda i, g, t: (i, g, 0, t)),
                  k_spec(kd), v_spec, k_spec(LANES), v_spec, gs_spec(1), gs_spec(2)],
        out_specs=o_spec,
        out_shape=jax.ShapeDtypeStruct((b, D_ATTN, s), BF16),
        scratch_shapes=[pltpu.VMEM((1, R * HEAD_DIM, tq), F32),
                        pltpu.VMEM(((SEL_TBL_TILES + WIN_TBL_TILES) * tq, nq), F32),
                        pltpu.VMEM((kd, nq), BF16),
                        pltpu.VMEM((1, nq), F32),
                        pltpu.VMEM((V_ROWS, nq), F32),
                        pltpu.VMEM((1, nq), F32)]
                       + [pltpu.VMEM((tq, nq), F32)] * N_SLOTS
                       + [pltpu.VMEM((1, nq), F32)] * N_SLOTS
                       + [pltpu.VMEM((1, nq), F32)] * N_SLOTS
                       + [pltpu.VMEM((tq, nq), BF16)] * N_SLOTS,
        compiler_params=_cparams(("parallel", "parallel", "arbitrary")),
        name="sel_win_attn",
    )(rel_bias, qT, selb, ksel, vselT, kwin, vwinT, gs, gs)

    hb = tm // HALO_ROWS
    col = lambda c: pl.BlockSpec((1, tm, D_CONV), lambda i, j, c=c: (i, j, c))
    halo = lambda c: pl.BlockSpec((1, HALO_ROWS, D_CONV), lambda i, j, c=c: (i, jnp.maximum(j * hb - 1, 0), c))
    oT = pl.BlockSpec((1, D_ATTN, tm), lambda i, j: (i, 0, j))
    out = pl.pallas_call(
        partial(_out_kernel, tm=tm),
        grid=(b, nt),
        in_specs=[oT, oT, col(0), col(1), col(2), col(3), col(4), halo(1), halo(3),
                  pl.BlockSpec((1, tm, D_MODEL), lambda i, j: (i, j, 0)),
                  pl.BlockSpec((3, D_CONV), lambda i, j: (0, 0)),
                  pl.BlockSpec((D_MODEL, D_MODEL), lambda i, j: (0, 0)),
                  pl.BlockSpec((1, D_MODEL), lambda i, j: (0, 0))],
        out_specs=pl.BlockSpec((1, tm, D_MODEL), lambda i, j: (i, j, 0)),
        out_shape=jax.ShapeDtypeStruct((b, s, D_MODEL), F32),
        compiler_params=_cparams(("parallel", "parallel")),
        name="out_proj",
    )(ocT, oswT, rest, rest, rest, rest, rest, rest, rest, x, conv_w[0], w_out[0].astype(BF16),
      final_norm_w.reshape(1, D_MODEL))
    return out
```

```python
import math
from functools import partial

import numpy as np
import jax
import jax.numpy as jnp
from jax import lax
from jax.experimental import pallas as pl
from jax.experimental.pallas import tpu as pltpu

D_MODEL = 1024
N_HEADS = 8
HEAD_DIM = 64
N_KV_HEADS = 2
GQA_REP = N_HEADS // N_KV_HEADS
D_ATTN = N_HEADS * HEAD_DIM
D_KV = N_KV_HEADS * HEAD_DIM
D_CONV = D_MODEL - D_ATTN
CMP_BLOCK = 32
CMP_STRIDE = 16
CMP_HIDDEN = 2 * HEAD_DIM
SEL_BLOCK = 64
N_SEL = 16
WINDOW = 512
N_BUCKETS = 32
MAX_DISTANCE = 128
EPS = 1e-6
NEG = -1e30
FORCE_SCORE = 1e3

LOG2E = 1.4426950408889634
LANES = 128
MXU_COLS = 256
V_ROWS = HEAD_DIM + 16
ROW_TILE = 512
KEY_TILE = 256
Q_TILE_CMP = 512
Q_TILE = 256
HALO_ROWS = 16
MAX_LAG = 64.0
N_SLOTS = 2
VMEM_LIMIT = 48 * 1024 * 1024

F32 = jnp.float32
BF16 = jnp.bfloat16

_C_Q = 0
_C_CMP = D_ATTN
_C_SW = _C_CMP + 2 * D_KV
_C_REST = _C_SW + 4 * D_KV
_C_GATE = _C_REST + D_ATTN + 4 * D_CONV
_C_END = _C_GATE + LANES
N_GATE = 3 * N_HEADS


def _bucket_lower_bounds():
    max_exact = N_BUCKETS // 2
    d = np.arange(2 * MAX_DISTANCE, dtype=np.int32)
    nf = np.maximum(d, 1).astype(np.float32)
    large = max_exact + (np.log(nf / np.float32(max_exact)) / np.float32(math.log(MAX_DISTANCE / max_exact))
                         * np.float32(N_BUCKETS - max_exact)).astype(np.int32)
    bucket = np.where(d < max_exact, d, np.minimum(large, N_BUCKETS - 1))
    assert np.all(np.diff(bucket) >= 0) and np.all(bucket[MAX_DISTANCE:] == N_BUCKETS - 1)
    return [int(np.argmax(bucket >= b)) for b in range(N_BUCKETS)]


_BUCKET_LO = _bucket_lower_bounds()


def _rel_bias_delta(dist, bias_of_bucket):
    last = bias_of_bucket(N_BUCKETS - 1)
    val = jnp.full(dist.shape, (bias_of_bucket(0) - last) * LOG2E, F32)
    for b in range(1, N_BUCKETS - 1):
        val = jnp.where(dist >= _BUCKET_LO[b], (bias_of_bucket(b) - last) * LOG2E, val)
    val = jnp.where(dist >= _BUCKET_LO[N_BUCKETS - 1], 0.0, val)
    return jnp.where(dist < 0, NEG, val)


def _silu(z):
    return z * (1.0 / (1.0 + jnp.exp2(z * -LOG2E)))


def _proj_kernel(x_ref, nw_ref, wT_ref, qT_ref, kvc_ref, ksel_ref, vselT_ref, kwin_ref, vwinT_ref,
                 gsT_ref, rest_ref, kv_scr, w_scr, *, tm, nsel, kd):
    st = pl.program_id(1)

    @pl.when((pl.program_id(0) == 0) & (st == 0))
    def _():
        c0 = _C_REST
        chunk = 256
        for dst, src, rows in ((0, 0, c0), (c0, c0 + N_GATE, _C_GATE - c0)):
            for r in range(0, rows, chunk):
                w_scr[dst + r:dst + r + chunk, :] = wT_ref[src + r:src + r + chunk, :].astype(BF16)
        w_scr[_C_GATE:_C_END, :] = jnp.concatenate(
            [wT_ref[c0:c0 + N_GATE, :], jnp.zeros((LANES - N_GATE, D_MODEL), F32)], axis=0).astype(BF16)

    x = x_ref[0]
    ms = jnp.mean(x * x, axis=-1, keepdims=True)
    h = (x * lax.rsqrt(ms + EPS) * nw_ref[...]).astype(BF16)

    def mm(lo, hi):
        return lax.dot_general(h, w_scr[lo:hi, :], (((1,), (1,)), ((), ())), preferred_element_type=F32)

    q = mm(_C_Q, _C_CMP) * (HEAD_DIM ** -0.5 * LOG2E)
    qT_ref[0] = q.T.astype(BF16)

    kvc = mm(_C_CMP, _C_SW)
    for xk in range(2):
        kv_scr[xk] = kvc[:, xk * D_KV:(xk + 1) * D_KV]
        for j in range(CMP_STRIDE):
            kvc_ref[xk, 0, :, j * D_KV:(j + 1) * D_KV] = kv_scr[xk, pl.ds(j, tm // CMP_STRIDE, stride=CMP_STRIDE), :]

    a = mm(_C_SW, _C_REST)
    pos = st * tm + lax.broadcasted_iota(jnp.int32, (tm, nsel), 0)
    col = lax.broadcasted_iota(jnp.int32, (tm, nsel), 1)
    onehot = jnp.where((pos // SEL_BLOCK) == col, 1.0, 0.0).astype(BF16)
    ones = jnp.ones((V_ROWS - HEAD_DIM, tm), BF16)
    zeros_k = jnp.zeros((tm, LANES - HEAD_DIM), BF16)
    vsT = a[:, D_KV:2 * D_KV].T
    vwT = a[:, 3 * D_KV:4 * D_KV].T
    for g in range(N_KV_HEADS):
        lo, hi = g * HEAD_DIM, (g + 1) * HEAD_DIM
        parts = [a[:, lo:hi].astype(BF16), onehot]
        if kd > HEAD_DIM + nsel:
            parts.append(jnp.zeros((tm, kd - HEAD_DIM - nsel), BF16))
        ksel_ref[0, g] = jnp.concatenate(parts, axis=1)
        kwin_ref[0, g] = jnp.concatenate([a[:, 2 * D_KV + lo:2 * D_KV + hi].astype(BF16), zeros_k], axis=1)
        vs = jnp.concatenate([vsT[lo:hi].astype(BF16), ones], axis=0)
        vw = jnp.concatenate([vwT[lo:hi].astype(BF16), ones], axis=0)
        for kt in range(tm // KEY_TILE):
            vselT_ref[0, g, kt] = vs[:, kt * KEY_TILE:(kt + 1) * KEY_TILE]
            vwinT_ref[0, g, kt] = vw[:, kt * KEY_TILE:(kt + 1) * KEY_TILE]

    rest_ref[0] = mm(_C_REST, _C_GATE).astype(BF16)
    gates = jax.nn.sigmoid(mm(_C_GATE, _C_END))
    gsT_ref[0] = gates.T[:N_GATE]


def _compress_kernel(c_ref, pe_ref, w1_ref, w2_ref, lhs_tail_ref, aug_ref, t_ref, *, nch):
    c = c_ref[0, 0]
    a0 = jnp.dot((c + pe_ref[0, 0]).astype(BF16), w1_ref[0, 0], preferred_element_type=F32)
    a1 = jnp.dot((c + pe_ref[0, 1]).astype(BF16), w1_ref[0, 1], preferred_element_type=F32)
    hid = a0 + pltpu.roll(a1, nch - 1, 0)
    out = jnp.dot(_silu(hid).astype(BF16), w2_ref[0], preferred_element_type=F32)
    grp = lax.broadcasted_iota(jnp.int32, (nch, LANES - HEAD_DIM), 0) // 8
    col = lax.broadcasted_iota(jnp.int32, (nch, LANES - HEAD_DIM), 1)
    onehot = jnp.where(grp == col, 1.0, 0.0).astype(BF16)
    out_t = out.T.astype(BF16)
    for g in range(N_KV_HEADS):
        lo, hi = g * HEAD_DIM, (g + 1) * HEAD_DIM
        aug_ref[0, 0, g] = jnp.concatenate([out[:, lo:hi].astype(BF16), onehot], axis=1)
        t_ref[0, 0, g] = jnp.concatenate([out_t[lo:hi], lhs_tail_ref[...]], axis=0)


def _cmp_kernel(rb_ref, qT_ref, kc_ref, lhs_ref, gs_ref, ocT_ref, selb_ref,
                qa_ref, band_ref, acc_ref, s0_ref, s1_ref, s2_ref, s3_ref, *, nch, nsel, tq, k_top):
    g = pl.program_id(1)
    qt = pl.program_id(2)
    s_refs = (s0_ref, s1_ref, s2_ref, s3_ref)
    cpq = tq // CMP_STRIDE
    win = cpq + 16
    lim = cpq * (qt + 1)
    nq = GQA_REP * tq

    @pl.when(qt <= 1)
    def _():
        row = lax.broadcasted_iota(jnp.int32, (win, tq), 0) + jnp.maximum(cpq * qt - 16, 0)
        lane = lax.broadcasted_iota(jnp.int32, (win, tq), 1)
        dist = (qt * tq + lane) - (row * CMP_STRIDE + CMP_BLOCK - 1)
        for r in range(GQA_REP):
            band_ref[:, r * tq:(r + 1) * tq] = _rel_bias_delta(dist, lambda b, r=r: rb_ref[b, g * GQA_REP + r])

    q = qT_ref[0]
    qa_ref[0:HEAD_DIM, :] = jnp.concatenate([q[r * HEAD_DIM:(r + 1) * HEAD_DIM] for r in range(GQA_REP)], axis=1)
    grp = lax.broadcasted_iota(jnp.int32, (LANES - HEAD_DIM, nq), 0)
    qa_ref[HEAD_DIM:LANES, :] = jnp.where(grp * 8 >= lim, NEG, 0.0).astype(BF16)

    def run(rows):
        start = pl.multiple_of(jnp.clip(cpq * qt - 16, 0, rows - win), 8)
        k = kc_ref[0, 0, 0, 0:rows, :]
        lhs = lhs_ref[0, 0, 0, :, 0:rows]
        cols = [slice(r * tq, (r + 1) * tq) for r in range(GQA_REP)]

        def qk(r):
            s_refs[r][0:rows, :] = jnp.dot(k, qa_ref[:, cols[r]], preferred_element_type=F32)

        ahead = 3
        for r in range(ahead):
            qk(r)
        for r in range(GQA_REP):
            if r + ahead < GQA_REP:
                qk(r + ahead)
            sr = s_refs[r]
            sr[pl.ds(start, win), :] = sr[pl.ds(start, win), :] + band_ref[:, cols[r]]
            s = sr[0:rows, :]
            p = jnp.exp2(s - jnp.max(s, axis=0, keepdims=True)).astype(BF16)
            acc_ref[:, cols[r]] = jnp.dot(lhs, p, preferred_element_type=F32)

    half = nch // 2

    @pl.when(lim <= half)
    def _():
        run(half)

    @pl.when(lim > half)
    def _():
        run(nch)

    l = acc_ref[HEAD_DIM:HEAD_DIM + 1, :]
    t4 = qt * tq + (lax.broadcasted_iota(jnp.int32, l.shape, 1) & (tq - 1))
    inv = jnp.where(t4 >= CMP_BLOCK - 1, 1.0 / jnp.maximum(l, 1e-30), 0.0)
    imp = None
    for r in range(GQA_REP):
        cs = slice(r * tq, (r + 1) * tq)
        ocT_ref[0, r * HEAD_DIM:(r + 1) * HEAD_DIM, :] = (acc_ref[0:HEAD_DIM, cs] * inv[:, cs]
                                                          * gs_ref[0, 0, 0, r:r + 1, :]).astype(BF16)
        part = acc_ref[V_ROWS:V_ROWS + nsel, cs] * inv[:, cs]
        imp = part if imp is None else imp + part

    n = lax.broadcasted_iota(jnp.int32, (nsel, tq), 0)
    blk = (qt * tq + lax.broadcasted_iota(jnp.int32, (nsel, tq), 1)) // SEL_BLOCK
    forced = (n == 0) | (n == blk) | (n == blk - 1)
    candidate = (n <= blk) & jnp.logical_not(forced)
    score = jnp.where(forced, -2.0, jnp.where(n <= blk, imp, -1.0))

    def pick_all(_, score):
        return jnp.where(score == jnp.max(score, axis=0, keepdims=True), -2.0, score)

    taken = lax.fori_loop(0, k_top - 3, pick_all, score) == -2.0
    selb_ref[0, 0] = jnp.where(taken, 0.0, NEG).astype(BF16)
    n_taken = jnp.sum(jnp.where(taken & candidate, 1.0, 0.0), axis=0, keepdims=True)

    @pl.when(jnp.max(n_taken) > k_top - 3)
    def _():
        def pick_one(_, score):
            mx = jnp.max(score, axis=0, keepdims=True)
            idx = jnp.min(jnp.where(score == mx, n, nsel), axis=0, keepdims=True)
            return jnp.where(n == idx, -2.0, score)

        taken = lax.fori_loop(0, k_top - 3, pick_one, score) == -2.0
        selb_ref[0, 0] = jnp.where(taken, 0.0, NEG).astype(BF16)


def _flash_kernel(*refs, tq, nsel, kd, n_tbl_tiles, window, selected, add_ref=None):
    if selected:
        rb_ref, qT_ref, selb_ref, k_ref, vT_ref, gs_ref, o_ref, tbl_ref, qa_ref, m_ref, acc_ref, lag_ref, *slot_refs = refs
    else:
        rb_ref, qT_ref, k_ref, vT_ref, gs_ref, o_ref, tbl_ref, qa_ref, m_ref, acc_ref, lag_ref, *slot_refs = refs
    s_ref, mt_ref = slot_refs[0:1], slot_refs[1:2]
    al_ref, p_ref = slot_refs[2:2 + N_SLOTS], slot_refs[2 + N_SLOTS:2 + 2 * N_SLOTS]
    g = pl.program_id(1)
    qt = pl.program_id(2)
    nq = GQA_REP * tq

    @pl.when(qt == 0)
    def _():
        for j in range(n_tbl_tiles):
            key = lax.broadcasted_iota(jnp.int32, (tq, tq), 0) + j * tq
            qry = lax.broadcasted_iota(jnp.int32, (tq, tq), 1) + (n_tbl_tiles - 1) * tq
            dist = qry - key
            for r in range(GQA_REP):
                val = _rel_bias_delta(dist, lambda b, r=r: rb_ref[b, g * GQA_REP + r])
                if window is not None:
                    val = jnp.where(dist >= window, NEG, val)
                tbl_ref[j * tq:(j + 1) * tq, r * tq:(r + 1) * tq] = val

    m_ref[...] = jnp.full((1, nq), NEG, F32)
    acc_ref[...] = jnp.zeros((V_ROWS, nq), F32)

    def serial(t, tbl_tile):
        k = k_ref[0, 0, pl.ds(pl.multiple_of(t * tq, tq), tq), :]
        s = jnp.dot(k, qa_ref[...], preferred_element_type=F32)
        if tbl_tile is not None:
            s = s + tbl_ref[tbl_tile * tq:(tbl_tile + 1) * tq, :]
        s_ref[0][...] = s
        mt_ref[0][...] = jnp.max(s, axis=0, keepdims=True)
        m_old = m_ref[...]
        m_new = jnp.maximum(m_old, mt_ref[0][...])
        m_ref[...] = m_new
        p_ref[0][...] = jnp.exp2(s_ref[0][...] - m_new).astype(BF16)
        acc_ref[...] = jnp.exp2(m_old - m_new) * acc_ref[...] + jnp.dot(vT_ref[0, 0, t], p_ref[0][...],
                                                                        preferred_element_type=F32)

    last = n_tbl_tiles - 1

    def stage_lagged(t, slot, tbl_tile, first=False):
        k = k_ref[0, 0, pl.ds(pl.multiple_of(t * tq, tq), tq), :]
        s = jnp.dot(k, qa_ref[...], preferred_element_type=F32)
        if tbl_tile is not None:
            s = s + tbl_ref[tbl_tile * tq:(tbl_tile + 1) * tq, :]
        m = s[0:1, :] if first else m_ref[...]
        mt = jnp.max(s, axis=0, keepdims=True)
        p_ref[slot][...] = jnp.exp2((s - m).astype(BF16))
        m_new = jnp.maximum(m, mt)
        m_ref[...] = m_new
        al_ref[slot][...] = jnp.exp2(m - m_new)
        lag_ref[...] = (mt - m) if first else jnp.maximum(lag_ref[...], mt - m)

    def stage_pv_lagged(t, slot):
        acc_ref[...] = (acc_ref[...] + jnp.dot(vT_ref[0, 0, t], p_ref[slot][...],
                                               preferred_element_type=F32)) * al_ref[slot][...]

    def lagged_step(t, slot, tbl_tile, prev=None):
        stage_lagged(t, slot, tbl_tile)
        stage_pv_lagged(t - 1 if prev is None else prev, 1 - slot)

    if selected:
        n = qt + 1

        @pl.when((n >= 4) & ((n & 1) == 0))
        def _():
            stage_lagged(0, 1, None, first=True)
            lagged_step(1, 0, None)

        @pl.when((n >= 4) & ((n & 1) == 1))
        def _():
            stage_lagged(0, 0, None, first=True)

        @pl.when(n >= 4)
        def _():
            t0 = 2 - (n & 1)

            pairs = (n - 2 - t0) >> 1

            def pair(t):
                lagged_step(t, 1, None)
                lagged_step(t + 1, 0, None)

            def body(j, carry):
                pair(t0 + 4 * j)
                pair(t0 + 4 * j + 2)
                return carry
            lax.fori_loop(0, pairs >> 1, body, 0)

            @pl.when((pairs & 1) == 1)
            def _():
                pair(n - 4)

        @pl.when(n >= 4)
        def _():
            lagged_step(n - 2, 1, 0)
            lagged_step(n - 1, 0, 1)
            stage_pv_lagged(n - 1, 0)

            @pl.when(jnp.max(lag_ref[...]) > MAX_LAG)
            def _():
                m_ref[...] = jnp.full((1, nq), NEG, F32)
                acc_ref[...] = jnp.zeros((V_ROWS, nq), F32)

                def redo(t, carry):
                    serial(t, None)
                    return carry
                lax.fori_loop(0, n - 2, redo, 0)
                serial(n - 2, 0)
                serial(n - 1, 1)

        @pl.when(n == 3)
        def _():
            serial(0, None)

        @pl.when((n >= 2) & (n < 4))
        def _():
            serial(qt - 1, 0)

        @pl.when(n < 4)
        def _():
            serial(qt, 1)
    else:
        @pl.when(qt >= last)
        def _():
            stage_lagged(qt, 0, 2, first=True)
            lagged_step(qt - 1, 1, 1, prev=qt)
            lagged_step(qt - 2, 0, 0, prev=qt - 1)
            stage_pv_lagged(qt - 2, 0)

            @pl.when(jnp.max(lag_ref[...]) > MAX_LAG)
            def _():
                m_ref[...] = jnp.full((1, nq), NEG, F32)
                acc_ref[...] = jnp.zeros((V_ROWS, nq), F32)
                for j in range(n_tbl_tiles):
                    serial(qt - last + j, j)

        @pl.when(qt == 1)
        def _():
            serial(0, 1)

        @pl.when(qt < last)
        def _():
            serial(qt, 2)

    acc = acc_ref[...]
    o = acc[0:HEAD_DIM] / jnp.maximum(acc[HEAD_DIM:HEAD_DIM + 1], 1e-30)
    for r in range(GQA_REP):
        rows = slice(r * HEAD_DIM, (r + 1) * HEAD_DIM)
        val = o[:, r * tq:(r + 1) * tq] * gs_ref[0, 0, 0, r:r + 1, :]
        if add_ref is not None:
            val = val + add_ref[0, rows, :]
        o_ref[0, rows, :] = val.astype(o_ref.dtype)


SEL_TBL_TILES = 2
WIN_TBL_TILES = 3


def _branches_kernel(rb_ref, qT_ref, selb_ref, ksel_ref, vselT_ref, kwin_ref, vwinT_ref, gs_sel_ref, gs_win_ref,
                     o_ref, owin_ref, tbl_ref, qa_ref, *state, tq, nsel, kd):
    q = qT_ref[0]
    nq = GQA_REP * tq
    qa_ref[0:HEAD_DIM, :] = jnp.concatenate([q[r * HEAD_DIM:(r + 1) * HEAD_DIM] for r in range(GQA_REP)], axis=1)
    qa_ref[HEAD_DIM:HEAD_DIM + nsel, :] = jnp.concatenate([selb_ref[0, 0]] * GQA_REP, axis=1)
    if kd > HEAD_DIM + nsel:
        qa_ref[HEAD_DIM + nsel:kd, :] = jnp.zeros((kd - HEAD_DIM - nsel, nq), BF16)

    win_tbl = tbl_ref.at[SEL_TBL_TILES * tq:(SEL_TBL_TILES + WIN_TBL_TILES) * tq]
    _flash_kernel(rb_ref, qT_ref, kwin_ref, vwinT_ref, gs_win_ref, owin_ref, win_tbl, qa_ref.at[0:LANES], *state,
                  tq=tq, nsel=nsel, kd=LANES, n_tbl_tiles=WIN_TBL_TILES, window=WINDOW, selected=False)
    _flash_kernel(rb_ref, qT_ref, selb_ref, ksel_ref, vselT_ref, gs_sel_ref, o_ref,
                  tbl_ref.at[0:SEL_TBL_TILES * tq], qa_ref, *state,
                  tq=tq, nsel=nsel, kd=kd, n_tbl_tiles=SEL_TBL_TILES, window=None, selected=True, add_ref=owin_ref)


def _out_kernel(oc_ref, osw_ref, za_ref, ch_ref, cb_ref, cc_ref, zc_ref, hh_ref, hc_ref,
                x_ref, cw_ref, wo_ref, fw_ref, out_ref, *, tm):
    st = pl.program_id(1)
    f32 = lambda ref: ref[0].astype(F32)
    attn = (f32(oc_ref) + f32(osw_ref)).T * _silu(f32(za_ref))

    u = f32(cc_ref) * f32(ch_ref)
    halo = jnp.where(st == 0, 0.0, f32(hc_ref) * f32(hh_ref))
    row = lax.broadcasted_iota(jnp.int32, u.shape, 0)
    h1, h2 = halo[HALO_ROWS - 1:HALO_ROWS], halo[HALO_ROWS - 2:HALO_ROWS - 1]
    u1 = jnp.where(row == 0, h1, pltpu.roll(u, 1, 0))
    u2 = jnp.where(row == 0, h2, jnp.where(row == 1, h1, pltpu.roll(u, 2, 0)))
    y = cw_ref[0:1] * u2 + cw_ref[1:2] * u1 + cw_ref[2:3] * u
    conv = f32(cb_ref) * y * _silu(f32(zc_ref))

    mixed = jnp.concatenate([attn, conv], axis=1).astype(BF16)
    y2 = x_ref[0] + jnp.dot(mixed, wo_ref[...], preferred_element_type=F32)
    ms = jnp.mean(y2 * y2, axis=-1, keepdims=True)
    out_ref[0] = y2 * lax.rsqrt(ms + EPS) * fw_ref[...]


def _overlap_t(nsel, nch):
    c = np.arange(nch)[None, :] * CMP_STRIDE
    n = np.arange(nsel)[:, None] * SEL_BLOCK
    return ((c < n + SEL_BLOCK) & (c + CMP_BLOCK > n)).astype(np.float32)


def _cparams(sem):
    return pltpu.CompilerParams(dimension_semantics=sem, vmem_limit_bytes=VMEM_LIMIT)


def kernel(x, norm_w, w_in, w_ck1, w_ck2, pe_k, w_cv1, w_cv2, pe_v, conv_w, w_out, rel_bias, final_norm_w):
    b, s, _ = x.shape
    assert norm_w.shape[0] == 1 and s % ROW_TILE == 0 and ROW_TILE % KEY_TILE == 0
    assert WINDOW == 2 * Q_TILE and KEY_TILE == Q_TILE
    tm = ROW_TILE
    nt = s // tm
    nkt = s // KEY_TILE
    kpt = tm // KEY_TILE
    nch = s // CMP_STRIDE
    nsel = s // SEL_BLOCK
    k_top = min(N_SEL, nsel)
    kd = -(-(HEAD_DIM + nsel) // LANES) * LANES
    G, R = N_KV_HEADS, GQA_REP

    w_in_t = jnp.swapaxes(w_in[0], 0, 1)
    assert w_in_t.shape[0] == _C_GATE + N_GATE
    eye = jnp.eye(G, dtype=F32)

    def expand1(w1):
        w1 = w1.astype(BF16).reshape(2, CMP_STRIDE, HEAD_DIM, CMP_HIDDEN)
        z = jnp.zeros_like(w1)
        rows = [jnp.concatenate([w1 if a == g else z for a in range(G)], axis=-1) for g in range(G)]
        return jnp.stack(rows, axis=2).reshape(2, CMP_STRIDE * D_KV, G * CMP_HIDDEN)

    def expand_pe(pe):
        pe = pe.reshape(2, CMP_STRIDE, 1, HEAD_DIM)
        return jnp.broadcast_to(pe, (2, CMP_STRIDE, G, HEAD_DIM)).reshape(2, 1, CMP_STRIDE * D_KV)

    def expand2(w2):
        return jnp.einsum('hd,ab->ahbd', w2, eye).reshape(G * CMP_HIDDEN, D_KV)

    w1e = jnp.stack([expand1(w_ck1[0]), expand1(w_cv1[0])]).astype(BF16)
    pee = jnp.stack([expand_pe(pe_k[0]), expand_pe(pe_v[0])])
    w2e = jnp.stack([expand2(w_ck2[0]), expand2(w_cv2[0])]).astype(BF16)
    rel_bias = rel_bias.astype(F32)
    smem = pl.BlockSpec(memory_space=pltpu.SMEM)

    n_rest = D_ATTN + 4 * D_CONV
    cw = CMP_STRIDE * D_KV
    qT, kvc, ksel, vselT, kwin, vwinT, gsT, rest = pl.pallas_call(
        partial(_proj_kernel, tm=tm, nsel=nsel, kd=kd),
        grid=(b, nt),
        in_specs=[pl.BlockSpec((1, tm, D_MODEL), lambda i, j: (i, j, 0)),
                  pl.BlockSpec((1, D_MODEL), lambda i, j: (0, 0)),
                  pl.BlockSpec((_C_GATE + N_GATE, D_MODEL), lambda i, j: (0, 0), pipeline_mode=pl.Buffered(1))],
        out_specs=[pl.BlockSpec((1, D_ATTN, tm), lambda i, j: (i, 0, j)),
                   pl.BlockSpec((2, 1, tm // CMP_STRIDE, cw), lambda i, j: (0, i, j, 0)),
                   pl.BlockSpec((1, G, tm, kd), lambda i, j: (i, 0, j, 0)),
                   pl.BlockSpec((1, G, kpt, V_ROWS, KEY_TILE), lambda i, j: (i, 0, j, 0, 0)),
                   pl.BlockSpec((1, G, tm, LANES), lambda i, j: (i, 0, j, 0)),
                   pl.BlockSpec((1, G, kpt, V_ROWS, KEY_TILE), lambda i, j: (i, 0, j, 0, 0)),
                   pl.BlockSpec((1, N_GATE, tm), lambda i, j: (i, 0, j)),
                   pl.BlockSpec((1, tm, n_rest), lambda i, j: (i, j, 0))],
        out_shape=[jax.ShapeDtypeStruct((b, D_ATTN, s), BF16),
                   jax.ShapeDtypeStruct((2, b, nch, cw), F32),
                   jax.ShapeDtypeStruct((b, G, s, kd), BF16),
                   jax.ShapeDtypeStruct((b, G, nkt, V_ROWS, KEY_TILE), BF16),
                   jax.ShapeDtypeStruct((b, G, s, LANES), BF16),
                   jax.ShapeDtypeStruct((b, G, nkt, V_ROWS, KEY_TILE), BF16),
                   jax.ShapeDtypeStruct((b, N_GATE, s), F32),
                   jax.ShapeDtypeStruct((b, s, n_rest), BF16)],
        scratch_shapes=[pltpu.VMEM((2, tm, D_KV), F32),
                        pltpu.VMEM((_C_END, D_MODEL), BF16)],
        compiler_params=_cparams(("arbitrary", "arbitrary")),
        name="in_proj",
    )(x, norm_w, w_in_t)
    gs = gsT.reshape(b, 3, G, R, s)

    lhs_rows = V_ROWS + nsel
    lhs_tail = jnp.asarray(np.concatenate([np.ones((V_ROWS - HEAD_DIM, nch), np.float32),
                                           _overlap_t(nsel, nch)], axis=0), BF16)
    kc_aug, c_lhs = pl.pallas_call(
        partial(_compress_kernel, nch=nch),
        grid=(2, b),
        in_specs=[pl.BlockSpec((1, 1, nch, cw), lambda xk, i: (xk, i, 0, 0)),
                  pl.BlockSpec((1, 2, 1, cw), lambda xk, i: (xk, 0, 0, 0)),
                  pl.BlockSpec((1, 2, cw, G * CMP_HIDDEN), lambda xk, i: (xk, 0, 0, 0)),
                  pl.BlockSpec((1, G * CMP_HIDDEN, D_KV), lambda xk, i: (xk, 0, 0)),
                  pl.BlockSpec((lhs_rows - HEAD_DIM, nch), lambda xk, i: (0, 0))],
        out_specs=[pl.BlockSpec((1, 1, G, nch, LANES), lambda xk, i: (xk, i, 0, 0, 0)),
                   pl.BlockSpec((1, 1, G, lhs_rows, nch), lambda xk, i: (xk, i, 0, 0, 0))],
        out_shape=[jax.ShapeDtypeStruct((2, b, G, nch, LANES), BF16),
                   jax.ShapeDtypeStruct((2, b, G, lhs_rows, nch), BF16)],
        compiler_params=_cparams(("parallel", "parallel")),
        name="compress",
    )(kvc, pee, w1e, w2e, lhs_tail)

    tq3 = Q_TILE_CMP
    cpq = tq3 // CMP_STRIDE
    assert nch // 8 <= LANES - HEAD_DIM and k_top > 3 and (nch // 2) % LANES == 0
    ocT, selb = pl.pallas_call(
        partial(_cmp_kernel, nch=nch, nsel=nsel, tq=tq3, k_top=k_top),
        grid=(b, G, s // tq3),
        in_specs=[smem,
                  pl.BlockSpec((1, R * HEAD_DIM, tq3), lambda i, g, t: (i, g, t)),
                  pl.BlockSpec((1, 1, 1, nch, LANES), lambda i, g, t: (0, i, g, 0, 0)),
                  pl.BlockSpec((1, 1, 1, lhs_rows, nch), lambda i, g, t: (1, i, g, 0, 0)),
                  pl.BlockSpec((1, 1, 1, R, tq3), lambda i, g, t: (i, 0, g, 0, t))],
        out_specs=[pl.BlockSpec((1, R * HEAD_DIM, tq3), lambda i, g, t: (i, g, t)),
                   pl.BlockSpec((1, 1, nsel, tq3), lambda i, g, t: (i, g, 0, t))],
        out_shape=[jax.ShapeDtypeStruct((b, D_ATTN, s), BF16),
                   jax.ShapeDtypeStruct((b, G, nsel, s), BF16)],
        scratch_shapes=[pltpu.VMEM((LANES, R * tq3), BF16),
                        pltpu.VMEM((cpq + 16, R * tq3), F32),
                        pltpu.VMEM((lhs_rows, R * tq3), F32)]
                       + [pltpu.VMEM((nch, tq3), F32)] * R,
        compiler_params=_cparams(("parallel", "parallel", "arbitrary")),
        name="cmp_select",
    )(rel_bias, qT, kc_aug, c_lhs, gs)

    tq = Q_TILE
    nq = R * tq

    o_spec = pl.BlockSpec((1, R * HEAD_DIM, tq), lambda i, g, t: (i, g, t))
    k_spec = lambda kdim: pl.BlockSpec((1, 1, s, kdim), lambda i, g, t: (i, g, 0, 0))
    v_spec = pl.BlockSpec((1, 1, nkt, V_ROWS, KEY_TILE), lambda i, g, t: (i, g, 0, 0, 0))
    gs_spec = lambda branch: pl.BlockSpec((1, 1, 1, R, tq), lambda i, g, t: (i, branch, g, 0, t))
    oswT = pl.pallas_call(
        partial(_branches_kernel, tq=tq, nsel=nsel, kd=kd),
        grid=(b, G, s // tq),
        in_specs=[smem, o_spec, pl.BlockSpec((1, 1, nsel, tq), lambda i, g, t: (i, g, 0, t)),
                  k_spec(kd), v_spec, k_spec(LANES), v_spec, gs_spec(1), gs_spec(2)],
        out_specs=o_spec,
        out_shape=jax.ShapeDtypeStruct((b, D_ATTN, s), BF16),
        scratch_shapes=[pltpu.VMEM((1, R * HEAD_DIM, tq), F32),
                        pltpu.VMEM(((SEL_TBL_TILES + WIN_TBL_TILES) * tq, nq), F32),
                        pltpu.VMEM((kd, nq), BF16),
                        pltpu.VMEM((1, nq), F32),
                        pltpu.VMEM((V_ROWS, nq), F32),
                        pltpu.VMEM((1, nq), F32)]
                       + [pltpu.VMEM((tq, nq), F32)]
                       + [pltpu.VMEM((1, nq), F32)]
                       + [pltpu.VMEM((1, nq), F32)] * N_SLOTS
                       + [pltpu.VMEM((tq, nq), BF16)] * N_SLOTS,
        compiler_params=_cparams(("parallel", "parallel", "arbitrary")),
        name="sel_win_attn",
    )(rel_bias, qT, selb, ksel, vselT, kwin, vwinT, gs, gs)

    hb = tm // HALO_ROWS
    col = lambda c: pl.BlockSpec((1, tm, D_CONV), lambda i, j, c=c: (i, j, c))
    halo = lambda c: pl.BlockSpec((1, HALO_ROWS, D_CONV), lambda i, j, c=c: (i, jnp.maximum(j * hb - 1, 0), c))
    oT = pl.BlockSpec((1, D_ATTN, tm), lambda i, j: (i, 0, j))
    out = pl.pallas_call(
        partial(_out_kernel, tm=tm),
        grid=(b, nt),
        in_specs=[oT, oT, col(0), col(1), col(2), col(3), col(4), halo(1), halo(3),
                  pl.BlockSpec((1, tm, D_MODEL), lambda i, j: (i, j, 0)),
                  pl.BlockSpec((3, D_CONV), lambda i, j: (0, 0)),
                  pl.BlockSpec((D_MODEL, D_MODEL), lambda i, j: (0, 0)),
                  pl.BlockSpec((1, D_MODEL), lambda i, j: (0, 0))],
        out_specs=pl.BlockSpec((1, tm, D_MODEL), lambda i, j: (i, j, 0)),
        out_shape=jax.ShapeDtypeStruct((b, s, D_MODEL), F32),
        compiler_params=_cparams(("parallel", "parallel")),
        name="out_proj",
    )(ocT, oswT, rest, rest, rest, rest, rest, rest, rest, x, conv_w[0], w_out[0].astype(BF16),
      final_norm_w.reshape(1, D_MODEL))
    return out
```

```python
import math
from functools import partial

import numpy as np
import jax
import jax.numpy as jnp
from jax import lax
from jax.experimental import pallas as pl
from jax.experimental.pallas import tpu as pltpu

D_MODEL = 1024
N_HEADS = 8
HEAD_DIM = 64
N_KV_HEADS = 2
GQA_REP = N_HEADS // N_KV_HEADS
D_ATTN = N_HEADS * HEAD_DIM
D_KV = N_KV_HEADS * HEAD_DIM
D_CONV = D_MODEL - D_ATTN
CMP_BLOCK = 32
CMP_STRIDE = 16
CMP_HIDDEN = 2 * HEAD_DIM
SEL_BLOCK = 64
N_SEL = 16
WINDOW = 512
N_BUCKETS = 32
MAX_DISTANCE = 128
EPS = 1e-6
NEG = -1e30
FORCE_SCORE = 1e3

LOG2E = 1.4426950408889634
LANES = 128
MXU_COLS = 256
V_ROWS = HEAD_DIM + 16
ROW_TILE = 512
KEY_TILE = 256
Q_TILE_CMP = 512
Q_TILE = 256
HALO_ROWS = 16
MAX_LAG = 64.0
N_SLOTS = 2
VMEM_LIMIT = 48 * 1024 * 1024

F32 = jnp.float32
BF16 = jnp.bfloat16

_C_Q = 0
_C_CMP = D_ATTN
_C_SW = _C_CMP + 2 * D_KV
_C_REST = _C_SW + 4 * D_KV
_C_GATE = _C_REST + D_ATTN + 4 * D_CONV
_C_END = _C_GATE + LANES
N_GATE = 3 * N_HEADS


def _bucket_lower_bounds():
    max_exact = N_BUCKETS // 2
    d = np.arange(2 * MAX_DISTANCE, dtype=np.int32)
    nf = np.maximum(d, 1).astype(np.float32)
    large = max_exact + (np.log(nf / np.float32(max_exact)) / np.float32(math.log(MAX_DISTANCE / max_exact))
                         * np.float32(N_BUCKETS - max_exact)).astype(np.int32)
    bucket = np.where(d < max_exact, d, np.minimum(large, N_BUCKETS - 1))
    assert np.all(np.diff(bucket) >= 0) and np.all(bucket[MAX_DISTANCE:] == N_BUCKETS - 1)
    return [int(np.argmax(bucket >= b)) for b in range(N_BUCKETS)]


_BUCKET_LO = _bucket_lower_bounds()


def _rel_bias_delta(dist, bias_of_bucket):
    last = bias_of_bucket(N_BUCKETS - 1)
    val = jnp.full(dist.shape, (bias_of_bucket(0) - last) * LOG2E, F32)
    for b in range(1, N_BUCKETS - 1):
        val = jnp.where(dist >= _BUCKET_LO[b], (bias_of_bucket(b) - last) * LOG2E, val)
    val = jnp.where(dist >= _BUCKET_LO[N_BUCKETS - 1], 0.0, val)
    return jnp.where(dist < 0, NEG, val)


def _silu(z):
    return z * (1.0 / (1.0 + jnp.exp2(z * -LOG2E)))


def _proj_kernel(x_ref, nw_ref, wT_ref, qT_ref, kvc_ref, ksel_ref, vselT_ref, kwin_ref, vwinT_ref,
                 gsT_ref, rest_ref, kv_scr, w_scr, *, tm, nsel, kd):
    st = pl.program_id(1)

    @pl.when((pl.program_id(0) == 0) & (st == 0))
    def _():
        c0 = _C_REST
        chunk = 256
        for dst, src, rows in ((0, 0, c0), (c0, c0 + N_GATE, _C_GATE - c0)):
            for r in range(0, rows, chunk):
                w_scr[dst + r:dst + r + chunk, :] = wT_ref[src + r:src + r + chunk, :].astype(BF16)
        w_scr[_C_GATE:_C_END, :] = jnp.concatenate(
            [wT_ref[c0:c0 + N_GATE, :], jnp.zeros((LANES - N_GATE, D_MODEL), F32)], axis=0).astype(BF16)

    x = x_ref[0]
    ms = jnp.mean(x * x, axis=-1, keepdims=True)
    h = (x * lax.rsqrt(ms + EPS) * nw_ref[...]).astype(BF16)

    def mm(lo, hi):
        return lax.dot_general(h, w_scr[lo:hi, :], (((1,), (1,)), ((), ())), preferred_element_type=F32)

    q = mm(_C_Q, _C_CMP) * (HEAD_DIM ** -0.5 * LOG2E)
    qT_ref[0] = q.T.astype(BF16)

    kvc = mm(_C_CMP, _C_SW)
    for xk in range(2):
        kv_scr[xk] = kvc[:, xk * D_KV:(xk + 1) * D_KV]
        for j in range(CMP_STRIDE):
            kvc_ref[xk, 0, :, j * D_KV:(j + 1) * D_KV] = kv_scr[xk, pl.ds(j, tm // CMP_STRIDE, stride=CMP_STRIDE), :]

    a = mm(_C_SW, _C_REST)
    pos = st * tm + lax.broadcasted_iota(jnp.int32, (tm, nsel), 0)
    col = lax.broadcasted_iota(jnp.int32, (tm, nsel), 1)
    onehot = jnp.where((pos // SEL_BLOCK) == col, 1.0, 0.0).astype(BF16)
    ones = jnp.ones((V_ROWS - HEAD_DIM, tm), BF16)
    zeros_k = jnp.zeros((tm, LANES - HEAD_DIM), BF16)
    vsT = a[:, D_KV:2 * D_KV].T
    vwT = a[:, 3 * D_KV:4 * D_KV].T
    for g in range(N_KV_HEADS):
        lo, hi = g * HEAD_DIM, (g + 1) * HEAD_DIM
        parts = [a[:, lo:hi].astype(BF16), onehot]
        if kd > HEAD_DIM + nsel:
            parts.append(jnp.zeros((tm, kd - HEAD_DIM - nsel), BF16))
        ksel_ref[0, g] = jnp.concatenate(parts, axis=1)
        kwin_ref[0, g] = jnp.concatenate([a[:, 2 * D_KV + lo:2 * D_KV + hi].astype(BF16), zeros_k], axis=1)
        vs = jnp.concatenate([vsT[lo:hi].astype(BF16), ones], axis=0)
        vw = jnp.concatenate([vwT[lo:hi].astype(BF16), ones], axis=0)
        for kt in range(tm // KEY_TILE):
            vselT_ref[0, g, kt] = vs[:, kt * KEY_TILE:(kt + 1) * KEY_TILE]
            vwinT_ref[0, g, kt] = vw[:, kt * KEY_TILE:(kt + 1) * KEY_TILE]

    rest_ref[0] = mm(_C_REST, _C_GATE).astype(BF16)
    gates = jax.nn.sigmoid(mm(_C_GATE, _C_END))
    gsT_ref[0] = gates.T[:N_GATE]


def _compress_kernel(c_ref, pe_ref, w1_ref, w2_ref, lhs_tail_ref, aug_ref, t_ref, *, nch):
    c = c_ref[0, 0]
    a0 = jnp.dot((c + pe_ref[0, 0]).astype(BF16), w1_ref[0, 0], preferred_element_type=F32)
    a1 = jnp.dot((c + pe_ref[0, 1]).astype(BF16), w1_ref[0, 1], preferred_element_type=F32)
    hid = a0 + pltpu.roll(a1, nch - 1, 0)
    out = jnp.dot(_silu(hid).astype(BF16), w2_ref[0], preferred_element_type=F32)
    grp = lax.broadcasted_iota(jnp.int32, (nch, LANES - HEAD_DIM), 0) // 8
    col = lax.broadcasted_iota(jnp.int32, (nch, LANES - HEAD_DIM), 1)
    onehot = jnp.where(grp == col, 1.0, 0.0).astype(BF16)
    out_t = out.T.astype(BF16)
    for g in range(N_KV_HEADS):
        lo, hi = g * HEAD_DIM, (g + 1) * HEAD_DIM
        aug_ref[0, 0, g] = jnp.concatenate([out[:, lo:hi].astype(BF16), onehot], axis=1)
        t_ref[0, 0, g] = jnp.concatenate([out_t[lo:hi], lhs_tail_ref[...]], axis=0)


def _cmp_kernel(rb_ref, qT_ref, kc_ref, lhs_ref, gs_ref, ocT_ref, selb_ref,
                qa_ref, band_ref, acc_ref, s0_ref, s1_ref, s2_ref, s3_ref, *, nch, nsel, tq, k_top):
    g = pl.program_id(1)
    qt = pl.program_id(2)
    s_refs = (s0_ref, s1_ref, s2_ref, s3_ref)
    cpq = tq // CMP_STRIDE
    win = cpq + 16
    lim = cpq * (qt + 1)
    nq = GQA_REP * tq

    @pl.when(qt <= 1)
    def _():
        row = lax.broadcasted_iota(jnp.int32, (win, tq), 0) + jnp.maximum(cpq * qt - 16, 0)
        lane = lax.broadcasted_iota(jnp.int32, (win, tq), 1)
        dist = (qt * tq + lane) - (row * CMP_STRIDE + CMP_BLOCK - 1)
        for r in range(GQA_REP):
            band_ref[:, r * tq:(r + 1) * tq] = _rel_bias_delta(dist, lambda b, r=r: rb_ref[b, g * GQA_REP + r])

    q = qT_ref[0]
    qa_ref[0:HEAD_DIM, :] = jnp.concatenate([q[r * HEAD_DIM:(r + 1) * HEAD_DIM] for r in range(GQA_REP)], axis=1)
    grp = lax.broadcasted_iota(jnp.int32, (LANES - HEAD_DIM, nq), 0)
    qa_ref[HEAD_DIM:LANES, :] = jnp.where(grp * 8 >= lim, NEG, 0.0).astype(BF16)

    def run(rows):
        start = pl.multiple_of(jnp.clip(cpq * qt - 16, 0, rows - win), 8)
        k = kc_ref[0, 0, 0, 0:rows, :]
        lhs = lhs_ref[0, 0, 0, :, 0:rows]
        cols = [slice(r * tq, (r + 1) * tq) for r in range(GQA_REP)]

        def qk(r):
            s_refs[r][0:rows, :] = jnp.dot(k, qa_ref[:, cols[r]], preferred_element_type=F32)

        ahead = 3
        for r in range(ahead):
            qk(r)
        for r in range(GQA_REP):
            if r + ahead < GQA_REP:
                qk(r + ahead)
            sr = s_refs[r]
            sr[pl.ds(start, win), :] = sr[pl.ds(start, win), :] + band_ref[:, cols[r]]
            s = sr[0:rows, :]
            p = jnp.exp2(s - jnp.max(s, axis=0, keepdims=True)).astype(BF16)
            acc_ref[:, cols[r]] = jnp.dot(lhs, p, preferred_element_type=F32)

    half = nch // 2

    @pl.when(lim <= half)
    def _():
        run(half)

    @pl.when(lim > half)
    def _():
        run(nch)

    l = acc_ref[HEAD_DIM:HEAD_DIM + 1, :]
    t4 = qt * tq + (lax.broadcasted_iota(jnp.int32, l.shape, 1) & (tq - 1))
    inv = jnp.where(t4 >= CMP_BLOCK - 1, 1.0 / jnp.maximum(l, 1e-30), 0.0)
    imp = None
    for r in range(GQA_REP):
        cs = slice(r * tq, (r + 1) * tq)
        ocT_ref[0, r * HEAD_DIM:(r + 1) * HEAD_DIM, :] = (acc_ref[0:HEAD_DIM, cs] * inv[:, cs]
                                                          * gs_ref[0, 0, 0, r:r + 1, :]).astype(BF16)
        part = acc_ref[V_ROWS:V_ROWS + nsel, cs] * inv[:, cs]
        imp = part if imp is None else imp + part

    n = lax.broadcasted_iota(jnp.int32, (nsel, tq), 0)
    blk = (qt * tq + lax.broadcasted_iota(jnp.int32, (nsel, tq), 1)) // SEL_BLOCK
    forced = (n == 0) | (n == blk) | (n == blk - 1)
    candidate = (n <= blk) & jnp.logical_not(forced)
    score = jnp.where(forced, -2.0, jnp.where(n <= blk, imp, -1.0))

    def pick_all(_, score):
        return jnp.where(score == jnp.max(score, axis=0, keepdims=True), -2.0, score)

    taken = lax.fori_loop(0, k_top - 3, pick_all, score) == -2.0
    selb_ref[0, 0] = jnp.where(taken, 0.0, NEG).astype(BF16)
    n_taken = jnp.sum(jnp.where(taken & candidate, 1.0, 0.0), axis=0, keepdims=True)

    @pl.when(jnp.max(n_taken) > k_top - 3)
    def _():
        def pick_one(_, score):
            mx = jnp.max(score, axis=0, keepdims=True)
            idx = jnp.min(jnp.where(score == mx, n, nsel), axis=0, keepdims=True)
            return jnp.where(n == idx, -2.0, score)

        taken = lax.fori_loop(0, k_top - 3, pick_one, score) == -2.0
        selb_ref[0, 0] = jnp.where(taken, 0.0, NEG).astype(BF16)


def _flash_kernel(*refs, tq, nsel, kd, n_tbl_tiles, window, selected, add_ref=None):
    if selected:
        rb_ref, qT_ref, selb_ref, k_ref, vT_ref, gs_ref, o_ref, tbl_ref, qa_ref, m_ref, acc_ref, lag_ref, *slot_refs = refs
    else:
        rb_ref, qT_ref, k_ref, vT_ref, gs_ref, o_ref, tbl_ref, qa_ref, m_ref, acc_ref, lag_ref, *slot_refs = refs
    s_ref, mt_ref = slot_refs[0:1], slot_refs[1:2]
    al_ref, p_ref = slot_refs[2:2 + N_SLOTS], slot_refs[2 + N_SLOTS:2 + 2 * N_SLOTS]
    g = pl.program_id(1)
    qt = pl.program_id(2)
    nq = GQA_REP * tq

    @pl.when(qt == 0)
    def _():
        for j in range(n_tbl_tiles):
            key = lax.broadcasted_iota(jnp.int32, (tq, tq), 0) + j * tq
            qry = lax.broadcasted_iota(jnp.int32, (tq, tq), 1) + (n_tbl_tiles - 1) * tq
            dist = qry - key
            for r in range(GQA_REP):
                val = _rel_bias_delta(dist, lambda b, r=r: rb_ref[b, g * GQA_REP + r])
                if window is not None:
                    val = jnp.where(dist >= window, NEG, val)
                tbl_ref[j * tq:(j + 1) * tq, r * tq:(r + 1) * tq] = val

    m_ref[...] = jnp.full((1, nq), NEG, F32)
    acc_ref[...] = jnp.zeros((V_ROWS, nq), F32)

    def serial(t, tbl_tile):
        k = k_ref[0, 0, pl.ds(pl.multiple_of(t * tq, tq), tq), :]
        s = jnp.dot(k, qa_ref[...], preferred_element_type=F32)
        if tbl_tile is not None:
            s = s + tbl_ref[tbl_tile * tq:(tbl_tile + 1) * tq, :]
        s_ref[0][...] = s
        mt_ref[0][...] = jnp.max(s, axis=0, keepdims=True)
        m_old = m_ref[...]
        m_new = jnp.maximum(m_old, mt_ref[0][...])
        m_ref[...] = m_new
        p_ref[0][...] = jnp.exp2(s_ref[0][...] - m_new).astype(BF16)
        acc_ref[...] = jnp.exp2(m_old - m_new) * acc_ref[...] + jnp.dot(vT_ref[0, 0, t], p_ref[0][...],
                                                                        preferred_element_type=F32)

    last = n_tbl_tiles - 1

    def stage_lagged(t, slot, tbl_tile, first=False):
        k = k_ref[0, 0, pl.ds(pl.multiple_of(t * tq, tq), tq), :]
        s = jnp.dot(k, qa_ref[...], preferred_element_type=F32)
        if tbl_tile is not None:
            s = s + tbl_ref[tbl_tile * tq:(tbl_tile + 1) * tq, :]
        m = s[0:1, :] if first else m_ref[...]
        mt = jnp.max(s, axis=0, keepdims=True)
        p_ref[slot][...] = jnp.exp2((s - m).astype(BF16))
        m_new = jnp.maximum(m, mt)
        m_ref[...] = m_new
        al_ref[slot][...] = jnp.exp2(m - m_new)
        lag_ref[...] = (mt - m) if first else jnp.maximum(lag_ref[...], mt - m)

    def stage_pv_lagged(t, slot):
        acc_ref[...] = (acc_ref[...] + jnp.dot(vT_ref[0, 0, t], p_ref[slot][...],
                                               preferred_element_type=F32)) * al_ref[slot][...]

    def lagged_step(t, slot, tbl_tile, prev=None):
        stage_lagged(t, slot, tbl_tile)
        stage_pv_lagged(t - 1 if prev is None else prev, 1 - slot)

    if selected:
        n = qt + 1

        @pl.when((n >= 4) & ((n & 1) == 0))
        def _():
            stage_lagged(0, 1, None, first=True)
            lagged_step(1, 0, None)

        @pl.when((n >= 4) & ((n & 1) == 1))
        def _():
            stage_lagged(0, 0, None, first=True)

        @pl.when(n >= 4)
        def _():
            t0 = 2 - (n & 1)

            pairs = (n - 2 - t0) >> 1

            def pair(t):
                lagged_step(t, 1, None)
                lagged_step(t + 1, 0, None)

            def body(j, carry):
                for i in range(4):
                    pair(t0 + 8 * j + 2 * i)
                return carry
            lax.fori_loop(0, pairs >> 2, body, 0)

            @pl.when((pairs & 2) == 2)
            def _():
                t = n - 2 - 2 * (pairs & 3)
                pair(t)
                pair(t + 2)

            @pl.when((pairs & 1) == 1)
            def _():
                pair(n - 4)

        @pl.when(n >= 4)
        def _():
            lagged_step(n - 2, 1, 0)
            lagged_step(n - 1, 0, 1)
            stage_pv_lagged(n - 1, 0)

            @pl.when(jnp.max(lag_ref[...]) > MAX_LAG)
            def _():
                m_ref[...] = jnp.full((1, nq), NEG, F32)
                acc_ref[...] = jnp.zeros((V_ROWS, nq), F32)

                def redo(t, carry):
                    serial(t, None)
                    return carry
                lax.fori_loop(0, n - 2, redo, 0)
                serial(n - 2, 0)
                serial(n - 1, 1)

        @pl.when(n == 3)
        def _():
            serial(0, None)

        @pl.when((n >= 2) & (n < 4))
        def _():
            serial(qt - 1, 0)

        @pl.when(n < 4)
        def _():
            serial(qt, 1)
    else:
        @pl.when(qt >= last)
        def _():
            stage_lagged(qt, 0, 2, first=True)
            lagged_step(qt - 1, 1, 1, prev=qt)
            lagged_step(qt - 2, 0, 0, prev=qt - 1)
            stage_pv_lagged(qt - 2, 0)

            @pl.when(jnp.max(lag_ref[...]) > MAX_LAG)
            def _():
                m_ref[...] = jnp.full((1, nq), NEG, F32)
                acc_ref[...] = jnp.zeros((V_ROWS, nq), F32)
                for j in range(n_tbl_tiles):
                    serial(qt - last + j, j)

        @pl.when(qt == 1)
        def _():
            serial(0, 1)

        @pl.when(qt < last)
        def _():
            serial(qt, 2)

    acc = acc_ref[...]
    o = acc[0:HEAD_DIM] / jnp.maximum(acc[HEAD_DIM:HEAD_DIM + 1], 1e-30)
    for r in range(GQA_REP):
        rows = slice(r * HEAD_DIM, (r + 1) * HEAD_DIM)
        val = o[:, r * tq:(r + 1) * tq] * gs_ref[0, 0, 0, r:r + 1, :]
        if add_ref is not None:
            val = val + add_ref[0, rows, :]
        o_ref[0, rows, :] = val.astype(o_ref.dtype)


SEL_TBL_TILES = 2
WIN_TBL_TILES = 3


def _branches_kernel(rb_ref, qT_ref, selb_ref, ksel_ref, vselT_ref, kwin_ref, vwinT_ref, gs_sel_ref, gs_win_ref,
                     o_ref, owin_ref, tbl_ref, qa_ref, *state, tq, nsel, kd):
    q = qT_ref[0]
    nq = GQA_REP * tq
    qa_ref[0:HEAD_DIM, :] = jnp.concatenate([q[r * HEAD_DIM:(r + 1) * HEAD_DIM] for r in range(GQA_REP)], axis=1)
    qa_ref[HEAD_DIM:HEAD_DIM + nsel, :] = jnp.concatenate([selb_ref[0, 0]] * GQA_REP, axis=1)
    if kd > HEAD_DIM + nsel:
        qa_ref[HEAD_DIM + nsel:kd, :] = jnp.zeros((kd - HEAD_DIM - nsel, nq), BF16)

    win_tbl = tbl_ref.at[SEL_TBL_TILES * tq:(SEL_TBL_TILES + WIN_TBL_TILES) * tq]
    _flash_kernel(rb_ref, qT_ref, kwin_ref, vwinT_ref, gs_win_ref, owin_ref, win_tbl, qa_ref.at[0:LANES], *state,
                  tq=tq, nsel=nsel, kd=LANES, n_tbl_tiles=WIN_TBL_TILES, window=WINDOW, selected=False)
    _flash_kernel(rb_ref, qT_ref, selb_ref, ksel_ref, vselT_ref, gs_sel_ref, o_ref,
                  tbl_ref.at[0:SEL_TBL_TILES * tq], qa_ref, *state,
                  tq=tq, nsel=nsel, kd=kd, n_tbl_tiles=SEL_TBL_TILES, window=None, selected=True, add_ref=owin_ref)


def _out_kernel(oc_ref, osw_ref, za_ref, ch_ref, cb_ref, cc_ref, zc_ref, hh_ref, hc_ref,
                x_ref, cw_ref, wo_ref, fw_ref, out_ref, *, tm):
    st = pl.program_id(1)
    f32 = lambda ref: ref[0].astype(F32)
    attn = (f32(oc_ref) + f32(osw_ref)).T * _silu(f32(za_ref))

    u = f32(cc_ref) * f32(ch_ref)
    halo = jnp.where(st == 0, 0.0, f32(hc_ref) * f32(hh_ref))
    row = lax.broadcasted_iota(jnp.int32, u.shape, 0)
    h1, h2 = halo[HALO_ROWS - 1:HALO_ROWS], halo[HALO_ROWS - 2:HALO_ROWS - 1]
    u1 = jnp.where(row == 0, h1, pltpu.roll(u, 1, 0))
    u2 = jnp.where(row == 0, h2, jnp.where(row == 1, h1, pltpu.roll(u, 2, 0)))
    y = cw_ref[0:1] * u2 + cw_ref[1:2] * u1 + cw_ref[2:3] * u
    conv = f32(cb_ref) * y * _silu(f32(zc_ref))

    mixed = jnp.concatenate([attn, conv], axis=1).astype(BF16)
    y2 = x_ref[0] + jnp.dot(mixed, wo_ref[...], preferred_element_type=F32)
    ms = jnp.mean(y2 * y2, axis=-1, keepdims=True)
    out_ref[0] = y2 * lax.rsqrt(ms + EPS) * fw_ref[...]


def _overlap_t(nsel, nch):
    c = np.arange(nch)[None, :] * CMP_STRIDE
    n = np.arange(nsel)[:, None] * SEL_BLOCK
    return ((c < n + SEL_BLOCK) & (c + CMP_BLOCK > n)).astype(np.float32)


def _cparams(sem):
    return pltpu.CompilerParams(dimension_semantics=sem, vmem_limit_bytes=VMEM_LIMIT)


def kernel(x, norm_w, w_in, w_ck1, w_ck2, pe_k, w_cv1, w_cv2, pe_v, conv_w, w_out, rel_bias, final_norm_w):
    b, s, _ = x.shape
    assert norm_w.shape[0] == 1 and s % ROW_TILE == 0 and ROW_TILE % KEY_TILE == 0
    assert WINDOW == 2 * Q_TILE and KEY_TILE == Q_TILE
    tm = ROW_TILE
    nt = s // tm
    nkt = s // KEY_TILE
    kpt = tm // KEY_TILE
    nch = s // CMP_STRIDE
    nsel = s // SEL_BLOCK
    k_top = min(N_SEL, nsel)
    kd = -(-(HEAD_DIM + nsel) // LANES) * LANES
    G, R = N_KV_HEADS, GQA_REP

    w_in_t = jnp.swapaxes(w_in[0], 0, 1)
    assert w_in_t.shape[0] == _C_GATE + N_GATE
    eye = jnp.eye(G, dtype=F32)

    def expand1(w1):
        w1 = w1.astype(BF16).reshape(2, CMP_STRIDE, HEAD_DIM, CMP_HIDDEN)
        z = jnp.zeros_like(w1)
        rows = [jnp.concatenate([w1 if a == g else z for a in range(G)], axis=-1) for g in range(G)]
        return jnp.stack(rows, axis=2).reshape(2, CMP_STRIDE * D_KV, G * CMP_HIDDEN)

    def expand_pe(pe):
        pe = pe.reshape(2, CMP_STRIDE, 1, HEAD_DIM)
        return jnp.broadcast_to(pe, (2, CMP_STRIDE, G, HEAD_DIM)).reshape(2, 1, CMP_STRIDE * D_KV)

    def expand2(w2):
        return jnp.einsum('hd,ab->ahbd', w2, eye).reshape(G * CMP_HIDDEN, D_KV)

    w1e = jnp.stack([expand1(w_ck1[0]), expand1(w_cv1[0])]).astype(BF16)
    pee = jnp.stack([expand_pe(pe_k[0]), expand_pe(pe_v[0])])
    w2e = jnp.stack([expand2(w_ck2[0]), expand2(w_cv2[0])]).astype(BF16)
    rel_bias = rel_bias.astype(F32)
    smem = pl.BlockSpec(memory_space=pltpu.SMEM)

    n_rest = D_ATTN + 4 * D_CONV
    cw = CMP_STRIDE * D_KV
    qT, kvc, ksel, vselT, kwin, vwinT, gsT, rest = pl.pallas_call(
        partial(_proj_kernel, tm=tm, nsel=nsel, kd=kd),
        grid=(b, nt),
        in_specs=[pl.BlockSpec((1, tm, D_MODEL), lambda i, j: (i, j, 0)),
                  pl.BlockSpec((1, D_MODEL), lambda i, j: (0, 0)),
                  pl.BlockSpec((_C_GATE + N_GATE, D_MODEL), lambda i, j: (0, 0), pipeline_mode=pl.Buffered(1))],
        out_specs=[pl.BlockSpec((1, D_ATTN, tm), lambda i, j: (i, 0, j)),
                   pl.BlockSpec((2, 1, tm // CMP_STRIDE, cw), lambda i, j: (0, i, j, 0)),
                   pl.BlockSpec((1, G, tm, kd), lambda i, j: (i, 0, j, 0)),
                   pl.BlockSpec((1, G, kpt, V_ROWS, KEY_TILE), lambda i, j: (i, 0, j, 0, 0)),
                   pl.BlockSpec((1, G, tm, LANES), lambda i, j: (i, 0, j, 0)),
                   pl.BlockSpec((1, G, kpt, V_ROWS, KEY_TILE), lambda i, j: (i, 0, j, 0, 0)),
                   pl.BlockSpec((1, N_GATE, tm), lambda i, j: (i, 0, j)),
                   pl.BlockSpec((1, tm, n_rest), lambda i, j: (i, j, 0))],
        out_shape=[jax.ShapeDtypeStruct((b, D_ATTN, s), BF16),
                   jax.ShapeDtypeStruct((2, b, nch, cw), F32),
                   jax.ShapeDtypeStruct((b, G, s, kd), BF16),
                   jax.ShapeDtypeStruct((b, G, nkt, V_ROWS, KEY_TILE), BF16),
                   jax.ShapeDtypeStruct((b, G, s, LANES), BF16),
                   jax.ShapeDtypeStruct((b, G, nkt, V_ROWS, KEY_TILE), BF16),
                   jax.ShapeDtypeStruct((b, N_GATE, s), F32),
                   jax.ShapeDtypeStruct((b, s, n_rest), BF16)],
        scratch_shapes=[pltpu.VMEM((2, tm, D_KV), F32),
                        pltpu.VMEM((_C_END, D_MODEL), BF16)],
        compiler_params=_cparams(("arbitrary", "arbitrary")),
        name="in_proj",
    )(x, norm_w, w_in_t)
    gs = gsT.reshape(b, 3, G, R, s)

    lhs_rows = V_ROWS + nsel
    lhs_tail = jnp.asarray(np.concatenate([np.ones((V_ROWS - HEAD_DIM, nch), np.float32),
                                           _overlap_t(nsel, nch)], axis=0), BF16)
    kc_aug, c_lhs = pl.pallas_call(
        partial(_compress_kernel, nch=nch),
        grid=(2, b),
        in_specs=[pl.BlockSpec((1, 1, nch, cw), lambda xk, i: (xk, i, 0, 0)),
                  pl.BlockSpec((1, 2, 1, cw), lambda xk, i: (xk, 0, 0, 0)),
                  pl.BlockSpec((1, 2, cw, G * CMP_HIDDEN), lambda xk, i: (xk, 0, 0, 0)),
                  pl.BlockSpec((1, G * CMP_HIDDEN, D_KV), lambda xk, i: (xk, 0, 0)),
                  pl.BlockSpec((lhs_rows - HEAD_DIM, nch), lambda xk, i: (0, 0))],
        out_specs=[pl.BlockSpec((1, 1, G, nch, LANES), lambda xk, i: (xk, i, 0, 0, 0)),
                   pl.BlockSpec((1, 1, G, lhs_rows, nch), lambda xk, i: (xk, i, 0, 0, 0))],
        out_shape=[jax.ShapeDtypeStruct((2, b, G, nch, LANES), BF16),
                   jax.ShapeDtypeStruct((2, b, G, lhs_rows, nch), BF16)],
        compiler_params=_cparams(("parallel", "parallel")),
        name="compress",
    )(kvc, pee, w1e, w2e, lhs_tail)

    tq3 = Q_TILE_CMP
    cpq = tq3 // CMP_STRIDE
    assert nch // 8 <= LANES - HEAD_DIM and k_top > 3 and (nch // 2) % LANES == 0
    ocT, selb = pl.pallas_call(
        partial(_cmp_kernel, nch=nch, nsel=nsel, tq=tq3, k_top=k_top),
        grid=(b, G, s // tq3),
        in_specs=[smem,
                  pl.BlockSpec((1, R * HEAD_DIM, tq3), lambda i, g, t: (i, g, t)),
                  pl.BlockSpec((1, 1, 1, nch, LANES), lambda i, g, t: (0, i, g, 0, 0)),
                  pl.BlockSpec((1, 1, 1, lhs_rows, nch), lambda i, g, t: (1, i, g, 0, 0)),
                  pl.BlockSpec((1, 1, 1, R, tq3), lambda i, g, t: (i, 0, g, 0, t))],
        out_specs=[pl.BlockSpec((1, R * HEAD_DIM, tq3), lambda i, g, t: (i, g, t)),
                   pl.BlockSpec((1, 1, nsel, tq3), lambda i, g, t: (i, g, 0, t))],
        out_shape=[jax.ShapeDtypeStruct((b, D_ATTN, s), BF16),
                   jax.ShapeDtypeStruct((b, G, nsel, s), BF16)],
        scratch_shapes=[pltpu.VMEM((LANES, R * tq3), BF16),
                        pltpu.VMEM((cpq + 16, R * tq3), F32),
                        pltpu.VMEM((lhs_rows, R * tq3), F32)]
                       + [pltpu.VMEM((nch, tq3), F32)] * R,
        compiler_params=_cparams(("parallel", "parallel", "arbitrary")),
        name="cmp_select",
    )(rel_bias, qT, kc_aug, c_lhs, gs)

    tq = Q_TILE
    nq = R * tq

    o_spec = pl.BlockSpec((1, R * HEAD_DIM, tq), lambda i, g, t: (i, g, t))
    k_spec = lambda kdim: pl.BlockSpec((1, 1, s, kdim), lambda i, g, t: (i, g, 0, 0))
    v_spec = pl.BlockSpec((1, 1, nkt, V_ROWS, KEY_TILE), lambda i, g, t: (i, g, 0, 0, 0))
    gs_spec = lambda branch: pl.BlockSpec((1, 1, 1, R, tq), lambda i, g, t: (i, branch, g, 0, t))
    oswT = pl.pallas_call(
        partial(_branches_kernel, tq=tq, nsel=nsel, kd=kd),
        grid=(b, G, s // tq),
        in_specs=[smem, o_spec, pl.BlockSpec((1, 1, nsel, tq), lambda i, g, t: (i, g, 0, t)),
                  k_spec(kd), v_spec, k_spec(LANES), v_spec, gs_spec(1), gs_spec(2)],
        out_specs=o_spec,
        out_shape=jax.ShapeDtypeStruct((b, D_ATTN, s), BF16),
        scratch_shapes=[pltpu.VMEM((1, R * HEAD_DIM, tq), F32),
                        pltpu.VMEM(((SEL_TBL_TILES + WIN_TBL_TILES) * tq, nq), F32),
                        pltpu.VMEM((kd, nq), BF16),
                        pltpu.VMEM((1, nq), F32),
                        pltpu.VMEM((V_ROWS, nq), F32),
                        pltpu.VMEM((1, nq), F32)]
                       + [pltpu.VMEM((tq, nq), F32)]
                       + [pltpu.VMEM((1, nq), F32)]
                       + [pltpu.VMEM((1, nq), F32)] * N_SLOTS
                       + [pltpu.VMEM((tq, nq), BF16)] * N_SLOTS,
        compiler_params=_cparams(("parallel", "parallel", "arbitrary")),
        name="sel_win_attn",
    )(rel_bias, qT, selb, ksel, vselT, kwin, vwinT, gs, gs)

    hb = tm // HALO_ROWS
    col = lambda c: pl.BlockSpec((1, tm, D_CONV), lambda i, j, c=c: (i, j, c))
    halo = lambda c: pl.BlockSpec((1, HALO_ROWS, D_CONV), lambda i, j, c=c: (i, jnp.maximum(j * hb - 1, 0), c))
    oT = pl.BlockSpec((1, D_ATTN, tm), lambda i, j: (i, 0, j))
    out = pl.pallas_call(
        partial(_out_kernel, tm=tm),
        grid=(b, nt),
        in_specs=[oT, oT, col(0), col(1), col(2), col(3), col(4), halo(1), halo(3),
                  pl.BlockSpec((1, tm, D_MODEL), lambda i, j: (i, j, 0)),
                  pl.BlockSpec((3, D_CONV), lambda i, j: (0, 0)),
                  pl.BlockSpec((D_MODEL, D_MODEL), lambda i, j: (0, 0)),
                  pl.BlockSpec((1, D_MODEL), lambda i, j: (0, 0))],
        out_specs=pl.BlockSpec((1, tm, D_MODEL), lambda i, j: (i, j, 0)),
        out_shape=jax.ShapeDtypeStruct((b, s, D_MODEL), F32),
        compiler_params=_cparams(("parallel", "parallel")),
        name="out_proj",
    )(ocT, oswT, rest, rest, rest, rest, rest, rest, rest, x, conv_w[0], w_out[0].astype(BF16),
      final_norm_w.reshape(1, D_MODEL))
    return out
```

```python
import math
from functools import partial

import numpy as np
import jax
import jax.numpy as jnp
from jax import lax
from jax.experimental import pallas as pl
from jax.experimental.pallas import tpu as pltpu

D_MODEL = 1024
N_HEADS = 8
HEAD_DIM = 64
N_KV_HEADS = 2
GQA_REP = N_HEADS // N_KV_HEADS
D_ATTN = N_HEADS * HEAD_DIM
D_KV = N_KV_HEADS * HEAD_DIM
D_CONV = D_MODEL - D_ATTN
CMP_BLOCK = 32
CMP_STRIDE = 16
CMP_HIDDEN = 2 * HEAD_DIM
SEL_BLOCK = 64
N_SEL = 16
WINDOW = 512
N_BUCKETS = 32
MAX_DISTANCE = 128
EPS = 1e-6
NEG = -1e30
FORCE_SCORE = 1e3

LOG2E = 1.4426950408889634
LANES = 128
MXU_COLS = 256
V_ROWS = HEAD_DIM + 16
ROW_TILE = 512
KEY_TILE = 256
Q_TILE_CMP = 512
Q_TILE = 256
HALO_ROWS = 16
MAX_LAG = 64.0
N_SLOTS = 2
VMEM_LIMIT = 48 * 1024 * 1024

F32 = jnp.float32
BF16 = jnp.bfloat16

_C_Q = 0
_C_CMP = D_ATTN
_C_SW = _C_CMP + 2 * D_KV
_C_REST = _C_SW + 4 * D_KV
_C_GATE = _C_REST + D_ATTN + 4 * D_CONV
_C_END = _C_GATE + LANES
N_GATE = 3 * N_HEADS


def _bucket_lower_bounds():
    max_exact = N_BUCKETS // 2
    d = np.arange(2 * MAX_DISTANCE, dtype=np.int32)
    nf = np.maximum(d, 1).astype(np.float32)
    large = max_exact + (np.log(nf / np.float32(max_exact)) / np.float32(math.log(MAX_DISTANCE / max_exact))
                         * np.float32(N_BUCKETS - max_exact)).astype(np.int32)
    bucket = np.where(d < max_exact, d, np.minimum(large, N_BUCKETS - 1))
    assert np.all(np.diff(bucket) >= 0) and np.all(bucket[MAX_DISTANCE:] == N_BUCKETS - 1)
    return [int(np.argmax(bucket >= b)) for b in range(N_BUCKETS)]


_BUCKET_LO = _bucket_lower_bounds()


def _rel_bias_delta(dist, bias_of_bucket):
    last = bias_of_bucket(N_BUCKETS - 1)
    val = jnp.full(dist.shape, (bias_of_bucket(0) - last) * LOG2E, F32)
    for b in range(1, N_BUCKETS - 1):
        val = jnp.where(dist >= _BUCKET_LO[b], (bias_of_bucket(b) - last) * LOG2E, val)
    val = jnp.where(dist >= _BUCKET_LO[N_BUCKETS - 1], 0.0, val)
    return jnp.where(dist < 0, NEG, val)


def _silu(z):
    return z * (1.0 / (1.0 + jnp.exp2(z * -LOG2E)))


def _proj_kernel(x_ref, nw_ref, wT_ref, qT_ref, kvc_ref, ksel_ref, vselT_ref, kwin_ref, vwinT_ref,
                 gsT_ref, rest_ref, kv_scr, w_scr, *, tm, nsel, kd):
    st = pl.program_id(1)

    @pl.when((pl.program_id(0) == 0) & (st == 0))
    def _():
        c0 = _C_REST
        chunk = 256
        for dst, src, rows in ((0, 0, c0), (c0, c0 + N_GATE, _C_GATE - c0)):
            for r in range(0, rows, chunk):
                w_scr[dst + r:dst + r + chunk, :] = wT_ref[src + r:src + r + chunk, :].astype(BF16)
        w_scr[_C_GATE:_C_END, :] = jnp.concatenate(
            [wT_ref[c0:c0 + N_GATE, :], jnp.zeros((LANES - N_GATE, D_MODEL), F32)], axis=0).astype(BF16)

    x = x_ref[0]
    ms = jnp.mean(x * x, axis=-1, keepdims=True)
    h = (x * lax.rsqrt(ms + EPS) * nw_ref[...]).astype(BF16)

    def mm(lo, hi):
        return lax.dot_general(h, w_scr[lo:hi, :], (((1,), (1,)), ((), ())), preferred_element_type=F32)

    q = mm(_C_Q, _C_CMP) * (HEAD_DIM ** -0.5 * LOG2E)
    qT_ref[0] = q.T.astype(BF16)

    kvc = mm(_C_CMP, _C_SW)
    for xk in range(2):
        kv_scr[xk] = kvc[:, xk * D_KV:(xk + 1) * D_KV]
        for j in range(CMP_STRIDE):
            kvc_ref[xk, 0, :, j * D_KV:(j + 1) * D_KV] = kv_scr[xk, pl.ds(j, tm // CMP_STRIDE, stride=CMP_STRIDE), :]

    a = mm(_C_SW, _C_REST)
    pos = st * tm + lax.broadcasted_iota(jnp.int32, (tm, nsel), 0)
    col = lax.broadcasted_iota(jnp.int32, (tm, nsel), 1)
    onehot = jnp.where((pos // SEL_BLOCK) == col, 1.0, 0.0).astype(BF16)
    ones = jnp.ones((V_ROWS - HEAD_DIM, tm), BF16)
    zeros_k = jnp.zeros((tm, LANES - HEAD_DIM), BF16)
    vsT = a[:, D_KV:2 * D_KV].T
    vwT = a[:, 3 * D_KV:4 * D_KV].T
    for g in range(N_KV_HEADS):
        lo, hi = g * HEAD_DIM, (g + 1) * HEAD_DIM
        parts = [a[:, lo:hi].astype(BF16), onehot]
        if kd > HEAD_DIM + nsel:
            parts.append(jnp.zeros((tm, kd - HEAD_DIM - nsel), BF16))
        ksel_ref[0, g] = jnp.concatenate(parts, axis=1)
        kwin_ref[0, g] = jnp.concatenate([a[:, 2 * D_KV + lo:2 * D_KV + hi].astype(BF16), zeros_k], axis=1)
        vs = jnp.concatenate([vsT[lo:hi].astype(BF16), ones], axis=0)
        vw = jnp.concatenate([vwT[lo:hi].astype(BF16), ones], axis=0)
        for kt in range(tm // KEY_TILE):
            vselT_ref[0, g, kt] = vs[:, kt * KEY_TILE:(kt + 1) * KEY_TILE]
            vwinT_ref[0, g, kt] = vw[:, kt * KEY_TILE:(kt + 1) * KEY_TILE]

    r = mm(_C_REST, _C_GATE)
    z_attn, conv_h, conv_b, conv_c, z_conv = (r[:, i * D_CONV:(i + 1) * D_CONV] for i in range(5))
    rest_ref[0] = jnp.concatenate([z_attn, conv_c * conv_h, conv_b, z_conv], axis=1).astype(BF16)
    gates = jax.nn.sigmoid(mm(_C_GATE, _C_END))
    gsT_ref[0] = gates.T[:N_GATE]


def _compress_kernel(c_ref, pe_ref, w1_ref, w2_ref, lhs_tail_ref, aug_ref, t_ref, *, nch):
    c = c_ref[0, 0]
    a0 = jnp.dot((c + pe_ref[0, 0]).astype(BF16), w1_ref[0, 0], preferred_element_type=F32)
    a1 = jnp.dot((c + pe_ref[0, 1]).astype(BF16), w1_ref[0, 1], preferred_element_type=F32)
    hid = a0 + pltpu.roll(a1, nch - 1, 0)
    out = jnp.dot(_silu(hid).astype(BF16), w2_ref[0], preferred_element_type=F32)
    grp = lax.broadcasted_iota(jnp.int32, (nch, LANES - HEAD_DIM), 0) // 8
    col = lax.broadcasted_iota(jnp.int32, (nch, LANES - HEAD_DIM), 1)
    onehot = jnp.where(grp == col, 1.0, 0.0).astype(BF16)
    out_t = out.T.astype(BF16)
    for g in range(N_KV_HEADS):
        lo, hi = g * HEAD_DIM, (g + 1) * HEAD_DIM
        aug_ref[0, 0, g] = jnp.concatenate([out[:, lo:hi].astype(BF16), onehot], axis=1)
        t_ref[0, 0, g] = jnp.concatenate([out_t[lo:hi], lhs_tail_ref[...]], axis=0)


def _cmp_kernel(rb_ref, qT_ref, kc_ref, lhs_ref, gs_ref, ocT_ref, selb_ref,
                qa_ref, band_ref, acc_ref, s0_ref, s1_ref, s2_ref, s3_ref, *, nch, nsel, tq, k_top):
    g = pl.program_id(1)
    qt = pl.program_id(2)
    s_refs = (s0_ref, s1_ref, s2_ref, s3_ref)
    cpq = tq // CMP_STRIDE
    win = cpq + 16
    lim = cpq * (qt + 1)
    nq = GQA_REP * tq

    @pl.when(qt <= 1)
    def _():
        row = lax.broadcasted_iota(jnp.int32, (win, tq), 0) + jnp.maximum(cpq * qt - 16, 0)
        lane = lax.broadcasted_iota(jnp.int32, (win, tq), 1)
        dist = (qt * tq + lane) - (row * CMP_STRIDE + CMP_BLOCK - 1)
        for r in range(GQA_REP):
            band_ref[:, r * tq:(r + 1) * tq] = _rel_bias_delta(dist, lambda b, r=r: rb_ref[b, g * GQA_REP + r])

    q = qT_ref[0]
    qa_ref[0:HEAD_DIM, :] = jnp.concatenate([q[r * HEAD_DIM:(r + 1) * HEAD_DIM] for r in range(GQA_REP)], axis=1)
    grp = lax.broadcasted_iota(jnp.int32, (LANES - HEAD_DIM, nq), 0)
    qa_ref[HEAD_DIM:LANES, :] = jnp.where(grp * 8 >= lim, NEG, 0.0).astype(BF16)

    def run(rows):
        start = pl.multiple_of(jnp.clip(cpq * qt - 16, 0, rows - win), 8)
        k = kc_ref[0, 0, 0, 0:rows, :]
        lhs = lhs_ref[0, 0, 0, :, 0:rows]
        cols = [slice(r * tq, (r + 1) * tq) for r in range(GQA_REP)]

        def qk(r):
            s_refs[r][0:rows, :] = jnp.dot(k, qa_ref[:, cols[r]], preferred_element_type=F32)

        ahead = 3
        for r in range(ahead):
            qk(r)
        for r in range(GQA_REP):
            if r + ahead < GQA_REP:
                qk(r + ahead)
            sr = s_refs[r]
            sr[pl.ds(start, win), :] = sr[pl.ds(start, win), :] + band_ref[:, cols[r]]
            s = sr[0:rows, :]
            p = jnp.exp2(s - jnp.max(s, axis=0, keepdims=True)).astype(BF16)
            acc_ref[:, cols[r]] = jnp.dot(lhs, p, preferred_element_type=F32)

    half = nch // 2

    @pl.when(lim <= half)
    def _():
        run(half)

    @pl.when(lim > half)
    def _():
        run(nch)

    l = acc_ref[HEAD_DIM:HEAD_DIM + 1, :]
    t4 = qt * tq + (lax.broadcasted_iota(jnp.int32, l.shape, 1) & (tq - 1))
    inv = jnp.where(t4 >= CMP_BLOCK - 1, 1.0 / jnp.maximum(l, 1e-30), 0.0)
    imp = None
    for r in range(GQA_REP):
        cs = slice(r * tq, (r + 1) * tq)
        ocT_ref[0, r * HEAD_DIM:(r + 1) * HEAD_DIM, :] = (acc_ref[0:HEAD_DIM, cs] * inv[:, cs]
                                                          * gs_ref[0, 0, 0, r:r + 1, :]).astype(BF16)
        part = acc_ref[V_ROWS:V_ROWS + nsel, cs] * inv[:, cs]
        imp = part if imp is None else imp + part

    n = lax.broadcasted_iota(jnp.int32, (nsel, tq), 0)
    blk = (qt * tq + lax.broadcasted_iota(jnp.int32, (nsel, tq), 1)) // SEL_BLOCK
    forced = (n == 0) | (n == blk) | (n == blk - 1)
    candidate = (n <= blk) & jnp.logical_not(forced)
    score = jnp.where(forced, -2.0, jnp.where(n <= blk, imp, -1.0))

    def pick_all(_, score):
        return jnp.where(score == jnp.max(score, axis=0, keepdims=True), -2.0, score)

    taken = lax.fori_loop(0, k_top - 3, pick_all, score) == -2.0
    selb_ref[0, 0] = jnp.where(taken, 0.0, NEG).astype(BF16)
    n_taken = jnp.sum(jnp.where(taken & candidate, 1.0, 0.0), axis=0, keepdims=True)

    @pl.when(jnp.max(n_taken) > k_top - 3)
    def _():
        def pick_one(_, score):
            mx = jnp.max(score, axis=0, keepdims=True)
            idx = jnp.min(jnp.where(score == mx, n, nsel), axis=0, keepdims=True)
            return jnp.where(n == idx, -2.0, score)

        taken = lax.fori_loop(0, k_top - 3, pick_one, score) == -2.0
        selb_ref[0, 0] = jnp.where(taken, 0.0, NEG).astype(BF16)


def _flash_kernel(*refs, tq, nsel, kd, n_tbl_tiles, window, selected, add_ref=None):
    if selected:
        rb_ref, qT_ref, selb_ref, k_ref, vT_ref, gs_ref, o_ref, tbl_ref, qa_ref, m_ref, acc_ref, lag_ref, *slot_refs = refs
    else:
        rb_ref, qT_ref, k_ref, vT_ref, gs_ref, o_ref, tbl_ref, qa_ref, m_ref, acc_ref, lag_ref, *slot_refs = refs
    s_ref, mt_ref = slot_refs[0:1], slot_refs[1:2]
    al_ref, p_ref = slot_refs[2:2 + N_SLOTS], slot_refs[2 + N_SLOTS:2 + 2 * N_SLOTS]
    g = pl.program_id(1)
    qt = pl.program_id(2)
    nq = GQA_REP * tq

    @pl.when(qt == 0)
    def _():
        for j in range(n_tbl_tiles):
            key = lax.broadcasted_iota(jnp.int32, (tq, tq), 0) + j * tq
            qry = lax.broadcasted_iota(jnp.int32, (tq, tq), 1) + (n_tbl_tiles - 1) * tq
            dist = qry - key
            for r in range(GQA_REP):
                val = _rel_bias_delta(dist, lambda b, r=r: rb_ref[b, g * GQA_REP + r])
                if window is not None:
                    val = jnp.where(dist >= window, NEG, val)
                tbl_ref[j * tq:(j + 1) * tq, r * tq:(r + 1) * tq] = val

    m_ref[...] = jnp.full((1, nq), NEG, F32)
    acc_ref[...] = jnp.zeros((V_ROWS, nq), F32)

    def serial(t, tbl_tile):
        k = k_ref[0, 0, pl.ds(pl.multiple_of(t * tq, tq), tq), :]
        s = jnp.dot(k, qa_ref[...], preferred_element_type=F32)
        if tbl_tile is not None:
            s = s + tbl_ref[tbl_tile * tq:(tbl_tile + 1) * tq, :]
        s_ref[0][...] = s
        mt_ref[0][...] = jnp.max(s, axis=0, keepdims=True)
        m_old = m_ref[...]
        m_new = jnp.maximum(m_old, mt_ref[0][...])
        m_ref[...] = m_new
        p_ref[0][...] = jnp.exp2(s_ref[0][...] - m_new).astype(BF16)
        acc_ref[...] = jnp.exp2(m_old - m_new) * acc_ref[...] + jnp.dot(vT_ref[0, 0, t], p_ref[0][...],
                                                                        preferred_element_type=F32)

    last = n_tbl_tiles - 1

    def stage_lagged(t, slot, tbl_tile, first=False):
        k = k_ref[0, 0, pl.ds(pl.multiple_of(t * tq, tq), tq), :]
        s = jnp.dot(k, qa_ref[...], preferred_element_type=F32)
        if tbl_tile is not None:
            s = s + tbl_ref[tbl_tile * tq:(tbl_tile + 1) * tq, :]
        m = s[0:1, :] if first else m_ref[...]
        mt = jnp.max(s, axis=0, keepdims=True)
        p_ref[slot][...] = jnp.exp2((s - m).astype(BF16))
        m_new = jnp.maximum(m, mt)
        m_ref[...] = m_new
        al_ref[slot][...] = jnp.exp2(m - m_new)
        lag_ref[...] = (mt - m) if first else jnp.maximum(lag_ref[...], mt - m)

    def stage_pv_lagged(t, slot):
        acc_ref[...] = (acc_ref[...] + jnp.dot(vT_ref[0, 0, t], p_ref[slot][...],
                                               preferred_element_type=F32)) * al_ref[slot][...]

    def lagged_step(t, slot, tbl_tile, prev=None):
        stage_lagged(t, slot, tbl_tile)
        stage_pv_lagged(t - 1 if prev is None else prev, 1 - slot)

    if selected:
        n = qt + 1

        @pl.when((n >= 4) & ((n & 1) == 0))
        def _():
            stage_lagged(0, 1, None, first=True)
            lagged_step(1, 0, None)

        @pl.when((n >= 4) & ((n & 1) == 1))
        def _():
            stage_lagged(0, 0, None, first=True)

        @pl.when(n >= 4)
        def _():
            t0 = 2 - (n & 1)

            pairs = (n - 2 - t0) >> 1

            def pair(t):
                lagged_step(t, 1, None)
                lagged_step(t + 1, 0, None)

            def body(j, carry):
                for i in range(4):
                    pair(t0 + 8 * j + 2 * i)
                return carry
            lax.fori_loop(0, pairs >> 2, body, 0)

            @pl.when((pairs & 2) == 2)
            def _():
                t = n - 2 - 2 * (pairs & 3)
                pair(t)
                pair(t + 2)

            @pl.when((pairs & 1) == 1)
            def _():
                pair(n - 4)

        @pl.when(n >= 4)
        def _():
            lagged_step(n - 2, 1, 0)
            lagged_step(n - 1, 0, 1)
            stage_pv_lagged(n - 1, 0)

            @pl.when(jnp.max(lag_ref[...]) > MAX_LAG)
            def _():
                m_ref[...] = jnp.full((1, nq), NEG, F32)
                acc_ref[...] = jnp.zeros((V_ROWS, nq), F32)

                def redo(t, carry):
                    serial(t, None)
                    return carry
                lax.fori_loop(0, n - 2, redo, 0)
                serial(n - 2, 0)
                serial(n - 1, 1)

        @pl.when(n == 3)
        def _():
            serial(0, None)

        @pl.when((n >= 2) & (n < 4))
        def _():
            serial(qt - 1, 0)

        @pl.when(n < 4)
        def _():
            serial(qt, 1)
    else:
        @pl.when(qt >= last)
        def _():
            stage_lagged(qt, 0, 2, first=True)
            lagged_step(qt - 1, 1, 1, prev=qt)
            lagged_step(qt - 2, 0, 0, prev=qt - 1)
            stage_pv_lagged(qt - 2, 0)

            @pl.when(jnp.max(lag_ref[...]) > MAX_LAG)
            def _():
                m_ref[...] = jnp.full((1, nq), NEG, F32)
                acc_ref[...] = jnp.zeros((V_ROWS, nq), F32)
                for j in range(n_tbl_tiles):
                    serial(qt - last + j, j)

        @pl.when(qt == 1)
        def _():
            serial(0, 1)

        @pl.when(qt < last)
        def _():
            serial(qt, 2)

    acc = acc_ref[...]
    o = acc[0:HEAD_DIM] / jnp.maximum(acc[HEAD_DIM:HEAD_DIM + 1], 1e-30)
    for r in range(GQA_REP):
        rows = slice(r * HEAD_DIM, (r + 1) * HEAD_DIM)
        val = o[:, r * tq:(r + 1) * tq] * gs_ref[0, 0, 0, r:r + 1, :]
        if add_ref is not None:
            val = val + add_ref[0, rows, :]
        o_ref[0, rows, :] = val.astype(o_ref.dtype)


SEL_TBL_TILES = 2
WIN_TBL_TILES = 3


def _branches_kernel(rb_ref, qT_ref, selb_ref, ksel_ref, vselT_ref, kwin_ref, vwinT_ref, gs_sel_ref, gs_win_ref,
                     o_ref, owin_ref, tbl_ref, qa_ref, *state, tq, nsel, kd):
    q = qT_ref[0]
    nq = GQA_REP * tq
    qa_ref[0:HEAD_DIM, :] = jnp.concatenate([q[r * HEAD_DIM:(r + 1) * HEAD_DIM] for r in range(GQA_REP)], axis=1)
    qa_ref[HEAD_DIM:HEAD_DIM + nsel, :] = jnp.concatenate([selb_ref[0, 0]] * GQA_REP, axis=1)
    if kd > HEAD_DIM + nsel:
        qa_ref[HEAD_DIM + nsel:kd, :] = jnp.zeros((kd - HEAD_DIM - nsel, nq), BF16)

    win_tbl = tbl_ref.at[SEL_TBL_TILES * tq:(SEL_TBL_TILES + WIN_TBL_TILES) * tq]
    _flash_kernel(rb_ref, qT_ref, kwin_ref, vwinT_ref, gs_win_ref, owin_ref, win_tbl, qa_ref.at[0:LANES], *state,
                  tq=tq, nsel=nsel, kd=LANES, n_tbl_tiles=WIN_TBL_TILES, window=WINDOW, selected=False)
    _flash_kernel(rb_ref, qT_ref, selb_ref, ksel_ref, vselT_ref, gs_sel_ref, o_ref,
                  tbl_ref.at[0:SEL_TBL_TILES * tq], qa_ref, *state,
                  tq=tq, nsel=nsel, kd=kd, n_tbl_tiles=SEL_TBL_TILES, window=None, selected=True, add_ref=owin_ref)


def _out_kernel(oc_ref, osw_ref, za_ref, u_ref, cb_ref, zc_ref, hu_ref,
                x_ref, cw_ref, wo_ref, fw_ref, out_ref, *, tm):
    st = pl.program_id(1)
    f32 = lambda ref: ref[0].astype(F32)
    attn = (f32(oc_ref) + f32(osw_ref)).T * _silu(f32(za_ref))

    u = f32(u_ref)
    halo = jnp.where(st == 0, 0.0, f32(hu_ref))
    row = lax.broadcasted_iota(jnp.int32, u.shape, 0)
    h1, h2 = halo[HALO_ROWS - 1:HALO_ROWS], halo[HALO_ROWS - 2:HALO_ROWS - 1]
    u1 = jnp.where(row == 0, h1, pltpu.roll(u, 1, 0))
    u2 = jnp.where(row == 0, h2, jnp.where(row == 1, h1, pltpu.roll(u, 2, 0)))
    y = cw_ref[0:1] * u2 + cw_ref[1:2] * u1 + cw_ref[2:3] * u
    conv = f32(cb_ref) * y * _silu(f32(zc_ref))

    mixed = jnp.concatenate([attn, conv], axis=1).astype(BF16)
    y2 = x_ref[0] + jnp.dot(mixed, wo_ref[...], preferred_element_type=F32)
    ms = jnp.mean(y2 * y2, axis=-1, keepdims=True)
    out_ref[0] = y2 * lax.rsqrt(ms + EPS) * fw_ref[...]


def _overlap_t(nsel, nch):
    c = np.arange(nch)[None, :] * CMP_STRIDE
    n = np.arange(nsel)[:, None] * SEL_BLOCK
    return ((c < n + SEL_BLOCK) & (c + CMP_BLOCK > n)).astype(np.float32)


def _cparams(sem):
    return pltpu.CompilerParams(dimension_semantics=sem, vmem_limit_bytes=VMEM_LIMIT)


def kernel(x, norm_w, w_in, w_ck1, w_ck2, pe_k, w_cv1, w_cv2, pe_v, conv_w, w_out, rel_bias, final_norm_w):
    b, s, _ = x.shape
    assert norm_w.shape[0] == 1 and s % ROW_TILE == 0 and ROW_TILE % KEY_TILE == 0
    assert WINDOW == 2 * Q_TILE and KEY_TILE == Q_TILE
    tm = ROW_TILE
    nt = s // tm
    nkt = s // KEY_TILE
    kpt = tm // KEY_TILE
    nch = s // CMP_STRIDE
    nsel = s // SEL_BLOCK
    k_top = min(N_SEL, nsel)
    kd = -(-(HEAD_DIM + nsel) // LANES) * LANES
    G, R = N_KV_HEADS, GQA_REP

    w_in_t = jnp.swapaxes(w_in[0], 0, 1)
    assert w_in_t.shape[0] == _C_GATE + N_GATE
    eye = jnp.eye(G, dtype=F32)

    def expand1(w1):
        w1 = w1.astype(BF16).reshape(2, CMP_STRIDE, HEAD_DIM, CMP_HIDDEN)
        z = jnp.zeros_like(w1)
        rows = [jnp.concatenate([w1 if a == g else z for a in range(G)], axis=-1) for g in range(G)]
        return jnp.stack(rows, axis=2).reshape(2, CMP_STRIDE * D_KV, G * CMP_HIDDEN)

    def expand_pe(pe):
        pe = pe.reshape(2, CMP_STRIDE, 1, HEAD_DIM)
        return jnp.broadcast_to(pe, (2, CMP_STRIDE, G, HEAD_DIM)).reshape(2, 1, CMP_STRIDE * D_KV)

    def expand2(w2):
        return jnp.einsum('hd,ab->ahbd', w2, eye).reshape(G * CMP_HIDDEN, D_KV)

    w1e = jnp.stack([expand1(w_ck1[0]), expand1(w_cv1[0])]).astype(BF16)
    pee = jnp.stack([expand_pe(pe_k[0]), expand_pe(pe_v[0])])
    w2e = jnp.stack([expand2(w_ck2[0]), expand2(w_cv2[0])]).astype(BF16)
    rel_bias = rel_bias.astype(F32)
    smem = pl.BlockSpec(memory_space=pltpu.SMEM)

    n_rest = D_ATTN + 3 * D_CONV
    cw = CMP_STRIDE * D_KV
    qT, kvc, ksel, vselT, kwin, vwinT, gsT, rest = pl.pallas_call(
        partial(_proj_kernel, tm=tm, nsel=nsel, kd=kd),
        grid=(b, nt),
        in_specs=[pl.BlockSpec((1, tm, D_MODEL), lambda i, j: (i, j, 0)),
                  pl.BlockSpec((1, D_MODEL), lambda i, j: (0, 0)),
                  pl.BlockSpec((_C_GATE + N_GATE, D_MODEL), lambda i, j: (0, 0), pipeline_mode=pl.Buffered(1))],
        out_specs=[pl.BlockSpec((1, D_ATTN, tm), lambda i, j: (i, 0, j)),
                   pl.BlockSpec((2, 1, tm // CMP_STRIDE, cw), lambda i, j: (0, i, j, 0)),
                   pl.BlockSpec((1, G, tm, kd), lambda i, j: (i, 0, j, 0)),
                   pl.BlockSpec((1, G, kpt, V_ROWS, KEY_TILE), lambda i, j: (i, 0, j, 0, 0)),
                   pl.BlockSpec((1, G, tm, LANES), lambda i, j: (i, 0, j, 0)),
                   pl.BlockSpec((1, G, kpt, V_ROWS, KEY_TILE), lambda i, j: (i, 0, j, 0, 0)),
                   pl.BlockSpec((1, N_GATE, tm), lambda i, j: (i, 0, j)),
                   pl.BlockSpec((1, tm, n_rest), lambda i, j: (i, j, 0))],
        out_shape=[jax.ShapeDtypeStruct((b, D_ATTN, s), BF16),
                   jax.ShapeDtypeStruct((2, b, nch, cw), F32),
                   jax.ShapeDtypeStruct((b, G, s, kd), BF16),
                   jax.ShapeDtypeStruct((b, G, nkt, V_ROWS, KEY_TILE), BF16),
                   jax.ShapeDtypeStruct((b, G, s, LANES), BF16),
                   jax.ShapeDtypeStruct((b, G, nkt, V_ROWS, KEY_TILE), BF16),
                   jax.ShapeDtypeStruct((b, N_GATE, s), F32),
                   jax.ShapeDtypeStruct((b, s, n_rest), BF16)],
        scratch_shapes=[pltpu.VMEM((2, tm, D_KV), F32),
                        pltpu.VMEM((_C_END, D_MODEL), BF16)],
        compiler_params=_cparams(("arbitrary", "arbitrary")),
        name="in_proj",
    )(x, norm_w, w_in_t)
    gs = gsT.reshape(b, 3, G, R, s)

    lhs_rows = V_ROWS + nsel
    lhs_tail = jnp.asarray(np.concatenate([np.ones((V_ROWS - HEAD_DIM, nch), np.float32),
                                           _overlap_t(nsel, nch)], axis=0), BF16)
    kc_aug, c_lhs = pl.pallas_call(
        partial(_compress_kernel, nch=nch),
        grid=(2, b),
        in_specs=[pl.BlockSpec((1, 1, nch, cw), lambda xk, i: (xk, i, 0, 0)),
                  pl.BlockSpec((1, 2, 1, cw), lambda xk, i: (xk, 0, 0, 0)),
                  pl.BlockSpec((1, 2, cw, G * CMP_HIDDEN), lambda xk, i: (xk, 0, 0, 0)),
                  pl.BlockSpec((1, G * CMP_HIDDEN, D_KV), lambda xk, i: (xk, 0, 0)),
                  pl.BlockSpec((lhs_rows - HEAD_DIM, nch), lambda xk, i: (0, 0))],
        out_specs=[pl.BlockSpec((1, 1, G, nch, LANES), lambda xk, i: (xk, i, 0, 0, 0)),
                   pl.BlockSpec((1, 1, G, lhs_rows, nch), lambda xk, i: (xk, i, 0, 0, 0))],
        out_shape=[jax.ShapeDtypeStruct((2, b, G, nch, LANES), BF16),
                   jax.ShapeDtypeStruct((2, b, G, lhs_rows, nch), BF16)],
        compiler_params=_cparams(("parallel", "parallel")),
        name="compress",
    )(kvc, pee, w1e, w2e, lhs_tail)

    tq3 = Q_TILE_CMP
    cpq = tq3 // CMP_STRIDE
    assert nch // 8 <= LANES - HEAD_DIM and k_top > 3 and (nch // 2) % LANES == 0
    ocT, selb = pl.pallas_call(
        partial(_cmp_kernel, nch=nch, nsel=nsel, tq=tq3, k_top=k_top),
        grid=(b, G, s // tq3),
        in_specs=[smem,
                  pl.BlockSpec((1, R * HEAD_DIM, tq3), lambda i, g, t: (i, g, t)),
                  pl.BlockSpec((1, 1, 1, nch, LANES), lambda i, g, t: (0, i, g, 0, 0)),
                  pl.BlockSpec((1, 1, 1, lhs_rows, nch), lambda i, g, t: (1, i, g, 0, 0)),
                  pl.BlockSpec((1, 1, 1, R, tq3), lambda i, g, t: (i, 0, g, 0, t))],
        out_specs=[pl.BlockSpec((1, R * HEAD_DIM, tq3), lambda i, g, t: (i, g, t)),
                   pl.BlockSpec((1, 1, nsel, tq3), lambda i, g, t: (i, g, 0, t))],
        out_shape=[jax.ShapeDtypeStruct((b, D_ATTN, s), BF16),
                   jax.ShapeDtypeStruct((b, G, nsel, s), BF16)],
        scratch_shapes=[pltpu.VMEM((LANES, R * tq3), BF16),
                        pltpu.VMEM((cpq + 16, R * tq3), F32),
                        pltpu.VMEM((lhs_rows, R * tq3), F32)]
                       + [pltpu.VMEM((nch, tq3), F32)] * R,
        compiler_params=_cparams(("parallel", "parallel", "arbitrary")),
        name="cmp_select",
    )(rel_bias, qT, kc_aug, c_lhs, gs)

    tq = Q_TILE
    nq = R * tq

    o_spec = pl.BlockSpec((1, R * HEAD_DIM, tq), lambda i, g, t: (i, g, t))
    k_spec = lambda kdim: pl.BlockSpec((1, 1, s, kdim), lambda i, g, t: (i, g, 0, 0))
    v_spec = pl.BlockSpec((1, 1, nkt, V_ROWS, KEY_TILE), lambda i, g, t: (i, g, 0, 0, 0))
    gs_spec = lambda branch: pl.BlockSpec((1, 1, 1, R, tq), lambda i, g, t: (i, branch, g, 0, t))
    oswT = pl.pallas_call(
        partial(_branches_kernel, tq=tq, nsel=nsel, kd=kd),
        grid=(b, G, s // tq),
        in_specs=[smem, o_spec, pl.BlockSpec((1, 1, nsel, tq), lambda i, g, t: (i, g, 0, t)),
                  k_spec(kd), v_spec, k_spec(LANES), v_spec, gs_spec(1), gs_spec(2)],
        out_specs=o_spec,
        out_shape=jax.ShapeDtypeStruct((b, D_ATTN, s), BF16),
        scratch_shapes=[pltpu.VMEM((1, R * HEAD_DIM, tq), F32),
                        pltpu.VMEM(((SEL_TBL_TILES + WIN_TBL_TILES) * tq, nq), F32),
                        pltpu.VMEM((kd, nq), BF16),
                        pltpu.VMEM((1, nq), F32),
                        pltpu.VMEM((V_ROWS, nq), F32),
                        pltpu.VMEM((1, nq), F32)]
                       + [pltpu.VMEM((tq, nq), F32)]
                       + [pltpu.VMEM((1, nq), F32)]
                       + [pltpu.VMEM((1, nq), F32)] * N_SLOTS
                       + [pltpu.VMEM((tq, nq), BF16)] * N_SLOTS,
        compiler_params=_cparams(("parallel", "parallel", "arbitrary")),
        name="sel_win_attn",
    )(rel_bias, qT, selb, ksel, vselT, kwin, vwinT, gs, gs)

    hb = tm // HALO_ROWS
    col = lambda c: pl.BlockSpec((1, tm, D_CONV), lambda i, j, c=c: (i, j, c))
    halo = lambda c: pl.BlockSpec((1, HALO_ROWS, D_CONV), lambda i, j, c=c: (i, jnp.maximum(j * hb - 1, 0), c))
    oT = pl.BlockSpec((1, D_ATTN, tm), lambda i, j: (i, 0, j))
    out = pl.pallas_call(
        partial(_out_kernel, tm=tm),
        grid=(b, nt),
        in_specs=[oT, oT, col(0), col(1), col(2), col(3), halo(1),
                  pl.BlockSpec((1, tm, D_MODEL), lambda i, j: (i, j, 0)),
                  pl.BlockSpec((3, D_CONV), lambda i, j: (0, 0)),
                  pl.BlockSpec((D_MODEL, D_MODEL), lambda i, j: (0, 0)),
                  pl.BlockSpec((1, D_MODEL), lambda i, j: (0, 0))],
        out_specs=pl.BlockSpec((1, tm, D_MODEL), lambda i, j: (i, j, 0)),
        out_shape=jax.ShapeDtypeStruct((b, s, D_MODEL), F32),
        compiler_params=_cparams(("parallel", "parallel")),
        name="out_proj",
    )(ocT, oswT, rest, rest, rest, rest, rest, x, conv_w[0], w_out[0].astype(BF16),
      final_norm_w.reshape(1, D_MODEL))
    return out
```

```python
import math
from functools import partial

import numpy as np
import jax
import jax.numpy as jnp
from jax import lax
from jax.experimental import pallas as pl
from jax.experimental.pallas import tpu as pltpu

D_MODEL = 1024
N_HEADS = 8
HEAD_DIM = 64
N_KV_HEADS = 2
GQA_REP = N_HEADS // N_KV_HEADS
D_ATTN = N_HEADS * HEAD_DIM
D_KV = N_KV_HEADS * HEAD_DIM
D_CONV = D_MODEL - D_ATTN
CMP_BLOCK = 32
CMP_STRIDE = 16
CMP_HIDDEN = 2 * HEAD_DIM
SEL_BLOCK = 64
N_SEL = 16
WINDOW = 512
N_BUCKETS = 32
MAX_DISTANCE = 128
EPS = 1e-6
NEG = -1e30

LOG2E = 1.4426950408889634
LANES = 128
V_ROWS = HEAD_DIM + 16
ROW_TILE = 512
KEY_TILE = 256
Q_TILE_CMP = 512
Q_TILE = 256
HALO_ROWS = 16
MAX_LAG = 64.0
N_SLOTS = 2
VMEM_LIMIT = 48 * 1024 * 1024

F32 = jnp.float32
BF16 = jnp.bfloat16

_C_Q = 0
_C_CMP = D_ATTN
_C_SW = _C_CMP + 2 * D_KV
_C_REST = _C_SW + 4 * D_KV
_C_GATE = _C_REST + D_ATTN + 4 * D_CONV
_C_END = _C_GATE + LANES
N_GATE = 3 * N_HEADS


def _bucket_lower_bounds():
    max_exact = N_BUCKETS // 2
    d = np.arange(2 * MAX_DISTANCE, dtype=np.int32)
    nf = np.maximum(d, 1).astype(np.float32)
    large = max_exact + (np.log(nf / np.float32(max_exact)) / np.float32(math.log(MAX_DISTANCE / max_exact))
                         * np.float32(N_BUCKETS - max_exact)).astype(np.int32)
    bucket = np.where(d < max_exact, d, np.minimum(large, N_BUCKETS - 1))
    assert np.all(np.diff(bucket) >= 0) and np.all(bucket[MAX_DISTANCE:] == N_BUCKETS - 1)
    return [int(np.argmax(bucket >= b)) for b in range(N_BUCKETS)]


_BUCKET_LO = _bucket_lower_bounds()


def _rel_bias_delta(dist, bias_of_bucket):
    last = bias_of_bucket(N_BUCKETS - 1)
    val = jnp.full(dist.shape, (bias_of_bucket(0) - last) * LOG2E, F32)
    for b in range(1, N_BUCKETS - 1):
        val = jnp.where(dist >= _BUCKET_LO[b], (bias_of_bucket(b) - last) * LOG2E, val)
    val = jnp.where(dist >= _BUCKET_LO[N_BUCKETS - 1], 0.0, val)
    return jnp.where(dist < 0, NEG, val)


def _silu(z):
    return z * (1.0 / (1.0 + jnp.exp2(z * -LOG2E)))


def _proj_kernel(x_ref, nw_ref, wT_ref, qT_ref, kvc_ref, ksel_ref, vselT_ref, kwin_ref, vwinT_ref,
                 gsT_ref, rest_ref, kv_scr, w_scr, *, tm, nsel, kd):
    st = pl.program_id(1)

    @pl.when((pl.program_id(0) == 0) & (st == 0))
    def _():
        c0 = _C_REST
        chunk = 256
        for dst, src, rows in ((0, 0, c0), (c0, c0 + N_GATE, _C_GATE - c0)):
            for r in range(0, rows, chunk):
                w_scr[dst + r:dst + r + chunk, :] = wT_ref[src + r:src + r + chunk, :].astype(BF16)
        w_scr[_C_GATE:_C_END, :] = jnp.concatenate(
            [wT_ref[c0:c0 + N_GATE, :], jnp.zeros((LANES - N_GATE, D_MODEL), F32)], axis=0).astype(BF16)

    x = x_ref[0]
    ms = jnp.mean(x * x, axis=-1, keepdims=True)
    h = (x * lax.rsqrt(ms + EPS) * nw_ref[...]).astype(BF16)

    def mm(lo, hi):
        return lax.dot_general(h, w_scr[lo:hi, :], (((1,), (1,)), ((), ())), preferred_element_type=F32)

    q = mm(_C_Q, _C_CMP) * (HEAD_DIM ** -0.5 * LOG2E)
    qT_ref[0] = q.T.astype(BF16)

    kvc = mm(_C_CMP, _C_SW)
    for xk in range(2):
        kv_scr[xk] = kvc[:, xk * D_KV:(xk + 1) * D_KV]
        for j in range(CMP_STRIDE):
            kvc_ref[xk, 0, :, j * D_KV:(j + 1) * D_KV] = kv_scr[xk, pl.ds(j, tm // CMP_STRIDE, stride=CMP_STRIDE), :]

    a = mm(_C_SW, _C_REST)
    pos = st * tm + lax.broadcasted_iota(jnp.int32, (tm, nsel), 0)
    col = lax.broadcasted_iota(jnp.int32, (tm, nsel), 1)
    onehot = jnp.where((pos // SEL_BLOCK) == col, 1.0, 0.0).astype(BF16)
    ones = jnp.ones((V_ROWS - HEAD_DIM, tm), BF16)
    zeros_k = jnp.zeros((tm, LANES - HEAD_DIM), BF16)
    vsT = a[:, D_KV:2 * D_KV].T
    vwT = a[:, 3 * D_KV:4 * D_KV].T
    for g in range(N_KV_HEADS):
        lo, hi = g * HEAD_DIM, (g + 1) * HEAD_DIM
        parts = [a[:, lo:hi].astype(BF16), onehot]
        if kd > HEAD_DIM + nsel:
            parts.append(jnp.zeros((tm, kd - HEAD_DIM - nsel), BF16))
        ksel_ref[0, g] = jnp.concatenate(parts, axis=1)
        kwin_ref[0, g] = jnp.concatenate([a[:, 2 * D_KV + lo:2 * D_KV + hi].astype(BF16), zeros_k], axis=1)
        vs = jnp.concatenate([vsT[lo:hi].astype(BF16), ones], axis=0)
        vw = jnp.concatenate([vwT[lo:hi].astype(BF16), ones], axis=0)
        for kt in range(tm // KEY_TILE):
            vselT_ref[0, g, kt] = vs[:, kt * KEY_TILE:(kt + 1) * KEY_TILE]
            vwinT_ref[0, g, kt] = vw[:, kt * KEY_TILE:(kt + 1) * KEY_TILE]

    r = mm(_C_REST, _C_GATE)
    z_attn, conv_h, conv_b, conv_c, z_conv = (r[:, i * D_CONV:(i + 1) * D_CONV] for i in range(5))
    rest_ref[0] = jnp.concatenate([z_attn, conv_c * conv_h, conv_b, z_conv], axis=1).astype(BF16)
    gates = jax.nn.sigmoid(mm(_C_GATE, _C_END))
    gsT_ref[0] = gates.T[:N_GATE]


def _compress_kernel(c_ref, pe_ref, w1_ref, w2_ref, lhs_tail_ref, aug_ref, t_ref, *, nch):
    c = c_ref[0, 0]
    a0 = jnp.dot((c + pe_ref[0, 0]).astype(BF16), w1_ref[0, 0], preferred_element_type=F32)
    a1 = jnp.dot((c + pe_ref[0, 1]).astype(BF16), w1_ref[0, 1], preferred_element_type=F32)
    hid = a0 + pltpu.roll(a1, nch - 1, 0)
    out = jnp.dot(_silu(hid).astype(BF16), w2_ref[0], preferred_element_type=F32)
    grp = lax.broadcasted_iota(jnp.int32, (nch, LANES - HEAD_DIM), 0) // 8
    col = lax.broadcasted_iota(jnp.int32, (nch, LANES - HEAD_DIM), 1)
    onehot = jnp.where(grp == col, 1.0, 0.0).astype(BF16)
    out_t = out.T.astype(BF16)
    for g in range(N_KV_HEADS):
        lo, hi = g * HEAD_DIM, (g + 1) * HEAD_DIM
        aug_ref[0, 0, g] = jnp.concatenate([out[:, lo:hi].astype(BF16), onehot], axis=1)
        t_ref[0, 0, g] = jnp.concatenate([out_t[lo:hi], lhs_tail_ref[...]], axis=0)


def _cmp_kernel(rb_ref, qT_ref, kc_ref, lhs_ref, gs_ref, ocT_ref, selb_ref,
                qa_ref, band_ref, acc_ref, score_ref, ntk_ref, s0_ref, s1_ref, s2_ref, s3_ref,
                *, nch, nsel, tq, k_top, nqt):
    g = pl.program_id(1)
    qt = pl.program_id(2)
    s_refs = (s0_ref, s1_ref, s2_ref, s3_ref)
    cpq = tq // CMP_STRIDE
    win = cpq + 16
    lim = cpq * (qt + 1)
    nq = GQA_REP * tq

    @pl.when(qt <= 1)
    def _():
        row = lax.broadcasted_iota(jnp.int32, (win, tq), 0) + jnp.maximum(cpq * qt - 16, 0)
        lane = lax.broadcasted_iota(jnp.int32, (win, tq), 1)
        dist = (qt * tq + lane) - (row * CMP_STRIDE + CMP_BLOCK - 1)
        for r in range(GQA_REP):
            band_ref[:, r * tq:(r + 1) * tq] = _rel_bias_delta(dist, lambda b, r=r: rb_ref[b, g * GQA_REP + r])

    q = qT_ref[0]
    qa_ref[0:HEAD_DIM, :] = jnp.concatenate([q[r * HEAD_DIM:(r + 1) * HEAD_DIM] for r in range(GQA_REP)], axis=1)
    grp = lax.broadcasted_iota(jnp.int32, (LANES - HEAD_DIM, nq), 0)
    qa_ref[HEAD_DIM:LANES, :] = jnp.where(grp * 8 >= lim, NEG, 0.0).astype(BF16)

    def run(rows):
        start = pl.multiple_of(jnp.clip(cpq * qt - 16, 0, rows - win), 8)
        k = kc_ref[0, 0, 0, 0:rows, :]
        lhs = lhs_ref[0, 0, 0, :, 0:rows]
        cols = [slice(r * tq, (r + 1) * tq) for r in range(GQA_REP)]

        def qk(r):
            s_refs[r][0:rows, :] = jnp.dot(k, qa_ref[:, cols[r]], preferred_element_type=F32)

        ahead = 3
        for r in range(ahead):
            qk(r)
        for r in range(GQA_REP):
            if r + ahead < GQA_REP:
                qk(r + ahead)
            sr = s_refs[r]
            sr[pl.ds(start, win), :] = sr[pl.ds(start, win), :] + band_ref[:, cols[r]]
            s = sr[0:rows, :]
            p = jnp.exp2(s - jnp.max(s, axis=0, keepdims=True)).astype(BF16)
            acc_ref[:, cols[r]] = jnp.dot(lhs, p, preferred_element_type=F32)

    n = lax.broadcasted_iota(jnp.int32, (nsel, tq), 0)
    lane = lax.broadcasted_iota(jnp.int32, (nsel, tq), 1)

    def forced_and_candidate(tile):
        blk = (tile * tq + lane) // SEL_BLOCK
        forced = (n == 0) | (n == blk) | (n == blk - 1)
        return forced, (n <= blk) & jnp.logical_not(forced), n <= blk

    def select(tile):
        score = score_ref[...]
        for _ in range(k_top - 3):
            score = jnp.where(score == jnp.max(score, axis=0, keepdims=True), -2.0, score)
        taken = score == -2.0
        selb_ref[0, 0, jnp.maximum(tile, 0)] = jnp.where(taken, 0.0, NEG).astype(BF16)
        ntk_ref[...] = jnp.sum(jnp.where(taken & forced_and_candidate(tile)[1], 1.0, 0.0), axis=0, keepdims=True)

    def select_exact_if_tied(tile):
        @pl.when(jnp.max(ntk_ref[...]) > k_top - 3)
        def _():
            def pick_one(_, score):
                mx = jnp.max(score, axis=0, keepdims=True)
                idx = jnp.min(jnp.where(score == mx, n, nsel), axis=0, keepdims=True)
                return jnp.where(n == idx, -2.0, score)

            taken = lax.fori_loop(0, k_top - 3, pick_one, score_ref[...]) == -2.0
            selb_ref[0, 0, jnp.maximum(tile, 0)] = jnp.where(taken, 0.0, NEG).astype(BF16)

    @pl.when(qt == 0)
    def _():
        score_ref[...] = jnp.full((nsel, tq), -2.0, F32)

    half = nch // 2

    @pl.when(lim <= half)
    def _():
        run(half)
        select(qt - 1)

    @pl.when(lim > half)
    def _():
        run(nch)
        select(qt - 1)

    select_exact_if_tied(qt - 1)

    l = acc_ref[HEAD_DIM:HEAD_DIM + 1, :]
    t4 = qt * tq + (lax.broadcasted_iota(jnp.int32, l.shape, 1) & (tq - 1))
    inv = jnp.where(t4 >= CMP_BLOCK - 1, 1.0 / jnp.maximum(l, 1e-30), 0.0)
    imp = None
    for r in range(GQA_REP):
        cs = slice(r * tq, (r + 1) * tq)
        ocT_ref[0, r * HEAD_DIM:(r + 1) * HEAD_DIM, :] = (acc_ref[0:HEAD_DIM, cs] * inv[:, cs]
                                                          * gs_ref[0, 0, 0, r:r + 1, :]).astype(BF16)
        part = acc_ref[V_ROWS:V_ROWS + nsel, cs] * inv[:, cs]
        imp = part if imp is None else imp + part

    forced, _, causal = forced_and_candidate(qt)
    score_ref[...] = jnp.where(forced, -2.0, jnp.where(causal, imp, -1.0))

    @pl.when(qt == nqt - 1)
    def _():
        select(qt)
        select_exact_if_tied(qt)


def _flash_branch(rb_ref, k_ref, vT_ref, gs_ref, o_ref, tbl_ref, qa_ref, m_ref, acc_ref, lag_ref, *slot_refs,
                  tq, n_tbl_tiles, window, selected, add_ref=None):
    s_ref, mt_ref = slot_refs[0:1], slot_refs[1:2]
    al_ref, p_ref = slot_refs[2:2 + N_SLOTS], slot_refs[2 + N_SLOTS:2 + 2 * N_SLOTS]
    g = pl.program_id(1)
    qt = pl.program_id(2)
    nq = GQA_REP * tq

    @pl.when(qt == 0)
    def _():
        for j in range(n_tbl_tiles):
            key = lax.broadcasted_iota(jnp.int32, (tq, tq), 0) + j * tq
            qry = lax.broadcasted_iota(jnp.int32, (tq, tq), 1) + (n_tbl_tiles - 1) * tq
            dist = qry - key
            for r in range(GQA_REP):
                val = _rel_bias_delta(dist, lambda b, r=r: rb_ref[b, g * GQA_REP + r])
                if window is not None:
                    val = jnp.where(dist >= window, NEG, val)
                tbl_ref[j * tq:(j + 1) * tq, r * tq:(r + 1) * tq] = val

    m_ref[...] = jnp.full((1, nq), NEG, F32)
    acc_ref[...] = jnp.zeros((V_ROWS, nq), F32)

    def serial(t, tbl_tile):
        k = k_ref[0, 0, pl.ds(pl.multiple_of(t * tq, tq), tq), :]
        s = jnp.dot(k, qa_ref[...], preferred_element_type=F32)
        if tbl_tile is not None:
            s = s + tbl_ref[tbl_tile * tq:(tbl_tile + 1) * tq, :]
        s_ref[0][...] = s
        mt_ref[0][...] = jnp.max(s, axis=0, keepdims=True)
        m_old = m_ref[...]
        m_new = jnp.maximum(m_old, mt_ref[0][...])
        m_ref[...] = m_new
        p_ref[0][...] = jnp.exp2(s_ref[0][...] - m_new).astype(BF16)
        acc_ref[...] = jnp.exp2(m_old - m_new) * acc_ref[...] + jnp.dot(vT_ref[0, 0, t], p_ref[0][...],
                                                                        preferred_element_type=F32)

    last = n_tbl_tiles - 1

    def stage_lagged(t, slot, tbl_tile, first=False):
        k = k_ref[0, 0, pl.ds(pl.multiple_of(t * tq, tq), tq), :]
        s = jnp.dot(k, qa_ref[...], preferred_element_type=F32)
        if tbl_tile is not None:
            s = s + tbl_ref[tbl_tile * tq:(tbl_tile + 1) * tq, :]
        m = s[0:1, :] if first else m_ref[...]
        mt = jnp.max(s, axis=0, keepdims=True)
        p_ref[slot][...] = jnp.exp2((s - m).astype(BF16))
        m_new = jnp.maximum(m, mt)
        m_ref[...] = m_new
        al_ref[slot][...] = jnp.exp2(m - m_new)
        lag_ref[...] = (mt - m) if first else jnp.maximum(lag_ref[...], mt - m)

    def stage_pv_lagged(t, slot):
        acc_ref[...] = (acc_ref[...] + jnp.dot(vT_ref[0, 0, t], p_ref[slot][...],
                                               preferred_element_type=F32)) * al_ref[slot][...]

    def lagged_step(t, slot, tbl_tile, prev=None):
        stage_lagged(t, slot, tbl_tile)
        stage_pv_lagged(t - 1 if prev is None else prev, 1 - slot)

    if selected:
        n = qt + 1

        @pl.when((n >= 4) & ((n & 1) == 0))
        def _():
            stage_lagged(0, 1, None, first=True)
            lagged_step(1, 0, None)

        @pl.when((n >= 4) & ((n & 1) == 1))
        def _():
            stage_lagged(0, 0, None, first=True)

        @pl.when(n >= 4)
        def _():
            t0 = 2 - (n & 1)

            pairs = (n - 2 - t0) >> 1

            def pair(t):
                lagged_step(t, 1, None)
                lagged_step(t + 1, 0, None)

            def body(j, carry):
                for i in range(4):
                    pair(t0 + 8 * j + 2 * i)
                return carry
            lax.fori_loop(0, pairs >> 2, body, 0)

            @pl.when((pairs & 2) == 2)
            def _():
                t = n - 2 - 2 * (pairs & 3)
                pair(t)
                pair(t + 2)

            @pl.when((pairs & 1) == 1)
            def _():
                pair(n - 4)

        @pl.when(n >= 4)
        def _():
            lagged_step(n - 2, 1, 0)
            lagged_step(n - 1, 0, 1)
            stage_pv_lagged(n - 1, 0)

            @pl.when(jnp.max(lag_ref[...]) > MAX_LAG)
            def _():
                m_ref[...] = jnp.full((1, nq), NEG, F32)
                acc_ref[...] = jnp.zeros((V_ROWS, nq), F32)

                def redo(t, carry):
                    serial(t, None)
                    return carry
                lax.fori_loop(0, n - 2, redo, 0)
                serial(n - 2, 0)
                serial(n - 1, 1)

        @pl.when(n == 3)
        def _():
            serial(0, None)

        @pl.when((n >= 2) & (n < 4))
        def _():
            serial(qt - 1, 0)

        @pl.when(n < 4)
        def _():
            serial(qt, 1)
    else:
        @pl.when(qt >= last)
        def _():
            stage_lagged(qt, 0, 2, first=True)
            lagged_step(qt - 1, 1, 1, prev=qt)
            lagged_step(qt - 2, 0, 0, prev=qt - 1)
            stage_pv_lagged(qt - 2, 0)

            @pl.when(jnp.max(lag_ref[...]) > MAX_LAG)
            def _():
                m_ref[...] = jnp.full((1, nq), NEG, F32)
                acc_ref[...] = jnp.zeros((V_ROWS, nq), F32)
                for j in range(n_tbl_tiles):
                    serial(qt - last + j, j)

        @pl.when(qt == 1)
        def _():
            serial(0, 1)

        @pl.when(qt < last)
        def _():
            serial(qt, 2)

    acc = acc_ref[...]
    o = acc[0:HEAD_DIM] / jnp.maximum(acc[HEAD_DIM:HEAD_DIM + 1], 1e-30)
    for r in range(GQA_REP):
        rows = slice(r * HEAD_DIM, (r + 1) * HEAD_DIM)
        val = o[:, r * tq:(r + 1) * tq] * gs_ref[0, 0, 0, r:r + 1, :]
        if add_ref is not None:
            val = val + add_ref[0, rows, :]
        o_ref[0, rows, :] = val.astype(o_ref.dtype)


SEL_TBL_TILES = 2
WIN_TBL_TILES = 3


def _branches_kernel(rb_ref, qT_ref, selb_ref, ksel_ref, vselT_ref, kwin_ref, vwinT_ref, gs_sel_ref, gs_win_ref,
                     o_ref, owin_ref, tbl_ref, qa_ref, *state, tq, nsel, kd):
    q = qT_ref[0]
    nq = GQA_REP * tq
    qa_ref[0:HEAD_DIM, :] = jnp.concatenate([q[r * HEAD_DIM:(r + 1) * HEAD_DIM] for r in range(GQA_REP)], axis=1)
    qa_ref[HEAD_DIM:HEAD_DIM + nsel, :] = jnp.concatenate([selb_ref[0, 0, 0]] * GQA_REP, axis=1)
    if kd > HEAD_DIM + nsel:
        qa_ref[HEAD_DIM + nsel:kd, :] = jnp.zeros((kd - HEAD_DIM - nsel, nq), BF16)

    win_tbl = tbl_ref.at[SEL_TBL_TILES * tq:(SEL_TBL_TILES + WIN_TBL_TILES) * tq]
    _flash_branch(rb_ref, kwin_ref, vwinT_ref, gs_win_ref, owin_ref, win_tbl, qa_ref.at[0:LANES], *state,
                  tq=tq, n_tbl_tiles=WIN_TBL_TILES, window=WINDOW, selected=False)
    _flash_branch(rb_ref, ksel_ref, vselT_ref, gs_sel_ref, o_ref, tbl_ref.at[0:SEL_TBL_TILES * tq], qa_ref, *state,
                  tq=tq, n_tbl_tiles=SEL_TBL_TILES, window=None, selected=True, add_ref=owin_ref)


def _out_kernel(oc_ref, osw_ref, za_ref, u_ref, cb_ref, zc_ref, hu_ref,
                x_ref, cw_ref, wo_ref, fw_ref, out_ref, *, tm):
    st = pl.program_id(1)
    f32 = lambda ref: ref[0].astype(F32)
    attn = (f32(oc_ref) + f32(osw_ref)).T * _silu(f32(za_ref))

    u = f32(u_ref)
    halo = jnp.where(st == 0, 0.0, f32(hu_ref))
    row = lax.broadcasted_iota(jnp.int32, u.shape, 0)
    h1, h2 = halo[HALO_ROWS - 1:HALO_ROWS], halo[HALO_ROWS - 2:HALO_ROWS - 1]
    u1 = jnp.where(row == 0, h1, pltpu.roll(u, 1, 0))
    u2 = jnp.where(row == 0, h2, jnp.where(row == 1, h1, pltpu.roll(u, 2, 0)))
    y = cw_ref[0:1] * u2 + cw_ref[1:2] * u1 + cw_ref[2:3] * u
    conv = f32(cb_ref) * y * _silu(f32(zc_ref))

    mixed = jnp.concatenate([attn, conv], axis=1).astype(BF16)
    y2 = x_ref[0] + jnp.dot(mixed, wo_ref[...], preferred_element_type=F32)
    ms = jnp.mean(y2 * y2, axis=-1, keepdims=True)
    out_ref[0] = y2 * lax.rsqrt(ms + EPS) * fw_ref[...]


def _overlap_t(nsel, nch):
    c = np.arange(nch)[None, :] * CMP_STRIDE
    n = np.arange(nsel)[:, None] * SEL_BLOCK
    return ((c < n + SEL_BLOCK) & (c + CMP_BLOCK > n)).astype(np.float32)


def _cparams(sem):
    return pltpu.CompilerParams(dimension_semantics=sem, vmem_limit_bytes=VMEM_LIMIT)


def kernel(x, norm_w, w_in, w_ck1, w_ck2, pe_k, w_cv1, w_cv2, pe_v, conv_w, w_out, rel_bias, final_norm_w):
    b, s, _ = x.shape
    assert norm_w.shape[0] == 1 and s % ROW_TILE == 0 and ROW_TILE % KEY_TILE == 0
    assert WINDOW == 2 * Q_TILE and KEY_TILE == Q_TILE
    tm = ROW_TILE
    nt = s // tm
    nkt = s // KEY_TILE
    kpt = tm // KEY_TILE
    nch = s // CMP_STRIDE
    nsel = s // SEL_BLOCK
    k_top = min(N_SEL, nsel)
    kd = -(-(HEAD_DIM + nsel) // LANES) * LANES
    G, R = N_KV_HEADS, GQA_REP

    w_in_t = jnp.swapaxes(w_in[0], 0, 1)
    assert w_in_t.shape[0] == _C_GATE + N_GATE
    eye = jnp.eye(G, dtype=F32)

    def expand1(w1):
        w1 = w1.astype(BF16).reshape(2, CMP_STRIDE, HEAD_DIM, CMP_HIDDEN)
        z = jnp.zeros_like(w1)
        rows = [jnp.concatenate([w1 if a == g else z for a in range(G)], axis=-1) for g in range(G)]
        return jnp.stack(rows, axis=2).reshape(2, CMP_STRIDE * D_KV, G * CMP_HIDDEN)

    def expand_pe(pe):
        pe = pe.reshape(2, CMP_STRIDE, 1, HEAD_DIM)
        return jnp.broadcast_to(pe, (2, CMP_STRIDE, G, HEAD_DIM)).reshape(2, 1, CMP_STRIDE * D_KV)

    def expand2(w2):
        return jnp.einsum('hd,ab->ahbd', w2, eye).reshape(G * CMP_HIDDEN, D_KV)

    w1e = jnp.stack([expand1(w_ck1[0]), expand1(w_cv1[0])]).astype(BF16)
    pee = jnp.stack([expand_pe(pe_k[0]), expand_pe(pe_v[0])])
    w2e = jnp.stack([expand2(w_ck2[0]), expand2(w_cv2[0])]).astype(BF16)
    rel_bias = rel_bias.astype(F32)
    smem = pl.BlockSpec(memory_space=pltpu.SMEM)

    n_rest = D_ATTN + 3 * D_CONV
    cw = CMP_STRIDE * D_KV
    qT, kvc, ksel, vselT, kwin, vwinT, gsT, rest = pl.pallas_call(
        partial(_proj_kernel, tm=tm, nsel=nsel, kd=kd),
        grid=(b, nt),
        in_specs=[pl.BlockSpec((1, tm, D_MODEL), lambda i, j: (i, j, 0)),
                  pl.BlockSpec((1, D_MODEL), lambda i, j: (0, 0)),
                  pl.BlockSpec((_C_GATE + N_GATE, D_MODEL), lambda i, j: (0, 0), pipeline_mode=pl.Buffered(1))],
        out_specs=[pl.BlockSpec((1, D_ATTN, tm), lambda i, j: (i, 0, j)),
                   pl.BlockSpec((2, 1, tm // CMP_STRIDE, cw), lambda i, j: (0, i, j, 0)),
                   pl.BlockSpec((1, G, tm, kd), lambda i, j: (i, 0, j, 0)),
                   pl.BlockSpec((1, G, kpt, V_ROWS, KEY_TILE), lambda i, j: (i, 0, j, 0, 0)),
                   pl.BlockSpec((1, G, tm, LANES), lambda i, j: (i, 0, j, 0)),
                   pl.BlockSpec((1, G, kpt, V_ROWS, KEY_TILE), lambda i, j: (i, 0, j, 0, 0)),
                   pl.BlockSpec((1, N_GATE, tm), lambda i, j: (i, 0, j)),
                   pl.BlockSpec((1, tm, n_rest), lambda i, j: (i, j, 0))],
        out_shape=[jax.ShapeDtypeStruct((b, D_ATTN, s), BF16),
                   jax.ShapeDtypeStruct((2, b, nch, cw), F32),
                   jax.ShapeDtypeStruct((b, G, s, kd), BF16),
                   jax.ShapeDtypeStruct((b, G, nkt, V_ROWS, KEY_TILE), BF16),
                   jax.ShapeDtypeStruct((b, G, s, LANES), BF16),
                   jax.ShapeDtypeStruct((b, G, nkt, V_ROWS, KEY_TILE), BF16),
                   jax.ShapeDtypeStruct((b, N_GATE, s), F32),
                   jax.ShapeDtypeStruct((b, s, n_rest), BF16)],
        scratch_shapes=[pltpu.VMEM((2, tm, D_KV), F32),
                        pltpu.VMEM((_C_END, D_MODEL), BF16)],
        compiler_params=_cparams(("arbitrary", "arbitrary")),
        name="in_proj",
    )(x, norm_w, w_in_t)
    gs = gsT.reshape(b, 3, G, R, s)

    lhs_rows = V_ROWS + nsel
    lhs_tail = jnp.asarray(np.concatenate([np.ones((V_ROWS - HEAD_DIM, nch), np.float32),
                                           _overlap_t(nsel, nch)], axis=0), BF16)
    kc_aug, c_lhs = pl.pallas_call(
        partial(_compress_kernel, nch=nch),
        grid=(2, b),
        in_specs=[pl.BlockSpec((1, 1, nch, cw), lambda xk, i: (xk, i, 0, 0)),
                  pl.BlockSpec((1, 2, 1, cw), lambda xk, i: (xk, 0, 0, 0)),
                  pl.BlockSpec((1, 2, cw, G * CMP_HIDDEN), lambda xk, i: (xk, 0, 0, 0)),
                  pl.BlockSpec((1, G * CMP_HIDDEN, D_KV), lambda xk, i: (xk, 0, 0)),
                  pl.BlockSpec((lhs_rows - HEAD_DIM, nch), lambda xk, i: (0, 0))],
        out_specs=[pl.BlockSpec((1, 1, G, nch, LANES), lambda xk, i: (xk, i, 0, 0, 0)),
                   pl.BlockSpec((1, 1, G, lhs_rows, nch), lambda xk, i: (xk, i, 0, 0, 0))],
        out_shape=[jax.ShapeDtypeStruct((2, b, G, nch, LANES), BF16),
                   jax.ShapeDtypeStruct((2, b, G, lhs_rows, nch), BF16)],
        compiler_params=_cparams(("parallel", "parallel")),
        name="compress",
    )(kvc, pee, w1e, w2e, lhs_tail)

    tq3 = Q_TILE_CMP
    cpq = tq3 // CMP_STRIDE
    assert nch // 8 <= LANES - HEAD_DIM and k_top > 3 and (nch // 2) % LANES == 0
    ocT, selb = pl.pallas_call(
        partial(_cmp_kernel, nch=nch, nsel=nsel, tq=tq3, k_top=k_top, nqt=s // tq3),
        grid=(b, G, s // tq3),
        in_specs=[smem,
                  pl.BlockSpec((1, R * HEAD_DIM, tq3), lambda i, g, t: (i, g, t)),
                  pl.BlockSpec((1, 1, 1, nch, LANES), lambda i, g, t: (0, i, g, 0, 0)),
                  pl.BlockSpec((1, 1, 1, lhs_rows, nch), lambda i, g, t: (1, i, g, 0, 0)),
                  pl.BlockSpec((1, 1, 1, R, tq3), lambda i, g, t: (i, 0, g, 0, t))],
        out_specs=[pl.BlockSpec((1, R * HEAD_DIM, tq3), lambda i, g, t: (i, g, t)),
                   pl.BlockSpec((1, 1, s // tq3, nsel, tq3), lambda i, g, t: (i, g, 0, 0, 0))],
        out_shape=[jax.ShapeDtypeStruct((b, D_ATTN, s), BF16),
                   jax.ShapeDtypeStruct((b, G, s // tq3, nsel, tq3), BF16)],
        scratch_shapes=[pltpu.VMEM((LANES, R * tq3), BF16),
                        pltpu.VMEM((cpq + 16, R * tq3), F32),
                        pltpu.VMEM((lhs_rows, R * tq3), F32),
                        pltpu.VMEM((nsel, tq3), F32),
                        pltpu.VMEM((1, tq3), F32)]
                       + [pltpu.VMEM((nch, tq3), F32)] * R,
        compiler_params=_cparams(("parallel", "parallel", "arbitrary")),
        name="cmp_select",
    )(rel_bias, qT, kc_aug, c_lhs, gs)

    tq = Q_TILE
    nq = R * tq

    o_spec = pl.BlockSpec((1, R * HEAD_DIM, tq), lambda i, g, t: (i, g, t))
    k_spec = lambda kdim: pl.BlockSpec((1, 1, s, kdim), lambda i, g, t: (i, g, 0, 0))
    v_spec = pl.BlockSpec((1, 1, nkt, V_ROWS, KEY_TILE), lambda i, g, t: (i, g, 0, 0, 0))
    gs_spec = lambda branch: pl.BlockSpec((1, 1, 1, R, tq), lambda i, g, t: (i, branch, g, 0, t))
    oswT = pl.pallas_call(
        partial(_branches_kernel, tq=tq, nsel=nsel, kd=kd),
        grid=(b, G, s // tq),
        in_specs=[smem, o_spec,
                  pl.BlockSpec((1, 1, 1, nsel, tq), lambda i, g, t: (i, g, t // (tq3 // tq), 0, t % (tq3 // tq))),
                  k_spec(kd), v_spec, k_spec(LANES), v_spec, gs_spec(1), gs_spec(2)],
        out_specs=o_spec,
        out_shape=jax.ShapeDtypeStruct((b, D_ATTN, s), BF16),
        scratch_shapes=[pltpu.VMEM((1, R * HEAD_DIM, tq), F32),
                        pltpu.VMEM(((SEL_TBL_TILES + WIN_TBL_TILES) * tq, nq), F32),
                        pltpu.VMEM((kd, nq), BF16),
                        pltpu.VMEM((1, nq), F32),
                        pltpu.VMEM((V_ROWS, nq), F32),
                        pltpu.VMEM((1, nq), F32)]
                       + [pltpu.VMEM((tq, nq), F32)]
                       + [pltpu.VMEM((1, nq), F32)]
                       + [pltpu.VMEM((1, nq), F32)] * N_SLOTS
                       + [pltpu.VMEM((tq, nq), BF16)] * N_SLOTS,
        compiler_params=_cparams(("parallel", "parallel", "arbitrary")),
        name="sel_win_attn",
    )(rel_bias, qT, selb, ksel, vselT, kwin, vwinT, gs, gs)

    hb = tm // HALO_ROWS
    col = lambda c: pl.BlockSpec((1, tm, D_CONV), lambda i, j, c=c: (i, j, c))
    halo = lambda c: pl.BlockSpec((1, HALO_ROWS, D_CONV), lambda i, j, c=c: (i, jnp.maximum(j * hb - 1, 0), c))
    oT = pl.BlockSpec((1, D_ATTN, tm), lambda i, j: (i, 0, j))
    out = pl.pallas_call(
        partial(_out_kernel, tm=tm),
        grid=(b, nt),
        in_specs=[oT, oT, col(0), col(1), col(2), col(3), halo(1),
                  pl.BlockSpec((1, tm, D_MODEL), lambda i, j: (i, j, 0)),
                  pl.BlockSpec((3, D_CONV), lambda i, j: (0, 0)),
                  pl.BlockSpec((D_MODEL, D_MODEL), lambda i, j: (0, 0)),
                  pl.BlockSpec((1, D_MODEL), lambda i, j: (0, 0))],
        out_specs=pl.BlockSpec((1, tm, D_MODEL), lambda i, j: (i, j, 0)),
        out_shape=jax.ShapeDtypeStruct((b, s, D_MODEL), F32),
        compiler_params=_cparams(("parallel", "parallel")),
        name="out_proj",
    )(ocT, oswT, rest, rest, rest, rest, rest, x, conv_w[0], w_out[0].astype(BF16),
      final_norm_w.reshape(1, D_MODEL))
    return out
```

```python
import math
from functools import partial

import numpy as np
import jax
import jax.numpy as jnp
from jax import lax
from jax.experimental import pallas as pl
from jax.experimental.pallas import tpu as pltpu

D_MODEL = 1024
N_HEADS = 8
HEAD_DIM = 64
N_KV_HEADS = 2
GQA_REP = N_HEADS // N_KV_HEADS
D_ATTN = N_HEADS * HEAD_DIM
D_KV = N_KV_HEADS * HEAD_DIM
D_CONV = D_MODEL - D_ATTN
CMP_BLOCK = 32
CMP_STRIDE = 16
CMP_HIDDEN = 2 * HEAD_DIM
SEL_BLOCK = 64
N_SEL = 16
WINDOW = 512
N_BUCKETS = 32
MAX_DISTANCE = 128
EPS = 1e-6
NEG = -1e30

LOG2E = 1.4426950408889634
LANES = 128
V_ROWS = HEAD_DIM + 16
ROW_TILE = 512
KEY_TILE = 256
Q_TILE_CMP = 512
Q_TILE = 256
HALO_ROWS = 16
MAX_LAG = 64.0
N_SLOTS = 2
VMEM_LIMIT = 48 * 1024 * 1024

F32 = jnp.float32
BF16 = jnp.bfloat16

_C_Q = 0
_C_CMP = D_ATTN
_C_SW = _C_CMP + 2 * D_KV
_C_REST = _C_SW + 4 * D_KV
_C_GATE = _C_REST + D_ATTN + 4 * D_CONV
_C_END = _C_GATE + LANES
N_GATE = 3 * N_HEADS


def _bucket_lower_bounds():
    max_exact = N_BUCKETS // 2
    d = np.arange(2 * MAX_DISTANCE, dtype=np.int32)
    nf = np.maximum(d, 1).astype(np.float32)
    large = max_exact + (np.log(nf / np.float32(max_exact)) / np.float32(math.log(MAX_DISTANCE / max_exact))
                         * np.float32(N_BUCKETS - max_exact)).astype(np.int32)
    bucket = np.where(d < max_exact, d, np.minimum(large, N_BUCKETS - 1))
    assert np.all(np.diff(bucket) >= 0) and np.all(bucket[MAX_DISTANCE:] == N_BUCKETS - 1)
    return [int(np.argmax(bucket >= b)) for b in range(N_BUCKETS)]


_BUCKET_LO = _bucket_lower_bounds()


def _rel_bias_delta(dist, bias_of_bucket):
    last = bias_of_bucket(N_BUCKETS - 1)
    val = jnp.full(dist.shape, (bias_of_bucket(0) - last) * LOG2E, F32)
    for b in range(1, N_BUCKETS - 1):
        val = jnp.where(dist >= _BUCKET_LO[b], (bias_of_bucket(b) - last) * LOG2E, val)
    val = jnp.where(dist >= _BUCKET_LO[N_BUCKETS - 1], 0.0, val)
    return jnp.where(dist < 0, NEG, val)


def _silu(z):
    return z * (1.0 / (1.0 + jnp.exp2(z * -LOG2E)))


def _proj_kernel(x_ref, nw_ref, wT_ref, qT_ref, kvc_ref, ksel_ref, vselT_ref, kwin_ref, vwinT_ref,
                 gsT_ref, rest_ref, kv_scr, w_scr, *, tm, nsel, kd):
    st = pl.program_id(1)

    @pl.when((pl.program_id(0) == 0) & (st == 0))
    def _():
        c0 = _C_REST
        chunk = 256
        for dst, src, rows in ((0, 0, c0), (c0, c0 + N_GATE, _C_GATE - c0)):
            for r in range(0, rows, chunk):
                w_scr[dst + r:dst + r + chunk, :] = wT_ref[src + r:src + r + chunk, :].astype(BF16)
        w_scr[_C_GATE:_C_END, :] = jnp.concatenate(
            [wT_ref[c0:c0 + N_GATE, :], jnp.zeros((LANES - N_GATE, D_MODEL), F32)], axis=0).astype(BF16)

    x = x_ref[0]
    ms = jnp.mean(x * x, axis=-1, keepdims=True)
    h = (x * lax.rsqrt(ms + EPS) * nw_ref[...]).astype(BF16)

    def mm(lo, hi):
        return lax.dot_general(h, w_scr[lo:hi, :], (((1,), (1,)), ((), ())), preferred_element_type=F32)

    q = mm(_C_Q, _C_CMP) * (HEAD_DIM ** -0.5 * LOG2E)
    qT_ref[0] = q.T.astype(BF16)

    kvc = mm(_C_CMP, _C_SW)
    for xk in range(2):
        kv_scr[xk] = kvc[:, xk * D_KV:(xk + 1) * D_KV]
        for j in range(CMP_STRIDE):
            kvc_ref[xk, 0, :, j * D_KV:(j + 1) * D_KV] = kv_scr[xk, pl.ds(j, tm // CMP_STRIDE, stride=CMP_STRIDE), :]

    a = mm(_C_SW, _C_REST)
    pos = st * tm + lax.broadcasted_iota(jnp.int32, (tm, nsel), 0)
    col = lax.broadcasted_iota(jnp.int32, (tm, nsel), 1)
    onehot = jnp.where((pos // SEL_BLOCK) == col, 1.0, 0.0).astype(BF16)
    ones = jnp.ones((V_ROWS - HEAD_DIM, tm), BF16)
    zeros_k = jnp.zeros((tm, LANES - HEAD_DIM), BF16)
    vsT = a[:, D_KV:2 * D_KV].T
    vwT = a[:, 3 * D_KV:4 * D_KV].T
    for g in range(N_KV_HEADS):
        lo, hi = g * HEAD_DIM, (g + 1) * HEAD_DIM
        parts = [a[:, lo:hi].astype(BF16), onehot]
        if kd > HEAD_DIM + nsel:
            parts.append(jnp.zeros((tm, kd - HEAD_DIM - nsel), BF16))
        ksel_ref[0, g] = jnp.concatenate(parts, axis=1)
        kwin_ref[0, g] = jnp.concatenate([a[:, 2 * D_KV + lo:2 * D_KV + hi].astype(BF16), zeros_k], axis=1)
        vs = jnp.concatenate([vsT[lo:hi].astype(BF16), ones], axis=0)
        vw = jnp.concatenate([vwT[lo:hi].astype(BF16), ones], axis=0)
        for kt in range(tm // KEY_TILE):
            vselT_ref[0, g, kt] = vs[:, kt * KEY_TILE:(kt + 1) * KEY_TILE]
            vwinT_ref[0, g, kt] = vw[:, kt * KEY_TILE:(kt + 1) * KEY_TILE]

    r = mm(_C_REST, _C_GATE)
    z_attn, conv_h, conv_b, conv_c, z_conv = (r[:, i * D_CONV:(i + 1) * D_CONV] for i in range(5))
    rest_ref[0] = jnp.concatenate([z_attn, conv_c * conv_h, conv_b, z_conv], axis=1).astype(BF16)
    gates = jax.nn.sigmoid(mm(_C_GATE, _C_END))
    gsT_ref[0] = gates.T[:N_GATE]


def _compress_kernel(c_ref, pe_ref, w1_ref, w2_ref, lhs_tail_ref, aug_ref, t_ref, *, nch):
    c = c_ref[0, 0]
    a0 = jnp.dot((c + pe_ref[0, 0]).astype(BF16), w1_ref[0, 0], preferred_element_type=F32)
    a1 = jnp.dot((c + pe_ref[0, 1]).astype(BF16), w1_ref[0, 1], preferred_element_type=F32)
    hid = a0 + pltpu.roll(a1, nch - 1, 0)
    out = jnp.dot(_silu(hid).astype(BF16), w2_ref[0], preferred_element_type=F32)
    grp = lax.broadcasted_iota(jnp.int32, (nch, LANES - HEAD_DIM), 0) // 8
    col = lax.broadcasted_iota(jnp.int32, (nch, LANES - HEAD_DIM), 1)
    onehot = jnp.where(grp == col, 1.0, 0.0).astype(BF16)
    out_t = out.T.astype(BF16)
    for g in range(N_KV_HEADS):
        lo, hi = g * HEAD_DIM, (g + 1) * HEAD_DIM
        aug_ref[0, 0, g] = jnp.concatenate([out[:, lo:hi].astype(BF16), onehot], axis=1)
        t_ref[0, 0, g] = jnp.concatenate([out_t[lo:hi], lhs_tail_ref[...]], axis=0)


def _cmp_kernel(rb_ref, qT_ref, kc_ref, lhs_ref, gs_ref, ocT_ref, selb_ref,
                qa_ref, band_ref, acc_ref, score_ref, ntk_ref, s0_ref, s1_ref, s2_ref, s3_ref,
                *, nch, nsel, tq, k_top, nqt):
    g = pl.program_id(1)
    qt = pl.program_id(2)
    s_refs = (s0_ref, s1_ref, s2_ref, s3_ref)
    cpq = tq // CMP_STRIDE
    win = cpq + 16
    lim = cpq * (qt + 1)
    nq = GQA_REP * tq

    @pl.when(qt <= 1)
    def _():
        row = lax.broadcasted_iota(jnp.int32, (win, tq), 0) + jnp.maximum(cpq * qt - 16, 0)
        lane = lax.broadcasted_iota(jnp.int32, (win, tq), 1)
        dist = (qt * tq + lane) - (row * CMP_STRIDE + CMP_BLOCK - 1)
        for r in range(GQA_REP):
            band_ref[:, r * tq:(r + 1) * tq] = _rel_bias_delta(dist, lambda b, r=r: rb_ref[b, g * GQA_REP + r])

    q = qT_ref[0]
    qa_ref[0:HEAD_DIM, :] = jnp.concatenate([q[r * HEAD_DIM:(r + 1) * HEAD_DIM] for r in range(GQA_REP)], axis=1)
    grp = lax.broadcasted_iota(jnp.int32, (LANES - HEAD_DIM, nq), 0)
    qa_ref[HEAD_DIM:LANES, :] = jnp.where(grp * 8 >= lim, NEG, 0.0).astype(BF16)

    def run(rows):
        start = pl.multiple_of(jnp.clip(cpq * qt - 16, 0, rows - win), 8)
        k = kc_ref[0, 0, 0, 0:rows, :]
        lhs = lhs_ref[0, 0, 0, :, 0:rows]
        cols = [slice(r * tq, (r + 1) * tq) for r in range(GQA_REP)]

        def qk(r):
            s_refs[r][0:rows, :] = jnp.dot(k, qa_ref[:, cols[r]], preferred_element_type=F32)

        ahead = 3
        for r in range(ahead):
            qk(r)
        for r in range(GQA_REP):
            if r + ahead < GQA_REP:
                qk(r + ahead)
            sr = s_refs[r]
            sr[pl.ds(start, win), :] = sr[pl.ds(start, win), :] + band_ref[:, cols[r]]
            s = sr[0:rows, :]
            p = jnp.exp2(s - jnp.max(s, axis=0, keepdims=True)).astype(BF16)
            acc_ref[:, cols[r]] = jnp.dot(lhs, p, preferred_element_type=F32)

    n = lax.broadcasted_iota(jnp.int32, (nsel, tq), 0)
    lane = lax.broadcasted_iota(jnp.int32, (nsel, tq), 1)

    def forced_and_candidate(tile):
        blk = (tile * tq + lane) // SEL_BLOCK
        forced = (n == 0) | (n == blk) | (n == blk - 1)
        return forced, (n <= blk) & jnp.logical_not(forced), n <= blk

    def select(tile):
        score = score_ref[...]
        for _ in range(k_top - 3):
            score = jnp.where(score == jnp.max(score, axis=0, keepdims=True), -2.0, score)
        taken = score == -2.0
        selb_ref[0, 0, jnp.maximum(tile, 0)] = jnp.where(taken, 0.0, NEG).astype(BF16)
        ntk_ref[...] = jnp.sum(jnp.where(taken & forced_and_candidate(tile)[1], 1.0, 0.0), axis=0, keepdims=True)

    def select_exact_if_tied(tile):
        @pl.when(jnp.max(ntk_ref[...]) > k_top - 3)
        def _():
            def pick_one(_, score):
                mx = jnp.max(score, axis=0, keepdims=True)
                idx = jnp.min(jnp.where(score == mx, n, nsel), axis=0, keepdims=True)
                return jnp.where(n == idx, -2.0, score)

            taken = lax.fori_loop(0, k_top - 3, pick_one, score_ref[...]) == -2.0
            selb_ref[0, 0, jnp.maximum(tile, 0)] = jnp.where(taken, 0.0, NEG).astype(BF16)

    @pl.when(qt == 0)
    def _():
        score_ref[...] = jnp.full((nsel, tq), -2.0, F32)

    half = nch // 2

    @pl.when(lim <= half)
    def _():
        run(half)
        select(qt - 1)

    @pl.when(lim > half)
    def _():
        run(nch)
        select(qt - 1)

    select_exact_if_tied(qt - 1)

    l = acc_ref[HEAD_DIM:HEAD_DIM + 1, :]
    t4 = qt * tq + (lax.broadcasted_iota(jnp.int32, l.shape, 1) & (tq - 1))
    inv = jnp.where(t4 >= CMP_BLOCK - 1, 1.0 / jnp.maximum(l, 1e-30), 0.0)
    imp = None
    for r in range(GQA_REP):
        cs = slice(r * tq, (r + 1) * tq)
        ocT_ref[0, r * HEAD_DIM:(r + 1) * HEAD_DIM, :] = (acc_ref[0:HEAD_DIM, cs] * inv[:, cs]
                                                          * gs_ref[0, 0, 0, r:r + 1, :]).astype(BF16)
        part = acc_ref[V_ROWS:V_ROWS + nsel, cs] * inv[:, cs]
        imp = part if imp is None else imp + part

    forced, _, causal = forced_and_candidate(qt)
    score_ref[...] = jnp.where(forced, -2.0, jnp.where(causal, imp, -1.0))

    @pl.when(qt == nqt - 1)
    def _():
        select(qt)
        select_exact_if_tied(qt)


def _flash_branch(rb_ref, k_ref, vT_ref, gs_ref, o_ref, tbl_ref, qa_ref, m_ref, acc_ref, lag_ref, *slot_refs,
                  tq, n_tbl_tiles, window, selected, add_ref=None):
    s_ref, mt_ref = slot_refs[0:1], slot_refs[1:2]
    al_ref, p_ref = slot_refs[2:2 + N_SLOTS], slot_refs[2 + N_SLOTS:2 + 2 * N_SLOTS]
    g = pl.program_id(1)
    qt = pl.program_id(2)
    nq = GQA_REP * tq

    @pl.when(qt == 0)
    def _():
        for j in range(n_tbl_tiles):
            key = lax.broadcasted_iota(jnp.int32, (tq, tq), 0) + j * tq
            qry = lax.broadcasted_iota(jnp.int32, (tq, tq), 1) + (n_tbl_tiles - 1) * tq
            dist = qry - key
            for r in range(GQA_REP):
                val = _rel_bias_delta(dist, lambda b, r=r: rb_ref[b, g * GQA_REP + r])
                if window is not None:
                    val = jnp.where(dist >= window, NEG, val)
                tbl_ref[j * tq:(j + 1) * tq, r * tq:(r + 1) * tq] = val

    m_ref[...] = jnp.full((1, nq), NEG, F32)
    acc_ref[...] = jnp.zeros((V_ROWS, nq), F32)

    def serial(t, tbl_tile):
        k = k_ref[0, 0, pl.ds(pl.multiple_of(t * tq, tq), tq), :]
        s = jnp.dot(k, qa_ref[...], preferred_element_type=F32)
        if tbl_tile is not None:
            s = s + tbl_ref[tbl_tile * tq:(tbl_tile + 1) * tq, :]
        s_ref[0][...] = s
        mt_ref[0][...] = jnp.max(s, axis=0, keepdims=True)
        m_old = m_ref[...]
        m_new = jnp.maximum(m_old, mt_ref[0][...])
        m_ref[...] = m_new
        p_ref[0][...] = jnp.exp2(s_ref[0][...] - m_new).astype(BF16)
        acc_ref[...] = jnp.exp2(m_old - m_new) * acc_ref[...] + jnp.dot(vT_ref[0, 0, t], p_ref[0][...],
                                                                        preferred_element_type=F32)

    last = n_tbl_tiles - 1

    def stage_lagged(t, slot, tbl_tile, first=False):
        k = k_ref[0, 0, pl.ds(pl.multiple_of(t * tq, tq), tq), :]
        s = jnp.dot(k, qa_ref[...], preferred_element_type=F32)
        if tbl_tile is not None:
            s = s + tbl_ref[tbl_tile * tq:(tbl_tile + 1) * tq, :]
        m = s[0:1, :] if first else m_ref[...]
        mt = jnp.max(s, axis=0, keepdims=True)
        p_ref[slot][...] = jnp.exp2(s - m).astype(BF16)
        m_new = jnp.maximum(m, mt)
        m_ref[...] = m_new
        al_ref[slot][...] = jnp.exp2(m - m_new)
        lag_ref[...] = (mt - m) if first else jnp.maximum(lag_ref[...], mt - m)

    def stage_pv_lagged(t, slot):
        acc_ref[...] = (acc_ref[...] + jnp.dot(vT_ref[0, 0, t], p_ref[slot][...],
                                               preferred_element_type=F32)) * al_ref[slot][...]

    def lagged_step(t, slot, tbl_tile, prev=None):
        stage_lagged(t, slot, tbl_tile)
        stage_pv_lagged(t - 1 if prev is None else prev, 1 - slot)

    if selected:
        n = qt + 1

        @pl.when((n >= 4) & ((n & 1) == 0))
        def _():
            stage_lagged(0, 1, None, first=True)
            lagged_step(1, 0, None)

        @pl.when((n >= 4) & ((n & 1) == 1))
        def _():
            stage_lagged(0, 0, None, first=True)

        @pl.when(n >= 4)
        def _():
            t0 = 2 - (n & 1)

            pairs = (n - 2 - t0) >> 1

            def pair(t):
                lagged_step(t, 1, None)
                lagged_step(t + 1, 0, None)

            def body(j, carry):
                for i in range(4):
                    pair(t0 + 8 * j + 2 * i)
                return carry
            lax.fori_loop(0, pairs >> 2, body, 0)

            @pl.when((pairs & 2) == 2)
            def _():
                t = n - 2 - 2 * (pairs & 3)
                pair(t)
                pair(t + 2)

            @pl.when((pairs & 1) == 1)
            def _():
                pair(n - 4)

        @pl.when(n >= 4)
        def _():
            lagged_step(n - 2, 1, 0)
            lagged_step(n - 1, 0, 1)
            stage_pv_lagged(n - 1, 0)

            @pl.when(jnp.max(lag_ref[...]) > MAX_LAG)
            def _():
                m_ref[...] = jnp.full((1, nq), NEG, F32)
                acc_ref[...] = jnp.zeros((V_ROWS, nq), F32)

                def redo(t, carry):
                    serial(t, None)
                    return carry
                lax.fori_loop(0, n - 2, redo, 0)
                serial(n - 2, 0)
                serial(n - 1, 1)

        @pl.when(n == 3)
        def _():
            serial(0, None)

        @pl.when((n >= 2) & (n < 4))
        def _():
            serial(qt - 1, 0)

        @pl.when(n < 4)
        def _():
            serial(qt, 1)
    else:
        @pl.when(qt >= last)
        def _():
            stage_lagged(qt, 0, 2, first=True)
            lagged_step(qt - 1, 1, 1, prev=qt)
            lagged_step(qt - 2, 0, 0, prev=qt - 1)
            stage_pv_lagged(qt - 2, 0)

            @pl.when(jnp.max(lag_ref[...]) > MAX_LAG)
            def _():
                m_ref[...] = jnp.full((1, nq), NEG, F32)
                acc_ref[...] = jnp.zeros((V_ROWS, nq), F32)
                for j in range(n_tbl_tiles):
                    serial(qt - last + j, j)

        @pl.when(qt == 1)
        def _():
            serial(0, 1)

        @pl.when(qt < last)
        def _():
            serial(qt, 2)

    acc = acc_ref[...]
    o = acc[0:HEAD_DIM] / jnp.maximum(acc[HEAD_DIM:HEAD_DIM + 1], 1e-30)
    for r in range(GQA_REP):
        rows = slice(r * HEAD_DIM, (r + 1) * HEAD_DIM)
        val = o[:, r * tq:(r + 1) * tq] * gs_ref[0, 0, 0, r:r + 1, :]
        if add_ref is not None:
            val = val + add_ref[0, rows, :]
        o_ref[0, rows, :] = val.astype(o_ref.dtype)


SEL_TBL_TILES = 2
WIN_TBL_TILES = 3


def _branches_kernel(rb_ref, qT_ref, selb_ref, ksel_ref, vselT_ref, kwin_ref, vwinT_ref, gs_sel_ref, gs_win_ref,
                     o_ref, owin_ref, tbl_ref, qa_ref, *state, tq, nsel, kd):
    q = qT_ref[0]
    nq = GQA_REP * tq
    qa_ref[0:HEAD_DIM, :] = jnp.concatenate([q[r * HEAD_DIM:(r + 1) * HEAD_DIM] for r in range(GQA_REP)], axis=1)
    qa_ref[HEAD_DIM:HEAD_DIM + nsel, :] = jnp.concatenate([selb_ref[0, 0, 0]] * GQA_REP, axis=1)
    if kd > HEAD_DIM + nsel:
        qa_ref[HEAD_DIM + nsel:kd, :] = jnp.zeros((kd - HEAD_DIM - nsel, nq), BF16)

    win_tbl = tbl_ref.at[SEL_TBL_TILES * tq:(SEL_TBL_TILES + WIN_TBL_TILES) * tq]
    _flash_branch(rb_ref, kwin_ref, vwinT_ref, gs_win_ref, owin_ref, win_tbl, qa_ref.at[0:LANES], *state,
                  tq=tq, n_tbl_tiles=WIN_TBL_TILES, window=WINDOW, selected=False)
    _flash_branch(rb_ref, ksel_ref, vselT_ref, gs_sel_ref, o_ref, tbl_ref.at[0:SEL_TBL_TILES * tq], qa_ref, *state,
                  tq=tq, n_tbl_tiles=SEL_TBL_TILES, window=None, selected=True, add_ref=owin_ref)


def _out_kernel(oc_ref, osw_ref, za_ref, u_ref, cb_ref, zc_ref, hu_ref,
                x_ref, cw_ref, wo_ref, fw_ref, out_ref, *, tm):
    st = pl.program_id(1)
    f32 = lambda ref: ref[0].astype(F32)
    attn = (f32(oc_ref) + f32(osw_ref)).T * _silu(f32(za_ref))

    u = f32(u_ref)
    halo = jnp.where(st == 0, 0.0, f32(hu_ref))
    row = lax.broadcasted_iota(jnp.int32, u.shape, 0)
    h1, h2 = halo[HALO_ROWS - 1:HALO_ROWS], halo[HALO_ROWS - 2:HALO_ROWS - 1]
    u1 = jnp.where(row == 0, h1, pltpu.roll(u, 1, 0))
    u2 = jnp.where(row == 0, h2, jnp.where(row == 1, h1, pltpu.roll(u, 2, 0)))
    y = cw_ref[0:1] * u2 + cw_ref[1:2] * u1 + cw_ref[2:3] * u
    conv = f32(cb_ref) * y * _silu(f32(zc_ref))

    mixed = jnp.concatenate([attn, conv], axis=1).astype(BF16)
    y2 = x_ref[0] + jnp.dot(mixed, wo_ref[...], preferred_element_type=F32)
    ms = jnp.mean(y2 * y2, axis=-1, keepdims=True)
    out_ref[0] = y2 * lax.rsqrt(ms + EPS) * fw_ref[...]


def _overlap_t(nsel, nch):
    c = np.arange(nch)[None, :] * CMP_STRIDE
    n = np.arange(nsel)[:, None] * SEL_BLOCK
    return ((c < n + SEL_BLOCK) & (c + CMP_BLOCK > n)).astype(np.float32)


def _cparams(sem):
    return pltpu.CompilerParams(dimension_semantics=sem, vmem_limit_bytes=VMEM_LIMIT)


def kernel(x, norm_w, w_in, w_ck1, w_ck2, pe_k, w_cv1, w_cv2, pe_v, conv_w, w_out, rel_bias, final_norm_w):
    b, s, _ = x.shape
    assert norm_w.shape[0] == 1 and s % ROW_TILE == 0 and ROW_TILE % KEY_TILE == 0
    assert WINDOW == 2 * Q_TILE and KEY_TILE == Q_TILE
    tm = ROW_TILE
    nt = s // tm
    nkt = s // KEY_TILE
    kpt = tm // KEY_TILE
    nch = s // CMP_STRIDE
    nsel = s // SEL_BLOCK
    k_top = min(N_SEL, nsel)
    kd = -(-(HEAD_DIM + nsel) // LANES) * LANES
    G, R = N_KV_HEADS, GQA_REP

    w_in_t = jnp.swapaxes(w_in[0], 0, 1)
    assert w_in_t.shape[0] == _C_GATE + N_GATE
    eye = jnp.eye(G, dtype=F32)

    def expand1(w1):
        w1 = w1.astype(BF16).reshape(2, CMP_STRIDE, HEAD_DIM, CMP_HIDDEN)
        z = jnp.zeros_like(w1)
        rows = [jnp.concatenate([w1 if a == g else z for a in range(G)], axis=-1) for g in range(G)]
        return jnp.stack(rows, axis=2).reshape(2, CMP_STRIDE * D_KV, G * CMP_HIDDEN)

    def expand_pe(pe):
        pe = pe.reshape(2, CMP_STRIDE, 1, HEAD_DIM)
        return jnp.broadcast_to(pe, (2, CMP_STRIDE, G, HEAD_DIM)).reshape(2, 1, CMP_STRIDE * D_KV)

    def expand2(w2):
        return jnp.einsum('hd,ab->ahbd', w2, eye).reshape(G * CMP_HIDDEN, D_KV)

    w1e = jnp.stack([expand1(w_ck1[0]), expand1(w_cv1[0])]).astype(BF16)
    pee = jnp.stack([expand_pe(pe_k[0]), expand_pe(pe_v[0])])
    w2e = jnp.stack([expand2(w_ck2[0]), expand2(w_cv2[0])]).astype(BF16)
    rel_bias = rel_bias.astype(F32)
    smem = pl.BlockSpec(memory_space=pltpu.SMEM)

    n_rest = D_ATTN + 3 * D_CONV
    cw = CMP_STRIDE * D_KV
    qT, kvc, ksel, vselT, kwin, vwinT, gsT, rest = pl.pallas_call(
        partial(_proj_kernel, tm=tm, nsel=nsel, kd=kd),
        grid=(b, nt),
        in_specs=[pl.BlockSpec((1, tm, D_MODEL), lambda i, j: (i, j, 0)),
                  pl.BlockSpec((1, D_MODEL), lambda i, j: (0, 0)),
                  pl.BlockSpec((_C_GATE + N_GATE, D_MODEL), lambda i, j: (0, 0), pipeline_mode=pl.Buffered(1))],
        out_specs=[pl.BlockSpec((1, D_ATTN, tm), lambda i, j: (i, 0, j)),
                   pl.BlockSpec((2, 1, tm // CMP_STRIDE, cw), lambda i, j: (0, i, j, 0)),
                   pl.BlockSpec((1, G, tm, kd), lambda i, j: (i, 0, j, 0)),
                   pl.BlockSpec((1, G, kpt, V_ROWS, KEY_TILE), lambda i, j: (i, 0, j, 0, 0)),
                   pl.BlockSpec((1, G, tm, LANES), lambda i, j: (i, 0, j, 0)),
                   pl.BlockSpec((1, G, kpt, V_ROWS, KEY_TILE), lambda i, j: (i, 0, j, 0, 0)),
                   pl.BlockSpec((1, N_GATE, tm), lambda i, j: (i, 0, j)),
                   pl.BlockSpec((1, tm, n_rest), lambda i, j: (i, j, 0))],
        out_shape=[jax.ShapeDtypeStruct((b, D_ATTN, s), BF16),
                   jax.ShapeDtypeStruct((2, b, nch, cw), F32),
                   jax.ShapeDtypeStruct((b, G, s, kd), BF16),
                   jax.ShapeDtypeStruct((b, G, nkt, V_ROWS, KEY_TILE), BF16),
                   jax.ShapeDtypeStruct((b, G, s, LANES), BF16),
                   jax.ShapeDtypeStruct((b, G, nkt, V_ROWS, KEY_TILE), BF16),
                   jax.ShapeDtypeStruct((b, N_GATE, s), F32),
                   jax.ShapeDtypeStruct((b, s, n_rest), BF16)],
        scratch_shapes=[pltpu.VMEM((2, tm, D_KV), F32),
                        pltpu.VMEM((_C_END, D_MODEL), BF16)],
        compiler_params=_cparams(("arbitrary", "arbitrary")),
        name="in_proj",
    )(x, norm_w, w_in_t)
    gs = gsT.reshape(b, 3, G, R, s)

    lhs_rows = V_ROWS + nsel
    lhs_tail = jnp.asarray(np.concatenate([np.ones((V_ROWS - HEAD_DIM, nch), np.float32),
                                           _overlap_t(nsel, nch)], axis=0), BF16)
    kc_aug, c_lhs = pl.pallas_call(
        partial(_compress_kernel, nch=nch),
        grid=(2, b),
        in_specs=[pl.BlockSpec((1, 1, nch, cw), lambda xk, i: (xk, i, 0, 0)),
                  pl.BlockSpec((1, 2, 1, cw), lambda xk, i: (xk, 0, 0, 0)),
                  pl.BlockSpec((1, 2, cw, G * CMP_HIDDEN), lambda xk, i: (xk, 0, 0, 0)),
                  pl.BlockSpec((1, G * CMP_HIDDEN, D_KV), lambda xk, i: (xk, 0, 0)),
                  pl.BlockSpec((lhs_rows - HEAD_DIM, nch), lambda xk, i: (0, 0))],
        out_specs=[pl.BlockSpec((1, 1, G, nch, LANES), lambda xk, i: (xk, i, 0, 0, 0)),
                   pl.BlockSpec((1, 1, G, lhs_rows, nch), lambda xk, i: (xk, i, 0, 0, 0))],
        out_shape=[jax.ShapeDtypeStruct((2, b, G, nch, LANES), BF16),
                   jax.ShapeDtypeStruct((2, b, G, lhs_rows, nch), BF16)],
        compiler_params=_cparams(("parallel", "parallel")),
        name="compress",
    )(kvc, pee, w1e, w2e, lhs_tail)

    tq3 = Q_TILE_CMP
    cpq = tq3 // CMP_STRIDE
    assert nch // 8 <= LANES - HEAD_DIM and k_top > 3 and (nch // 2) % LANES == 0
    ocT, selb = pl.pallas_call(
        partial(_cmp_kernel, nch=nch, nsel=nsel, tq=tq3, k_top=k_top, nqt=s // tq3),
        grid=(b, G, s // tq3),
        in_specs=[smem,
                  pl.BlockSpec((1, R * HEAD_DIM, tq3), lambda i, g, t: (i, g, t)),
                  pl.BlockSpec((1, 1, 1, nch, LANES), lambda i, g, t: (0, i, g, 0, 0)),
                  pl.BlockSpec((1, 1, 1, lhs_rows, nch), lambda i, g, t: (1, i, g, 0, 0)),
                  pl.BlockSpec((1, 1, 1, R, tq3), lambda i, g, t: (i, 0, g, 0, t))],
        out_specs=[pl.BlockSpec((1, R * HEAD_DIM, tq3), lambda i, g, t: (i, g, t)),
                   pl.BlockSpec((1, 1, s // tq3, nsel, tq3), lambda i, g, t: (i, g, 0, 0, 0))],
        out_shape=[jax.ShapeDtypeStruct((b, D_ATTN, s), BF16),
                   jax.ShapeDtypeStruct((b, G, s // tq3, nsel, tq3), BF16)],
        scratch_shapes=[pltpu.VMEM((LANES, R * tq3), BF16),
                        pltpu.VMEM((cpq + 16, R * tq3), F32),
                        pltpu.VMEM((lhs_rows, R * tq3), F32),
                        pltpu.VMEM((nsel, tq3), F32),
                        pltpu.VMEM((1, tq3), F32)]
                       + [pltpu.VMEM((nch, tq3), F32)] * R,
        compiler_params=_cparams(("parallel", "parallel", "arbitrary")),
        name="cmp_select",
    )(rel_bias, qT, kc_aug, c_lhs, gs)

    tq = Q_TILE
    nq = R * tq

    o_spec = pl.BlockSpec((1, R * HEAD_DIM, tq), lambda i, g, t: (i, g, t))
    k_spec = lambda kdim: pl.BlockSpec((1, 1, s, kdim), lambda i, g, t: (i, g, 0, 0))
    v_spec = pl.BlockSpec((1, 1, nkt, V_ROWS, KEY_TILE), lambda i, g, t: (i, g, 0, 0, 0))
    gs_spec = lambda branch: pl.BlockSpec((1, 1, 1, R, tq), lambda i, g, t: (i, branch, g, 0, t))
    oswT = pl.pallas_call(
        partial(_branches_kernel, tq=tq, nsel=nsel, kd=kd),
        grid=(b, G, s // tq),
        in_specs=[smem, o_spec,
                  pl.BlockSpec((1, 1, 1, nsel, tq), lambda i, g, t: (i, g, t // (tq3 // tq), 0, t % (tq3 // tq))),
                  k_spec(kd), v_spec, k_spec(LANES), v_spec, gs_spec(1), gs_spec(2)],
        out_specs=o_spec,
        out_shape=jax.ShapeDtypeStruct((b, D_ATTN, s), BF16),
        scratch_shapes=[pltpu.VMEM((1, R * HEAD_DIM, tq), F32),
                        pltpu.VMEM(((SEL_TBL_TILES + WIN_TBL_TILES) * tq, nq), F32),
                        pltpu.VMEM((kd, nq), BF16),
                        pltpu.VMEM((1, nq), F32),
                        pltpu.VMEM((V_ROWS, nq), F32),
                        pltpu.VMEM((1, nq), F32)]
                       + [pltpu.VMEM((tq, nq), F32)]
                       + [pltpu.VMEM((1, nq), F32)]
                       + [pltpu.VMEM((1, nq), F32)] * N_SLOTS
                       + [pltpu.VMEM((tq, nq), BF16)] * N_SLOTS,
        compiler_params=_cparams(("parallel", "parallel", "arbitrary")),
        name="sel_win_attn",
    )(rel_bias, qT, selb, ksel, vselT, kwin, vwinT, gs, gs)

    hb = tm // HALO_ROWS
    col = lambda c: pl.BlockSpec((1, tm, D_CONV), lambda i, j, c=c: (i, j, c))
    halo = lambda c: pl.BlockSpec((1, HALO_ROWS, D_CONV), lambda i, j, c=c: (i, jnp.maximum(j * hb - 1, 0), c))
    oT = pl.BlockSpec((1, D_ATTN, tm), lambda i, j: (i, 0, j))
    out = pl.pallas_call(
        partial(_out_kernel, tm=tm),
        grid=(b, nt),
        in_specs=[oT, oT, col(0), col(1), col(2), col(3), halo(1),
                  pl.BlockSpec((1, tm, D_MODEL), lambda i, j: (i, j, 0)),
                  pl.BlockSpec((3, D_CONV), lambda i, j: (0, 0)),
                  pl.BlockSpec((D_MODEL, D_MODEL), lambda i, j: (0, 0)),
                  pl.BlockSpec((1, D_MODEL), lambda i, j: (0, 0))],
        out_specs=pl.BlockSpec((1, tm, D_MODEL), lambda i, j: (i, j, 0)),
        out_shape=jax.ShapeDtypeStruct((b, s, D_MODEL), F32),
        compiler_params=_cparams(("parallel", "parallel")),
        name="out_proj",
    )(ocT, oswT, rest, rest, rest, rest, rest, x, conv_w[0], w_out[0].astype(BF16),
      final_norm_w.reshape(1, D_MODEL))
    return out
```

```python
import math
from functools import partial

import numpy as np
import jax
import jax.numpy as jnp
from jax import lax
from jax.experimental import pallas as pl
from jax.experimental.pallas import tpu as pltpu

D_MODEL = 1024
N_HEADS = 8
HEAD_DIM = 64
N_KV_HEADS = 2
GQA_REP = N_HEADS // N_KV_HEADS
D_ATTN = N_HEADS * HEAD_DIM
D_KV = N_KV_HEADS * HEAD_DIM
D_CONV = D_MODEL - D_ATTN
CMP_BLOCK = 32
CMP_STRIDE = 16
CMP_HIDDEN = 2 * HEAD_DIM
SEL_BLOCK = 64
N_SEL = 16
WINDOW = 512
N_BUCKETS = 32
MAX_DISTANCE = 128
EPS = 1e-6
NEG = -1e30

LOG2E = 1.4426950408889634
LANES = 128
SUBLANES = 8
BAND_BACK = -(-(-(-(MAX_DISTANCE + CMP_BLOCK - 1) // CMP_STRIDE)) // SUBLANES) * SUBLANES
TAKEN = -2.0
NOT_CAUSAL = -1.0
N_FORCED = 3
V_ROWS = HEAD_DIM + 16
ROW_TILE = 512
KEY_TILE = 256
Q_TILE_CMP = 512
Q_TILE = 256
HALO_ROWS = 16
MAX_BLOCK_PAIRS = 8
MAX_LAG = 64.0
N_SLOTS = 2
VMEM_LIMIT = 48 * 1024 * 1024

F32 = jnp.float32
BF16 = jnp.bfloat16

_C_Q = 0
_C_CMP = D_ATTN
_C_SW = _C_CMP + 2 * D_KV
_C_REST = _C_SW + 4 * D_KV
_C_GATE = _C_REST + D_ATTN + 4 * D_CONV
_C_END = _C_GATE + LANES
N_GATE = 3 * N_HEADS


def _bucket_lower_bounds():
    max_exact = N_BUCKETS // 2
    d = np.arange(2 * MAX_DISTANCE, dtype=np.int32)
    nf = np.maximum(d, 1).astype(np.float32)
    large = max_exact + (np.log(nf / np.float32(max_exact)) / np.float32(math.log(MAX_DISTANCE / max_exact))
                         * np.float32(N_BUCKETS - max_exact)).astype(np.int32)
    bucket = np.where(d < max_exact, d, np.minimum(large, N_BUCKETS - 1))
    assert np.all(np.diff(bucket) >= 0) and np.all(bucket[MAX_DISTANCE:] == N_BUCKETS - 1)
    return [int(np.argmax(bucket >= b)) for b in range(N_BUCKETS)]


_BUCKET_LO = _bucket_lower_bounds()


def _rel_bias_delta(dist, bias_of_bucket):
    last = bias_of_bucket(N_BUCKETS - 1)
    val = jnp.full(dist.shape, (bias_of_bucket(0) - last) * LOG2E, F32)
    for b in range(1, N_BUCKETS - 1):
        val = jnp.where(dist >= _BUCKET_LO[b], (bias_of_bucket(b) - last) * LOG2E, val)
    val = jnp.where(dist >= _BUCKET_LO[N_BUCKETS - 1], 0.0, val)
    return jnp.where(dist < 0, NEG, val)


def _silu(z):
    return z * (1.0 / (1.0 + jnp.exp2(z * -LOG2E)))


def _proj_kernel(x_ref, nw_ref, wT_ref, qT_ref, kvc_ref, ksel_ref, vselT_ref, kwin_ref, vwinT_ref,
                 gsT_ref, rest_ref, kv_scr, w_scr, *, tm, nsel, kd):
    st = pl.program_id(1)

    @pl.when((pl.program_id(0) == 0) & (st == 0))
    def _():
        c0 = _C_REST
        chunk = KEY_TILE
        for dst, src, rows in ((0, 0, c0), (c0, c0 + N_GATE, _C_GATE - c0)):
            for r in range(0, rows, chunk):
                w_scr[dst + r:dst + r + chunk, :] = wT_ref[src + r:src + r + chunk, :].astype(BF16)
        w_scr[_C_GATE:_C_END, :] = jnp.concatenate(
            [wT_ref[c0:c0 + N_GATE, :], jnp.zeros((LANES - N_GATE, D_MODEL), F32)], axis=0).astype(BF16)

    x = x_ref[0]
    ms = jnp.mean(x * x, axis=-1, keepdims=True)
    h = (x * lax.rsqrt(ms + EPS) * nw_ref[...]).astype(BF16)

    def mm(lo, hi):
        return lax.dot_general(h, w_scr[lo:hi, :], (((1,), (1,)), ((), ())), preferred_element_type=F32)

    q = mm(_C_Q, _C_CMP) * (HEAD_DIM ** -0.5 * LOG2E)
    qT_ref[0] = q.T.astype(BF16)

    kvc = mm(_C_CMP, _C_SW)
    for xk in range(2):
        kv_scr[xk] = kvc[:, xk * D_KV:(xk + 1) * D_KV]
        for j in range(CMP_STRIDE):
            kvc_ref[xk, 0, :, j * D_KV:(j + 1) * D_KV] = kv_scr[xk, pl.ds(j, tm // CMP_STRIDE, stride=CMP_STRIDE), :]

    a = mm(_C_SW, _C_REST)
    pos = st * tm + lax.broadcasted_iota(jnp.int32, (tm, nsel), 0)
    col = lax.broadcasted_iota(jnp.int32, (tm, nsel), 1)
    onehot = jnp.where((pos // SEL_BLOCK) == col, 1.0, 0.0).astype(BF16)
    ones = jnp.ones((V_ROWS - HEAD_DIM, tm), BF16)
    zeros_k = jnp.zeros((tm, LANES - HEAD_DIM), BF16)
    vsT = a[:, D_KV:2 * D_KV].T
    vwT = a[:, 3 * D_KV:4 * D_KV].T
    for g in range(N_KV_HEADS):
        lo, hi = g * HEAD_DIM, (g + 1) * HEAD_DIM
        parts = [a[:, lo:hi].astype(BF16), onehot]
        if kd > HEAD_DIM + nsel:
            parts.append(jnp.zeros((tm, kd - HEAD_DIM - nsel), BF16))
        ksel_ref[0, g] = jnp.concatenate(parts, axis=1)
        kwin_ref[0, g] = jnp.concatenate([a[:, 2 * D_KV + lo:2 * D_KV + hi].astype(BF16), zeros_k], axis=1)
        vs = jnp.concatenate([vsT[lo:hi].astype(BF16), ones], axis=0)
        vw = jnp.concatenate([vwT[lo:hi].astype(BF16), ones], axis=0)
        for kt in range(tm // KEY_TILE):
            vselT_ref[0, g, kt] = vs[:, kt * KEY_TILE:(kt + 1) * KEY_TILE]
            vwinT_ref[0, g, kt] = vw[:, kt * KEY_TILE:(kt + 1) * KEY_TILE]

    r = mm(_C_REST, _C_GATE)
    z_attn, conv_h, conv_b, conv_c, z_conv = (r[:, i * D_CONV:(i + 1) * D_CONV] for i in range(5))
    rest_ref[0] = jnp.concatenate([z_attn, conv_c * conv_h, conv_b, z_conv], axis=1).astype(BF16)
    gates = jax.nn.sigmoid(mm(_C_GATE, _C_END))
    gsT_ref[0] = gates.T[:N_GATE]


def _compress_kernel(c_ref, pe_ref, w1_ref, w2_ref, lhs_tail_ref, aug_ref, t_ref, *, nch):
    c = c_ref[0, 0]
    a0 = jnp.dot((c + pe_ref[0, 0]).astype(BF16), w1_ref[0, 0], preferred_element_type=F32)
    a1 = jnp.dot((c + pe_ref[0, 1]).astype(BF16), w1_ref[0, 1], preferred_element_type=F32)
    hid = a0 + pltpu.roll(a1, nch - 1, 0)
    out = jnp.dot(_silu(hid).astype(BF16), w2_ref[0], preferred_element_type=F32)
    grp = lax.broadcasted_iota(jnp.int32, (nch, LANES - HEAD_DIM), 0) // SUBLANES
    col = lax.broadcasted_iota(jnp.int32, (nch, LANES - HEAD_DIM), 1)
    onehot = jnp.where(grp == col, 1.0, 0.0).astype(BF16)
    out_t = out.T.astype(BF16)
    for g in range(N_KV_HEADS):
        lo, hi = g * HEAD_DIM, (g + 1) * HEAD_DIM
        aug_ref[0, 0, g] = jnp.concatenate([out[:, lo:hi].astype(BF16), onehot], axis=1)
        t_ref[0, 0, g] = jnp.concatenate([out_t[lo:hi], lhs_tail_ref[...]], axis=0)


def _cmp_kernel(rb_ref, qT_ref, kc_ref, lhs_ref, gs_ref, ocT_ref, selb_ref,
                qa_ref, band_ref, acc_ref, score_ref, ntk_ref, s0_ref, s1_ref, s2_ref, s3_ref,
                *, nch, nsel, tq, k_top, nqt):
    g = pl.program_id(1)
    qt = pl.program_id(2)
    s_refs = (s0_ref, s1_ref, s2_ref, s3_ref)
    cpq = tq // CMP_STRIDE
    win = cpq + BAND_BACK
    lim = cpq * (qt + 1)
    nq = GQA_REP * tq

    @pl.when(qt <= 1)
    def _():
        row = lax.broadcasted_iota(jnp.int32, (win, tq), 0) + jnp.maximum(cpq * qt - BAND_BACK, 0)
        lane = lax.broadcasted_iota(jnp.int32, (win, tq), 1)
        dist = (qt * tq + lane) - (row * CMP_STRIDE + CMP_BLOCK - 1)
        for r in range(GQA_REP):
            band_ref[:, r * tq:(r + 1) * tq] = _rel_bias_delta(dist, lambda b, r=r: rb_ref[b, g * GQA_REP + r])

    q = qT_ref[0]
    qa_ref[0:HEAD_DIM, :] = jnp.concatenate([q[r * HEAD_DIM:(r + 1) * HEAD_DIM] for r in range(GQA_REP)], axis=1)
    grp = lax.broadcasted_iota(jnp.int32, (LANES - HEAD_DIM, nq), 0)
    qa_ref[HEAD_DIM:LANES, :] = jnp.where(grp * SUBLANES >= lim, NEG, 0.0).astype(BF16)

    def run(rows):
        start = pl.multiple_of(jnp.clip(cpq * qt - BAND_BACK, 0, rows - win), SUBLANES)
        k = kc_ref[0, 0, 0, 0:rows, :]
        lhs = lhs_ref[0, 0, 0, :, 0:rows]
        cols = [slice(r * tq, (r + 1) * tq) for r in range(GQA_REP)]

        def qk(r):
            s_refs[r][0:rows, :] = jnp.dot(k, qa_ref[:, cols[r]], preferred_element_type=F32)

        ahead = 3
        for r in range(ahead):
            qk(r)
        for r in range(GQA_REP):
            if r + ahead < GQA_REP:
                qk(r + ahead)
            sr = s_refs[r]
            sr[pl.ds(start, win), :] = sr[pl.ds(start, win), :] + band_ref[:, cols[r]]
            s = sr[0:rows, :]
            p = jnp.exp2(s - jnp.max(s, axis=0, keepdims=True)).astype(BF16)
            acc_ref[:, cols[r]] = jnp.dot(lhs, p, preferred_element_type=F32)

    n = lax.broadcasted_iota(jnp.int32, (nsel, tq), 0)
    lane = lax.broadcasted_iota(jnp.int32, (nsel, tq), 1)

    def forced_and_candidate(tile):
        blk = (tile * tq + lane) // SEL_BLOCK
        forced = (n == 0) | (n == blk) | (n == blk - 1)
        return forced, (n <= blk) & jnp.logical_not(forced), n <= blk

    def select(tile):
        score = score_ref[...]
        for _ in range(k_top - N_FORCED):
            score = jnp.where(score == jnp.max(score, axis=0, keepdims=True), TAKEN, score)
        taken = score == TAKEN
        selb_ref[0, 0, jnp.maximum(tile, 0)] = jnp.where(taken, 0.0, NEG).astype(BF16)
        ntk_ref[...] = jnp.sum(jnp.where(taken & forced_and_candidate(tile)[1], 1.0, 0.0), axis=0, keepdims=True)

    def select_exact_if_tied(tile):
        @pl.when(jnp.max(ntk_ref[...]) > k_top - N_FORCED)
        def _():
            def pick_one(_, score):
                mx = jnp.max(score, axis=0, keepdims=True)
                idx = jnp.min(jnp.where(score == mx, n, nsel), axis=0, keepdims=True)
                return jnp.where(n == idx, TAKEN, score)

            taken = lax.fori_loop(0, k_top - N_FORCED, pick_one, score_ref[...]) == TAKEN
            selb_ref[0, 0, jnp.maximum(tile, 0)] = jnp.where(taken, 0.0, NEG).astype(BF16)

    @pl.when(qt == 0)
    def _():
        score_ref[...] = jnp.full((nsel, tq), TAKEN, F32)

    half = nch // 2

    @pl.when(lim <= half)
    def _():
        run(half)
        select(qt - 1)

    @pl.when(lim > half)
    def _():
        run(nch)
        select(qt - 1)

    select_exact_if_tied(qt - 1)

    l = acc_ref[HEAD_DIM:HEAD_DIM + 1, :]
    t4 = qt * tq + (lax.broadcasted_iota(jnp.int32, l.shape, 1) & (tq - 1))
    inv = jnp.where(t4 >= CMP_BLOCK - 1, 1.0 / jnp.maximum(l, 1e-30), 0.0)
    imp = None
    for r in range(GQA_REP):
        cs = slice(r * tq, (r + 1) * tq)
        ocT_ref[0, r * HEAD_DIM:(r + 1) * HEAD_DIM, :] = (acc_ref[0:HEAD_DIM, cs] * inv[:, cs]
                                                          * gs_ref[0, 0, 0, r:r + 1, :]).astype(BF16)
        part = acc_ref[V_ROWS:V_ROWS + nsel, cs] * inv[:, cs]
        imp = part if imp is None else imp + part

    forced, _, causal = forced_and_candidate(qt)
    score_ref[...] = jnp.where(forced, TAKEN, jnp.where(causal, imp, NOT_CAUSAL))

    @pl.when(qt == nqt - 1)
    def _():
        select(qt)
        select_exact_if_tied(qt)


def _flash_branch(rb_ref, k_ref, vT_ref, gs_ref, o_ref, tbl_ref, qa_ref, m_ref, acc_ref, lag_ref, *slot_refs,
                  tq, n_tbl_tiles, window, selected, add_ref=None):
    s_ref, mt_ref = slot_refs[0:1], slot_refs[1:2]
    al_ref, p_ref = slot_refs[2:2 + N_SLOTS], slot_refs[2 + N_SLOTS:2 + 2 * N_SLOTS]
    g = pl.program_id(1)
    qt = pl.program_id(2)
    nq = GQA_REP * tq

    @pl.when(qt == 0)
    def _():
        for j in range(n_tbl_tiles):
            key = lax.broadcasted_iota(jnp.int32, (tq, tq), 0) + j * tq
            qry = lax.broadcasted_iota(jnp.int32, (tq, tq), 1) + (n_tbl_tiles - 1) * tq
            dist = qry - key
            for r in range(GQA_REP):
                val = _rel_bias_delta(dist, lambda b, r=r: rb_ref[b, g * GQA_REP + r])
                if window is not None:
                    val = jnp.where(dist >= window, NEG, val)
                tbl_ref[j * tq:(j + 1) * tq, r * tq:(r + 1) * tq] = val

    m_ref[...] = jnp.full((1, nq), NEG, F32)
    acc_ref[...] = jnp.zeros((V_ROWS, nq), F32)

    def serial(t, tbl_tile):
        k = k_ref[0, 0, pl.ds(pl.multiple_of(t * tq, tq), tq), :]
        s = jnp.dot(k, qa_ref[...], preferred_element_type=F32)
        if tbl_tile is not None:
            s = s + tbl_ref[tbl_tile * tq:(tbl_tile + 1) * tq, :]
        s_ref[0][...] = s
        mt_ref[0][...] = jnp.max(s, axis=0, keepdims=True)
        m_old = m_ref[...]
        m_new = jnp.maximum(m_old, mt_ref[0][...])
        m_ref[...] = m_new
        p_ref[0][...] = jnp.exp2(s_ref[0][...] - m_new).astype(BF16)
        acc_ref[...] = jnp.exp2(m_old - m_new) * acc_ref[...] + jnp.dot(vT_ref[0, 0, t], p_ref[0][...],
                                                                        preferred_element_type=F32)

    last = n_tbl_tiles - 1

    def stage_lagged(t, slot, tbl_tile, first=False):
        k = k_ref[0, 0, pl.ds(pl.multiple_of(t * tq, tq), tq), :]
        s = jnp.dot(k, qa_ref[...], preferred_element_type=F32)
        if tbl_tile is not None:
            s = s + tbl_ref[tbl_tile * tq:(tbl_tile + 1) * tq, :]
        m = s[0:1, :] if first else m_ref[...]
        mt = jnp.max(s, axis=0, keepdims=True)
        p_ref[slot][...] = jnp.exp2(s - m).astype(BF16)
        m_new = jnp.maximum(m, mt)
        m_ref[...] = m_new
        al_ref[slot][...] = jnp.exp2(m - m_new)
        lag_ref[...] = (mt - m) if first else jnp.maximum(lag_ref[...], mt - m)

    def stage_pv_lagged(t, slot):
        acc_ref[...] = (acc_ref[...] + jnp.dot(vT_ref[0, 0, t], p_ref[slot][...],
                                               preferred_element_type=F32)) * al_ref[slot][...]

    def lagged_step(t, slot, tbl_tile, prev=None):
        stage_lagged(t, slot, tbl_tile)
        stage_pv_lagged(t - 1 if prev is None else prev, 1 - slot)

    if selected:
        n = qt + 1

        @pl.when((n >= 4) & ((n & 1) == 0))
        def _():
            stage_lagged(0, 1, None, first=True)
            lagged_step(1, 0, None)

        @pl.when((n >= 4) & ((n & 1) == 1))
        def _():
            stage_lagged(0, 0, None, first=True)

        @pl.when(n >= 4)
        def _():
            t0 = 2 - (n & 1)

            pairs = (n - 2 - t0) >> 1

            def pair(t):
                lagged_step(t, 1, None)
                lagged_step(t + 1, 0, None)

            big = MAX_BLOCK_PAIRS

            def body(j, carry):
                for i in range(big):
                    pair(t0 + 2 * (big * j + i))
                return carry
            lax.fori_loop(0, pairs >> (big.bit_length() - 1), body, 0)

            bp = big // 2
            while bp >= 1:
                @pl.when((pairs & bp) == bp)
                def _(bp=bp):
                    t = n - 2 - 2 * (pairs & (2 * bp - 1))
                    for i in range(bp):
                        pair(t + 2 * i)
                bp //= 2

        @pl.when(n >= 4)
        def _():
            lagged_step(n - 2, 1, 0)
            lagged_step(n - 1, 0, 1)
            stage_pv_lagged(n - 1, 0)

            @pl.when(jnp.max(lag_ref[...]) > MAX_LAG)
            def _():
                m_ref[...] = jnp.full((1, nq), NEG, F32)
                acc_ref[...] = jnp.zeros((V_ROWS, nq), F32)

                def redo(t, carry):
                    serial(t, None)
                    return carry
                lax.fori_loop(0, n - 2, redo, 0)
                serial(n - 2, 0)
                serial(n - 1, 1)

        @pl.when(n == 3)
        def _():
            serial(0, None)

        @pl.when((n >= 2) & (n < 4))
        def _():
            serial(qt - 1, 0)

        @pl.when(n < 4)
        def _():
            serial(qt, 1)
    else:
        @pl.when(qt >= last)
        def _():
            stage_lagged(qt, 0, 2, first=True)
            lagged_step(qt - 1, 1, 1, prev=qt)
            lagged_step(qt - 2, 0, 0, prev=qt - 1)
            stage_pv_lagged(qt - 2, 0)

            @pl.when(jnp.max(lag_ref[...]) > MAX_LAG)
            def _():
                m_ref[...] = jnp.full((1, nq), NEG, F32)
                acc_ref[...] = jnp.zeros((V_ROWS, nq), F32)
                for j in range(n_tbl_tiles):
                    serial(qt - last + j, j)

        @pl.when(qt == 1)
        def _():
            serial(0, 1)

        @pl.when(qt < last)
        def _():
            serial(qt, 2)

    acc = acc_ref[...]
    o = acc[0:HEAD_DIM] / jnp.maximum(acc[HEAD_DIM:HEAD_DIM + 1], 1e-30)
    for r in range(GQA_REP):
        rows = slice(r * HEAD_DIM, (r + 1) * HEAD_DIM)
        val = o[:, r * tq:(r + 1) * tq] * gs_ref[0, 0, 0, r:r + 1, :]
        if add_ref is not None:
            val = val + add_ref[0, rows, :]
        o_ref[0, rows, :] = val.astype(o_ref.dtype)


SEL_TBL_TILES = 2
WIN_TBL_TILES = 3


def _branches_kernel(rb_ref, qT_ref, selb_ref, ksel_ref, vselT_ref, kwin_ref, vwinT_ref, gs_sel_ref, gs_win_ref,
                     o_ref, owin_ref, tbl_ref, qa_ref, *state, tq, nsel, kd):
    q = qT_ref[0]
    nq = GQA_REP * tq
    qa_ref[0:HEAD_DIM, :] = jnp.concatenate([q[r * HEAD_DIM:(r + 1) * HEAD_DIM] for r in range(GQA_REP)], axis=1)
    qa_ref[HEAD_DIM:HEAD_DIM + nsel, :] = jnp.concatenate([selb_ref[0, 0, 0]] * GQA_REP, axis=1)
    if kd > HEAD_DIM + nsel:
        qa_ref[HEAD_DIM + nsel:kd, :] = jnp.zeros((kd - HEAD_DIM - nsel, nq), BF16)

    win_tbl = tbl_ref.at[SEL_TBL_TILES * tq:(SEL_TBL_TILES + WIN_TBL_TILES) * tq]
    _flash_branch(rb_ref, kwin_ref, vwinT_ref, gs_win_ref, owin_ref, win_tbl, qa_ref.at[0:LANES], *state,
                  tq=tq, n_tbl_tiles=WIN_TBL_TILES, window=WINDOW, selected=False)
    _flash_branch(rb_ref, ksel_ref, vselT_ref, gs_sel_ref, o_ref, tbl_ref.at[0:SEL_TBL_TILES * tq], qa_ref, *state,
                  tq=tq, n_tbl_tiles=SEL_TBL_TILES, window=None, selected=True, add_ref=owin_ref)


def _out_kernel(oc_ref, osw_ref, za_ref, u_ref, cb_ref, zc_ref, hu_ref,
                x_ref, cw_ref, wo_ref, fw_ref, out_ref, *, tm):
    st = pl.program_id(1)
    f32 = lambda ref: ref[0].astype(F32)
    attn = (f32(oc_ref) + f32(osw_ref)).T * _silu(f32(za_ref))

    u = f32(u_ref)
    halo = jnp.where(st == 0, 0.0, f32(hu_ref))
    row = lax.broadcasted_iota(jnp.int32, u.shape, 0)
    h1, h2 = halo[HALO_ROWS - 1:HALO_ROWS], halo[HALO_ROWS - 2:HALO_ROWS - 1]
    u1 = jnp.where(row == 0, h1, pltpu.roll(u, 1, 0))
    u2 = jnp.where(row == 0, h2, jnp.where(row == 1, h1, pltpu.roll(u, 2, 0)))
    y = cw_ref[0:1] * u2 + cw_ref[1:2] * u1 + cw_ref[2:3] * u
    conv = f32(cb_ref) * y * _silu(f32(zc_ref))

    mixed = jnp.concatenate([attn, conv], axis=1).astype(BF16)
    y2 = x_ref[0] + jnp.dot(mixed, wo_ref[...], preferred_element_type=F32)
    ms = jnp.mean(y2 * y2, axis=-1, keepdims=True)
    out_ref[0] = y2 * lax.rsqrt(ms + EPS) * fw_ref[...]


def _overlap_t(nsel, nch):
    c = np.arange(nch)[None, :] * CMP_STRIDE
    n = np.arange(nsel)[:, None] * SEL_BLOCK
    return ((c < n + SEL_BLOCK) & (c + CMP_BLOCK > n)).astype(np.float32)


def _cparams(sem):
    return pltpu.CompilerParams(dimension_semantics=sem, vmem_limit_bytes=VMEM_LIMIT)


def kernel(x, norm_w, w_in, w_ck1, w_ck2, pe_k, w_cv1, w_cv2, pe_v, conv_w, w_out, rel_bias, final_norm_w):
    b, s, _ = x.shape
    assert norm_w.shape[0] == 1 and s % ROW_TILE == 0 and ROW_TILE % KEY_TILE == 0
    assert WINDOW == 2 * Q_TILE and KEY_TILE == Q_TILE
    tm = ROW_TILE
    nt = s // tm
    nkt = s // KEY_TILE
    kpt = tm // KEY_TILE
    nch = s // CMP_STRIDE
    nsel = s // SEL_BLOCK
    k_top = min(N_SEL, nsel)
    kd = -(-(HEAD_DIM + nsel) // LANES) * LANES
    G, R = N_KV_HEADS, GQA_REP

    w_in_t = jnp.swapaxes(w_in[0], 0, 1)
    assert w_in_t.shape[0] == _C_GATE + N_GATE
    eye = jnp.eye(G, dtype=F32)

    def expand1(w1):
        w1 = w1.astype(BF16).reshape(2, CMP_STRIDE, HEAD_DIM, CMP_HIDDEN)
        z = jnp.zeros_like(w1)
        rows = [jnp.concatenate([w1 if a == g else z for a in range(G)], axis=-1) for g in range(G)]
        return jnp.stack(rows, axis=2).reshape(2, CMP_STRIDE * D_KV, G * CMP_HIDDEN)

    def expand_pe(pe):
        pe = pe.reshape(2, CMP_STRIDE, 1, HEAD_DIM)
        return jnp.broadcast_to(pe, (2, CMP_STRIDE, G, HEAD_DIM)).reshape(2, 1, CMP_STRIDE * D_KV)

    def expand2(w2):
        return jnp.einsum('hd,ab->ahbd', w2, eye).reshape(G * CMP_HIDDEN, D_KV)

    w1e = jnp.stack([expand1(w_ck1[0]), expand1(w_cv1[0])]).astype(BF16)
    pee = jnp.stack([expand_pe(pe_k[0]), expand_pe(pe_v[0])])
    w2e = jnp.stack([expand2(w_ck2[0]), expand2(w_cv2[0])]).astype(BF16)
    rel_bias = rel_bias.astype(F32)
    smem = pl.BlockSpec(memory_space=pltpu.SMEM)

    n_rest = D_ATTN + 3 * D_CONV
    cw = CMP_STRIDE * D_KV
    qT, kvc, ksel, vselT, kwin, vwinT, gsT, rest = pl.pallas_call(
        partial(_proj_kernel, tm=tm, nsel=nsel, kd=kd),
        grid=(b, nt),
        in_specs=[pl.BlockSpec((1, tm, D_MODEL), lambda i, j: (i, j, 0)),
                  pl.BlockSpec((1, D_MODEL), lambda i, j: (0, 0)),
                  pl.BlockSpec((_C_GATE + N_GATE, D_MODEL), lambda i, j: (0, 0), pipeline_mode=pl.Buffered(1))],
        out_specs=[pl.BlockSpec((1, D_ATTN, tm), lambda i, j: (i, 0, j)),
                   pl.BlockSpec((2, 1, tm // CMP_STRIDE, cw), lambda i, j: (0, i, j, 0)),
                   pl.BlockSpec((1, G, tm, kd), lambda i, j: (i, 0, j, 0)),
                   pl.BlockSpec((1, G, kpt, V_ROWS, KEY_TILE), lambda i, j: (i, 0, j, 0, 0)),
                   pl.BlockSpec((1, G, tm, LANES), lambda i, j: (i, 0, j, 0)),
                   pl.BlockSpec((1, G, kpt, V_ROWS, KEY_TILE), lambda i, j: (i, 0, j, 0, 0)),
                   pl.BlockSpec((1, N_GATE, tm), lambda i, j: (i, 0, j)),
                   pl.BlockSpec((1, tm, n_rest), lambda i, j: (i, j, 0))],
        out_shape=[jax.ShapeDtypeStruct((b, D_ATTN, s), BF16),
                   jax.ShapeDtypeStruct((2, b, nch, cw), F32),
                   jax.ShapeDtypeStruct((b, G, s, kd), BF16),
                   jax.ShapeDtypeStruct((b, G, nkt, V_ROWS, KEY_TILE), BF16),
                   jax.ShapeDtypeStruct((b, G, s, LANES), BF16),
                   jax.ShapeDtypeStruct((b, G, nkt, V_ROWS, KEY_TILE), BF16),
                   jax.ShapeDtypeStruct((b, N_GATE, s), F32),
                   jax.ShapeDtypeStruct((b, s, n_rest), BF16)],
        scratch_shapes=[pltpu.VMEM((2, tm, D_KV), F32),
                        pltpu.VMEM((_C_END, D_MODEL), BF16)],
        compiler_params=_cparams(("arbitrary", "arbitrary")),
        name="in_proj",
    )(x, norm_w, w_in_t)
    gs = gsT.reshape(b, 3, G, R, s)

    lhs_rows = V_ROWS + nsel
    lhs_tail = jnp.asarray(np.concatenate([np.ones((V_ROWS - HEAD_DIM, nch), np.float32),
                                           _overlap_t(nsel, nch)], axis=0), BF16)
    kc_aug, c_lhs = pl.pallas_call(
        partial(_compress_kernel, nch=nch),
        grid=(2, b),
        in_specs=[pl.BlockSpec((1, 1, nch, cw), lambda xk, i: (xk, i, 0, 0)),
                  pl.BlockSpec((1, 2, 1, cw), lambda xk, i: (xk, 0, 0, 0)),
                  pl.BlockSpec((1, 2, cw, G * CMP_HIDDEN), lambda xk, i: (xk, 0, 0, 0)),
                  pl.BlockSpec((1, G * CMP_HIDDEN, D_KV), lambda xk, i: (xk, 0, 0)),
                  pl.BlockSpec((lhs_rows - HEAD_DIM, nch), lambda xk, i: (0, 0))],
        out_specs=[pl.BlockSpec((1, 1, G, nch, LANES), lambda xk, i: (xk, i, 0, 0, 0)),
                   pl.BlockSpec((1, 1, G, lhs_rows, nch), lambda xk, i: (xk, i, 0, 0, 0))],
        out_shape=[jax.ShapeDtypeStruct((2, b, G, nch, LANES), BF16),
                   jax.ShapeDtypeStruct((2, b, G, lhs_rows, nch), BF16)],
        compiler_params=_cparams(("parallel", "parallel")),
        name="compress",
    )(kvc, pee, w1e, w2e, lhs_tail)

    tq3 = Q_TILE_CMP
    cpq = tq3 // CMP_STRIDE
    assert nch // SUBLANES <= LANES - HEAD_DIM and k_top > N_FORCED and (nch // 2) % LANES == 0
    ocT, selb = pl.pallas_call(
        partial(_cmp_kernel, nch=nch, nsel=nsel, tq=tq3, k_top=k_top, nqt=s // tq3),
        grid=(b, G, s // tq3),
        in_specs=[smem,
                  pl.BlockSpec((1, R * HEAD_DIM, tq3), lambda i, g, t: (i, g, t)),
                  pl.BlockSpec((1, 1, 1, nch, LANES), lambda i, g, t: (0, i, g, 0, 0)),
                  pl.BlockSpec((1, 1, 1, lhs_rows, nch), lambda i, g, t: (1, i, g, 0, 0)),
                  pl.BlockSpec((1, 1, 1, R, tq3), lambda i, g, t: (i, 0, g, 0, t))],
        out_specs=[pl.BlockSpec((1, R * HEAD_DIM, tq3), lambda i, g, t: (i, g, t)),
                   pl.BlockSpec((1, 1, s // tq3, nsel, tq3), lambda i, g, t: (i, g, 0, 0, 0))],
        out_shape=[jax.ShapeDtypeStruct((b, D_ATTN, s), BF16),
                   jax.ShapeDtypeStruct((b, G, s // tq3, nsel, tq3), BF16)],
        scratch_shapes=[pltpu.VMEM((LANES, R * tq3), BF16),
                        pltpu.VMEM((cpq + BAND_BACK, R * tq3), F32),
                        pltpu.VMEM((lhs_rows, R * tq3), F32),
                        pltpu.VMEM((nsel, tq3), F32),
                        pltpu.VMEM((1, tq3), F32)]
                       + [pltpu.VMEM((nch, tq3), F32)] * R,
        compiler_params=_cparams(("parallel", "parallel", "arbitrary")),
        name="cmp_select",
    )(rel_bias, qT, kc_aug, c_lhs, gs)

    tq = Q_TILE
    nq = R * tq

    o_spec = pl.BlockSpec((1, R * HEAD_DIM, tq), lambda i, g, t: (i, g, t))
    k_spec = lambda kdim: pl.BlockSpec((1, 1, s, kdim), lambda i, g, t: (i, g, 0, 0))
    v_spec = pl.BlockSpec((1, 1, nkt, V_ROWS, KEY_TILE), lambda i, g, t: (i, g, 0, 0, 0))
    gs_spec = lambda branch: pl.BlockSpec((1, 1, 1, R, tq), lambda i, g, t: (i, branch, g, 0, t))
    oswT = pl.pallas_call(
        partial(_branches_kernel, tq=tq, nsel=nsel, kd=kd),
        grid=(b, G, s // tq),
        in_specs=[smem, o_spec,
                  pl.BlockSpec((1, 1, 1, nsel, tq), lambda i, g, t: (i, g, t // (tq3 // tq), 0, t % (tq3 // tq))),
                  k_spec(kd), v_spec, k_spec(LANES), v_spec, gs_spec(1), gs_spec(2)],
        out_specs=o_spec,
        out_shape=jax.ShapeDtypeStruct((b, D_ATTN, s), BF16),
        scratch_shapes=[pltpu.VMEM((1, R * HEAD_DIM, tq), F32),
                        pltpu.VMEM(((SEL_TBL_TILES + WIN_TBL_TILES) * tq, nq), F32),
                        pltpu.VMEM((kd, nq), BF16),
                        pltpu.VMEM((1, nq), F32),
                        pltpu.VMEM((V_ROWS, nq), F32),
                        pltpu.VMEM((1, nq), F32)]
                       + [pltpu.VMEM((tq, nq), F32)]
                       + [pltpu.VMEM((1, nq), F32)]
                       + [pltpu.VMEM((1, nq), F32)] * N_SLOTS
                       + [pltpu.VMEM((tq, nq), BF16)] * N_SLOTS,
        compiler_params=_cparams(("parallel", "parallel", "arbitrary")),
        name="sel_win_attn",
    )(rel_bias, qT, selb, ksel, vselT, kwin, vwinT, gs, gs)

    hb = tm // HALO_ROWS
    col = lambda c: pl.BlockSpec((1, tm, D_CONV), lambda i, j, c=c: (i, j, c))
    halo = lambda c: pl.BlockSpec((1, HALO_ROWS, D_CONV), lambda i, j, c=c: (i, jnp.maximum(j * hb - 1, 0), c))
    oT = pl.BlockSpec((1, D_ATTN, tm), lambda i, j: (i, 0, j))
    out = pl.pallas_call(
        partial(_out_kernel, tm=tm),
        grid=(b, nt),
        in_specs=[oT, oT, col(0), col(1), col(2), col(3), halo(1),
                  pl.BlockSpec((1, tm, D_MODEL), lambda i, j: (i, j, 0)),
                  pl.BlockSpec((3, D_CONV), lambda i, j: (0, 0)),
                  pl.BlockSpec((D_MODEL, D_MODEL), lambda i, j: (0, 0)),
                  pl.BlockSpec((1, D_MODEL), lambda i, j: (0, 0))],
        out_specs=pl.BlockSpec((1, tm, D_MODEL), lambda i, j: (i, j, 0)),
        out_shape=jax.ShapeDtypeStruct((b, s, D_MODEL), F32),
        compiler_params=_cparams(("parallel", "parallel")),
        name="out_proj",
    )(ocT, oswT, rest, rest, rest, rest, rest, x, conv_w[0], w_out[0].astype(BF16),
      final_norm_w.reshape(1, D_MODEL))
    return out
```

```python
import math
from functools import partial
from types import SimpleNamespace

import numpy as np
import jax
import jax.numpy as jnp
from jax import lax
from jax.experimental import pallas as pl
from jax.experimental.pallas import tpu as pltpu

D_MODEL = 1024
N_HEADS = 8
HEAD_DIM = 64
N_KV_HEADS = 2
GQA_REP = N_HEADS // N_KV_HEADS
D_ATTN = N_HEADS * HEAD_DIM
D_KV = N_KV_HEADS * HEAD_DIM
D_CONV = D_MODEL - D_ATTN
CMP_BLOCK = 32
CMP_STRIDE = 16
CMP_HIDDEN = 2 * HEAD_DIM
SEL_BLOCK = 64
N_SEL = 16
WINDOW = 512
N_BUCKETS = 32
MAX_DISTANCE = 128
EPS = 1e-6
NEG = -1e30

LOG2E = 1.4426950408889634
LANES = 128
SUBLANES = 8
BAND_BACK = -(-(-(-(MAX_DISTANCE + CMP_BLOCK - 1) // CMP_STRIDE)) // SUBLANES) * SUBLANES
TAKEN = -2.0
NOT_CAUSAL = -1.0
N_FORCED = 3
V_ROWS = HEAD_DIM + 16
ROW_TILE = 512
KEY_TILE = 256
Q_TILE_CMP = 512
Q_TILE = 256
HALO_ROWS = 16
MAX_BLOCK_PAIRS = 8
MAX_LAG = 64.0
N_SLOTS = 2
VMEM_LIMIT = 48 * 1024 * 1024

F32 = jnp.float32
BF16 = jnp.bfloat16

_C_Q = 0
_C_CMP = D_ATTN
_C_SW = _C_CMP + 2 * D_KV
_C_REST = _C_SW + 4 * D_KV
_C_GATE = _C_REST + D_ATTN + 4 * D_CONV
_C_END = _C_GATE + LANES
N_GATE = 3 * N_HEADS


def _bucket_lower_bounds():
    max_exact = N_BUCKETS // 2
    d = np.arange(2 * MAX_DISTANCE, dtype=np.int32)
    nf = np.maximum(d, 1).astype(np.float32)
    large = max_exact + (np.log(nf / np.float32(max_exact)) / np.float32(math.log(MAX_DISTANCE / max_exact))
                         * np.float32(N_BUCKETS - max_exact)).astype(np.int32)
    bucket = np.where(d < max_exact, d, np.minimum(large, N_BUCKETS - 1))
    assert np.all(np.diff(bucket) >= 0) and np.all(bucket[MAX_DISTANCE:] == N_BUCKETS - 1)
    return [int(np.argmax(bucket >= b)) for b in range(N_BUCKETS)]


_BUCKET_LO = _bucket_lower_bounds()


def _rel_bias_delta(dist, bias_of_bucket):
    last = bias_of_bucket(N_BUCKETS - 1)
    val = jnp.full(dist.shape, (bias_of_bucket(0) - last) * LOG2E, F32)
    for b in range(1, N_BUCKETS - 1):
        val = jnp.where(dist >= _BUCKET_LO[b], (bias_of_bucket(b) - last) * LOG2E, val)
    val = jnp.where(dist >= _BUCKET_LO[N_BUCKETS - 1], 0.0, val)
    return jnp.where(dist < 0, NEG, val)


def _silu(z):
    return z * (1.0 / (1.0 + jnp.exp2(z * -LOG2E)))


def _proj_kernel(x_ref, nw_ref, wT_ref, qT_ref, kvc_ref, ksel_ref, vselT_ref, kwin_ref, vwinT_ref,
                 gsT_ref, rest_ref, kv_scr, w_scr, *, tm, nsel, kd):
    st = pl.program_id(1)

    @pl.when((pl.program_id(0) == 0) & (st == 0))
    def _():
        c0 = _C_REST
        chunk = KEY_TILE
        for dst, src, rows in ((0, 0, c0), (c0, c0 + N_GATE, _C_GATE - c0)):
            for r in range(0, rows, chunk):
                w_scr[dst + r:dst + r + chunk, :] = wT_ref[src + r:src + r + chunk, :].astype(BF16)
        w_scr[_C_GATE:_C_END, :] = jnp.concatenate(
            [wT_ref[c0:c0 + N_GATE, :], jnp.zeros((LANES - N_GATE, D_MODEL), F32)], axis=0).astype(BF16)

    x = x_ref[0]
    ms = jnp.mean(x * x, axis=-1, keepdims=True)
    h = (x * lax.rsqrt(ms + EPS) * nw_ref[...]).astype(BF16)

    def mm(lo, hi):
        return lax.dot_general(h, w_scr[lo:hi, :], (((1,), (1,)), ((), ())), preferred_element_type=F32)

    q = mm(_C_Q, _C_CMP) * (HEAD_DIM ** -0.5 * LOG2E)
    qT_ref[0] = q.T.astype(BF16)

    kvc = mm(_C_CMP, _C_SW)
    for xk in range(2):
        kv_scr[xk] = kvc[:, xk * D_KV:(xk + 1) * D_KV]
        for j in range(CMP_STRIDE):
            kvc_ref[xk, 0, :, j * D_KV:(j + 1) * D_KV] = kv_scr[xk, pl.ds(j, tm // CMP_STRIDE, stride=CMP_STRIDE), :]

    a = mm(_C_SW, _C_REST)
    pos = st * tm + lax.broadcasted_iota(jnp.int32, (tm, nsel), 0)
    col = lax.broadcasted_iota(jnp.int32, (tm, nsel), 1)
    onehot = jnp.where((pos // SEL_BLOCK) == col, 1.0, 0.0).astype(BF16)
    ones = jnp.ones((V_ROWS - HEAD_DIM, tm), BF16)
    zeros_k = jnp.zeros((tm, LANES - HEAD_DIM), BF16)
    vsT = a[:, D_KV:2 * D_KV].T
    vwT = a[:, 3 * D_KV:4 * D_KV].T
    for g in range(N_KV_HEADS):
        lo, hi = g * HEAD_DIM, (g + 1) * HEAD_DIM
        parts = [a[:, lo:hi].astype(BF16), onehot]
        if kd > HEAD_DIM + nsel:
            parts.append(jnp.zeros((tm, kd - HEAD_DIM - nsel), BF16))
        ksel_ref[0, g] = jnp.concatenate(parts, axis=1)
        kwin_ref[0, g] = jnp.concatenate([a[:, 2 * D_KV + lo:2 * D_KV + hi].astype(BF16), zeros_k], axis=1)
        vs = jnp.concatenate([vsT[lo:hi].astype(BF16), ones], axis=0)
        vw = jnp.concatenate([vwT[lo:hi].astype(BF16), ones], axis=0)
        for kt in range(tm // KEY_TILE):
            vselT_ref[0, g, kt] = vs[:, kt * KEY_TILE:(kt + 1) * KEY_TILE]
            vwinT_ref[0, g, kt] = vw[:, kt * KEY_TILE:(kt + 1) * KEY_TILE]

    r = mm(_C_REST, _C_GATE)
    z_attn, conv_h, conv_b, conv_c, z_conv = (r[:, i * D_CONV:(i + 1) * D_CONV] for i in range(5))
    rest_ref[0] = jnp.concatenate([z_attn, conv_c * conv_h, conv_b, z_conv], axis=1).astype(BF16)
    gates = jax.nn.sigmoid(mm(_C_GATE, _C_END))
    gsT_ref[0] = gates.T[:N_GATE]


def _compress_kernel(c_ref, pe_ref, w1_ref, w2_ref, lhs_tail_ref, aug_ref, t_ref, *, nch):
    c = c_ref[0, 0]
    a0 = jnp.dot((c + pe_ref[0, 0]).astype(BF16), w1_ref[0, 0], preferred_element_type=F32)
    a1 = jnp.dot((c + pe_ref[0, 1]).astype(BF16), w1_ref[0, 1], preferred_element_type=F32)
    hid = a0 + pltpu.roll(a1, nch - 1, 0)
    out = jnp.dot(_silu(hid).astype(BF16), w2_ref[0], preferred_element_type=F32)
    grp = lax.broadcasted_iota(jnp.int32, (nch, LANES - HEAD_DIM), 0) // SUBLANES
    col = lax.broadcasted_iota(jnp.int32, (nch, LANES - HEAD_DIM), 1)
    onehot = jnp.where(grp == col, 1.0, 0.0).astype(BF16)
    out_t = out.T.astype(BF16)
    for g in range(N_KV_HEADS):
        lo, hi = g * HEAD_DIM, (g + 1) * HEAD_DIM
        aug_ref[0, 0, g] = jnp.concatenate([out[:, lo:hi].astype(BF16), onehot], axis=1)
        t_ref[0, 0, g] = jnp.concatenate([out_t[lo:hi], lhs_tail_ref[...]], axis=0)


def _cmp_kernel(rb_ref, qT_ref, kc_ref, lhs_ref, gs_ref, ocT_ref, selb_ref,
                qa_ref, band_ref, acc_ref, score_ref, ntk_ref, s0_ref, s1_ref, s2_ref, s3_ref,
                *, nch, nsel, tq, k_top, nqt):
    g = pl.program_id(1)
    qt = pl.program_id(2)
    s_refs = (s0_ref, s1_ref, s2_ref, s3_ref)
    cpq = tq // CMP_STRIDE
    win = cpq + BAND_BACK
    lim = cpq * (qt + 1)
    nq = GQA_REP * tq

    @pl.when(qt <= 1)
    def _():
        row = lax.broadcasted_iota(jnp.int32, (win, tq), 0) + jnp.maximum(cpq * qt - BAND_BACK, 0)
        lane = lax.broadcasted_iota(jnp.int32, (win, tq), 1)
        dist = (qt * tq + lane) - (row * CMP_STRIDE + CMP_BLOCK - 1)
        for r in range(GQA_REP):
            band_ref[:, r * tq:(r + 1) * tq] = _rel_bias_delta(dist, lambda b, r=r: rb_ref[b, g * GQA_REP + r])

    q = qT_ref[0]
    qa_ref[0:HEAD_DIM, :] = jnp.concatenate([q[r * HEAD_DIM:(r + 1) * HEAD_DIM] for r in range(GQA_REP)], axis=1)
    grp = lax.broadcasted_iota(jnp.int32, (LANES - HEAD_DIM, nq), 0)
    qa_ref[HEAD_DIM:LANES, :] = jnp.where(grp * SUBLANES >= lim, NEG, 0.0).astype(BF16)

    def run(rows):
        start = pl.multiple_of(jnp.clip(cpq * qt - BAND_BACK, 0, rows - win), SUBLANES)
        k = kc_ref[0, 0, 0, 0:rows, :]
        lhs = lhs_ref[0, 0, 0, :, 0:rows]
        cols = [slice(r * tq, (r + 1) * tq) for r in range(GQA_REP)]

        def qk(r):
            s_refs[r][0:rows, :] = jnp.dot(k, qa_ref[:, cols[r]], preferred_element_type=F32)

        ahead = 3
        for r in range(ahead):
            qk(r)
        for r in range(GQA_REP):
            if r + ahead < GQA_REP:
                qk(r + ahead)
            sr = s_refs[r]
            sr[pl.ds(start, win), :] = sr[pl.ds(start, win), :] + band_ref[:, cols[r]]
            s = sr[0:rows, :]
            p = jnp.exp2(s - jnp.max(s, axis=0, keepdims=True)).astype(BF16)
            acc_ref[:, cols[r]] = jnp.dot(lhs, p, preferred_element_type=F32)

    n = lax.broadcasted_iota(jnp.int32, (nsel, tq), 0)
    lane = lax.broadcasted_iota(jnp.int32, (nsel, tq), 1)

    def forced_and_candidate(tile):
        blk = (tile * tq + lane) // SEL_BLOCK
        forced = (n == 0) | (n == blk) | (n == blk - 1)
        return forced, (n <= blk) & jnp.logical_not(forced), n <= blk

    def select(tile):
        score = score_ref[...]
        for _ in range(k_top - N_FORCED):
            score = jnp.where(score == jnp.max(score, axis=0, keepdims=True), TAKEN, score)
        taken = score == TAKEN
        selb_ref[0, 0, jnp.maximum(tile, 0)] = jnp.where(taken, 0.0, NEG).astype(BF16)
        ntk_ref[...] = jnp.sum(jnp.where(taken & forced_and_candidate(tile)[1], 1.0, 0.0), axis=0, keepdims=True)

    def select_exact_if_tied(tile):
        @pl.when(jnp.max(ntk_ref[...]) > k_top - N_FORCED)
        def _():
            def pick_one(_, score):
                mx = jnp.max(score, axis=0, keepdims=True)
                idx = jnp.min(jnp.where(score == mx, n, nsel), axis=0, keepdims=True)
                return jnp.where(n == idx, TAKEN, score)

            taken = lax.fori_loop(0, k_top - N_FORCED, pick_one, score_ref[...]) == TAKEN
            selb_ref[0, 0, jnp.maximum(tile, 0)] = jnp.where(taken, 0.0, NEG).astype(BF16)

    @pl.when(qt == 0)
    def _():
        score_ref[...] = jnp.full((nsel, tq), TAKEN, F32)

    half = nch // 2

    @pl.when(lim <= half)
    def _():
        run(half)
        select(qt - 1)

    @pl.when(lim > half)
    def _():
        run(nch)
        select(qt - 1)

    select_exact_if_tied(qt - 1)

    l = acc_ref[HEAD_DIM:HEAD_DIM + 1, :]
    t4 = qt * tq + (lax.broadcasted_iota(jnp.int32, l.shape, 1) & (tq - 1))
    inv = jnp.where(t4 >= CMP_BLOCK - 1, 1.0 / jnp.maximum(l, 1e-30), 0.0)
    imp = None
    for r in range(GQA_REP):
        cs = slice(r * tq, (r + 1) * tq)
        ocT_ref[0, r * HEAD_DIM:(r + 1) * HEAD_DIM, :] = (acc_ref[0:HEAD_DIM, cs] * inv[:, cs]
                                                          * gs_ref[0, 0, 0, r:r + 1, :]).astype(BF16)
        part = acc_ref[V_ROWS:V_ROWS + nsel, cs] * inv[:, cs]
        imp = part if imp is None else imp + part

    forced, _, causal = forced_and_candidate(qt)
    score_ref[...] = jnp.where(forced, TAKEN, jnp.where(causal, imp, NOT_CAUSAL))

    @pl.when(qt == nqt - 1)
    def _():
        select(qt)
        select_exact_if_tied(qt)


N_BRANCH_STATE = 3 + 2 * N_SLOTS


def _branch_stages(rb_ref, k_ref, vT_ref, tbl_ref, qa_ref, s_ref, mt_ref, m_ref, acc_ref, lag_ref, *slot_refs,
                   tq, n_tbl_tiles, window):
    al_ref, p_ref = slot_refs[0:N_SLOTS], slot_refs[N_SLOTS:2 * N_SLOTS]
    g = pl.program_id(1)
    qt = pl.program_id(2)
    nq = GQA_REP * tq

    @pl.when(qt == 0)
    def _():
        for j in range(n_tbl_tiles):
            key = lax.broadcasted_iota(jnp.int32, (tq, tq), 0) + j * tq
            qry = lax.broadcasted_iota(jnp.int32, (tq, tq), 1) + (n_tbl_tiles - 1) * tq
            dist = qry - key
            for r in range(GQA_REP):
                val = _rel_bias_delta(dist, lambda b, r=r: rb_ref[b, g * GQA_REP + r])
                if window is not None:
                    val = jnp.where(dist >= window, NEG, val)
                tbl_ref[j * tq:(j + 1) * tq, r * tq:(r + 1) * tq] = val

    def reset():
        m_ref[...] = jnp.full((1, nq), NEG, F32)
        acc_ref[...] = jnp.zeros((V_ROWS, nq), F32)

    reset()

    def serial(t, tbl_tile):
        k = k_ref[0, 0, pl.ds(pl.multiple_of(t * tq, tq), tq), :]
        s = jnp.dot(k, qa_ref[...], preferred_element_type=F32)
        if tbl_tile is not None:
            s = s + tbl_ref[tbl_tile * tq:(tbl_tile + 1) * tq, :]
        s_ref[...] = s
        mt_ref[...] = jnp.max(s, axis=0, keepdims=True)
        m_old = m_ref[...]
        m_new = jnp.maximum(m_old, mt_ref[...])
        m_ref[...] = m_new
        p_ref[0][...] = jnp.exp2(s_ref[...] - m_new).astype(BF16)
        acc_ref[...] = jnp.exp2(m_old - m_new) * acc_ref[...] + jnp.dot(vT_ref[0, 0, t], p_ref[0][...],
                                                                        preferred_element_type=F32)

    def stage_lagged(t, slot, tbl_tile, first=False):
        k = k_ref[0, 0, pl.ds(pl.multiple_of(t * tq, tq), tq), :]
        s = jnp.dot(k, qa_ref[...], preferred_element_type=F32)
        if tbl_tile is not None:
            s = s + tbl_ref[tbl_tile * tq:(tbl_tile + 1) * tq, :]
        m = s[0:1, :] if first else m_ref[...]
        mt = jnp.max(s, axis=0, keepdims=True)
        p_ref[slot][...] = jnp.exp2(s - m).astype(BF16)
        m_new = jnp.maximum(m, mt)
        m_ref[...] = m_new
        al_ref[slot][...] = jnp.exp2(m - m_new)
        lag_ref[...] = (mt - m) if first else jnp.maximum(lag_ref[...], mt - m)

    def stage_pv_lagged(t, slot):
        acc_ref[...] = (acc_ref[...] + jnp.dot(vT_ref[0, 0, t], p_ref[slot][...],
                                               preferred_element_type=F32)) * al_ref[slot][...]

    def lagged_step(t, slot, tbl_tile, prev=None):
        stage_lagged(t, slot, tbl_tile)
        stage_pv_lagged(t - 1 if prev is None else prev, 1 - slot)

    def exceeded():
        return jnp.max(lag_ref[...]) > MAX_LAG

    def gated_output(gs_ref):
        acc = acc_ref[...]
        o = acc[0:HEAD_DIM] / jnp.maximum(acc[HEAD_DIM:HEAD_DIM + 1], 1e-30)
        return [o[:, r * tq:(r + 1) * tq] * gs_ref[0, 0, 0, r:r + 1, :] for r in range(GQA_REP)]

    return SimpleNamespace(reset=reset, serial=serial, stage_lagged=stage_lagged, stage_pv_lagged=stage_pv_lagged,
                           lagged_step=lagged_step, exceeded=exceeded, gated_output=gated_output)


SEL_TBL_TILES = 2
WIN_TBL_TILES = 3


def _branches_kernel(rb_ref, qT_ref, selb_ref, ksel_ref, vselT_ref, kwin_ref, vwinT_ref, gs_sel_ref, gs_win_ref,
                     o_ref, tbl_ref, qa_ref, s_ref, mt_ref, *state, tq, nsel, kd):
    qt = pl.program_id(2)
    n = qt + 1
    nq = GQA_REP * tq

    q = qT_ref[0]
    qa_ref[0:HEAD_DIM, :] = jnp.concatenate([q[r * HEAD_DIM:(r + 1) * HEAD_DIM] for r in range(GQA_REP)], axis=1)
    qa_ref[HEAD_DIM:HEAD_DIM + nsel, :] = jnp.concatenate([selb_ref[0, 0, 0]] * GQA_REP, axis=1)
    if kd > HEAD_DIM + nsel:
        qa_ref[HEAD_DIM + nsel:kd, :] = jnp.zeros((kd - HEAD_DIM - nsel, nq), BF16)

    sel = _branch_stages(rb_ref, ksel_ref, vselT_ref, tbl_ref.at[0:SEL_TBL_TILES * tq], qa_ref, s_ref, mt_ref,
                         *state[:N_BRANCH_STATE], tq=tq, n_tbl_tiles=SEL_TBL_TILES, window=None)
    win = _branch_stages(rb_ref, kwin_ref, vwinT_ref,
                         tbl_ref.at[SEL_TBL_TILES * tq:(SEL_TBL_TILES + WIN_TBL_TILES) * tq], qa_ref.at[0:LANES],
                         s_ref, mt_ref, *state[N_BRANCH_STATE:], tq=tq, n_tbl_tiles=WIN_TBL_TILES, window=WINDOW)

    @pl.when((n >= 4) & ((n & 1) == 0))
    def _():
        sel.stage_lagged(0, 1, None, first=True)
        sel.lagged_step(1, 0, None)

    @pl.when((n >= 4) & ((n & 1) == 1))
    def _():
        sel.stage_lagged(0, 0, None, first=True)

    @pl.when(n >= 4)
    def _():
        t0 = 2 - (n & 1)
        pairs = (n - 2 - t0) >> 1

        def pair(t):
            sel.lagged_step(t, 1, None)
            sel.lagged_step(t + 1, 0, None)

        big = MAX_BLOCK_PAIRS

        def body(j, carry):
            for i in range(big):
                pair(t0 + 2 * (big * j + i))
            return carry
        lax.fori_loop(0, pairs >> (big.bit_length() - 1), body, 0)

        bp = big // 2
        while bp >= 1:
            @pl.when((pairs & bp) == bp)
            def _(bp=bp):
                t = n - 2 - 2 * (pairs & (2 * bp - 1))
                for i in range(bp):
                    pair(t + 2 * i)
            bp //= 2

    def window_tiles(after_first=lambda: None):
        win.stage_lagged(qt, 0, 2, first=True)
        after_first()
        win.lagged_step(qt - 1, 1, 1, prev=qt)
        win.lagged_step(qt - 2, 0, 0, prev=qt - 1)
        win.stage_pv_lagged(qt - 2, 0)

    def window_exact_if_exceeded():
        @pl.when(win.exceeded())
        def _():
            win.reset()
            for j in range(WIN_TBL_TILES):
                win.serial(qt - (WIN_TBL_TILES - 1) + j, j)

    @pl.when(n >= 4)
    def _():
        sel.lagged_step(n - 2, 1, 0)
        sel.lagged_step(n - 1, 0, 1)
        window_tiles(after_first=lambda: sel.stage_pv_lagged(n - 1, 0))

        @pl.when(sel.exceeded())
        def _():
            sel.reset()

            def redo(t, carry):
                sel.serial(t, None)
                return carry
            lax.fori_loop(0, n - 2, redo, 0)
            sel.serial(n - 2, 0)
            sel.serial(n - 1, 1)

        window_exact_if_exceeded()

    @pl.when(n == 3)
    def _():
        sel.serial(0, None)

    @pl.when((n >= 2) & (n < 4))
    def _():
        sel.serial(qt - 1, 0)

    @pl.when(n < 4)
    def _():
        sel.serial(qt, 1)

    @pl.when(n == 3)
    def _():
        window_tiles()
        window_exact_if_exceeded()

    @pl.when(qt == 1)
    def _():
        win.serial(0, 1)

    @pl.when(qt < WIN_TBL_TILES - 1)
    def _():
        win.serial(qt, 2)

    for r, (so, wo) in enumerate(zip(sel.gated_output(gs_sel_ref), win.gated_output(gs_win_ref))):
        o_ref[0, r * HEAD_DIM:(r + 1) * HEAD_DIM, :] = (so + wo).astype(o_ref.dtype)


def _out_kernel(oc_ref, osw_ref, za_ref, u_ref, cb_ref, zc_ref, hu_ref,
                x_ref, cw_ref, wo_ref, fw_ref, out_ref, *, tm):
    st = pl.program_id(1)
    f32 = lambda ref: ref[0].astype(F32)
    attn = (f32(oc_ref) + f32(osw_ref)).T * _silu(f32(za_ref))

    u = f32(u_ref)
    halo = jnp.where(st == 0, 0.0, f32(hu_ref))
    row = lax.broadcasted_iota(jnp.int32, u.shape, 0)
    h1, h2 = halo[HALO_ROWS - 1:HALO_ROWS], halo[HALO_ROWS - 2:HALO_ROWS - 1]
    u1 = jnp.where(row == 0, h1, pltpu.roll(u, 1, 0))
    u2 = jnp.where(row == 0, h2, jnp.where(row == 1, h1, pltpu.roll(u, 2, 0)))
    y = cw_ref[0:1] * u2 + cw_ref[1:2] * u1 + cw_ref[2:3] * u
    conv = f32(cb_ref) * y * _silu(f32(zc_ref))

    mixed = jnp.concatenate([attn, conv], axis=1).astype(BF16)
    y2 = x_ref[0] + jnp.dot(mixed, wo_ref[...], preferred_element_type=F32)
    ms = jnp.mean(y2 * y2, axis=-1, keepdims=True)
    out_ref[0] = y2 * lax.rsqrt(ms + EPS) * fw_ref[...]


def _overlap_t(nsel, nch):
    c = np.arange(nch)[None, :] * CMP_STRIDE
    n = np.arange(nsel)[:, None] * SEL_BLOCK
    return ((c < n + SEL_BLOCK) & (c + CMP_BLOCK > n)).astype(np.float32)


def _cparams(sem):
    return pltpu.CompilerParams(dimension_semantics=sem, vmem_limit_bytes=VMEM_LIMIT)


def kernel(x, norm_w, w_in, w_ck1, w_ck2, pe_k, w_cv1, w_cv2, pe_v, conv_w, w_out, rel_bias, final_norm_w):
    b, s, _ = x.shape
    assert norm_w.shape[0] == 1 and s % ROW_TILE == 0 and ROW_TILE % KEY_TILE == 0
    assert WINDOW == 2 * Q_TILE and KEY_TILE == Q_TILE
    tm = ROW_TILE
    nt = s // tm
    nkt = s // KEY_TILE
    kpt = tm // KEY_TILE
    nch = s // CMP_STRIDE
    nsel = s // SEL_BLOCK
    k_top = min(N_SEL, nsel)
    kd = -(-(HEAD_DIM + nsel) // LANES) * LANES
    G, R = N_KV_HEADS, GQA_REP

    w_in_t = jnp.swapaxes(w_in[0], 0, 1)
    assert w_in_t.shape[0] == _C_GATE + N_GATE
    eye = jnp.eye(G, dtype=F32)

    def expand1(w1):
        w1 = w1.astype(BF16).reshape(2, CMP_STRIDE, HEAD_DIM, CMP_HIDDEN)
        z = jnp.zeros_like(w1)
        rows = [jnp.concatenate([w1 if a == g else z for a in range(G)], axis=-1) for g in range(G)]
        return jnp.stack(rows, axis=2).reshape(2, CMP_STRIDE * D_KV, G * CMP_HIDDEN)

    def expand_pe(pe):
        pe = pe.reshape(2, CMP_STRIDE, 1, HEAD_DIM)
        return jnp.broadcast_to(pe, (2, CMP_STRIDE, G, HEAD_DIM)).reshape(2, 1, CMP_STRIDE * D_KV)

    def expand2(w2):
        return jnp.einsum('hd,ab->ahbd', w2, eye).reshape(G * CMP_HIDDEN, D_KV)

    w1e = jnp.stack([expand1(w_ck1[0]), expand1(w_cv1[0])]).astype(BF16)
    pee = jnp.stack([expand_pe(pe_k[0]), expand_pe(pe_v[0])])
    w2e = jnp.stack([expand2(w_ck2[0]), expand2(w_cv2[0])]).astype(BF16)
    rel_bias = rel_bias.astype(F32)
    smem = pl.BlockSpec(memory_space=pltpu.SMEM)

    n_rest = D_ATTN + 3 * D_CONV
    cw = CMP_STRIDE * D_KV
    qT, kvc, ksel, vselT, kwin, vwinT, gsT, rest = pl.pallas_call(
        partial(_proj_kernel, tm=tm, nsel=nsel, kd=kd),
        grid=(b, nt),
        in_specs=[pl.BlockSpec((1, tm, D_MODEL), lambda i, j: (i, j, 0)),
                  pl.BlockSpec((1, D_MODEL), lambda i, j: (0, 0)),
                  pl.BlockSpec((_C_GATE + N_GATE, D_MODEL), lambda i, j: (0, 0), pipeline_mode=pl.Buffered(1))],
        out_specs=[pl.BlockSpec((1, D_ATTN, tm), lambda i, j: (i, 0, j)),
                   pl.BlockSpec((2, 1, tm // CMP_STRIDE, cw), lambda i, j: (0, i, j, 0)),
                   pl.BlockSpec((1, G, tm, kd), lambda i, j: (i, 0, j, 0)),
                   pl.BlockSpec((1, G, kpt, V_ROWS, KEY_TILE), lambda i, j: (i, 0, j, 0, 0)),
                   pl.BlockSpec((1, G, tm, LANES), lambda i, j: (i, 0, j, 0)),
                   pl.BlockSpec((1, G, kpt, V_ROWS, KEY_TILE), lambda i, j: (i, 0, j, 0, 0)),
                   pl.BlockSpec((1, N_GATE, tm), lambda i, j: (i, 0, j)),
                   pl.BlockSpec((1, tm, n_rest), lambda i, j: (i, j, 0))],
        out_shape=[jax.ShapeDtypeStruct((b, D_ATTN, s), BF16),
                   jax.ShapeDtypeStruct((2, b, nch, cw), F32),
                   jax.ShapeDtypeStruct((b, G, s, kd), BF16),
                   jax.ShapeDtypeStruct((b, G, nkt, V_ROWS, KEY_TILE), BF16),
                   jax.ShapeDtypeStruct((b, G, s, LANES), BF16),
                   jax.ShapeDtypeStruct((b, G, nkt, V_ROWS, KEY_TILE), BF16),
                   jax.ShapeDtypeStruct((b, N_GATE, s), F32),
                   jax.ShapeDtypeStruct((b, s, n_rest), BF16)],
        scratch_shapes=[pltpu.VMEM((2, tm, D_KV), F32),
                        pltpu.VMEM((_C_END, D_MODEL), BF16)],
        compiler_params=_cparams(("arbitrary", "arbitrary")),
        name="in_proj",
    )(x, norm_w, w_in_t)
    gs = gsT.reshape(b, 3, G, R, s)

    lhs_rows = V_ROWS + nsel
    lhs_tail = jnp.asarray(np.concatenate([np.ones((V_ROWS - HEAD_DIM, nch), np.float32),
                                           _overlap_t(nsel, nch)], axis=0), BF16)
    kc_aug, c_lhs = pl.pallas_call(
        partial(_compress_kernel, nch=nch),
        grid=(2, b),
        in_specs=[pl.BlockSpec((1, 1, nch, cw), lambda xk, i: (xk, i, 0, 0)),
                  pl.BlockSpec((1, 2, 1, cw), lambda xk, i: (xk, 0, 0, 0)),
                  pl.BlockSpec((1, 2, cw, G * CMP_HIDDEN), lambda xk, i: (xk, 0, 0, 0)),
                  pl.BlockSpec((1, G * CMP_HIDDEN, D_KV), lambda xk, i: (xk, 0, 0)),
                  pl.BlockSpec((lhs_rows - HEAD_DIM, nch), lambda xk, i: (0, 0))],
        out_specs=[pl.BlockSpec((1, 1, G, nch, LANES), lambda xk, i: (xk, i, 0, 0, 0)),
                   pl.BlockSpec((1, 1, G, lhs_rows, nch), lambda xk, i: (xk, i, 0, 0, 0))],
        out_shape=[jax.ShapeDtypeStruct((2, b, G, nch, LANES), BF16),
                   jax.ShapeDtypeStruct((2, b, G, lhs_rows, nch), BF16)],
        compiler_params=_cparams(("parallel", "parallel")),
        name="compress",
    )(kvc, pee, w1e, w2e, lhs_tail)

    tq3 = Q_TILE_CMP
    cpq = tq3 // CMP_STRIDE
    assert nch // SUBLANES <= LANES - HEAD_DIM and k_top > N_FORCED and (nch // 2) % LANES == 0
    ocT, selb = pl.pallas_call(
        partial(_cmp_kernel, nch=nch, nsel=nsel, tq=tq3, k_top=k_top, nqt=s // tq3),
        grid=(b, G, s // tq3),
        in_specs=[smem,
                  pl.BlockSpec((1, R * HEAD_DIM, tq3), lambda i, g, t: (i, g, t)),
                  pl.BlockSpec((1, 1, 1, nch, LANES), lambda i, g, t: (0, i, g, 0, 0)),
                  pl.BlockSpec((1, 1, 1, lhs_rows, nch), lambda i, g, t: (1, i, g, 0, 0)),
                  pl.BlockSpec((1, 1, 1, R, tq3), lambda i, g, t: (i, 0, g, 0, t))],
        out_specs=[pl.BlockSpec((1, R * HEAD_DIM, tq3), lambda i, g, t: (i, g, t)),
                   pl.BlockSpec((1, 1, s // tq3, nsel, tq3), lambda i, g, t: (i, g, 0, 0, 0))],
        out_shape=[jax.ShapeDtypeStruct((b, D_ATTN, s), BF16),
                   jax.ShapeDtypeStruct((b, G, s // tq3, nsel, tq3), BF16)],
        scratch_shapes=[pltpu.VMEM((LANES, R * tq3), BF16),
                        pltpu.VMEM((cpq + BAND_BACK, R * tq3), F32),
                        pltpu.VMEM((lhs_rows, R * tq3), F32),
                        pltpu.VMEM((nsel, tq3), F32),
                        pltpu.VMEM((1, tq3), F32)]
                       + [pltpu.VMEM((nch, tq3), F32)] * R,
        compiler_params=_cparams(("parallel", "parallel", "arbitrary")),
        name="cmp_select",
    )(rel_bias, qT, kc_aug, c_lhs, gs)

    tq = Q_TILE
    nq = R * tq

    o_spec = pl.BlockSpec((1, R * HEAD_DIM, tq), lambda i, g, t: (i, g, t))
    k_spec = lambda kdim: pl.BlockSpec((1, 1, s, kdim), lambda i, g, t: (i, g, 0, 0))
    v_spec = pl.BlockSpec((1, 1, nkt, V_ROWS, KEY_TILE), lambda i, g, t: (i, g, 0, 0, 0))
    gs_spec = lambda branch: pl.BlockSpec((1, 1, 1, R, tq), lambda i, g, t: (i, branch, g, 0, t))
    oswT = pl.pallas_call(
        partial(_branches_kernel, tq=tq, nsel=nsel, kd=kd),
        grid=(b, G, s // tq),
        in_specs=[smem, o_spec,
                  pl.BlockSpec((1, 1, 1, nsel, tq), lambda i, g, t: (i, g, t // (tq3 // tq), 0, t % (tq3 // tq))),
                  k_spec(kd), v_spec, k_spec(LANES), v_spec, gs_spec(1), gs_spec(2)],
        out_specs=o_spec,
        out_shape=jax.ShapeDtypeStruct((b, D_ATTN, s), BF16),
        scratch_shapes=[pltpu.VMEM(((SEL_TBL_TILES + WIN_TBL_TILES) * tq, nq), F32),
                        pltpu.VMEM((kd, nq), BF16),
                        pltpu.VMEM((tq, nq), F32),
                        pltpu.VMEM((1, nq), F32)]
                       + ([pltpu.VMEM((1, nq), F32),
                           pltpu.VMEM((V_ROWS, nq), F32),
                           pltpu.VMEM((1, nq), F32)]
                          + [pltpu.VMEM((1, nq), F32)] * N_SLOTS
                          + [pltpu.VMEM((tq, nq), BF16)] * N_SLOTS) * 2,
        compiler_params=_cparams(("parallel", "parallel", "arbitrary")),
        name="sel_win_attn",
    )(rel_bias, qT, selb, ksel, vselT, kwin, vwinT, gs, gs)

    hb = tm // HALO_ROWS
    col = lambda c: pl.BlockSpec((1, tm, D_CONV), lambda i, j, c=c: (i, j, c))
    halo = lambda c: pl.BlockSpec((1, HALO_ROWS, D_CONV), lambda i, j, c=c: (i, jnp.maximum(j * hb - 1, 0), c))
    oT = pl.BlockSpec((1, D_ATTN, tm), lambda i, j: (i, 0, j))
    out = pl.pallas_call(
        partial(_out_kernel, tm=tm),
        grid=(b, nt),
        in_specs=[oT, oT, col(0), col(1), col(2), col(3), halo(1),
                  pl.BlockSpec((1, tm, D_MODEL), lambda i, j: (i, j, 0)),
                  pl.BlockSpec((3, D_CONV), lambda i, j: (0, 0)),
                  pl.BlockSpec((D_MODEL, D_MODEL), lambda i, j: (0, 0)),
                  pl.BlockSpec((1, D_MODEL), lambda i, j: (0, 0))],
        out_specs=pl.BlockSpec((1, tm, D_MODEL), lambda i, j: (i, j, 0)),
        out_shape=jax.ShapeDtypeStruct((b, s, D_MODEL), F32),
        compiler_params=_cparams(("parallel", "parallel")),
        name="out_proj",
    )(ocT, oswT, rest, rest, rest, rest, rest, x, conv_w[0], w_out[0].astype(BF16),
      final_norm_w.reshape(1, D_MODEL))
    return out
```

```python
import math
from functools import partial
from types import SimpleNamespace

import numpy as np
import jax
import jax.numpy as jnp
from jax import lax
from jax.experimental import pallas as pl
from jax.experimental.pallas import tpu as pltpu

D_MODEL = 1024
N_HEADS = 8
HEAD_DIM = 64
N_KV_HEADS = 2
GQA_REP = N_HEADS // N_KV_HEADS
D_ATTN = N_HEADS * HEAD_DIM
D_KV = N_KV_HEADS * HEAD_DIM
D_CONV = D_MODEL - D_ATTN
CMP_BLOCK = 32
CMP_STRIDE = 16
CMP_HIDDEN = 2 * HEAD_DIM
SEL_BLOCK = 64
N_SEL = 16
WINDOW = 512
N_BUCKETS = 32
MAX_DISTANCE = 128
EPS = 1e-6
NEG = -1e30

LOG2E = 1.4426950408889634
LANES = 128
SUBLANES = 8
BAND_BACK = -(-(-(-(MAX_DISTANCE + CMP_BLOCK - 1) // CMP_STRIDE)) // SUBLANES) * SUBLANES
TAKEN = -2.0
NOT_CAUSAL = -1.0
N_FORCED = 3
V_ROWS = HEAD_DIM + 16
ROW_TILE = 512
OUT_TILE = 1024
KEY_TILE = 256
Q_TILE_CMP = 512
Q_TILE = 256
HALO_ROWS = 16
MAX_BLOCK_PAIRS = 8
MAX_LAG = 64.0
N_SLOTS = 2
VMEM_LIMIT = 48 * 1024 * 1024

F32 = jnp.float32
BF16 = jnp.bfloat16

_C_Q = 0
_C_CMP = D_ATTN
_C_SW = _C_CMP + 2 * D_KV
_C_REST = _C_SW + 4 * D_KV
_C_GATE = _C_REST + D_ATTN + 4 * D_CONV
_C_END = _C_GATE + LANES
N_GATE = 3 * N_HEADS


def _bucket_lower_bounds():
    max_exact = N_BUCKETS // 2
    d = np.arange(2 * MAX_DISTANCE, dtype=np.int32)
    nf = np.maximum(d, 1).astype(np.float32)
    large = max_exact + (np.log(nf / np.float32(max_exact)) / np.float32(math.log(MAX_DISTANCE / max_exact))
                         * np.float32(N_BUCKETS - max_exact)).astype(np.int32)
    bucket = np.where(d < max_exact, d, np.minimum(large, N_BUCKETS - 1))
    assert np.all(np.diff(bucket) >= 0) and np.all(bucket[MAX_DISTANCE:] == N_BUCKETS - 1)
    return [int(np.argmax(bucket >= b)) for b in range(N_BUCKETS)]


_BUCKET_LO = _bucket_lower_bounds()


def _rel_bias_delta(dist, bias_of_bucket):
    last = bias_of_bucket(N_BUCKETS - 1)
    val = jnp.full(dist.shape, (bias_of_bucket(0) - last) * LOG2E, F32)
    for b in range(1, N_BUCKETS - 1):
        val = jnp.where(dist >= _BUCKET_LO[b], (bias_of_bucket(b) - last) * LOG2E, val)
    val = jnp.where(dist >= _BUCKET_LO[N_BUCKETS - 1], 0.0, val)
    return jnp.where(dist < 0, NEG, val)


def _silu(z):
    return z * (1.0 / (1.0 + jnp.exp2(z * -LOG2E)))


def _proj_kernel(x_ref, nw_ref, wT_ref, qT_ref, kvc_ref, ksel_ref, vselT_ref, kwin_ref, vwinT_ref,
                 gsT_ref, rest_ref, kv_scr, w_scr, *, tm, nsel, kd):
    st = pl.program_id(1)

    @pl.when((pl.program_id(0) == 0) & (st == 0))
    def _():
        c0 = _C_REST
        chunk = KEY_TILE
        for dst, src, rows in ((0, 0, c0), (c0, c0 + N_GATE, _C_GATE - c0)):
            for r in range(0, rows, chunk):
                w_scr[dst + r:dst + r + chunk, :] = wT_ref[src + r:src + r + chunk, :].astype(BF16)
        w_scr[_C_GATE:_C_END, :] = jnp.concatenate(
            [wT_ref[c0:c0 + N_GATE, :], jnp.zeros((LANES - N_GATE, D_MODEL), F32)], axis=0).astype(BF16)

    x = x_ref[0]
    ms = jnp.mean(x * x, axis=-1, keepdims=True)
    h = (x * lax.rsqrt(ms + EPS) * nw_ref[...]).astype(BF16)

    def mm(lo, hi):
        return lax.dot_general(h, w_scr[lo:hi, :], (((1,), (1,)), ((), ())), preferred_element_type=F32)

    q = mm(_C_Q, _C_CMP) * (HEAD_DIM ** -0.5 * LOG2E)
    qT_ref[0] = q.T.astype(BF16)

    kvc = mm(_C_CMP, _C_SW)
    for xk in range(2):
        kv_scr[xk] = kvc[:, xk * D_KV:(xk + 1) * D_KV]
        for j in range(CMP_STRIDE):
            kvc_ref[xk, 0, :, j * D_KV:(j + 1) * D_KV] = kv_scr[xk, pl.ds(j, tm // CMP_STRIDE, stride=CMP_STRIDE), :]

    a = mm(_C_SW, _C_REST)
    pos = st * tm + lax.broadcasted_iota(jnp.int32, (tm, nsel), 0)
    col = lax.broadcasted_iota(jnp.int32, (tm, nsel), 1)
    onehot = jnp.where((pos // SEL_BLOCK) == col, 1.0, 0.0).astype(BF16)
    ones = jnp.ones((V_ROWS - HEAD_DIM, tm), BF16)
    zeros_k = jnp.zeros((tm, LANES - HEAD_DIM), BF16)
    vsT = a[:, D_KV:2 * D_KV].T
    vwT = a[:, 3 * D_KV:4 * D_KV].T
    for g in range(N_KV_HEADS):
        lo, hi = g * HEAD_DIM, (g + 1) * HEAD_DIM
        parts = [a[:, lo:hi].astype(BF16), onehot]
        if kd > HEAD_DIM + nsel:
            parts.append(jnp.zeros((tm, kd - HEAD_DIM - nsel), BF16))
        ksel_ref[0, g] = jnp.concatenate(parts, axis=1)
        kwin_ref[0, g] = jnp.concatenate([a[:, 2 * D_KV + lo:2 * D_KV + hi].astype(BF16), zeros_k], axis=1)
        vs = jnp.concatenate([vsT[lo:hi].astype(BF16), ones], axis=0)
        vw = jnp.concatenate([vwT[lo:hi].astype(BF16), ones], axis=0)
        for kt in range(tm // KEY_TILE):
            vselT_ref[0, g, kt] = vs[:, kt * KEY_TILE:(kt + 1) * KEY_TILE]
            vwinT_ref[0, g, kt] = vw[:, kt * KEY_TILE:(kt + 1) * KEY_TILE]

    r = mm(_C_REST, _C_GATE)
    z_attn, conv_h, conv_b, conv_c, z_conv = (r[:, i * D_CONV:(i + 1) * D_CONV] for i in range(5))
    rest_ref[0] = jnp.concatenate([z_attn, conv_c * conv_h, conv_b, z_conv], axis=1).astype(BF16)
    gates = jax.nn.sigmoid(mm(_C_GATE, _C_END))
    gsT_ref[0] = gates.T[:N_GATE]


def _compress_kernel(c_ref, pe_ref, w1_ref, w2_ref, lhs_tail_ref, aug_ref, t_ref, *, nch):
    c = c_ref[0, 0]
    a0 = jnp.dot((c + pe_ref[0, 0]).astype(BF16), w1_ref[0, 0], preferred_element_type=F32)
    a1 = jnp.dot((c + pe_ref[0, 1]).astype(BF16), w1_ref[0, 1], preferred_element_type=F32)
    hid = a0 + pltpu.roll(a1, nch - 1, 0)
    out = jnp.dot(_silu(hid).astype(BF16), w2_ref[0], preferred_element_type=F32)
    grp = lax.broadcasted_iota(jnp.int32, (nch, LANES - HEAD_DIM), 0) // SUBLANES
    col = lax.broadcasted_iota(jnp.int32, (nch, LANES - HEAD_DIM), 1)
    onehot = jnp.where(grp == col, 1.0, 0.0).astype(BF16)
    out_t = out.T.astype(BF16)
    for g in range(N_KV_HEADS):
        lo, hi = g * HEAD_DIM, (g + 1) * HEAD_DIM
        aug_ref[0, 0, g] = jnp.concatenate([out[:, lo:hi].astype(BF16), onehot], axis=1)
        t_ref[0, 0, g] = jnp.concatenate([out_t[lo:hi], lhs_tail_ref[...]], axis=0)


def _cmp_kernel(rb_ref, qT_ref, kc_ref, lhs_ref, gs_ref, ocT_ref, selb_ref,
                qa_ref, band_ref, acc_ref, score_ref, ntk_ref, s0_ref, s1_ref, s2_ref, s3_ref,
                *, nch, nsel, tq, k_top, nqt):
    g = pl.program_id(1)
    qt = pl.program_id(2)
    s_refs = (s0_ref, s1_ref, s2_ref, s3_ref)
    cpq = tq // CMP_STRIDE
    win = cpq + BAND_BACK
    lim = cpq * (qt + 1)
    nq = GQA_REP * tq

    @pl.when(qt <= 1)
    def _():
        row = lax.broadcasted_iota(jnp.int32, (win, tq), 0) + jnp.maximum(cpq * qt - BAND_BACK, 0)
        lane = lax.broadcasted_iota(jnp.int32, (win, tq), 1)
        dist = (qt * tq + lane) - (row * CMP_STRIDE + CMP_BLOCK - 1)
        for r in range(GQA_REP):
            band_ref[:, r * tq:(r + 1) * tq] = _rel_bias_delta(dist, lambda b, r=r: rb_ref[b, g * GQA_REP + r])

    q = qT_ref[0]
    qa_ref[0:HEAD_DIM, :] = jnp.concatenate([q[r * HEAD_DIM:(r + 1) * HEAD_DIM] for r in range(GQA_REP)], axis=1)
    grp = lax.broadcasted_iota(jnp.int32, (LANES - HEAD_DIM, nq), 0)
    qa_ref[HEAD_DIM:LANES, :] = jnp.where(grp * SUBLANES >= lim, NEG, 0.0).astype(BF16)

    def run(rows):
        start = pl.multiple_of(jnp.clip(cpq * qt - BAND_BACK, 0, rows - win), SUBLANES)
        k = kc_ref[0, 0, 0, 0:rows, :]
        lhs = lhs_ref[0, 0, 0, :, 0:rows]
        cols = [slice(r * tq, (r + 1) * tq) for r in range(GQA_REP)]

        def qk(r):
            s_refs[r][0:rows, :] = jnp.dot(k, qa_ref[:, cols[r]], preferred_element_type=F32)

        ahead = 3
        for r in range(ahead):
            qk(r)
        for r in range(GQA_REP):
            if r + ahead < GQA_REP:
                qk(r + ahead)
            sr = s_refs[r]
            sr[pl.ds(start, win), :] = sr[pl.ds(start, win), :] + band_ref[:, cols[r]]
            s = sr[0:rows, :]
            p = jnp.exp2(s - jnp.max(s, axis=0, keepdims=True)).astype(BF16)
            acc_ref[:, cols[r]] = jnp.dot(lhs, p, preferred_element_type=F32)

    n = lax.broadcasted_iota(jnp.int32, (nsel, tq), 0)
    lane = lax.broadcasted_iota(jnp.int32, (nsel, tq), 1)

    def forced_and_candidate(tile):
        blk = (tile * tq + lane) // SEL_BLOCK
        forced = (n == 0) | (n == blk) | (n == blk - 1)
        return forced, (n <= blk) & jnp.logical_not(forced), n <= blk

    def select(tile):
        score = score_ref[...]
        for _ in range(k_top - N_FORCED):
            score = jnp.where(score == jnp.max(score, axis=0, keepdims=True), TAKEN, score)
        taken = score == TAKEN
        selb_ref[0, 0, jnp.maximum(tile, 0)] = jnp.where(taken, 0.0, NEG).astype(BF16)
        ntk_ref[...] = jnp.sum(jnp.where(taken & forced_and_candidate(tile)[1], 1.0, 0.0), axis=0, keepdims=True)

    def select_exact_if_tied(tile):
        @pl.when(jnp.max(ntk_ref[...]) > k_top - N_FORCED)
        def _():
            def pick_one(_, score):
                mx = jnp.max(score, axis=0, keepdims=True)
                idx = jnp.min(jnp.where(score == mx, n, nsel), axis=0, keepdims=True)
                return jnp.where(n == idx, TAKEN, score)

            taken = lax.fori_loop(0, k_top - N_FORCED, pick_one, score_ref[...]) == TAKEN
            selb_ref[0, 0, jnp.maximum(tile, 0)] = jnp.where(taken, 0.0, NEG).astype(BF16)

    @pl.when(qt == 0)
    def _():
        score_ref[...] = jnp.full((nsel, tq), TAKEN, F32)

    half = nch // 2

    @pl.when(lim <= half)
    def _():
        run(half)
        select(qt - 1)

    @pl.when(lim > half)
    def _():
        run(nch)
        select(qt - 1)

    select_exact_if_tied(qt - 1)

    l = acc_ref[HEAD_DIM:HEAD_DIM + 1, :]
    t4 = qt * tq + (lax.broadcasted_iota(jnp.int32, l.shape, 1) & (tq - 1))
    inv = jnp.where(t4 >= CMP_BLOCK - 1, 1.0 / jnp.maximum(l, 1e-30), 0.0)
    imp = None
    for r in range(GQA_REP):
        cs = slice(r * tq, (r + 1) * tq)
        ocT_ref[0, r * HEAD_DIM:(r + 1) * HEAD_DIM, :] = (acc_ref[0:HEAD_DIM, cs] * inv[:, cs]
                                                          * gs_ref[0, 0, 0, r:r + 1, :]).astype(BF16)
        part = acc_ref[V_ROWS:V_ROWS + nsel, cs] * inv[:, cs]
        imp = part if imp is None else imp + part

    forced, _, causal = forced_and_candidate(qt)
    score_ref[...] = jnp.where(forced, TAKEN, jnp.where(causal, imp, NOT_CAUSAL))

    @pl.when(qt == nqt - 1)
    def _():
        select(qt)
        select_exact_if_tied(qt)


N_BRANCH_STATE = 3 + 2 * N_SLOTS


def _branch_stages(rb_ref, k_ref, vT_ref, tbl_ref, qa_ref, s_ref, mt_ref, m_ref, acc_ref, lag_ref, *slot_refs,
                   tq, n_tbl_tiles, window):
    al_ref, p_ref = slot_refs[0:N_SLOTS], slot_refs[N_SLOTS:2 * N_SLOTS]
    g = pl.program_id(1)
    qt = pl.program_id(2)
    nq = GQA_REP * tq

    @pl.when(qt == 0)
    def _():
        for j in range(n_tbl_tiles):
            key = lax.broadcasted_iota(jnp.int32, (tq, tq), 0) + j * tq
            qry = lax.broadcasted_iota(jnp.int32, (tq, tq), 1) + (n_tbl_tiles - 1) * tq
            dist = qry - key
            for r in range(GQA_REP):
                val = _rel_bias_delta(dist, lambda b, r=r: rb_ref[b, g * GQA_REP + r])
                if window is not None:
                    val = jnp.where(dist >= window, NEG, val)
                tbl_ref[j * tq:(j + 1) * tq, r * tq:(r + 1) * tq] = val

    def reset():
        m_ref[...] = jnp.full((1, nq), NEG, F32)
        acc_ref[...] = jnp.zeros((V_ROWS, nq), F32)

    reset()

    def serial(t, tbl_tile):
        k = k_ref[0, 0, pl.ds(pl.multiple_of(t * tq, tq), tq), :]
        s = jnp.dot(k, qa_ref[...], preferred_element_type=F32)
        if tbl_tile is not None:
            s = s + tbl_ref[tbl_tile * tq:(tbl_tile + 1) * tq, :]
        s_ref[...] = s
        mt_ref[...] = jnp.max(s, axis=0, keepdims=True)
        m_old = m_ref[...]
        m_new = jnp.maximum(m_old, mt_ref[...])
        m_ref[...] = m_new
        p_ref[0][...] = jnp.exp2(s_ref[...] - m_new).astype(BF16)
        acc_ref[...] = jnp.exp2(m_old - m_new) * acc_ref[...] + jnp.dot(vT_ref[0, 0, t], p_ref[0][...],
                                                                        preferred_element_type=F32)

    def stage_lagged(t, slot, tbl_tile, first=False):
        k = k_ref[0, 0, pl.ds(pl.multiple_of(t * tq, tq), tq), :]
        s = jnp.dot(k, qa_ref[...], preferred_element_type=F32)
        if tbl_tile is not None:
            s = s + tbl_ref[tbl_tile * tq:(tbl_tile + 1) * tq, :]
        m = s[0:1, :] if first else m_ref[...]
        mt = jnp.max(s, axis=0, keepdims=True)
        p_ref[slot][...] = jnp.exp2(s - m).astype(BF16)
        m_new = jnp.maximum(m, mt)
        m_ref[...] = m_new
        al_ref[slot][...] = jnp.exp2(m - m_new)
        lag_ref[...] = (mt - m) if first else jnp.maximum(lag_ref[...], mt - m)

    def stage_pv_lagged(t, slot):
        acc_ref[...] = (acc_ref[...] + jnp.dot(vT_ref[0, 0, t], p_ref[slot][...],
                                               preferred_element_type=F32)) * al_ref[slot][...]

    def lagged_step(t, slot, tbl_tile, prev=None):
        stage_lagged(t, slot, tbl_tile)
        stage_pv_lagged(t - 1 if prev is None else prev, 1 - slot)

    def exceeded():
        return jnp.max(lag_ref[...]) > MAX_LAG

    def gated_output(gs_ref):
        acc = acc_ref[...]
        o = acc[0:HEAD_DIM] / jnp.maximum(acc[HEAD_DIM:HEAD_DIM + 1], 1e-30)
        return [o[:, r * tq:(r + 1) * tq] * gs_ref[0, 0, 0, r:r + 1, :] for r in range(GQA_REP)]

    return SimpleNamespace(reset=reset, serial=serial, stage_lagged=stage_lagged, stage_pv_lagged=stage_pv_lagged,
                           lagged_step=lagged_step, exceeded=exceeded, gated_output=gated_output)


SEL_TBL_TILES = 2
WIN_TBL_TILES = 3


def _branches_kernel(rb_ref, qT_ref, selb_ref, ksel_ref, vselT_ref, kwin_ref, vwinT_ref, gs_sel_ref, gs_win_ref,
                     o_ref, tbl_ref, qa_ref, s_ref, mt_ref, *state, tq, nsel, kd):
    qt = pl.program_id(2)
    n = qt + 1
    nq = GQA_REP * tq

    q = qT_ref[0]
    qa_ref[0:HEAD_DIM, :] = jnp.concatenate([q[r * HEAD_DIM:(r + 1) * HEAD_DIM] for r in range(GQA_REP)], axis=1)
    qa_ref[HEAD_DIM:HEAD_DIM + nsel, :] = jnp.concatenate([selb_ref[0, 0, 0]] * GQA_REP, axis=1)
    if kd > HEAD_DIM + nsel:
        qa_ref[HEAD_DIM + nsel:kd, :] = jnp.zeros((kd - HEAD_DIM - nsel, nq), BF16)

    sel = _branch_stages(rb_ref, ksel_ref, vselT_ref, tbl_ref.at[0:SEL_TBL_TILES * tq], qa_ref, s_ref, mt_ref,
                         *state[:N_BRANCH_STATE], tq=tq, n_tbl_tiles=SEL_TBL_TILES, window=None)
    win = _branch_stages(rb_ref, kwin_ref, vwinT_ref,
                         tbl_ref.at[SEL_TBL_TILES * tq:(SEL_TBL_TILES + WIN_TBL_TILES) * tq], qa_ref.at[0:LANES],
                         s_ref, mt_ref, *state[N_BRANCH_STATE:], tq=tq, n_tbl_tiles=WIN_TBL_TILES, window=WINDOW)

    @pl.when((n >= 4) & ((n & 1) == 0))
    def _():
        sel.stage_lagged(0, 1, None, first=True)
        sel.lagged_step(1, 0, None)

    @pl.when((n >= 4) & ((n & 1) == 1))
    def _():
        sel.stage_lagged(0, 0, None, first=True)

    @pl.when(n >= 4)
    def _():
        t0 = 2 - (n & 1)
        pairs = (n - 2 - t0) >> 1

        def pair(t):
            sel.lagged_step(t, 1, None)
            sel.lagged_step(t + 1, 0, None)

        big = MAX_BLOCK_PAIRS

        def body(j, carry):
            for i in range(big):
                pair(t0 + 2 * (big * j + i))
            return carry
        lax.fori_loop(0, pairs >> (big.bit_length() - 1), body, 0)

        bp = big // 2
        while bp >= 1:
            @pl.when((pairs & bp) == bp)
            def _(bp=bp):
                t = n - 2 - 2 * (pairs & (2 * bp - 1))
                for i in range(bp):
                    pair(t + 2 * i)
            bp //= 2

    def window_tiles(after_first=lambda: None):
        win.stage_lagged(qt, 0, 2, first=True)
        after_first()
        win.lagged_step(qt - 1, 1, 1, prev=qt)
        win.lagged_step(qt - 2, 0, 0, prev=qt - 1)
        win.stage_pv_lagged(qt - 2, 0)

    def window_exact_if_exceeded():
        @pl.when(win.exceeded())
        def _():
            win.reset()
            for j in range(WIN_TBL_TILES):
                win.serial(qt - (WIN_TBL_TILES - 1) + j, j)

    @pl.when(n >= 4)
    def _():
        sel.lagged_step(n - 2, 1, 0)
        sel.lagged_step(n - 1, 0, 1)
        window_tiles(after_first=lambda: sel.stage_pv_lagged(n - 1, 0))

        @pl.when(sel.exceeded())
        def _():
            sel.reset()

            def redo(t, carry):
                sel.serial(t, None)
                return carry
            lax.fori_loop(0, n - 2, redo, 0)
            sel.serial(n - 2, 0)
            sel.serial(n - 1, 1)

        window_exact_if_exceeded()

    @pl.when(n == 3)
    def _():
        sel.serial(0, None)

    @pl.when((n >= 2) & (n < 4))
    def _():
        sel.serial(qt - 1, 0)

    @pl.when(n < 4)
    def _():
        sel.serial(qt, 1)

    @pl.when(n == 3)
    def _():
        window_tiles()
        window_exact_if_exceeded()

    @pl.when(qt == 1)
    def _():
        win.serial(0, 1)

    @pl.when(qt < WIN_TBL_TILES - 1)
    def _():
        win.serial(qt, 2)

    for r, (so, wo) in enumerate(zip(sel.gated_output(gs_sel_ref), win.gated_output(gs_win_ref))):
        o_ref[0, r * HEAD_DIM:(r + 1) * HEAD_DIM, :] = (so + wo).astype(o_ref.dtype)


def _out_kernel(oc_ref, osw_ref, za_ref, u_ref, cb_ref, zc_ref, hu_ref,
                x_ref, cw_ref, wo_ref, fw_ref, out_ref, *, tm):
    st = pl.program_id(1)
    f32 = lambda ref: ref[0].astype(F32)
    attn = (f32(oc_ref) + f32(osw_ref)).T * _silu(f32(za_ref))

    u = f32(u_ref)
    halo = jnp.where(st == 0, 0.0, f32(hu_ref))
    row = lax.broadcasted_iota(jnp.int32, u.shape, 0)
    h1, h2 = halo[HALO_ROWS - 1:HALO_ROWS], halo[HALO_ROWS - 2:HALO_ROWS - 1]
    u1 = jnp.where(row == 0, h1, pltpu.roll(u, 1, 0))
    u2 = jnp.where(row == 0, h2, jnp.where(row == 1, h1, pltpu.roll(u, 2, 0)))
    y = cw_ref[0:1] * u2 + cw_ref[1:2] * u1 + cw_ref[2:3] * u
    conv = f32(cb_ref) * y * _silu(f32(zc_ref))

    mixed = jnp.concatenate([attn, conv], axis=1).astype(BF16)
    y2 = x_ref[0] + jnp.dot(mixed, wo_ref[...], preferred_element_type=F32)
    ms = jnp.mean(y2 * y2, axis=-1, keepdims=True)
    out_ref[0] = y2 * lax.rsqrt(ms + EPS) * fw_ref[...]


def _overlap_t(nsel, nch):
    c = np.arange(nch)[None, :] * CMP_STRIDE
    n = np.arange(nsel)[:, None] * SEL_BLOCK
    return ((c < n + SEL_BLOCK) & (c + CMP_BLOCK > n)).astype(np.float32)


def _cparams(sem):
    return pltpu.CompilerParams(dimension_semantics=sem, vmem_limit_bytes=VMEM_LIMIT)


def kernel(x, norm_w, w_in, w_ck1, w_ck2, pe_k, w_cv1, w_cv2, pe_v, conv_w, w_out, rel_bias, final_norm_w):
    b, s, _ = x.shape
    assert norm_w.shape[0] == 1 and s % ROW_TILE == 0 and ROW_TILE % KEY_TILE == 0
    assert WINDOW == 2 * Q_TILE and KEY_TILE == Q_TILE
    tm = ROW_TILE
    nt = s // tm
    nkt = s // KEY_TILE
    kpt = tm // KEY_TILE
    nch = s // CMP_STRIDE
    nsel = s // SEL_BLOCK
    k_top = min(N_SEL, nsel)
    kd = -(-(HEAD_DIM + nsel) // LANES) * LANES
    G, R = N_KV_HEADS, GQA_REP

    w_in_t = jnp.swapaxes(w_in[0], 0, 1)
    assert w_in_t.shape[0] == _C_GATE + N_GATE
    eye = jnp.eye(G, dtype=F32)

    def expand1(w1):
        w1 = w1.astype(BF16).reshape(2, CMP_STRIDE, HEAD_DIM, CMP_HIDDEN)
        z = jnp.zeros_like(w1)
        rows = [jnp.concatenate([w1 if a == g else z for a in range(G)], axis=-1) for g in range(G)]
        return jnp.stack(rows, axis=2).reshape(2, CMP_STRIDE * D_KV, G * CMP_HIDDEN)

    def expand_pe(pe):
        pe = pe.reshape(2, CMP_STRIDE, 1, HEAD_DIM)
        return jnp.broadcast_to(pe, (2, CMP_STRIDE, G, HEAD_DIM)).reshape(2, 1, CMP_STRIDE * D_KV)

    def expand2(w2):
        return jnp.einsum('hd,ab->ahbd', w2, eye).reshape(G * CMP_HIDDEN, D_KV)

    w1e = jnp.stack([expand1(w_ck1[0]), expand1(w_cv1[0])]).astype(BF16)
    pee = jnp.stack([expand_pe(pe_k[0]), expand_pe(pe_v[0])])
    w2e = jnp.stack([expand2(w_ck2[0]), expand2(w_cv2[0])]).astype(BF16)
    rel_bias = rel_bias.astype(F32)
    smem = pl.BlockSpec(memory_space=pltpu.SMEM)

    n_rest = D_ATTN + 3 * D_CONV
    cw = CMP_STRIDE * D_KV
    qT, kvc, ksel, vselT, kwin, vwinT, gsT, rest = pl.pallas_call(
        partial(_proj_kernel, tm=tm, nsel=nsel, kd=kd),
        grid=(b, nt),
        in_specs=[pl.BlockSpec((1, tm, D_MODEL), lambda i, j: (i, j, 0)),
                  pl.BlockSpec((1, D_MODEL), lambda i, j: (0, 0)),
                  pl.BlockSpec((_C_GATE + N_GATE, D_MODEL), lambda i, j: (0, 0), pipeline_mode=pl.Buffered(1))],
        out_specs=[pl.BlockSpec((1, D_ATTN, tm), lambda i, j: (i, 0, j)),
                   pl.BlockSpec((2, 1, tm // CMP_STRIDE, cw), lambda i, j: (0, i, j, 0)),
                   pl.BlockSpec((1, G, tm, kd), lambda i, j: (i, 0, j, 0)),
                   pl.BlockSpec((1, G, kpt, V_ROWS, KEY_TILE), lambda i, j: (i, 0, j, 0, 0)),
                   pl.BlockSpec((1, G, tm, LANES), lambda i, j: (i, 0, j, 0)),
                   pl.BlockSpec((1, G, kpt, V_ROWS, KEY_TILE), lambda i, j: (i, 0, j, 0, 0)),
                   pl.BlockSpec((1, N_GATE, tm), lambda i, j: (i, 0, j)),
                   pl.BlockSpec((1, tm, n_rest), lambda i, j: (i, j, 0))],
        out_shape=[jax.ShapeDtypeStruct((b, D_ATTN, s), BF16),
                   jax.ShapeDtypeStruct((2, b, nch, cw), F32),
                   jax.ShapeDtypeStruct((b, G, s, kd), BF16),
                   jax.ShapeDtypeStruct((b, G, nkt, V_ROWS, KEY_TILE), BF16),
                   jax.ShapeDtypeStruct((b, G, s, LANES), BF16),
                   jax.ShapeDtypeStruct((b, G, nkt, V_ROWS, KEY_TILE), BF16),
                   jax.ShapeDtypeStruct((b, N_GATE, s), F32),
                   jax.ShapeDtypeStruct((b, s, n_rest), BF16)],
        scratch_shapes=[pltpu.VMEM((2, tm, D_KV), F32),
                        pltpu.VMEM((_C_END, D_MODEL), BF16)],
        compiler_params=_cparams(("arbitrary", "arbitrary")),
        name="in_proj",
    )(x, norm_w, w_in_t)
    gs = gsT.reshape(b, 3, G, R, s)

    lhs_rows = V_ROWS + nsel
    lhs_tail = jnp.asarray(np.concatenate([np.ones((V_ROWS - HEAD_DIM, nch), np.float32),
                                           _overlap_t(nsel, nch)], axis=0), BF16)
    kc_aug, c_lhs = pl.pallas_call(
        partial(_compress_kernel, nch=nch),
        grid=(2, b),
        in_specs=[pl.BlockSpec((1, 1, nch, cw), lambda xk, i: (xk, i, 0, 0)),
                  pl.BlockSpec((1, 2, 1, cw), lambda xk, i: (xk, 0, 0, 0)),
                  pl.BlockSpec((1, 2, cw, G * CMP_HIDDEN), lambda xk, i: (xk, 0, 0, 0)),
                  pl.BlockSpec((1, G * CMP_HIDDEN, D_KV), lambda xk, i: (xk, 0, 0)),
                  pl.BlockSpec((lhs_rows - HEAD_DIM, nch), lambda xk, i: (0, 0))],
        out_specs=[pl.BlockSpec((1, 1, G, nch, LANES), lambda xk, i: (xk, i, 0, 0, 0)),
                   pl.BlockSpec((1, 1, G, lhs_rows, nch), lambda xk, i: (xk, i, 0, 0, 0))],
        out_shape=[jax.ShapeDtypeStruct((2, b, G, nch, LANES), BF16),
                   jax.ShapeDtypeStruct((2, b, G, lhs_rows, nch), BF16)],
        compiler_params=_cparams(("parallel", "parallel")),
        name="compress",
    )(kvc, pee, w1e, w2e, lhs_tail)

    tq3 = Q_TILE_CMP
    cpq = tq3 // CMP_STRIDE
    assert nch // SUBLANES <= LANES - HEAD_DIM and k_top > N_FORCED and (nch // 2) % LANES == 0
    ocT, selb = pl.pallas_call(
        partial(_cmp_kernel, nch=nch, nsel=nsel, tq=tq3, k_top=k_top, nqt=s // tq3),
        grid=(b, G, s // tq3),
        in_specs=[smem,
                  pl.BlockSpec((1, R * HEAD_DIM, tq3), lambda i, g, t: (i, g, t)),
                  pl.BlockSpec((1, 1, 1, nch, LANES), lambda i, g, t: (0, i, g, 0, 0)),
                  pl.BlockSpec((1, 1, 1, lhs_rows, nch), lambda i, g, t: (1, i, g, 0, 0)),
                  pl.BlockSpec((1, 1, 1, R, tq3), lambda i, g, t: (i, 0, g, 0, t))],
        out_specs=[pl.BlockSpec((1, R * HEAD_DIM, tq3), lambda i, g, t: (i, g, t)),
                   pl.BlockSpec((1, 1, s // tq3, nsel, tq3), lambda i, g, t: (i, g, 0, 0, 0))],
        out_shape=[jax.ShapeDtypeStruct((b, D_ATTN, s), BF16),
                   jax.ShapeDtypeStruct((b, G, s // tq3, nsel, tq3), BF16)],
        scratch_shapes=[pltpu.VMEM((LANES, R * tq3), BF16),
                        pltpu.VMEM((cpq + BAND_BACK, R * tq3), F32),
                        pltpu.VMEM((lhs_rows, R * tq3), F32),
                        pltpu.VMEM((nsel, tq3), F32),
                        pltpu.VMEM((1, tq3), F32)]
                       + [pltpu.VMEM((nch, tq3), F32)] * R,
        compiler_params=_cparams(("parallel", "parallel", "arbitrary")),
        name="cmp_select",
    )(rel_bias, qT, kc_aug, c_lhs, gs)

    tq = Q_TILE
    nq = R * tq

    o_spec = pl.BlockSpec((1, R * HEAD_DIM, tq), lambda i, g, t: (i, g, t))
    k_spec = lambda kdim: pl.BlockSpec((1, 1, s, kdim), lambda i, g, t: (i, g, 0, 0))
    v_spec = pl.BlockSpec((1, 1, nkt, V_ROWS, KEY_TILE), lambda i, g, t: (i, g, 0, 0, 0))
    gs_spec = lambda branch: pl.BlockSpec((1, 1, 1, R, tq), lambda i, g, t: (i, branch, g, 0, t))
    oswT = pl.pallas_call(
        partial(_branches_kernel, tq=tq, nsel=nsel, kd=kd),
        grid=(b, G, s // tq),
        in_specs=[smem, o_spec,
                  pl.BlockSpec((1, 1, 1, nsel, tq), lambda i, g, t: (i, g, t // (tq3 // tq), 0, t % (tq3 // tq))),
                  k_spec(kd), v_spec, k_spec(LANES), v_spec, gs_spec(1), gs_spec(2)],
        out_specs=o_spec,
        out_shape=jax.ShapeDtypeStruct((b, D_ATTN, s), BF16),
        scratch_shapes=[pltpu.VMEM(((SEL_TBL_TILES + WIN_TBL_TILES) * tq, nq), F32),
                        pltpu.VMEM((kd, nq), BF16),
                        pltpu.VMEM((tq, nq), F32),
                        pltpu.VMEM((1, nq), F32)]
                       + ([pltpu.VMEM((1, nq), F32),
                           pltpu.VMEM((V_ROWS, nq), F32),
                           pltpu.VMEM((1, nq), F32)]
                          + [pltpu.VMEM((1, nq), F32)] * N_SLOTS
                          + [pltpu.VMEM((tq, nq), BF16)] * N_SLOTS) * 2,
        compiler_params=_cparams(("parallel", "parallel", "arbitrary")),
        name="sel_win_attn",
    )(rel_bias, qT, selb, ksel, vselT, kwin, vwinT, gs, gs)

    tm, nt = OUT_TILE, s // OUT_TILE
    hb = tm // HALO_ROWS
    col = lambda c: pl.BlockSpec((1, tm, D_CONV), lambda i, j, c=c: (i, j, c))
    halo = lambda c: pl.BlockSpec((1, HALO_ROWS, D_CONV), lambda i, j, c=c: (i, jnp.maximum(j * hb - 1, 0), c))
    oT = pl.BlockSpec((1, D_ATTN, tm), lambda i, j: (i, 0, j))
    out = pl.pallas_call(
        partial(_out_kernel, tm=tm),
        grid=(b, nt),
        in_specs=[oT, oT, col(0), col(1), col(2), col(3), halo(1),
                  pl.BlockSpec((1, tm, D_MODEL), lambda i, j: (i, j, 0)),
                  pl.BlockSpec((3, D_CONV), lambda i, j: (0, 0)),
                  pl.BlockSpec((D_MODEL, D_MODEL), lambda i, j: (0, 0)),
                  pl.BlockSpec((1, D_MODEL), lambda i, j: (0, 0))],
        out_specs=pl.BlockSpec((1, tm, D_MODEL), lambda i, j: (i, j, 0)),
        out_shape=jax.ShapeDtypeStruct((b, s, D_MODEL), F32),
        compiler_params=_cparams(("parallel", "parallel")),
        name="out_proj",
    )(ocT, oswT, rest, rest, rest, rest, rest, x, conv_w[0], w_out[0].astype(BF16),
      final_norm_w.reshape(1, D_MODEL))
    return out
```

```python
import math
from functools import partial
from types import SimpleNamespace

import numpy as np
import jax
import jax.numpy as jnp
from jax import lax
from jax.experimental import pallas as pl
from jax.experimental.pallas import tpu as pltpu

D_MODEL = 1024
N_HEADS = 8
HEAD_DIM = 64
N_KV_HEADS = 2
GQA_REP = N_HEADS // N_KV_HEADS
D_ATTN = N_HEADS * HEAD_DIM
D_KV = N_KV_HEADS * HEAD_DIM
D_CONV = D_MODEL - D_ATTN
CMP_BLOCK = 32
CMP_STRIDE = 16
CMP_HIDDEN = 2 * HEAD_DIM
SEL_BLOCK = 64
N_SEL = 16
WINDOW = 512
N_BUCKETS = 32
MAX_DISTANCE = 128
EPS = 1e-6
NEG = -1e30

LOG2E = 1.4426950408889634
LANES = 128
SUBLANES = 8
BAND_BACK = -(-(-(-(MAX_DISTANCE + CMP_BLOCK - 1) // CMP_STRIDE)) // SUBLANES) * SUBLANES
TAKEN = -2.0
NOT_CAUSAL = -1.0
N_FORCED = 3
V_ROWS = HEAD_DIM + 16
ROW_TILE = 512
OUT_TILE = 1024
KEY_TILE = 256
Q_TILE_CMP = 512
Q_TILE = 256
HALO_ROWS = 16
MAX_BLOCK_PAIRS = 8
MAX_LAG = 64.0
N_SLOTS = 2
VMEM_LIMIT = 48 * 1024 * 1024

F32 = jnp.float32
BF16 = jnp.bfloat16

_C_Q = 0
_C_CMP = D_ATTN
_C_SW = _C_CMP + 2 * D_KV
_C_REST = _C_SW + 4 * D_KV
_C_GATE = _C_REST + D_ATTN + 4 * D_CONV
_C_END = _C_GATE + LANES
N_GATE = 3 * N_HEADS


def _bucket_lower_bounds():
    max_exact = N_BUCKETS // 2
    d = np.arange(2 * MAX_DISTANCE, dtype=np.int32)
    nf = np.maximum(d, 1).astype(np.float32)
    large = max_exact + (np.log(nf / np.float32(max_exact)) / np.float32(math.log(MAX_DISTANCE / max_exact))
                         * np.float32(N_BUCKETS - max_exact)).astype(np.int32)
    bucket = np.where(d < max_exact, d, np.minimum(large, N_BUCKETS - 1))
    assert np.all(np.diff(bucket) >= 0) and np.all(bucket[MAX_DISTANCE:] == N_BUCKETS - 1)
    return [int(np.argmax(bucket >= b)) for b in range(N_BUCKETS)]


_BUCKET_LO = _bucket_lower_bounds()


def _rel_bias_delta(dist, bias_of_bucket):
    last = bias_of_bucket(N_BUCKETS - 1)
    val = jnp.full(dist.shape, (bias_of_bucket(0) - last) * LOG2E, F32)
    for b in range(1, N_BUCKETS - 1):
        val = jnp.where(dist >= _BUCKET_LO[b], (bias_of_bucket(b) - last) * LOG2E, val)
    val = jnp.where(dist >= _BUCKET_LO[N_BUCKETS - 1], 0.0, val)
    return jnp.where(dist < 0, NEG, val)


def _silu(z):
    return z * (1.0 / (1.0 + jnp.exp2(z * -LOG2E)))


def _proj_kernel(x_ref, nw_ref, wT_ref, qT_ref, kvc_ref, ksel_ref, vselT_ref, kwin_ref, vwinT_ref,
                 gsT_ref, rest_ref, kv_scr, w_scr, *, tm, nsel, kd):
    st = pl.program_id(1)

    @pl.when((pl.program_id(0) == 0) & (st == 0))
    def _():
        c0 = _C_REST
        chunk = KEY_TILE
        for dst, src, rows in ((0, 0, c0), (c0, c0 + N_GATE, _C_GATE - c0)):
            for r in range(0, rows, chunk):
                w_scr[dst + r:dst + r + chunk, :] = wT_ref[src + r:src + r + chunk, :].astype(BF16)
        w_scr[_C_GATE:_C_END, :] = jnp.concatenate(
            [wT_ref[c0:c0 + N_GATE, :], jnp.zeros((LANES - N_GATE, D_MODEL), F32)], axis=0).astype(BF16)

    x = x_ref[0]
    ms = jnp.mean(x * x, axis=-1, keepdims=True)
    h = (x * lax.rsqrt(ms + EPS) * nw_ref[...]).astype(BF16)

    def mm(lo, hi):
        return lax.dot_general(h, w_scr[lo:hi, :], (((1,), (1,)), ((), ())), preferred_element_type=F32)

    q = mm(_C_Q, _C_CMP) * (HEAD_DIM ** -0.5 * LOG2E)
    qT_ref[0] = q.T.astype(BF16)

    kvc = mm(_C_CMP, _C_SW)
    for xk in range(2):
        kv_scr[xk] = kvc[:, xk * D_KV:(xk + 1) * D_KV]
        for j in range(CMP_STRIDE):
            kvc_ref[xk, 0, :, j * D_KV:(j + 1) * D_KV] = kv_scr[xk, pl.ds(j, tm // CMP_STRIDE, stride=CMP_STRIDE), :]

    a = mm(_C_SW, _C_REST)
    pos = st * tm + lax.broadcasted_iota(jnp.int32, (tm, nsel), 0)
    col = lax.broadcasted_iota(jnp.int32, (tm, nsel), 1)
    onehot = jnp.where((pos // SEL_BLOCK) == col, 1.0, 0.0).astype(BF16)
    ones = jnp.ones((V_ROWS - HEAD_DIM, tm), BF16)
    zeros_k = jnp.zeros((tm, LANES - HEAD_DIM), BF16)
    vsT = a[:, D_KV:2 * D_KV].T
    vwT = a[:, 3 * D_KV:4 * D_KV].T
    for g in range(N_KV_HEADS):
        lo, hi = g * HEAD_DIM, (g + 1) * HEAD_DIM
        parts = [a[:, lo:hi].astype(BF16), onehot]
        if kd > HEAD_DIM + nsel:
            parts.append(jnp.zeros((tm, kd - HEAD_DIM - nsel), BF16))
        ksel_ref[0, g] = jnp.concatenate(parts, axis=1)
        kwin_ref[0, g] = jnp.concatenate([a[:, 2 * D_KV + lo:2 * D_KV + hi].astype(BF16), zeros_k], axis=1)
        vs = jnp.concatenate([vsT[lo:hi].astype(BF16), ones], axis=0)
        vw = jnp.concatenate([vwT[lo:hi].astype(BF16), ones], axis=0)
        for kt in range(tm // KEY_TILE):
            vselT_ref[0, g, kt] = vs[:, kt * KEY_TILE:(kt + 1) * KEY_TILE]
            vwinT_ref[0, g, kt] = vw[:, kt * KEY_TILE:(kt + 1) * KEY_TILE]

    r = mm(_C_REST, _C_GATE)
    z_attn, conv_h, conv_b, conv_c, z_conv = (r[:, i * D_CONV:(i + 1) * D_CONV] for i in range(5))
    rest_ref[0] = jnp.concatenate([z_attn, conv_c * conv_h, conv_b, z_conv], axis=1).astype(BF16)
    gates = jax.nn.sigmoid(mm(_C_GATE, _C_END))
    gsT_ref[0] = gates.T[:N_GATE]


def _compress_kernel(c_ref, pe_ref, w1_ref, w2_ref, lhs_tail_ref, aug_ref, t_ref, w1_scr, *, nch):
    @pl.when(pl.program_id(1) == 0)
    def _():
        w1_scr[...] = jnp.zeros(w1_scr.shape, BF16)
        for m in range(CMP_BLOCK // CMP_STRIDE):
            for j in range(CMP_STRIDE):
                blk = w1_ref[0, m * CMP_STRIDE + j].astype(BF16)
                for g in range(N_KV_HEADS):
                    r0 = (j * N_KV_HEADS + g) * HEAD_DIM
                    w1_scr[m, r0:r0 + HEAD_DIM, g * CMP_HIDDEN:(g + 1) * CMP_HIDDEN] = blk

    c = c_ref[0, 0]
    a0 = jnp.dot((c + pe_ref[0, 0]).astype(BF16), w1_scr[0], preferred_element_type=F32)
    a1 = jnp.dot((c + pe_ref[0, 1]).astype(BF16), w1_scr[1], preferred_element_type=F32)
    hid = a0 + pltpu.roll(a1, nch - 1, 0)
    out = jnp.dot(_silu(hid).astype(BF16), w2_ref[0], preferred_element_type=F32)
    grp = lax.broadcasted_iota(jnp.int32, (nch, LANES - HEAD_DIM), 0) // SUBLANES
    col = lax.broadcasted_iota(jnp.int32, (nch, LANES - HEAD_DIM), 1)
    onehot = jnp.where(grp == col, 1.0, 0.0).astype(BF16)
    out_t = out.T.astype(BF16)
    for g in range(N_KV_HEADS):
        lo, hi = g * HEAD_DIM, (g + 1) * HEAD_DIM
        aug_ref[0, 0, g] = jnp.concatenate([out[:, lo:hi].astype(BF16), onehot], axis=1)
        t_ref[0, 0, g] = jnp.concatenate([out_t[lo:hi], lhs_tail_ref[...]], axis=0)


def _cmp_kernel(rb_ref, qT_ref, kc_ref, lhs_ref, gs_ref, ocT_ref, selb_ref,
                qa_ref, band_ref, acc_ref, score_ref, ntk_ref, s0_ref, s1_ref, s2_ref, s3_ref,
                *, nch, nsel, tq, k_top, nqt):
    g = pl.program_id(1)
    qt = pl.program_id(2)
    s_refs = (s0_ref, s1_ref, s2_ref, s3_ref)
    cpq = tq // CMP_STRIDE
    win = cpq + BAND_BACK
    lim = cpq * (qt + 1)
    nq = GQA_REP * tq

    @pl.when(qt <= 1)
    def _():
        row = lax.broadcasted_iota(jnp.int32, (win, tq), 0) + jnp.maximum(cpq * qt - BAND_BACK, 0)
        lane = lax.broadcasted_iota(jnp.int32, (win, tq), 1)
        dist = (qt * tq + lane) - (row * CMP_STRIDE + CMP_BLOCK - 1)
        for r in range(GQA_REP):
            band_ref[:, r * tq:(r + 1) * tq] = _rel_bias_delta(dist, lambda b, r=r: rb_ref[b, g * GQA_REP + r])

    q = qT_ref[0]
    qa_ref[0:HEAD_DIM, :] = jnp.concatenate([q[r * HEAD_DIM:(r + 1) * HEAD_DIM] for r in range(GQA_REP)], axis=1)
    grp = lax.broadcasted_iota(jnp.int32, (LANES - HEAD_DIM, nq), 0)
    qa_ref[HEAD_DIM:LANES, :] = jnp.where(grp * SUBLANES >= lim, NEG, 0.0).astype(BF16)

    def run(rows):
        start = pl.multiple_of(jnp.clip(cpq * qt - BAND_BACK, 0, rows - win), SUBLANES)
        k = kc_ref[0, 0, 0, 0:rows, :]
        lhs = lhs_ref[0, 0, 0, :, 0:rows]
        cols = [slice(r * tq, (r + 1) * tq) for r in range(GQA_REP)]

        def qk(r):
            s_refs[r][0:rows, :] = jnp.dot(k, qa_ref[:, cols[r]], preferred_element_type=F32)

        ahead = 3
        for r in range(ahead):
            qk(r)
        for r in range(GQA_REP):
            if r + ahead < GQA_REP:
                qk(r + ahead)
            sr = s_refs[r]
            sr[pl.ds(start, win), :] = sr[pl.ds(start, win), :] + band_ref[:, cols[r]]
            s = sr[0:rows, :]
            p = jnp.exp2(s - jnp.max(s, axis=0, keepdims=True)).astype(BF16)
            acc_ref[:, cols[r]] = jnp.dot(lhs, p, preferred_element_type=F32)

    n = lax.broadcasted_iota(jnp.int32, (nsel, tq), 0)
    lane = lax.broadcasted_iota(jnp.int32, (nsel, tq), 1)

    def forced_and_candidate(tile):
        blk = (tile * tq + lane) // SEL_BLOCK
        forced = (n == 0) | (n == blk) | (n == blk - 1)
        return forced, (n <= blk) & jnp.logical_not(forced), n <= blk

    def select(tile):
        score = score_ref[...]
        for _ in range(k_top - N_FORCED):
            score = jnp.where(score == jnp.max(score, axis=0, keepdims=True), TAKEN, score)
        taken = score == TAKEN
        selb_ref[0, 0, jnp.maximum(tile, 0)] = jnp.where(taken, 0.0, NEG).astype(BF16)
        ntk_ref[...] = jnp.sum(jnp.where(taken & forced_and_candidate(tile)[1], 1.0, 0.0), axis=0, keepdims=True)

    def select_exact_if_tied(tile):
        @pl.when(jnp.max(ntk_ref[...]) > k_top - N_FORCED)
        def _():
            def pick_one(_, score):
                mx = jnp.max(score, axis=0, keepdims=True)
                idx = jnp.min(jnp.where(score == mx, n, nsel), axis=0, keepdims=True)
                return jnp.where(n == idx, TAKEN, score)

            taken = lax.fori_loop(0, k_top - N_FORCED, pick_one, score_ref[...]) == TAKEN
            selb_ref[0, 0, jnp.maximum(tile, 0)] = jnp.where(taken, 0.0, NEG).astype(BF16)

    @pl.when(qt == 0)
    def _():
        score_ref[...] = jnp.full((nsel, tq), TAKEN, F32)

    half = nch // 2

    @pl.when(lim <= half)
    def _():
        run(half)
        select(qt - 1)

    @pl.when(lim > half)
    def _():
        run(nch)
        select(qt - 1)

    select_exact_if_tied(qt - 1)

    l = acc_ref[HEAD_DIM:HEAD_DIM + 1, :]
    t4 = qt * tq + (lax.broadcasted_iota(jnp.int32, l.shape, 1) & (tq - 1))
    inv = jnp.where(t4 >= CMP_BLOCK - 1, 1.0 / jnp.maximum(l, 1e-30), 0.0)
    imp = None
    for r in range(GQA_REP):
        cs = slice(r * tq, (r + 1) * tq)
        ocT_ref[0, r * HEAD_DIM:(r + 1) * HEAD_DIM, :] = (acc_ref[0:HEAD_DIM, cs] * inv[:, cs]
                                                          * gs_ref[0, 0, 0, r:r + 1, :]).astype(BF16)
        part = acc_ref[V_ROWS:V_ROWS + nsel, cs] * inv[:, cs]
        imp = part if imp is None else imp + part

    forced, _, causal = forced_and_candidate(qt)
    score_ref[...] = jnp.where(forced, TAKEN, jnp.where(causal, imp, NOT_CAUSAL))

    @pl.when(qt == nqt - 1)
    def _():
        select(qt)
        select_exact_if_tied(qt)


N_BRANCH_STATE = 3 + 2 * N_SLOTS


def _branch_stages(rb_ref, k_ref, vT_ref, tbl_ref, qa_ref, s_ref, mt_ref, m_ref, acc_ref, lag_ref, *slot_refs,
                   tq, n_tbl_tiles, window):
    al_ref, p_ref = slot_refs[0:N_SLOTS], slot_refs[N_SLOTS:2 * N_SLOTS]
    g = pl.program_id(1)
    qt = pl.program_id(2)
    nq = GQA_REP * tq

    @pl.when(qt == 0)
    def _():
        for j in range(n_tbl_tiles):
            key = lax.broadcasted_iota(jnp.int32, (tq, tq), 0) + j * tq
            qry = lax.broadcasted_iota(jnp.int32, (tq, tq), 1) + (n_tbl_tiles - 1) * tq
            dist = qry - key
            for r in range(GQA_REP):
                val = _rel_bias_delta(dist, lambda b, r=r: rb_ref[b, g * GQA_REP + r])
                if window is not None:
                    val = jnp.where(dist >= window, NEG, val)
                tbl_ref[j * tq:(j + 1) * tq, r * tq:(r + 1) * tq] = val

    def reset():
        m_ref[...] = jnp.full((1, nq), NEG, F32)
        acc_ref[...] = jnp.zeros((V_ROWS, nq), F32)

    reset()

    def serial(t, tbl_tile):
        k = k_ref[0, 0, pl.ds(pl.multiple_of(t * tq, tq), tq), :]
        s = jnp.dot(k, qa_ref[...], preferred_element_type=F32)
        if tbl_tile is not None:
            s = s + tbl_ref[tbl_tile * tq:(tbl_tile + 1) * tq, :]
        s_ref[...] = s
        mt_ref[...] = jnp.max(s, axis=0, keepdims=True)
        m_old = m_ref[...]
        m_new = jnp.maximum(m_old, mt_ref[...])
        m_ref[...] = m_new
        p_ref[0][...] = jnp.exp2(s_ref[...] - m_new).astype(BF16)
        acc_ref[...] = jnp.exp2(m_old - m_new) * acc_ref[...] + jnp.dot(vT_ref[0, 0, t], p_ref[0][...],
                                                                        preferred_element_type=F32)

    def stage_lagged(t, slot, tbl_tile, first=False):
        k = k_ref[0, 0, pl.ds(pl.multiple_of(t * tq, tq), tq), :]
        s = jnp.dot(k, qa_ref[...], preferred_element_type=F32)
        if tbl_tile is not None:
            s = s + tbl_ref[tbl_tile * tq:(tbl_tile + 1) * tq, :]
        m = s[0:1, :] if first else m_ref[...]
        mt = jnp.max(s, axis=0, keepdims=True)
        p_ref[slot][...] = jnp.exp2(s - m).astype(BF16)
        m_new = jnp.maximum(m, mt)
        m_ref[...] = m_new
        al_ref[slot][...] = jnp.exp2(m - m_new)
        lag_ref[...] = (mt - m) if first else jnp.maximum(lag_ref[...], mt - m)

    def stage_pv_lagged(t, slot):
        acc_ref[...] = (acc_ref[...] + jnp.dot(vT_ref[0, 0, t], p_ref[slot][...],
                                               preferred_element_type=F32)) * al_ref[slot][...]

    def lagged_step(t, slot, tbl_tile, prev=None):
        stage_lagged(t, slot, tbl_tile)
        stage_pv_lagged(t - 1 if prev is None else prev, 1 - slot)

    def exceeded():
        return jnp.max(lag_ref[...]) > MAX_LAG

    def gated_output(gs_ref):
        acc = acc_ref[...]
        o = acc[0:HEAD_DIM] / jnp.maximum(acc[HEAD_DIM:HEAD_DIM + 1], 1e-30)
        return [o[:, r * tq:(r + 1) * tq] * gs_ref[0, 0, 0, r:r + 1, :] for r in range(GQA_REP)]

    return SimpleNamespace(reset=reset, serial=serial, stage_lagged=stage_lagged, stage_pv_lagged=stage_pv_lagged,
                           lagged_step=lagged_step, exceeded=exceeded, gated_output=gated_output)


SEL_TBL_TILES = 2
WIN_TBL_TILES = 3


def _branches_kernel(rb_ref, qT_ref, selb_ref, ksel_ref, vselT_ref, kwin_ref, vwinT_ref, gs_sel_ref, gs_win_ref,
                     o_ref, tbl_ref, qa_ref, s_ref, mt_ref, *state, tq, nsel, kd):
    qt = pl.program_id(2)
    n = qt + 1
    nq = GQA_REP * tq

    q = qT_ref[0]
    qa_ref[0:HEAD_DIM, :] = jnp.concatenate([q[r * HEAD_DIM:(r + 1) * HEAD_DIM] for r in range(GQA_REP)], axis=1)
    qa_ref[HEAD_DIM:HEAD_DIM + nsel, :] = jnp.concatenate([selb_ref[0, 0, 0]] * GQA_REP, axis=1)
    if kd > HEAD_DIM + nsel:
        qa_ref[HEAD_DIM + nsel:kd, :] = jnp.zeros((kd - HEAD_DIM - nsel, nq), BF16)

    sel = _branch_stages(rb_ref, ksel_ref, vselT_ref, tbl_ref.at[0:SEL_TBL_TILES * tq], qa_ref, s_ref, mt_ref,
                         *state[:N_BRANCH_STATE], tq=tq, n_tbl_tiles=SEL_TBL_TILES, window=None)
    win = _branch_stages(rb_ref, kwin_ref, vwinT_ref,
                         tbl_ref.at[SEL_TBL_TILES * tq:(SEL_TBL_TILES + WIN_TBL_TILES) * tq], qa_ref.at[0:LANES],
                         s_ref, mt_ref, *state[N_BRANCH_STATE:], tq=tq, n_tbl_tiles=WIN_TBL_TILES, window=WINDOW)

    @pl.when((n >= 4) & ((n & 1) == 0))
    def _():
        sel.stage_lagged(0, 1, None, first=True)
        sel.lagged_step(1, 0, None)

    @pl.when((n >= 4) & ((n & 1) == 1))
    def _():
        sel.stage_lagged(0, 0, None, first=True)

    @pl.when(n >= 4)
    def _():
        t0 = 2 - (n & 1)
        pairs = (n - 2 - t0) >> 1

        def pair(t):
            sel.lagged_step(t, 1, None)
            sel.lagged_step(t + 1, 0, None)

        big = MAX_BLOCK_PAIRS

        def body(j, carry):
            for i in range(big):
                pair(t0 + 2 * (big * j + i))
            return carry
        lax.fori_loop(0, pairs >> (big.bit_length() - 1), body, 0)

        bp = big // 2
        while bp >= 1:
            @pl.when((pairs & bp) == bp)
            def _(bp=bp):
                t = n - 2 - 2 * (pairs & (2 * bp - 1))
                for i in range(bp):
                    pair(t + 2 * i)
            bp //= 2

    def window_tiles(after_first=lambda: None):
        win.stage_lagged(qt, 0, 2, first=True)
        after_first()
        win.lagged_step(qt - 1, 1, 1, prev=qt)
        win.lagged_step(qt - 2, 0, 0, prev=qt - 1)
        win.stage_pv_lagged(qt - 2, 0)

    def window_exact_if_exceeded():
        @pl.when(win.exceeded())
        def _():
            win.reset()
            for j in range(WIN_TBL_TILES):
                win.serial(qt - (WIN_TBL_TILES - 1) + j, j)

    @pl.when(n >= 4)
    def _():
        sel.lagged_step(n - 2, 1, 0)
        sel.lagged_step(n - 1, 0, 1)
        window_tiles(after_first=lambda: sel.stage_pv_lagged(n - 1, 0))

        @pl.when(sel.exceeded())
        def _():
            sel.reset()

            def redo(t, carry):
                sel.serial(t, None)
                return carry
            lax.fori_loop(0, n - 2, redo, 0)
            sel.serial(n - 2, 0)
            sel.serial(n - 1, 1)

        window_exact_if_exceeded()

    @pl.when(n == 3)
    def _():
        sel.serial(0, None)

    @pl.when((n >= 2) & (n < 4))
    def _():
        sel.serial(qt - 1, 0)

    @pl.when(n < 4)
    def _():
        sel.serial(qt, 1)

    @pl.when(n == 3)
    def _():
        window_tiles()
        window_exact_if_exceeded()

    @pl.when(qt == 1)
    def _():
        win.serial(0, 1)

    @pl.when(qt < WIN_TBL_TILES - 1)
    def _():
        win.serial(qt, 2)

    for r, (so, wo) in enumerate(zip(sel.gated_output(gs_sel_ref), win.gated_output(gs_win_ref))):
        o_ref[0, r * HEAD_DIM:(r + 1) * HEAD_DIM, :] = (so + wo).astype(o_ref.dtype)


def _out_kernel(oc_ref, osw_ref, za_ref, u_ref, cb_ref, zc_ref, hu_ref,
                x_ref, cw_ref, wo_ref, fw_ref, out_ref, wo_scr, *, tm):
    st = pl.program_id(1)

    @pl.when((pl.program_id(0) == 0) & (st == 0))
    def _():
        for r in range(0, D_MODEL, KEY_TILE):
            wo_scr[r:r + KEY_TILE, :] = wo_ref[r:r + KEY_TILE, :].astype(BF16)

    f32 = lambda ref: ref[0].astype(F32)
    attn = (f32(oc_ref) + f32(osw_ref)).T * _silu(f32(za_ref))

    u = f32(u_ref)
    halo = jnp.where(st == 0, 0.0, f32(hu_ref))
    row = lax.broadcasted_iota(jnp.int32, u.shape, 0)
    h1, h2 = halo[HALO_ROWS - 1:HALO_ROWS], halo[HALO_ROWS - 2:HALO_ROWS - 1]
    u1 = jnp.where(row == 0, h1, pltpu.roll(u, 1, 0))
    u2 = jnp.where(row == 0, h2, jnp.where(row == 1, h1, pltpu.roll(u, 2, 0)))
    y = cw_ref[0:1] * u2 + cw_ref[1:2] * u1 + cw_ref[2:3] * u
    conv = f32(cb_ref) * y * _silu(f32(zc_ref))

    mixed = jnp.concatenate([attn, conv], axis=1).astype(BF16)
    y2 = x_ref[0] + jnp.dot(mixed, wo_scr[...], preferred_element_type=F32)
    ms = jnp.mean(y2 * y2, axis=-1, keepdims=True)
    out_ref[0] = y2 * lax.rsqrt(ms + EPS) * fw_ref[...]


def _overlap_t(nsel, nch):
    c = np.arange(nch)[None, :] * CMP_STRIDE
    n = np.arange(nsel)[:, None] * SEL_BLOCK
    return ((c < n + SEL_BLOCK) & (c + CMP_BLOCK > n)).astype(np.float32)


def _cparams(sem):
    return pltpu.CompilerParams(dimension_semantics=sem, vmem_limit_bytes=VMEM_LIMIT)


def kernel(x, norm_w, w_in, w_ck1, w_ck2, pe_k, w_cv1, w_cv2, pe_v, conv_w, w_out, rel_bias, final_norm_w):
    b, s, _ = x.shape
    assert norm_w.shape[0] == 1 and s % ROW_TILE == 0 and ROW_TILE % KEY_TILE == 0
    assert WINDOW == 2 * Q_TILE and KEY_TILE == Q_TILE
    tm = ROW_TILE
    nt = s // tm
    nkt = s // KEY_TILE
    kpt = tm // KEY_TILE
    nch = s // CMP_STRIDE
    nsel = s // SEL_BLOCK
    k_top = min(N_SEL, nsel)
    kd = -(-(HEAD_DIM + nsel) // LANES) * LANES
    G, R = N_KV_HEADS, GQA_REP

    w_in_t = jnp.swapaxes(w_in[0], 0, 1)
    assert w_in_t.shape[0] == _C_GATE + N_GATE
    eye = jnp.eye(G, dtype=F32)

    def expand_pe(pe):
        pe = pe.reshape(2, CMP_STRIDE, 1, HEAD_DIM)
        return jnp.broadcast_to(pe, (2, CMP_STRIDE, G, HEAD_DIM)).reshape(2, 1, CMP_STRIDE * D_KV)

    def expand2(w2):
        return jnp.einsum('hd,ab->ahbd', w2, eye).reshape(G * CMP_HIDDEN, D_KV)

    w1 = jnp.concatenate([w_ck1, w_cv1], axis=0)
    pee = jnp.stack([expand_pe(pe_k[0]), expand_pe(pe_v[0])])
    w2e = jnp.stack([expand2(w_ck2[0]), expand2(w_cv2[0])]).astype(BF16)
    rel_bias = rel_bias.astype(F32)
    smem = pl.BlockSpec(memory_space=pltpu.SMEM)

    n_rest = D_ATTN + 3 * D_CONV
    cw = CMP_STRIDE * D_KV
    qT, kvc, ksel, vselT, kwin, vwinT, gsT, rest = pl.pallas_call(
        partial(_proj_kernel, tm=tm, nsel=nsel, kd=kd),
        grid=(b, nt),
        in_specs=[pl.BlockSpec((1, tm, D_MODEL), lambda i, j: (i, j, 0)),
                  pl.BlockSpec((1, D_MODEL), lambda i, j: (0, 0)),
                  pl.BlockSpec((_C_GATE + N_GATE, D_MODEL), lambda i, j: (0, 0), pipeline_mode=pl.Buffered(1))],
        out_specs=[pl.BlockSpec((1, D_ATTN, tm), lambda i, j: (i, 0, j)),
                   pl.BlockSpec((2, 1, tm // CMP_STRIDE, cw), lambda i, j: (0, i, j, 0)),
                   pl.BlockSpec((1, G, tm, kd), lambda i, j: (i, 0, j, 0)),
                   pl.BlockSpec((1, G, kpt, V_ROWS, KEY_TILE), lambda i, j: (i, 0, j, 0, 0)),
                   pl.BlockSpec((1, G, tm, LANES), lambda i, j: (i, 0, j, 0)),
                   pl.BlockSpec((1, G, kpt, V_ROWS, KEY_TILE), lambda i, j: (i, 0, j, 0, 0)),
                   pl.BlockSpec((1, N_GATE, tm), lambda i, j: (i, 0, j)),
                   pl.BlockSpec((1, tm, n_rest), lambda i, j: (i, j, 0))],
        out_shape=[jax.ShapeDtypeStruct((b, D_ATTN, s), BF16),
                   jax.ShapeDtypeStruct((2, b, nch, cw), F32),
                   jax.ShapeDtypeStruct((b, G, s, kd), BF16),
                   jax.ShapeDtypeStruct((b, G, nkt, V_ROWS, KEY_TILE), BF16),
                   jax.ShapeDtypeStruct((b, G, s, LANES), BF16),
                   jax.ShapeDtypeStruct((b, G, nkt, V_ROWS, KEY_TILE), BF16),
                   jax.ShapeDtypeStruct((b, N_GATE, s), F32),
                   jax.ShapeDtypeStruct((b, s, n_rest), BF16)],
        scratch_shapes=[pltpu.VMEM((2, tm, D_KV), F32),
                        pltpu.VMEM((_C_END, D_MODEL), BF16)],
        compiler_params=_cparams(("arbitrary", "arbitrary")),
        name="in_proj",
    )(x, norm_w, w_in_t)
    gs = gsT.reshape(b, 3, G, R, s)

    lhs_rows = V_ROWS + nsel
    lhs_tail = jnp.asarray(np.concatenate([np.ones((V_ROWS - HEAD_DIM, nch), np.float32),
                                           _overlap_t(nsel, nch)], axis=0), BF16)
    kc_aug, c_lhs = pl.pallas_call(
        partial(_compress_kernel, nch=nch),
        grid=(2, b),
        in_specs=[pl.BlockSpec((1, 1, nch, cw), lambda xk, i: (xk, i, 0, 0)),
                  pl.BlockSpec((1, 2, 1, cw), lambda xk, i: (xk, 0, 0, 0)),
                  pl.BlockSpec((1, CMP_BLOCK, HEAD_DIM, CMP_HIDDEN), lambda xk, i: (xk, 0, 0, 0)),
                  pl.BlockSpec((1, G * CMP_HIDDEN, D_KV), lambda xk, i: (xk, 0, 0)),
                  pl.BlockSpec((lhs_rows - HEAD_DIM, nch), lambda xk, i: (0, 0))],
        out_specs=[pl.BlockSpec((1, 1, G, nch, LANES), lambda xk, i: (xk, i, 0, 0, 0)),
                   pl.BlockSpec((1, 1, G, lhs_rows, nch), lambda xk, i: (xk, i, 0, 0, 0))],
        out_shape=[jax.ShapeDtypeStruct((2, b, G, nch, LANES), BF16),
                   jax.ShapeDtypeStruct((2, b, G, lhs_rows, nch), BF16)],
        scratch_shapes=[pltpu.VMEM((CMP_BLOCK // CMP_STRIDE, cw, G * CMP_HIDDEN), BF16)],
        compiler_params=_cparams(("arbitrary", "arbitrary")),
        name="compress",
    )(kvc, pee, w1, w2e, lhs_tail)

    tq3 = Q_TILE_CMP
    cpq = tq3 // CMP_STRIDE
    assert nch // SUBLANES <= LANES - HEAD_DIM and k_top > N_FORCED and (nch // 2) % LANES == 0
    ocT, selb = pl.pallas_call(
        partial(_cmp_kernel, nch=nch, nsel=nsel, tq=tq3, k_top=k_top, nqt=s // tq3),
        grid=(b, G, s // tq3),
        in_specs=[smem,
                  pl.BlockSpec((1, R * HEAD_DIM, tq3), lambda i, g, t: (i, g, t)),
                  pl.BlockSpec((1, 1, 1, nch, LANES), lambda i, g, t: (0, i, g, 0, 0)),
                  pl.BlockSpec((1, 1, 1, lhs_rows, nch), lambda i, g, t: (1, i, g, 0, 0)),
                  pl.BlockSpec((1, 1, 1, R, tq3), lambda i, g, t: (i, 0, g, 0, t))],
        out_specs=[pl.BlockSpec((1, R * HEAD_DIM, tq3), lambda i, g, t: (i, g, t)),
                   pl.BlockSpec((1, 1, s // tq3, nsel, tq3), lambda i, g, t: (i, g, 0, 0, 0))],
        out_shape=[jax.ShapeDtypeStruct((b, D_ATTN, s), BF16),
                   jax.ShapeDtypeStruct((b, G, s // tq3, nsel, tq3), BF16)],
        scratch_shapes=[pltpu.VMEM((LANES, R * tq3), BF16),
                        pltpu.VMEM((cpq + BAND_BACK, R * tq3), F32),
                        pltpu.VMEM((lhs_rows, R * tq3), F32),
                        pltpu.VMEM((nsel, tq3), F32),
                        pltpu.VMEM((1, tq3), F32)]
                       + [pltpu.VMEM((nch, tq3), F32)] * R,
        compiler_params=_cparams(("parallel", "parallel", "arbitrary")),
        name="cmp_select",
    )(rel_bias, qT, kc_aug, c_lhs, gs)

    tq = Q_TILE
    nq = R * tq

    o_spec = pl.BlockSpec((1, R * HEAD_DIM, tq), lambda i, g, t: (i, g, t))
    k_spec = lambda kdim: pl.BlockSpec((1, 1, s, kdim), lambda i, g, t: (i, g, 0, 0))
    v_spec = pl.BlockSpec((1, 1, nkt, V_ROWS, KEY_TILE), lambda i, g, t: (i, g, 0, 0, 0))
    gs_spec = lambda branch: pl.BlockSpec((1, 1, 1, R, tq), lambda i, g, t: (i, branch, g, 0, t))
    oswT = pl.pallas_call(
        partial(_branches_kernel, tq=tq, nsel=nsel, kd=kd),
        grid=(b, G, s // tq),
        in_specs=[smem, o_spec,
                  pl.BlockSpec((1, 1, 1, nsel, tq), lambda i, g, t: (i, g, t // (tq3 // tq), 0, t % (tq3 // tq))),
                  k_spec(kd), v_spec, k_spec(LANES), v_spec, gs_spec(1), gs_spec(2)],
        out_specs=o_spec,
        out_shape=jax.ShapeDtypeStruct((b, D_ATTN, s), BF16),
        scratch_shapes=[pltpu.VMEM(((SEL_TBL_TILES + WIN_TBL_TILES) * tq, nq), F32),
                        pltpu.VMEM((kd, nq), BF16),
                        pltpu.VMEM((tq, nq), F32),
                        pltpu.VMEM((1, nq), F32)]
                       + ([pltpu.VMEM((1, nq), F32),
                           pltpu.VMEM((V_ROWS, nq), F32),
                           pltpu.VMEM((1, nq), F32)]
                          + [pltpu.VMEM((1, nq), F32)] * N_SLOTS
                          + [pltpu.VMEM((tq, nq), BF16)] * N_SLOTS) * 2,
        compiler_params=_cparams(("parallel", "parallel", "arbitrary")),
        name="sel_win_attn",
    )(rel_bias, qT, selb, ksel, vselT, kwin, vwinT, gs, gs)

    tm, nt = OUT_TILE, s // OUT_TILE
    hb = tm // HALO_ROWS
    col = lambda c: pl.BlockSpec((1, tm, D_CONV), lambda i, j, c=c: (i, j, c))
    halo = lambda c: pl.BlockSpec((1, HALO_ROWS, D_CONV), lambda i, j, c=c: (i, jnp.maximum(j * hb - 1, 0), c))
    oT = pl.BlockSpec((1, D_ATTN, tm), lambda i, j: (i, 0, j))
    out = pl.pallas_call(
        partial(_out_kernel, tm=tm),
        grid=(b, nt),
        in_specs=[oT, oT, col(0), col(1), col(2), col(3), halo(1),
                  pl.BlockSpec((1, tm, D_MODEL), lambda i, j: (i, j, 0)),
                  pl.BlockSpec((3, D_CONV), lambda i, j: (0, 0)),
                  pl.BlockSpec((D_MODEL, D_MODEL), lambda i, j: (0, 0), pipeline_mode=pl.Buffered(1)),
                  pl.BlockSpec((1, D_MODEL), lambda i, j: (0, 0))],
        out_specs=pl.BlockSpec((1, tm, D_MODEL), lambda i, j: (i, j, 0)),
        out_shape=jax.ShapeDtypeStruct((b, s, D_MODEL), F32),
        scratch_shapes=[pltpu.VMEM((D_MODEL, D_MODEL), BF16)],
        compiler_params=_cparams(("arbitrary", "arbitrary")),
        name="out_proj",
    )(ocT, oswT, rest, rest, rest, rest, rest, x, conv_w[0], w_out[0],
      final_norm_w.reshape(1, D_MODEL))
    return out
```

```python
import math
from functools import partial
from types import SimpleNamespace

import numpy as np
import jax
import jax.numpy as jnp
from jax import lax
from jax.experimental import pallas as pl
from jax.experimental.pallas import tpu as pltpu

D_MODEL = 1024
N_HEADS = 8
HEAD_DIM = 64
N_KV_HEADS = 2
GQA_REP = N_HEADS // N_KV_HEADS
D_ATTN = N_HEADS * HEAD_DIM
D_KV = N_KV_HEADS * HEAD_DIM
D_CONV = D_MODEL - D_ATTN
CMP_BLOCK = 32
CMP_STRIDE = 16
CMP_HIDDEN = 2 * HEAD_DIM
SEL_BLOCK = 64
N_SEL = 16
WINDOW = 512
N_BUCKETS = 32
MAX_DISTANCE = 128
EPS = 1e-6
NEG = -1e30

LOG2E = 1.4426950408889634
LANES = 128
SUBLANES = 8
BAND_BACK = -(-(-(-(MAX_DISTANCE + CMP_BLOCK - 1) // CMP_STRIDE)) // SUBLANES) * SUBLANES
TAKEN = -2.0
NOT_CAUSAL = -1.0
N_FORCED = 3
V_ROWS = HEAD_DIM + 16
ROW_TILE = 512
OUT_TILE = 1024
KEY_TILE = 256
Q_TILE_CMP = 512
Q_TILE = 256
HALO_ROWS = 16
MAX_BLOCK_PAIRS = 8
MAX_LAG = 64.0
N_SLOTS = 2
VMEM_LIMIT = 48 * 1024 * 1024

F32 = jnp.float32
BF16 = jnp.bfloat16

_C_Q = 0
_C_CMP = D_ATTN
_C_SW = _C_CMP + 2 * D_KV
_C_REST = _C_SW + 4 * D_KV
_C_GATE = _C_REST + D_ATTN + 4 * D_CONV
_C_END = _C_GATE + LANES
N_GATE = 3 * N_HEADS


def _bucket_lower_bounds():
    max_exact = N_BUCKETS // 2
    d = np.arange(2 * MAX_DISTANCE, dtype=np.int32)
    nf = np.maximum(d, 1).astype(np.float32)
    large = max_exact + (np.log(nf / np.float32(max_exact)) / np.float32(math.log(MAX_DISTANCE / max_exact))
                         * np.float32(N_BUCKETS - max_exact)).astype(np.int32)
    bucket = np.where(d < max_exact, d, np.minimum(large, N_BUCKETS - 1))
    assert np.all(np.diff(bucket) >= 0) and np.all(bucket[MAX_DISTANCE:] == N_BUCKETS - 1)
    return [int(np.argmax(bucket >= b)) for b in range(N_BUCKETS)]


_BUCKET_LO = _bucket_lower_bounds()


def _rel_bias_delta(dist, bias_of_bucket):
    last = bias_of_bucket(N_BUCKETS - 1)
    val = jnp.full(dist.shape, (bias_of_bucket(0) - last) * LOG2E, F32)
    for b in range(1, N_BUCKETS - 1):
        val = jnp.where(dist >= _BUCKET_LO[b], (bias_of_bucket(b) - last) * LOG2E, val)
    val = jnp.where(dist >= _BUCKET_LO[N_BUCKETS - 1], 0.0, val)
    return jnp.where(dist < 0, NEG, val)


def _silu(z):
    return z * (1.0 / (1.0 + jnp.exp2(z * -LOG2E)))


def _proj_kernel(x_ref, nw_ref, wT_ref, qT_ref, kvc_ref, ksel_ref, vselT_ref, kwin_ref, vwinT_ref,
                 gsT_ref, rest_ref, kv_scr, w_scr, *, tm, nsel, kd):
    st = pl.program_id(1)

    @pl.when((pl.program_id(0) == 0) & (st == 0))
    def _():
        c0 = _C_REST
        chunk = KEY_TILE
        for dst, src, rows in ((0, 0, c0), (c0, c0 + N_GATE, _C_GATE - c0)):
            for r in range(0, rows, chunk):
                w_scr[dst + r:dst + r + chunk, :] = wT_ref[src + r:src + r + chunk, :].astype(BF16)
        w_scr[_C_GATE:_C_END, :] = jnp.concatenate(
            [wT_ref[c0:c0 + N_GATE, :], jnp.zeros((LANES - N_GATE, D_MODEL), F32)], axis=0).astype(BF16)

    x = x_ref[0]
    ms = jnp.mean(x * x, axis=-1, keepdims=True)
    h = (x * lax.rsqrt(ms + EPS) * nw_ref[...]).astype(BF16)

    def mm(lo, hi):
        return lax.dot_general(h, w_scr[lo:hi, :], (((1,), (1,)), ((), ())), preferred_element_type=F32)

    q = mm(_C_Q, _C_CMP) * (HEAD_DIM ** -0.5 * LOG2E)
    qT_ref[0] = q.T.astype(BF16)

    kvc = mm(_C_CMP, _C_SW)
    for xk in range(2):
        kv_scr[xk] = kvc[:, xk * D_KV:(xk + 1) * D_KV]
        for j in range(CMP_STRIDE):
            kvc_ref[xk, 0, :, j * D_KV:(j + 1) * D_KV] = kv_scr[xk, pl.ds(j, tm // CMP_STRIDE, stride=CMP_STRIDE), :]

    a = mm(_C_SW, _C_REST)
    pos = st * tm + lax.broadcasted_iota(jnp.int32, (tm, nsel), 0)
    col = lax.broadcasted_iota(jnp.int32, (tm, nsel), 1)
    onehot = jnp.where((pos // SEL_BLOCK) == col, 1.0, 0.0).astype(BF16)
    ones = jnp.ones((V_ROWS - HEAD_DIM, tm), BF16)
    zeros_k = jnp.zeros((tm, LANES - HEAD_DIM), BF16)
    vsT = a[:, D_KV:2 * D_KV].T
    vwT = a[:, 3 * D_KV:4 * D_KV].T
    for g in range(N_KV_HEADS):
        lo, hi = g * HEAD_DIM, (g + 1) * HEAD_DIM
        parts = [a[:, lo:hi].astype(BF16), onehot]
        if kd > HEAD_DIM + nsel:
            parts.append(jnp.zeros((tm, kd - HEAD_DIM - nsel), BF16))
        ksel_ref[0, g] = jnp.concatenate(parts, axis=1)
        kwin_ref[0, g] = jnp.concatenate([a[:, 2 * D_KV + lo:2 * D_KV + hi].astype(BF16), zeros_k], axis=1)
        vs = jnp.concatenate([vsT[lo:hi].astype(BF16), ones], axis=0)
        vw = jnp.concatenate([vwT[lo:hi].astype(BF16), ones], axis=0)
        for kt in range(tm // KEY_TILE):
            vselT_ref[0, g, kt] = vs[:, kt * KEY_TILE:(kt + 1) * KEY_TILE]
            vwinT_ref[0, g, kt] = vw[:, kt * KEY_TILE:(kt + 1) * KEY_TILE]

    r = mm(_C_REST, _C_GATE)
    z_attn, conv_h, conv_b, conv_c, z_conv = (r[:, i * D_CONV:(i + 1) * D_CONV] for i in range(5))
    rest_ref[0] = jnp.concatenate([z_attn, conv_c * conv_h, conv_b, z_conv], axis=1).astype(BF16)
    gates = jax.nn.sigmoid(mm(_C_GATE, _C_END))
    gsT_ref[0] = gates.T[:N_GATE]


def _compress_kernel(c_ref, pe_ref, w1_ref, w2_ref, lhs_tail_ref, aug_ref, t_ref, w1_scr, *, nch):
    @pl.when(pl.program_id(1) == 0)
    def _():
        w1_scr[...] = jnp.zeros(w1_scr.shape, BF16)
        for m in range(CMP_BLOCK // CMP_STRIDE):
            for j in range(CMP_STRIDE):
                blk = w1_ref[0, m * CMP_STRIDE + j].astype(BF16)
                for g in range(N_KV_HEADS):
                    r0 = (j * N_KV_HEADS + g) * HEAD_DIM
                    w1_scr[m, r0:r0 + HEAD_DIM, g * CMP_HIDDEN:(g + 1) * CMP_HIDDEN] = blk

    c = c_ref[0, 0]
    a0 = jnp.dot((c + pe_ref[0, 0]).astype(BF16), w1_scr[0], preferred_element_type=F32)
    a1 = jnp.dot((c + pe_ref[0, 1]).astype(BF16), w1_scr[1], preferred_element_type=F32)
    hid = a0 + pltpu.roll(a1, nch - 1, 0)
    out = jnp.dot(_silu(hid).astype(BF16), w2_ref[0], preferred_element_type=F32)
    grp = lax.broadcasted_iota(jnp.int32, (nch, LANES - HEAD_DIM), 0) // SUBLANES
    col = lax.broadcasted_iota(jnp.int32, (nch, LANES - HEAD_DIM), 1)
    onehot = jnp.where(grp == col, 1.0, 0.0).astype(BF16)
    out_t = out.T.astype(BF16)
    for g in range(N_KV_HEADS):
        lo, hi = g * HEAD_DIM, (g + 1) * HEAD_DIM
        aug_ref[0, 0, g] = jnp.concatenate([out[:, lo:hi].astype(BF16), onehot], axis=1)
        t_ref[0, 0, g] = jnp.concatenate([out_t[lo:hi], lhs_tail_ref[...]], axis=0)


def _cmp_kernel(rb_ref, qT_ref, kc_ref, lhs_ref, gs_ref, ocT_ref, selb_ref,
                qa_ref, band_ref, acc_ref, score_ref, ntk_ref, s0_ref, s1_ref, s2_ref, s3_ref,
                *, nch, nsel, tq, k_top, nqt):
    g = pl.program_id(1)
    qt = pl.program_id(2)
    s_refs = (s0_ref, s1_ref, s2_ref, s3_ref)
    cpq = tq // CMP_STRIDE
    win = cpq + BAND_BACK
    lim = cpq * (qt + 1)
    nq = GQA_REP * tq

    @pl.when(qt <= 1)
    def _():
        row = lax.broadcasted_iota(jnp.int32, (win, tq), 0) + jnp.maximum(cpq * qt - BAND_BACK, 0)
        lane = lax.broadcasted_iota(jnp.int32, (win, tq), 1)
        dist = (qt * tq + lane) - (row * CMP_STRIDE + CMP_BLOCK - 1)
        for r in range(GQA_REP):
            band_ref[:, r * tq:(r + 1) * tq] = _rel_bias_delta(dist, lambda b, r=r: rb_ref[b, g * GQA_REP + r])

    q = qT_ref[0]
    qa_ref[0:HEAD_DIM, :] = jnp.concatenate([q[r * HEAD_DIM:(r + 1) * HEAD_DIM] for r in range(GQA_REP)], axis=1)
    grp = lax.broadcasted_iota(jnp.int32, (LANES - HEAD_DIM, nq), 0)
    qa_ref[HEAD_DIM:LANES, :] = jnp.where(grp * SUBLANES >= lim, NEG, 0.0).astype(BF16)

    def run(rows):
        start = pl.multiple_of(jnp.clip(cpq * qt - BAND_BACK, 0, rows - win), SUBLANES)
        k = kc_ref[0, 0, 0, 0:rows, :]
        lhs = lhs_ref[0, 0, 0, :, 0:rows]
        cols = [slice(r * tq, (r + 1) * tq) for r in range(GQA_REP)]

        def qk(r):
            s_refs[r][0:rows, :] = jnp.dot(k, qa_ref[:, cols[r]], preferred_element_type=F32)

        ahead = 3
        for r in range(ahead):
            qk(r)
        for r in range(GQA_REP):
            if r + ahead < GQA_REP:
                qk(r + ahead)
            sr = s_refs[r]
            sr[pl.ds(start, win), :] = sr[pl.ds(start, win), :] + band_ref[:, cols[r]]
            s = sr[0:rows, :]
            p = jnp.exp2(s - jnp.max(s, axis=0, keepdims=True)).astype(BF16)
            acc_ref[:, cols[r]] = jnp.dot(lhs, p, preferred_element_type=F32)

    n = lax.broadcasted_iota(jnp.int32, (nsel, tq), 0)
    lane = lax.broadcasted_iota(jnp.int32, (nsel, tq), 1)

    def forced_and_candidate(tile):
        blk = (tile * tq + lane) // SEL_BLOCK
        forced = (n == 0) | (n == blk) | (n == blk - 1)
        return forced, (n <= blk) & jnp.logical_not(forced), n <= blk

    def select(tile):
        score = score_ref[...]
        for _ in range(k_top - N_FORCED):
            score = jnp.where(score == jnp.max(score, axis=0, keepdims=True), TAKEN, score)
        taken = score == TAKEN
        selb_ref[0, 0, jnp.maximum(tile, 0)] = jnp.where(taken, 0.0, NEG).astype(BF16)
        ntk_ref[...] = jnp.sum(jnp.where(taken & forced_and_candidate(tile)[1], 1.0, 0.0), axis=0, keepdims=True)

    def select_exact_if_tied(tile):
        @pl.when(jnp.max(ntk_ref[...]) > k_top - N_FORCED)
        def _():
            def pick_one(_, score):
                mx = jnp.max(score, axis=0, keepdims=True)
                idx = jnp.min(jnp.where(score == mx, n, nsel), axis=0, keepdims=True)
                return jnp.where(n == idx, TAKEN, score)

            taken = lax.fori_loop(0, k_top - N_FORCED, pick_one, score_ref[...]) == TAKEN
            selb_ref[0, 0, jnp.maximum(tile, 0)] = jnp.where(taken, 0.0, NEG).astype(BF16)

    @pl.when(qt == 0)
    def _():
        score_ref[...] = jnp.full((nsel, tq), TAKEN, F32)

    quarter = nch // 4
    for v in range(1, 5):
        @pl.when((lim > (v - 1) * quarter) & (lim <= v * quarter))
        def _(v=v):
            run(v * quarter)
            select(qt - 1)

    select_exact_if_tied(qt - 1)

    l = acc_ref[HEAD_DIM:HEAD_DIM + 1, :]
    t4 = qt * tq + (lax.broadcasted_iota(jnp.int32, l.shape, 1) & (tq - 1))
    inv = jnp.where(t4 >= CMP_BLOCK - 1, 1.0 / jnp.maximum(l, 1e-30), 0.0)
    imp = None
    for r in range(GQA_REP):
        cs = slice(r * tq, (r + 1) * tq)
        ocT_ref[0, r * HEAD_DIM:(r + 1) * HEAD_DIM, :] = (acc_ref[0:HEAD_DIM, cs] * inv[:, cs]
                                                          * gs_ref[0, 0, 0, r:r + 1, :]).astype(BF16)
        part = acc_ref[V_ROWS:V_ROWS + nsel, cs] * inv[:, cs]
        imp = part if imp is None else imp + part

    forced, _, causal = forced_and_candidate(qt)
    score_ref[...] = jnp.where(forced, TAKEN, jnp.where(causal, imp, NOT_CAUSAL))

    @pl.when(qt == nqt - 1)
    def _():
        select(qt)
        select_exact_if_tied(qt)


N_BRANCH_STATE = 3 + 2 * N_SLOTS


def _branch_stages(rb_ref, k_ref, vT_ref, tbl_ref, qa_ref, s_ref, mt_ref, m_ref, acc_ref, lag_ref, *slot_refs,
                   tq, n_tbl_tiles, window):
    al_ref, p_ref = slot_refs[0:N_SLOTS], slot_refs[N_SLOTS:2 * N_SLOTS]
    g = pl.program_id(1)
    qt = pl.program_id(2)
    nq = GQA_REP * tq

    @pl.when(qt == 0)
    def _():
        for j in range(n_tbl_tiles):
            key = lax.broadcasted_iota(jnp.int32, (tq, tq), 0) + j * tq
            qry = lax.broadcasted_iota(jnp.int32, (tq, tq), 1) + (n_tbl_tiles - 1) * tq
            dist = qry - key
            for r in range(GQA_REP):
                val = _rel_bias_delta(dist, lambda b, r=r: rb_ref[b, g * GQA_REP + r])
                if window is not None:
                    val = jnp.where(dist >= window, NEG, val)
                tbl_ref[j * tq:(j + 1) * tq, r * tq:(r + 1) * tq] = val

    def reset():
        m_ref[...] = jnp.full((1, nq), NEG, F32)
        acc_ref[...] = jnp.zeros((V_ROWS, nq), F32)

    reset()

    def serial(t, tbl_tile):
        k = k_ref[0, 0, pl.ds(pl.multiple_of(t * tq, tq), tq), :]
        s = jnp.dot(k, qa_ref[...], preferred_element_type=F32)
        if tbl_tile is not None:
            s = s + tbl_ref[tbl_tile * tq:(tbl_tile + 1) * tq, :]
        s_ref[...] = s
        mt_ref[...] = jnp.max(s, axis=0, keepdims=True)
        m_old = m_ref[...]
        m_new = jnp.maximum(m_old, mt_ref[...])
        m_ref[...] = m_new
        p_ref[0][...] = jnp.exp2(s_ref[...] - m_new).astype(BF16)
        acc_ref[...] = jnp.exp2(m_old - m_new) * acc_ref[...] + jnp.dot(vT_ref[0, 0, t], p_ref[0][...],
                                                                        preferred_element_type=F32)

    def stage_lagged(t, slot, tbl_tile, first=False):
        k = k_ref[0, 0, pl.ds(pl.multiple_of(t * tq, tq), tq), :]
        s = jnp.dot(k, qa_ref[...], preferred_element_type=F32)
        if tbl_tile is not None:
            s = s + tbl_ref[tbl_tile * tq:(tbl_tile + 1) * tq, :]
        m = s[0:1, :] if first else m_ref[...]
        mt = jnp.max(s, axis=0, keepdims=True)
        p_ref[slot][...] = jnp.exp2(s - m).astype(BF16)
        m_new = jnp.maximum(m, mt)
        m_ref[...] = m_new
        al_ref[slot][...] = jnp.exp2(m - m_new)
        lag_ref[...] = (mt - m) if first else jnp.maximum(lag_ref[...], mt - m)

    def stage_pv_lagged(t, slot):
        acc_ref[...] = (acc_ref[...] + jnp.dot(vT_ref[0, 0, t], p_ref[slot][...],
                                               preferred_element_type=F32)) * al_ref[slot][...]

    def lagged_step(t, slot, tbl_tile, prev=None):
        stage_lagged(t, slot, tbl_tile)
        stage_pv_lagged(t - 1 if prev is None else prev, 1 - slot)

    def exceeded():
        return jnp.max(lag_ref[...]) > MAX_LAG

    def gated_output(gs_ref):
        acc = acc_ref[...]
        o = acc[0:HEAD_DIM] / jnp.maximum(acc[HEAD_DIM:HEAD_DIM + 1], 1e-30)
        return [o[:, r * tq:(r + 1) * tq] * gs_ref[0, 0, 0, r:r + 1, :] for r in range(GQA_REP)]

    return SimpleNamespace(reset=reset, serial=serial, stage_lagged=stage_lagged, stage_pv_lagged=stage_pv_lagged,
                           lagged_step=lagged_step, exceeded=exceeded, gated_output=gated_output)


SEL_TBL_TILES = 2
WIN_TBL_TILES = 3


def _branches_kernel(rb_ref, qT_ref, selb_ref, ksel_ref, vselT_ref, kwin_ref, vwinT_ref, gs_sel_ref, gs_win_ref,
                     o_ref, tbl_ref, qa_ref, s_ref, mt_ref, *state, tq, nsel, kd):
    qt = pl.program_id(2)
    n = qt + 1
    nq = GQA_REP * tq

    q = qT_ref[0]
    qa_ref[0:HEAD_DIM, :] = jnp.concatenate([q[r * HEAD_DIM:(r + 1) * HEAD_DIM] for r in range(GQA_REP)], axis=1)
    qa_ref[HEAD_DIM:HEAD_DIM + nsel, :] = jnp.concatenate([selb_ref[0, 0, 0]] * GQA_REP, axis=1)
    if kd > HEAD_DIM + nsel:
        qa_ref[HEAD_DIM + nsel:kd, :] = jnp.zeros((kd - HEAD_DIM - nsel, nq), BF16)

    sel = _branch_stages(rb_ref, ksel_ref, vselT_ref, tbl_ref.at[0:SEL_TBL_TILES * tq], qa_ref, s_ref, mt_ref,
                         *state[:N_BRANCH_STATE], tq=tq, n_tbl_tiles=SEL_TBL_TILES, window=None)
    win = _branch_stages(rb_ref, kwin_ref, vwinT_ref,
                         tbl_ref.at[SEL_TBL_TILES * tq:(SEL_TBL_TILES + WIN_TBL_TILES) * tq], qa_ref.at[0:LANES],
                         s_ref, mt_ref, *state[N_BRANCH_STATE:], tq=tq, n_tbl_tiles=WIN_TBL_TILES, window=WINDOW)

    @pl.when((n >= 4) & ((n & 1) == 0))
    def _():
        sel.stage_lagged(0, 1, None, first=True)
        sel.lagged_step(1, 0, None)

    @pl.when((n >= 4) & ((n & 1) == 1))
    def _():
        sel.stage_lagged(0, 0, None, first=True)

    @pl.when(n >= 4)
    def _():
        t0 = 2 - (n & 1)
        pairs = (n - 2 - t0) >> 1

        def pair(t):
            sel.lagged_step(t, 1, None)
            sel.lagged_step(t + 1, 0, None)

        big = MAX_BLOCK_PAIRS

        def body(j, carry):
            for i in range(big):
                pair(t0 + 2 * (big * j + i))
            return carry
        lax.fori_loop(0, pairs >> (big.bit_length() - 1), body, 0)

        bp = big // 2
        while bp >= 1:
            @pl.when((pairs & bp) == bp)
            def _(bp=bp):
                t = n - 2 - 2 * (pairs & (2 * bp - 1))
                for i in range(bp):
                    pair(t + 2 * i)
            bp //= 2

    def window_tiles(after_first=lambda: None):
        win.stage_lagged(qt, 0, 2, first=True)
        after_first()
        win.lagged_step(qt - 1, 1, 1, prev=qt)
        win.lagged_step(qt - 2, 0, 0, prev=qt - 1)
        win.stage_pv_lagged(qt - 2, 0)

    def window_exact_if_exceeded():
        @pl.when(win.exceeded())
        def _():
            win.reset()
            for j in range(WIN_TBL_TILES):
                win.serial(qt - (WIN_TBL_TILES - 1) + j, j)

    @pl.when(n >= 4)
    def _():
        sel.lagged_step(n - 2, 1, 0)
        sel.lagged_step(n - 1, 0, 1)
        window_tiles(after_first=lambda: sel.stage_pv_lagged(n - 1, 0))

        @pl.when(sel.exceeded())
        def _():
            sel.reset()

            def redo(t, carry):
                sel.serial(t, None)
                return carry
            lax.fori_loop(0, n - 2, redo, 0)
            sel.serial(n - 2, 0)
            sel.serial(n - 1, 1)

        window_exact_if_exceeded()

    @pl.when(n == 3)
    def _():
        sel.serial(0, None)

    @pl.when((n >= 2) & (n < 4))
    def _():
        sel.serial(qt - 1, 0)

    @pl.when(n < 4)
    def _():
        sel.serial(qt, 1)

    @pl.when(n == 3)
    def _():
        window_tiles()
        window_exact_if_exceeded()

    @pl.when(qt == 1)
    def _():
        win.serial(0, 1)

    @pl.when(qt < WIN_TBL_TILES - 1)
    def _():
        win.serial(qt, 2)

    for r, (so, wo) in enumerate(zip(sel.gated_output(gs_sel_ref), win.gated_output(gs_win_ref))):
        o_ref[0, r * HEAD_DIM:(r + 1) * HEAD_DIM, :] = (so + wo).astype(o_ref.dtype)


def _out_kernel(oc_ref, osw_ref, za_ref, u_ref, cb_ref, zc_ref, hu_ref,
                x_ref, cw_ref, wo_ref, fw_ref, out_ref, wo_scr, *, tm):
    st = pl.program_id(1)

    @pl.when((pl.program_id(0) == 0) & (st == 0))
    def _():
        for r in range(0, D_MODEL, KEY_TILE):
            wo_scr[r:r + KEY_TILE, :] = wo_ref[r:r + KEY_TILE, :].astype(BF16)

    f32 = lambda ref: ref[0].astype(F32)
    attn = (f32(oc_ref) + f32(osw_ref)).T * _silu(f32(za_ref))

    u = f32(u_ref)
    halo = jnp.where(st == 0, 0.0, f32(hu_ref))
    row = lax.broadcasted_iota(jnp.int32, u.shape, 0)
    h1, h2 = halo[HALO_ROWS - 1:HALO_ROWS], halo[HALO_ROWS - 2:HALO_ROWS - 1]
    u1 = jnp.where(row == 0, h1, pltpu.roll(u, 1, 0))
    u2 = jnp.where(row == 0, h2, jnp.where(row == 1, h1, pltpu.roll(u, 2, 0)))
    y = cw_ref[0:1] * u2 + cw_ref[1:2] * u1 + cw_ref[2:3] * u
    conv = f32(cb_ref) * y * _silu(f32(zc_ref))

    mixed = jnp.concatenate([attn, conv], axis=1).astype(BF16)
    y2 = x_ref[0] + jnp.dot(mixed, wo_scr[...], preferred_element_type=F32)
    ms = jnp.mean(y2 * y2, axis=-1, keepdims=True)
    out_ref[0] = y2 * lax.rsqrt(ms + EPS) * fw_ref[...]


def _overlap_t(nsel, nch):
    c = np.arange(nch)[None, :] * CMP_STRIDE
    n = np.arange(nsel)[:, None] * SEL_BLOCK
    return ((c < n + SEL_BLOCK) & (c + CMP_BLOCK > n)).astype(np.float32)


def _cparams(sem):
    return pltpu.CompilerParams(dimension_semantics=sem, vmem_limit_bytes=VMEM_LIMIT)


def kernel(x, norm_w, w_in, w_ck1, w_ck2, pe_k, w_cv1, w_cv2, pe_v, conv_w, w_out, rel_bias, final_norm_w):
    b, s, _ = x.shape
    assert norm_w.shape[0] == 1 and s % ROW_TILE == 0 and ROW_TILE % KEY_TILE == 0
    assert WINDOW == 2 * Q_TILE and KEY_TILE == Q_TILE
    tm = ROW_TILE
    nt = s // tm
    nkt = s // KEY_TILE
    kpt = tm // KEY_TILE
    nch = s // CMP_STRIDE
    nsel = s // SEL_BLOCK
    k_top = min(N_SEL, nsel)
    kd = -(-(HEAD_DIM + nsel) // LANES) * LANES
    G, R = N_KV_HEADS, GQA_REP

    w_in_t = jnp.swapaxes(w_in[0], 0, 1)
    assert w_in_t.shape[0] == _C_GATE + N_GATE
    eye = jnp.eye(G, dtype=F32)

    def expand_pe(pe):
        pe = pe.reshape(2, CMP_STRIDE, 1, HEAD_DIM)
        return jnp.broadcast_to(pe, (2, CMP_STRIDE, G, HEAD_DIM)).reshape(2, 1, CMP_STRIDE * D_KV)

    def expand2(w2):
        return jnp.einsum('hd,ab->ahbd', w2, eye).reshape(G * CMP_HIDDEN, D_KV)

    w1 = jnp.concatenate([w_ck1, w_cv1], axis=0)
    pee = jnp.stack([expand_pe(pe_k[0]), expand_pe(pe_v[0])])
    w2e = jnp.stack([expand2(w_ck2[0]), expand2(w_cv2[0])]).astype(BF16)
    rel_bias = rel_bias.astype(F32)
    smem = pl.BlockSpec(memory_space=pltpu.SMEM)

    n_rest = D_ATTN + 3 * D_CONV
    cw = CMP_STRIDE * D_KV
    qT, kvc, ksel, vselT, kwin, vwinT, gsT, rest = pl.pallas_call(
        partial(_proj_kernel, tm=tm, nsel=nsel, kd=kd),
        grid=(b, nt),
        in_specs=[pl.BlockSpec((1, tm, D_MODEL), lambda i, j: (i, j, 0)),
                  pl.BlockSpec((1, D_MODEL), lambda i, j: (0, 0)),
                  pl.BlockSpec((_C_GATE + N_GATE, D_MODEL), lambda i, j: (0, 0), pipeline_mode=pl.Buffered(1))],
        out_specs=[pl.BlockSpec((1, D_ATTN, tm), lambda i, j: (i, 0, j)),
                   pl.BlockSpec((2, 1, tm // CMP_STRIDE, cw), lambda i, j: (0, i, j, 0)),
                   pl.BlockSpec((1, G, tm, kd), lambda i, j: (i, 0, j, 0)),
                   pl.BlockSpec((1, G, kpt, V_ROWS, KEY_TILE), lambda i, j: (i, 0, j, 0, 0)),
                   pl.BlockSpec((1, G, tm, LANES), lambda i, j: (i, 0, j, 0)),
                   pl.BlockSpec((1, G, kpt, V_ROWS, KEY_TILE), lambda i, j: (i, 0, j, 0, 0)),
                   pl.BlockSpec((1, N_GATE, tm), lambda i, j: (i, 0, j)),
                   pl.BlockSpec((1, tm, n_rest), lambda i, j: (i, j, 0))],
        out_shape=[jax.ShapeDtypeStruct((b, D_ATTN, s), BF16),
                   jax.ShapeDtypeStruct((2, b, nch, cw), F32),
                   jax.ShapeDtypeStruct((b, G, s, kd), BF16),
                   jax.ShapeDtypeStruct((b, G, nkt, V_ROWS, KEY_TILE), BF16),
                   jax.ShapeDtypeStruct((b, G, s, LANES), BF16),
                   jax.ShapeDtypeStruct((b, G, nkt, V_ROWS, KEY_TILE), BF16),
                   jax.ShapeDtypeStruct((b, N_GATE, s), F32),
                   jax.ShapeDtypeStruct((b, s, n_rest), BF16)],
        scratch_shapes=[pltpu.VMEM((2, tm, D_KV), F32),
                        pltpu.VMEM((_C_END, D_MODEL), BF16)],
        compiler_params=_cparams(("arbitrary", "arbitrary")),
        name="in_proj",
    )(x, norm_w, w_in_t)
    gs = gsT.reshape(b, 3, G, R, s)

    lhs_rows = V_ROWS + nsel
    lhs_tail = jnp.asarray(np.concatenate([np.ones((V_ROWS - HEAD_DIM, nch), np.float32),
                                           _overlap_t(nsel, nch)], axis=0), BF16)
    kc_aug, c_lhs = pl.pallas_call(
        partial(_compress_kernel, nch=nch),
        grid=(2, b),
        in_specs=[pl.BlockSpec((1, 1, nch, cw), lambda xk, i: (xk, i, 0, 0)),
                  pl.BlockSpec((1, 2, 1, cw), lambda xk, i: (xk, 0, 0, 0)),
                  pl.BlockSpec((1, CMP_BLOCK, HEAD_DIM, CMP_HIDDEN), lambda xk, i: (xk, 0, 0, 0)),
                  pl.BlockSpec((1, G * CMP_HIDDEN, D_KV), lambda xk, i: (xk, 0, 0)),
                  pl.BlockSpec((lhs_rows - HEAD_DIM, nch), lambda xk, i: (0, 0))],
        out_specs=[pl.BlockSpec((1, 1, G, nch, LANES), lambda xk, i: (xk, i, 0, 0, 0)),
                   pl.BlockSpec((1, 1, G, lhs_rows, nch), lambda xk, i: (xk, i, 0, 0, 0))],
        out_shape=[jax.ShapeDtypeStruct((2, b, G, nch, LANES), BF16),
                   jax.ShapeDtypeStruct((2, b, G, lhs_rows, nch), BF16)],
        scratch_shapes=[pltpu.VMEM((CMP_BLOCK // CMP_STRIDE, cw, G * CMP_HIDDEN), BF16)],
        compiler_params=_cparams(("arbitrary", "arbitrary")),
        name="compress",
    )(kvc, pee, w1, w2e, lhs_tail)

    tq3 = Q_TILE_CMP
    cpq = tq3 // CMP_STRIDE
    assert nch // SUBLANES <= LANES - HEAD_DIM and k_top > N_FORCED and (nch // 4) % LANES == 0
    ocT, selb = pl.pallas_call(
        partial(_cmp_kernel, nch=nch, nsel=nsel, tq=tq3, k_top=k_top, nqt=s // tq3),
        grid=(b, G, s // tq3),
        in_specs=[smem,
                  pl.BlockSpec((1, R * HEAD_DIM, tq3), lambda i, g, t: (i, g, t)),
                  pl.BlockSpec((1, 1, 1, nch, LANES), lambda i, g, t: (0, i, g, 0, 0)),
                  pl.BlockSpec((1, 1, 1, lhs_rows, nch), lambda i, g, t: (1, i, g, 0, 0)),
                  pl.BlockSpec((1, 1, 1, R, tq3), lambda i, g, t: (i, 0, g, 0, t))],
        out_specs=[pl.BlockSpec((1, R * HEAD_DIM, tq3), lambda i, g, t: (i, g, t)),
                   pl.BlockSpec((1, 1, s // tq3, nsel, tq3), lambda i, g, t: (i, g, 0, 0, 0))],
        out_shape=[jax.ShapeDtypeStruct((b, D_ATTN, s), BF16),
                   jax.ShapeDtypeStruct((b, G, s // tq3, nsel, tq3), BF16)],
        scratch_shapes=[pltpu.VMEM((LANES, R * tq3), BF16),
                        pltpu.VMEM((cpq + BAND_BACK, R * tq3), F32),
                        pltpu.VMEM((lhs_rows, R * tq3), F32),
                        pltpu.VMEM((nsel, tq3), F32),
                        pltpu.VMEM((1, tq3), F32)]
                       + [pltpu.VMEM((nch, tq3), F32)] * R,
        compiler_params=_cparams(("parallel", "parallel", "arbitrary")),
        name="cmp_select",
    )(rel_bias, qT, kc_aug, c_lhs, gs)

    tq = Q_TILE
    nq = R * tq

    o_spec = pl.BlockSpec((1, R * HEAD_DIM, tq), lambda i, g, t: (i, g, t))
    k_spec = lambda kdim: pl.BlockSpec((1, 1, s, kdim), lambda i, g, t: (i, g, 0, 0))
    v_spec = pl.BlockSpec((1, 1, nkt, V_ROWS, KEY_TILE), lambda i, g, t: (i, g, 0, 0, 0))
    gs_spec = lambda branch: pl.BlockSpec((1, 1, 1, R, tq), lambda i, g, t: (i, branch, g, 0, t))
    oswT = pl.pallas_call(
        partial(_branches_kernel, tq=tq, nsel=nsel, kd=kd),
        grid=(b, G, s // tq),
        in_specs=[smem, o_spec,
                  pl.BlockSpec((1, 1, 1, nsel, tq), lambda i, g, t: (i, g, t // (tq3 // tq), 0, t % (tq3 // tq))),
                  k_spec(kd), v_spec, k_spec(LANES), v_spec, gs_spec(1), gs_spec(2)],
        out_specs=o_spec,
        out_shape=jax.ShapeDtypeStruct((b, D_ATTN, s), BF16),
        scratch_shapes=[pltpu.VMEM(((SEL_TBL_TILES + WIN_TBL_TILES) * tq, nq), F32),
                        pltpu.VMEM((kd, nq), BF16),
                        pltpu.VMEM((tq, nq), F32),
                        pltpu.VMEM((1, nq), F32)]
                       + ([pltpu.VMEM((1, nq), F32),
                           pltpu.VMEM((V_ROWS, nq), F32),
                           pltpu.VMEM((1, nq), F32)]
                          + [pltpu.VMEM((1, nq), F32)] * N_SLOTS
                          + [pltpu.VMEM((tq, nq), BF16)] * N_SLOTS) * 2,
        compiler_params=_cparams(("parallel", "parallel", "arbitrary")),
        name="sel_win_attn",
    )(rel_bias, qT, selb, ksel, vselT, kwin, vwinT, gs, gs)

    tm, nt = OUT_TILE, s // OUT_TILE
    hb = tm // HALO_ROWS
    col = lambda c: pl.BlockSpec((1, tm, D_CONV), lambda i, j, c=c: (i, j, c))
    halo = lambda c: pl.BlockSpec((1, HALO_ROWS, D_CONV), lambda i, j, c=c: (i, jnp.maximum(j * hb - 1, 0), c))
    oT = pl.BlockSpec((1, D_ATTN, tm), lambda i, j: (i, 0, j))
    out = pl.pallas_call(
        partial(_out_kernel, tm=tm),
        grid=(b, nt),
        in_specs=[oT, oT, col(0), col(1), col(2), col(3), halo(1),
                  pl.BlockSpec((1, tm, D_MODEL), lambda i, j: (i, j, 0)),
                  pl.BlockSpec((3, D_CONV), lambda i, j: (0, 0)),
                  pl.BlockSpec((D_MODEL, D_MODEL), lambda i, j: (0, 0), pipeline_mode=pl.Buffered(1)),
                  pl.BlockSpec((1, D_MODEL), lambda i, j: (0, 0))],
        out_specs=pl.BlockSpec((1, tm, D_MODEL), lambda i, j: (i, j, 0)),
        out_shape=jax.ShapeDtypeStruct((b, s, D_MODEL), F32),
        scratch_shapes=[pltpu.VMEM((D_MODEL, D_MODEL), BF16)],
        compiler_params=_cparams(("arbitrary", "arbitrary")),
        name="out_proj",
    )(ocT, oswT, rest, rest, rest, rest, rest, x, conv_w[0], w_out[0],
      final_norm_w.reshape(1, D_MODEL))
    return out
```

```python
import math
from functools import partial
from types import SimpleNamespace

import numpy as np
import jax
import jax.numpy as jnp
from jax import lax
from jax.experimental import pallas as pl
from jax.experimental.pallas import tpu as pltpu

D_MODEL = 1024
N_HEADS = 8
HEAD_DIM = 64
N_KV_HEADS = 2
GQA_REP = N_HEADS // N_KV_HEADS
D_ATTN = N_HEADS * HEAD_DIM
D_KV = N_KV_HEADS * HEAD_DIM
D_CONV = D_MODEL - D_ATTN
CMP_BLOCK = 32
CMP_STRIDE = 16
CMP_HIDDEN = 2 * HEAD_DIM
SEL_BLOCK = 64
N_SEL = 16
WINDOW = 512
N_BUCKETS = 32
MAX_DISTANCE = 128
EPS = 1e-6
NEG = -1e30

LOG2E = 1.4426950408889634
LANES = 128
SUBLANES = 8
BAND_BACK = -(-(-(-(MAX_DISTANCE + CMP_BLOCK - 1) // CMP_STRIDE)) // SUBLANES) * SUBLANES
TAKEN = -2.0
NOT_CAUSAL = -1.0
N_FORCED = 3
V_ROWS = HEAD_DIM + 16
ROW_TILE = 512
OUT_TILE = 1024
KEY_TILE = 256
Q_TILE_CMP = 512
Q_TILE = 256
HALO_ROWS = 16
MAX_BLOCK_PAIRS = 8
MAX_LAG = 64.0
N_SLOTS = 2
VMEM_LIMIT = 48 * 1024 * 1024

F32 = jnp.float32
BF16 = jnp.bfloat16

_C_Q = 0
_C_CMP = D_ATTN
_C_SW = _C_CMP + 2 * D_KV
_C_REST = _C_SW + 4 * D_KV
_C_GATE = _C_REST + D_ATTN + 4 * D_CONV
_C_END = _C_GATE + LANES
N_GATE = 3 * N_HEADS


def _bucket_lower_bounds():
    max_exact = N_BUCKETS // 2
    d = np.arange(2 * MAX_DISTANCE, dtype=np.int32)
    nf = np.maximum(d, 1).astype(np.float32)
    large = max_exact + (np.log(nf / np.float32(max_exact)) / np.float32(math.log(MAX_DISTANCE / max_exact))
                         * np.float32(N_BUCKETS - max_exact)).astype(np.int32)
    bucket = np.where(d < max_exact, d, np.minimum(large, N_BUCKETS - 1))
    assert np.all(np.diff(bucket) >= 0) and np.all(bucket[MAX_DISTANCE:] == N_BUCKETS - 1)
    return [int(np.argmax(bucket >= b)) for b in range(N_BUCKETS)]


_BUCKET_LO = _bucket_lower_bounds()


def _rel_bias_delta(dist, bias_of_bucket):
    last = bias_of_bucket(N_BUCKETS - 1)
    val = jnp.full(dist.shape, (bias_of_bucket(0) - last) * LOG2E, F32)
    for b in range(1, N_BUCKETS - 1):
        val = jnp.where(dist >= _BUCKET_LO[b], (bias_of_bucket(b) - last) * LOG2E, val)
    val = jnp.where(dist >= _BUCKET_LO[N_BUCKETS - 1], 0.0, val)
    return jnp.where(dist < 0, NEG, val)


def _silu(z):
    return z * (1.0 / (1.0 + jnp.exp2(z * -LOG2E)))


def _proj_kernel(x_ref, nw_ref, wT_ref, qT_ref, kvc_ref, ksel_ref, vselT_ref, kwin_ref, vwinT_ref,
                 gsT_ref, rest_ref, kv_scr, w_scr, *, tm, nsel, kd):
    st = pl.program_id(1)

    @pl.when((pl.program_id(0) == 0) & (st == 0))
    def _():
        c0 = _C_REST
        chunk = KEY_TILE
        for dst, src, rows in ((0, 0, c0), (c0, c0 + N_GATE, _C_GATE - c0)):
            for r in range(0, rows, chunk):
                w_scr[dst + r:dst + r + chunk, :] = wT_ref[src + r:src + r + chunk, :].astype(BF16)
        w_scr[_C_GATE:_C_END, :] = jnp.concatenate(
            [wT_ref[c0:c0 + N_GATE, :], jnp.zeros((LANES - N_GATE, D_MODEL), F32)], axis=0).astype(BF16)

    x = x_ref[0]
    ms = jnp.mean(x * x, axis=-1, keepdims=True)
    h = (x * lax.rsqrt(ms + EPS) * nw_ref[...]).astype(BF16)

    def mm(lo, hi):
        return lax.dot_general(h, w_scr[lo:hi, :], (((1,), (1,)), ((), ())), preferred_element_type=F32)

    q = mm(_C_Q, _C_CMP) * (HEAD_DIM ** -0.5 * LOG2E)
    qT_ref[0] = q.T.astype(BF16)

    kvc = mm(_C_CMP, _C_SW)
    for xk in range(2):
        kv_scr[xk] = kvc[:, xk * D_KV:(xk + 1) * D_KV]
        for j in range(CMP_STRIDE):
            kvc_ref[xk, 0, :, j * D_KV:(j + 1) * D_KV] = kv_scr[xk, pl.ds(j, tm // CMP_STRIDE, stride=CMP_STRIDE), :]

    a = mm(_C_SW, _C_REST)
    pos = st * tm + lax.broadcasted_iota(jnp.int32, (tm, nsel), 0)
    col = lax.broadcasted_iota(jnp.int32, (tm, nsel), 1)
    onehot = jnp.where((pos // SEL_BLOCK) == col, 1.0, 0.0).astype(BF16)
    ones = jnp.ones((V_ROWS - HEAD_DIM, tm), BF16)
    zeros_k = jnp.zeros((tm, LANES - HEAD_DIM), BF16)
    vsT = a[:, D_KV:2 * D_KV].T
    vwT = a[:, 3 * D_KV:4 * D_KV].T
    for g in range(N_KV_HEADS):
        lo, hi = g * HEAD_DIM, (g + 1) * HEAD_DIM
        parts = [a[:, lo:hi].astype(BF16), onehot]
        if kd > HEAD_DIM + nsel:
            parts.append(jnp.zeros((tm, kd - HEAD_DIM - nsel), BF16))
        ksel_ref[0, g] = jnp.concatenate(parts, axis=1)
        kwin_ref[0, g] = jnp.concatenate([a[:, 2 * D_KV + lo:2 * D_KV + hi].astype(BF16), zeros_k], axis=1)
        vs = jnp.concatenate([vsT[lo:hi].astype(BF16), ones], axis=0)
        vw = jnp.concatenate([vwT[lo:hi].astype(BF16), ones], axis=0)
        for kt in range(tm // KEY_TILE):
            vselT_ref[0, g, kt] = vs[:, kt * KEY_TILE:(kt + 1) * KEY_TILE]
            vwinT_ref[0, g, kt] = vw[:, kt * KEY_TILE:(kt + 1) * KEY_TILE]

    r = mm(_C_REST, _C_GATE)
    z_attn, conv_h, conv_b, conv_c, z_conv = (r[:, i * D_CONV:(i + 1) * D_CONV] for i in range(5))
    rest_ref[0] = jnp.concatenate([z_attn, conv_c * conv_h, conv_b, z_conv], axis=1).astype(BF16)
    gates = jax.nn.sigmoid(mm(_C_GATE, _C_END))
    gsT_ref[0] = gates.T[:N_GATE]


def _compress_kernel(c_ref, pe_ref, w1_ref, w2_ref, lhs_tail_ref, aug_ref, t_ref, w1_scr, *, nch):
    @pl.when(pl.program_id(1) == 0)
    def _():
        w1_scr[...] = jnp.zeros(w1_scr.shape, BF16)
        for m in range(CMP_BLOCK // CMP_STRIDE):
            for j in range(CMP_STRIDE):
                blk = w1_ref[0, m * CMP_STRIDE + j].astype(BF16)
                for g in range(N_KV_HEADS):
                    r0 = (j * N_KV_HEADS + g) * HEAD_DIM
                    w1_scr[m, r0:r0 + HEAD_DIM, g * CMP_HIDDEN:(g + 1) * CMP_HIDDEN] = blk

    c = c_ref[0, 0]
    a0 = jnp.dot((c + pe_ref[0, 0]).astype(BF16), w1_scr[0], preferred_element_type=F32)
    a1 = jnp.dot((c + pe_ref[0, 1]).astype(BF16), w1_scr[1], preferred_element_type=F32)
    hid = a0 + pltpu.roll(a1, nch - 1, 0)
    out = jnp.dot(_silu(hid).astype(BF16), w2_ref[0], preferred_element_type=F32)
    grp = lax.broadcasted_iota(jnp.int32, (nch, LANES - HEAD_DIM), 0) // SUBLANES
    col = lax.broadcasted_iota(jnp.int32, (nch, LANES - HEAD_DIM), 1)
    onehot = jnp.where(grp == col, 1.0, 0.0).astype(BF16)
    out_t = out.T.astype(BF16)
    for g in range(N_KV_HEADS):
        lo, hi = g * HEAD_DIM, (g + 1) * HEAD_DIM
        aug_ref[0, 0, g] = jnp.concatenate([out[:, lo:hi].astype(BF16), onehot], axis=1)
        t_ref[0, 0, g] = jnp.concatenate([out_t[lo:hi], lhs_tail_ref[...]], axis=0)


def _cmp_kernel(rb_ref, qT_ref, kc_ref, lhs_ref, gs_ref, ocT_ref, selb_ref,
                qa_ref, band_ref, acc_ref, score_ref, ntk_ref, s0_ref, s1_ref, s2_ref, s3_ref,
                *, nch, nsel, tq, k_top, nqt):
    g = pl.program_id(1)
    qt = pl.program_id(2)
    s_refs = (s0_ref, s1_ref, s2_ref, s3_ref)
    cpq = tq // CMP_STRIDE
    win = cpq + BAND_BACK
    lim = cpq * (qt + 1)
    nq = GQA_REP * tq

    @pl.when(qt <= 1)
    def _():
        row = lax.broadcasted_iota(jnp.int32, (win, tq), 0) + jnp.maximum(cpq * qt - BAND_BACK, 0)
        lane = lax.broadcasted_iota(jnp.int32, (win, tq), 1)
        dist = (qt * tq + lane) - (row * CMP_STRIDE + CMP_BLOCK - 1)
        for r in range(GQA_REP):
            band_ref[:, r * tq:(r + 1) * tq] = _rel_bias_delta(dist, lambda b, r=r: rb_ref[b, g * GQA_REP + r])

    q = qT_ref[0]
    qa_ref[0:HEAD_DIM, :] = jnp.concatenate([q[r * HEAD_DIM:(r + 1) * HEAD_DIM] for r in range(GQA_REP)], axis=1)
    grp = lax.broadcasted_iota(jnp.int32, (LANES - HEAD_DIM, nq), 0)
    qa_ref[HEAD_DIM:LANES, :] = jnp.where(grp * SUBLANES >= lim, NEG, 0.0).astype(BF16)

    def run(rows, nblk):
        start = pl.multiple_of(jnp.clip(cpq * qt - BAND_BACK, 0, rows - win), SUBLANES)
        k = kc_ref[0, 0, 0, 0:rows, :]
        lhs = lhs_ref[0, 0, 0, 0:V_ROWS + nblk, 0:rows]
        cols = [slice(r * tq, (r + 1) * tq) for r in range(GQA_REP)]

        def qk(r):
            s_refs[r][0:rows, :] = jnp.dot(k, qa_ref[:, cols[r]], preferred_element_type=F32)

        ahead = 3
        for r in range(ahead):
            qk(r)
        for r in range(GQA_REP):
            if r + ahead < GQA_REP:
                qk(r + ahead)
            sr = s_refs[r]
            sr[pl.ds(start, win), :] = sr[pl.ds(start, win), :] + band_ref[:, cols[r]]
            s = sr[0:rows, :]
            p = jnp.exp2(s - jnp.max(s, axis=0, keepdims=True)).astype(BF16)
            acc_ref[0:V_ROWS + nblk, cols[r]] = jnp.dot(lhs, p, preferred_element_type=F32)

    n = lax.broadcasted_iota(jnp.int32, (nsel, tq), 0)
    lane = lax.broadcasted_iota(jnp.int32, (nsel, tq), 1)

    def forced_and_candidate(tile):
        blk = (tile * tq + lane) // SEL_BLOCK
        forced = (n == 0) | (n == blk) | (n == blk - 1)
        return forced, (n <= blk) & jnp.logical_not(forced), n <= blk

    def select(tile, nblk):
        score = score_ref[0:nblk, :]
        for _ in range(k_top - N_FORCED):
            score = jnp.where(score == jnp.max(score, axis=0, keepdims=True), TAKEN, score)
        taken = score == TAKEN
        slot = jnp.maximum(tile, 0)
        selb_ref[0, 0, slot, 0:nblk, :] = jnp.where(taken, 0.0, NEG).astype(BF16)
        if nblk < nsel:
            selb_ref[0, 0, slot, nblk:nsel, :] = jnp.full((nsel - nblk, tq), NEG, BF16)
        candidate = forced_and_candidate(tile)[1][0:nblk]
        ntk_ref[...] = jnp.sum(jnp.where(taken & candidate, 1.0, 0.0), axis=0, keepdims=True)

    def select_exact_if_tied(tile):
        @pl.when(jnp.max(ntk_ref[...]) > k_top - N_FORCED)
        def _():
            def pick_one(_, score):
                mx = jnp.max(score, axis=0, keepdims=True)
                idx = jnp.min(jnp.where(score == mx, n, nsel), axis=0, keepdims=True)
                return jnp.where(n == idx, TAKEN, score)

            taken = lax.fori_loop(0, k_top - N_FORCED, pick_one, score_ref[...]) == TAKEN
            selb_ref[0, 0, jnp.maximum(tile, 0)] = jnp.where(taken, 0.0, NEG).astype(BF16)

    @pl.when(qt == 0)
    def _():
        score_ref[...] = jnp.full((nsel, tq), TAKEN, F32)
        acc_ref[...] = jnp.zeros(acc_ref.shape, F32)

    quarter = nch // 4
    for v in range(1, 5):
        @pl.when((lim > (v - 1) * quarter) & (lim <= v * quarter))
        def _(v=v):
            run(v * quarter, v * nsel // 4)
            select(qt - 1, v * nsel // 4)

    select_exact_if_tied(qt - 1)

    l = acc_ref[HEAD_DIM:HEAD_DIM + 1, :]
    t4 = qt * tq + (lax.broadcasted_iota(jnp.int32, l.shape, 1) & (tq - 1))
    inv = jnp.where(t4 >= CMP_BLOCK - 1, 1.0 / jnp.maximum(l, 1e-30), 0.0)
    imp = None
    for r in range(GQA_REP):
        cs = slice(r * tq, (r + 1) * tq)
        ocT_ref[0, r * HEAD_DIM:(r + 1) * HEAD_DIM, :] = (acc_ref[0:HEAD_DIM, cs] * inv[:, cs]
                                                          * gs_ref[0, 0, 0, r:r + 1, :]).astype(BF16)
        part = acc_ref[V_ROWS:V_ROWS + nsel, cs] * inv[:, cs]
        imp = part if imp is None else imp + part

    forced, _, causal = forced_and_candidate(qt)
    score_ref[...] = jnp.where(forced, TAKEN, jnp.where(causal, imp, NOT_CAUSAL))

    @pl.when(qt == nqt - 1)
    def _():
        select(qt, nsel)
        select_exact_if_tied(qt)


N_BRANCH_STATE = 3 + 2 * N_SLOTS


def _branch_stages(rb_ref, k_ref, vT_ref, tbl_ref, qa_ref, s_ref, mt_ref, m_ref, acc_ref, lag_ref, *slot_refs,
                   tq, n_tbl_tiles, window):
    al_ref, p_ref = slot_refs[0:N_SLOTS], slot_refs[N_SLOTS:2 * N_SLOTS]
    g = pl.program_id(1)
    qt = pl.program_id(2)
    nq = GQA_REP * tq

    @pl.when(qt == 0)
    def _():
        for j in range(n_tbl_tiles):
            key = lax.broadcasted_iota(jnp.int32, (tq, tq), 0) + j * tq
            qry = lax.broadcasted_iota(jnp.int32, (tq, tq), 1) + (n_tbl_tiles - 1) * tq
            dist = qry - key
            for r in range(GQA_REP):
                val = _rel_bias_delta(dist, lambda b, r=r: rb_ref[b, g * GQA_REP + r])
                if window is not None:
                    val = jnp.where(dist >= window, NEG, val)
                tbl_ref[j * tq:(j + 1) * tq, r * tq:(r + 1) * tq] = val

    def reset():
        m_ref[...] = jnp.full((1, nq), NEG, F32)
        acc_ref[...] = jnp.zeros((V_ROWS, nq), F32)

    reset()

    def serial(t, tbl_tile):
        k = k_ref[0, 0, pl.ds(pl.multiple_of(t * tq, tq), tq), :]
        s = jnp.dot(k, qa_ref[...], preferred_element_type=F32)
        if tbl_tile is not None:
            s = s + tbl_ref[tbl_tile * tq:(tbl_tile + 1) * tq, :]
        s_ref[...] = s
        mt_ref[...] = jnp.max(s, axis=0, keepdims=True)
        m_old = m_ref[...]
        m_new = jnp.maximum(m_old, mt_ref[...])
        m_ref[...] = m_new
        p_ref[0][...] = jnp.exp2(s_ref[...] - m_new).astype(BF16)
        acc_ref[...] = jnp.exp2(m_old - m_new) * acc_ref[...] + jnp.dot(vT_ref[0, 0, t], p_ref[0][...],
                                                                        preferred_element_type=F32)

    def stage_lagged(t, slot, tbl_tile, first=False):
        k = k_ref[0, 0, pl.ds(pl.multiple_of(t * tq, tq), tq), :]
        s = jnp.dot(k, qa_ref[...], preferred_element_type=F32)
        if tbl_tile is not None:
            s = s + tbl_ref[tbl_tile * tq:(tbl_tile + 1) * tq, :]
        m = s[0:1, :] if first else m_ref[...]
        mt = jnp.max(s, axis=0, keepdims=True)
        p_ref[slot][...] = jnp.exp2(s - m).astype(BF16)
        m_new = jnp.maximum(m, mt)
        m_ref[...] = m_new
        al_ref[slot][...] = jnp.exp2(m - m_new)
        lag_ref[...] = (mt - m) if first else jnp.maximum(lag_ref[...], mt - m)

    def stage_pv_lagged(t, slot):
        acc_ref[...] = (acc_ref[...] + jnp.dot(vT_ref[0, 0, t], p_ref[slot][...],
                                               preferred_element_type=F32)) * al_ref[slot][...]

    def lagged_step(t, slot, tbl_tile, prev=None):
        stage_lagged(t, slot, tbl_tile)
        stage_pv_lagged(t - 1 if prev is None else prev, 1 - slot)

    def exceeded():
        return jnp.max(lag_ref[...]) > MAX_LAG

    def gated_output(gs_ref):
        acc = acc_ref[...]
        o = acc[0:HEAD_DIM] / jnp.maximum(acc[HEAD_DIM:HEAD_DIM + 1], 1e-30)
        return [o[:, r * tq:(r + 1) * tq] * gs_ref[0, 0, 0, r:r + 1, :] for r in range(GQA_REP)]

    return SimpleNamespace(reset=reset, serial=serial, stage_lagged=stage_lagged, stage_pv_lagged=stage_pv_lagged,
                           lagged_step=lagged_step, exceeded=exceeded, gated_output=gated_output)


SEL_TBL_TILES = 2
WIN_TBL_TILES = 3


def _branches_kernel(rb_ref, qT_ref, selb_ref, ksel_ref, vselT_ref, kwin_ref, vwinT_ref, gs_sel_ref, gs_win_ref,
                     o_ref, tbl_ref, qa_ref, s_ref, mt_ref, *state, tq, nsel, kd):
    qt = pl.program_id(2)
    n = qt + 1
    nq = GQA_REP * tq

    q = qT_ref[0]
    qa_ref[0:HEAD_DIM, :] = jnp.concatenate([q[r * HEAD_DIM:(r + 1) * HEAD_DIM] for r in range(GQA_REP)], axis=1)
    qa_ref[HEAD_DIM:HEAD_DIM + nsel, :] = jnp.concatenate([selb_ref[0, 0, 0]] * GQA_REP, axis=1)
    if kd > HEAD_DIM + nsel:
        qa_ref[HEAD_DIM + nsel:kd, :] = jnp.zeros((kd - HEAD_DIM - nsel, nq), BF16)

    sel = _branch_stages(rb_ref, ksel_ref, vselT_ref, tbl_ref.at[0:SEL_TBL_TILES * tq], qa_ref, s_ref, mt_ref,
                         *state[:N_BRANCH_STATE], tq=tq, n_tbl_tiles=SEL_TBL_TILES, window=None)
    win = _branch_stages(rb_ref, kwin_ref, vwinT_ref,
                         tbl_ref.at[SEL_TBL_TILES * tq:(SEL_TBL_TILES + WIN_TBL_TILES) * tq], qa_ref.at[0:LANES],
                         s_ref, mt_ref, *state[N_BRANCH_STATE:], tq=tq, n_tbl_tiles=WIN_TBL_TILES, window=WINDOW)

    @pl.when((n >= 4) & ((n & 1) == 0))
    def _():
        sel.stage_lagged(0, 1, None, first=True)
        sel.lagged_step(1, 0, None)

    @pl.when((n >= 4) & ((n & 1) == 1))
    def _():
        sel.stage_lagged(0, 0, None, first=True)

    @pl.when(n >= 4)
    def _():
        t0 = 2 - (n & 1)
        pairs = (n - 2 - t0) >> 1

        def pair(t):
            sel.lagged_step(t, 1, None)
            sel.lagged_step(t + 1, 0, None)

        big = MAX_BLOCK_PAIRS

        def body(j, carry):
            for i in range(big):
                pair(t0 + 2 * (big * j + i))
            return carry
        lax.fori_loop(0, pairs >> (big.bit_length() - 1), body, 0)

        bp = big // 2
        while bp >= 1:
            @pl.when((pairs & bp) == bp)
            def _(bp=bp):
                t = n - 2 - 2 * (pairs & (2 * bp - 1))
                for i in range(bp):
                    pair(t + 2 * i)
            bp //= 2

    def window_tiles(after_first=lambda: None):
        win.stage_lagged(qt, 0, 2, first=True)
        after_first()
        win.lagged_step(qt - 1, 1, 1, prev=qt)
        win.lagged_step(qt - 2, 0, 0, prev=qt - 1)
        win.stage_pv_lagged(qt - 2, 0)

    def window_exact_if_exceeded():
        @pl.when(win.exceeded())
        def _():
            win.reset()
            for j in range(WIN_TBL_TILES):
                win.serial(qt - (WIN_TBL_TILES - 1) + j, j)

    @pl.when(n >= 4)
    def _():
        sel.lagged_step(n - 2, 1, 0)
        sel.lagged_step(n - 1, 0, 1)
        window_tiles(after_first=lambda: sel.stage_pv_lagged(n - 1, 0))

        @pl.when(sel.exceeded())
        def _():
            sel.reset()

            def redo(t, carry):
                sel.serial(t, None)
                return carry
            lax.fori_loop(0, n - 2, redo, 0)
            sel.serial(n - 2, 0)
            sel.serial(n - 1, 1)

        window_exact_if_exceeded()

    @pl.when(n == 3)
    def _():
        sel.serial(0, None)

    @pl.when((n >= 2) & (n < 4))
    def _():
        sel.serial(qt - 1, 0)

    @pl.when(n < 4)
    def _():
        sel.serial(qt, 1)

    @pl.when(n == 3)
    def _():
        window_tiles()
        window_exact_if_exceeded()

    @pl.when(qt == 1)
    def _():
        win.serial(0, 1)

    @pl.when(qt < WIN_TBL_TILES - 1)
    def _():
        win.serial(qt, 2)

    for r, (so, wo) in enumerate(zip(sel.gated_output(gs_sel_ref), win.gated_output(gs_win_ref))):
        o_ref[0, r * HEAD_DIM:(r + 1) * HEAD_DIM, :] = (so + wo).astype(o_ref.dtype)


def _out_kernel(oc_ref, osw_ref, za_ref, u_ref, cb_ref, zc_ref, hu_ref,
                x_ref, cw_ref, wo_ref, fw_ref, out_ref, wo_scr, *, tm):
    st = pl.program_id(1)

    @pl.when((pl.program_id(0) == 0) & (st == 0))
    def _():
        for r in range(0, D_MODEL, KEY_TILE):
            wo_scr[r:r + KEY_TILE, :] = wo_ref[r:r + KEY_TILE, :].astype(BF16)

    f32 = lambda ref: ref[0].astype(F32)
    attn = (f32(oc_ref) + f32(osw_ref)).T * _silu(f32(za_ref))

    u = f32(u_ref)
    halo = jnp.where(st == 0, 0.0, f32(hu_ref))
    row = lax.broadcasted_iota(jnp.int32, u.shape, 0)
    h1, h2 = halo[HALO_ROWS - 1:HALO_ROWS], halo[HALO_ROWS - 2:HALO_ROWS - 1]
    u1 = jnp.where(row == 0, h1, pltpu.roll(u, 1, 0))
    u2 = jnp.where(row == 0, h2, jnp.where(row == 1, h1, pltpu.roll(u, 2, 0)))
    y = cw_ref[0:1] * u2 + cw_ref[1:2] * u1 + cw_ref[2:3] * u
    conv = f32(cb_ref) * y * _silu(f32(zc_ref))

    mixed = jnp.concatenate([attn, conv], axis=1).astype(BF16)
    y2 = x_ref[0] + jnp.dot(mixed, wo_scr[...], preferred_element_type=F32)
    ms = jnp.mean(y2 * y2, axis=-1, keepdims=True)
    out_ref[0] = y2 * lax.rsqrt(ms + EPS) * fw_ref[...]


def _overlap_t(nsel, nch):
    c = np.arange(nch)[None, :] * CMP_STRIDE
    n = np.arange(nsel)[:, None] * SEL_BLOCK
    return ((c < n + SEL_BLOCK) & (c + CMP_BLOCK > n)).astype(np.float32)


def _cparams(sem):
    return pltpu.CompilerParams(dimension_semantics=sem, vmem_limit_bytes=VMEM_LIMIT)


def kernel(x, norm_w, w_in, w_ck1, w_ck2, pe_k, w_cv1, w_cv2, pe_v, conv_w, w_out, rel_bias, final_norm_w):
    b, s, _ = x.shape
    assert norm_w.shape[0] == 1 and s % ROW_TILE == 0 and ROW_TILE % KEY_TILE == 0
    assert WINDOW == 2 * Q_TILE and KEY_TILE == Q_TILE
    tm = ROW_TILE
    nt = s // tm
    nkt = s // KEY_TILE
    kpt = tm // KEY_TILE
    nch = s // CMP_STRIDE
    nsel = s // SEL_BLOCK
    k_top = min(N_SEL, nsel)
    kd = -(-(HEAD_DIM + nsel) // LANES) * LANES
    G, R = N_KV_HEADS, GQA_REP

    w_in_t = jnp.swapaxes(w_in[0], 0, 1)
    assert w_in_t.shape[0] == _C_GATE + N_GATE
    eye = jnp.eye(G, dtype=F32)

    def expand_pe(pe):
        pe = pe.reshape(2, CMP_STRIDE, 1, HEAD_DIM)
        return jnp.broadcast_to(pe, (2, CMP_STRIDE, G, HEAD_DIM)).reshape(2, 1, CMP_STRIDE * D_KV)

    def expand2(w2):
        return jnp.einsum('hd,ab->ahbd', w2, eye).reshape(G * CMP_HIDDEN, D_KV)

    w1 = jnp.concatenate([w_ck1, w_cv1], axis=0)
    pee = jnp.stack([expand_pe(pe_k[0]), expand_pe(pe_v[0])])
    w2e = jnp.stack([expand2(w_ck2[0]), expand2(w_cv2[0])]).astype(BF16)
    rel_bias = rel_bias.astype(F32)
    smem = pl.BlockSpec(memory_space=pltpu.SMEM)

    n_rest = D_ATTN + 3 * D_CONV
    cw = CMP_STRIDE * D_KV
    qT, kvc, ksel, vselT, kwin, vwinT, gsT, rest = pl.pallas_call(
        partial(_proj_kernel, tm=tm, nsel=nsel, kd=kd),
        grid=(b, nt),
        in_specs=[pl.BlockSpec((1, tm, D_MODEL), lambda i, j: (i, j, 0)),
                  pl.BlockSpec((1, D_MODEL), lambda i, j: (0, 0)),
                  pl.BlockSpec((_C_GATE + N_GATE, D_MODEL), lambda i, j: (0, 0), pipeline_mode=pl.Buffered(1))],
        out_specs=[pl.BlockSpec((1, D_ATTN, tm), lambda i, j: (i, 0, j)),
                   pl.BlockSpec((2, 1, tm // CMP_STRIDE, cw), lambda i, j: (0, i, j, 0)),
                   pl.BlockSpec((1, G, tm, kd), lambda i, j: (i, 0, j, 0)),
                   pl.BlockSpec((1, G, kpt, V_ROWS, KEY_TILE), lambda i, j: (i, 0, j, 0, 0)),
                   pl.BlockSpec((1, G, tm, LANES), lambda i, j: (i, 0, j, 0)),
                   pl.BlockSpec((1, G, kpt, V_ROWS, KEY_TILE), lambda i, j: (i, 0, j, 0, 0)),
                   pl.BlockSpec((1, N_GATE, tm), lambda i, j: (i, 0, j)),
                   pl.BlockSpec((1, tm, n_rest), lambda i, j: (i, j, 0))],
        out_shape=[jax.ShapeDtypeStruct((b, D_ATTN, s), BF16),
                   jax.ShapeDtypeStruct((2, b, nch, cw), F32),
                   jax.ShapeDtypeStruct((b, G, s, kd), BF16),
                   jax.ShapeDtypeStruct((b, G, nkt, V_ROWS, KEY_TILE), BF16),
                   jax.ShapeDtypeStruct((b, G, s, LANES), BF16),
                   jax.ShapeDtypeStruct((b, G, nkt, V_ROWS, KEY_TILE), BF16),
                   jax.ShapeDtypeStruct((b, N_GATE, s), F32),
                   jax.ShapeDtypeStruct((b, s, n_rest), BF16)],
        scratch_shapes=[pltpu.VMEM((2, tm, D_KV), F32),
                        pltpu.VMEM((_C_END, D_MODEL), BF16)],
        compiler_params=_cparams(("arbitrary", "arbitrary")),
        name="in_proj",
    )(x, norm_w, w_in_t)
    gs = gsT.reshape(b, 3, G, R, s)

    lhs_rows = V_ROWS + nsel
    lhs_tail = jnp.asarray(np.concatenate([np.ones((V_ROWS - HEAD_DIM, nch), np.float32),
                                           _overlap_t(nsel, nch)], axis=0), BF16)
    kc_aug, c_lhs = pl.pallas_call(
        partial(_compress_kernel, nch=nch),
        grid=(2, b),
        in_specs=[pl.BlockSpec((1, 1, nch, cw), lambda xk, i: (xk, i, 0, 0)),
                  pl.BlockSpec((1, 2, 1, cw), lambda xk, i: (xk, 0, 0, 0)),
                  pl.BlockSpec((1, CMP_BLOCK, HEAD_DIM, CMP_HIDDEN), lambda xk, i: (xk, 0, 0, 0)),
                  pl.BlockSpec((1, G * CMP_HIDDEN, D_KV), lambda xk, i: (xk, 0, 0)),
                  pl.BlockSpec((lhs_rows - HEAD_DIM, nch), lambda xk, i: (0, 0))],
        out_specs=[pl.BlockSpec((1, 1, G, nch, LANES), lambda xk, i: (xk, i, 0, 0, 0)),
                   pl.BlockSpec((1, 1, G, lhs_rows, nch), lambda xk, i: (xk, i, 0, 0, 0))],
        out_shape=[jax.ShapeDtypeStruct((2, b, G, nch, LANES), BF16),
                   jax.ShapeDtypeStruct((2, b, G, lhs_rows, nch), BF16)],
        scratch_shapes=[pltpu.VMEM((CMP_BLOCK // CMP_STRIDE, cw, G * CMP_HIDDEN), BF16)],
        compiler_params=_cparams(("arbitrary", "arbitrary")),
        name="compress",
    )(kvc, pee, w1, w2e, lhs_tail)

    tq3 = Q_TILE_CMP
    cpq = tq3 // CMP_STRIDE
    assert nch // SUBLANES <= LANES - HEAD_DIM and k_top > N_FORCED and (nch // 4) % LANES == 0
    ocT, selb = pl.pallas_call(
        partial(_cmp_kernel, nch=nch, nsel=nsel, tq=tq3, k_top=k_top, nqt=s // tq3),
        grid=(b, G, s // tq3),
        in_specs=[smem,
                  pl.BlockSpec((1, R * HEAD_DIM, tq3), lambda i, g, t: (i, g, t)),
                  pl.BlockSpec((1, 1, 1, nch, LANES), lambda i, g, t: (0, i, g, 0, 0)),
                  pl.BlockSpec((1, 1, 1, lhs_rows, nch), lambda i, g, t: (1, i, g, 0, 0)),
                  pl.BlockSpec((1, 1, 1, R, tq3), lambda i, g, t: (i, 0, g, 0, t))],
        out_specs=[pl.BlockSpec((1, R * HEAD_DIM, tq3), lambda i, g, t: (i, g, t)),
                   pl.BlockSpec((1, 1, s // tq3, nsel, tq3), lambda i, g, t: (i, g, 0, 0, 0))],
        out_shape=[jax.ShapeDtypeStruct((b, D_ATTN, s), BF16),
                   jax.ShapeDtypeStruct((b, G, s // tq3, nsel, tq3), BF16)],
        scratch_shapes=[pltpu.VMEM((LANES, R * tq3), BF16),
                        pltpu.VMEM((cpq + BAND_BACK, R * tq3), F32),
                        pltpu.VMEM((lhs_rows, R * tq3), F32),
                        pltpu.VMEM((nsel, tq3), F32),
                        pltpu.VMEM((1, tq3), F32)]
                       + [pltpu.VMEM((nch, tq3), F32)] * R,
        compiler_params=_cparams(("parallel", "parallel", "arbitrary")),
        name="cmp_select",
    )(rel_bias, qT, kc_aug, c_lhs, gs)

    tq = Q_TILE
    nq = R * tq

    o_spec = pl.BlockSpec((1, R * HEAD_DIM, tq), lambda i, g, t: (i, g, t))
    k_spec = lambda kdim: pl.BlockSpec((1, 1, s, kdim), lambda i, g, t: (i, g, 0, 0))
    v_spec = pl.BlockSpec((1, 1, nkt, V_ROWS, KEY_TILE), lambda i, g, t: (i, g, 0, 0, 0))
    gs_spec = lambda branch: pl.BlockSpec((1, 1, 1, R, tq), lambda i, g, t: (i, branch, g, 0, t))
    oswT = pl.pallas_call(
        partial(_branches_kernel, tq=tq, nsel=nsel, kd=kd),
        grid=(b, G, s // tq),
        in_specs=[smem, o_spec,
                  pl.BlockSpec((1, 1, 1, nsel, tq), lambda i, g, t: (i, g, t // (tq3 // tq), 0, t % (tq3 // tq))),
                  k_spec(kd), v_spec, k_spec(LANES), v_spec, gs_spec(1), gs_spec(2)],
        out_specs=o_spec,
        out_shape=jax.ShapeDtypeStruct((b, D_ATTN, s), BF16),
        scratch_shapes=[pltpu.VMEM(((SEL_TBL_TILES + WIN_TBL_TILES) * tq, nq), F32),
                        pltpu.VMEM((kd, nq), BF16),
                        pltpu.VMEM((tq, nq), F32),
                        pltpu.VMEM((1, nq), F32)]
                       + ([pltpu.VMEM((1, nq), F32),
                           pltpu.VMEM((V_ROWS, nq), F32),
                           pltpu.VMEM((1, nq), F32)]
                          + [pltpu.VMEM((1, nq), F32)] * N_SLOTS
                          + [pltpu.VMEM((tq, nq), BF16)] * N_SLOTS) * 2,
        compiler_params=_cparams(("parallel", "parallel", "arbitrary")),
        name="sel_win_attn",
    )(rel_bias, qT, selb, ksel, vselT, kwin, vwinT, gs, gs)

    tm, nt = OUT_TILE, s // OUT_TILE
    hb = tm // HALO_ROWS
    col = lambda c: pl.BlockSpec((1, tm, D_CONV), lambda i, j, c=c: (i, j, c))
    halo = lambda c: pl.BlockSpec((1, HALO_ROWS, D_CONV), lambda i, j, c=c: (i, jnp.maximum(j * hb - 1, 0), c))
    oT = pl.BlockSpec((1, D_ATTN, tm), lambda i, j: (i, 0, j))
    out = pl.pallas_call(
        partial(_out_kernel, tm=tm),
        grid=(b, nt),
        in_specs=[oT, oT, col(0), col(1), col(2), col(3), halo(1),
                  pl.BlockSpec((1, tm, D_MODEL), lambda i, j: (i, j, 0)),
                  pl.BlockSpec((3, D_CONV), lambda i, j: (0, 0)),
                  pl.BlockSpec((D_MODEL, D_MODEL), lambda i, j: (0, 0), pipeline_mode=pl.Buffered(1)),
                  pl.BlockSpec((1, D_MODEL), lambda i, j: (0, 0))],
        out_specs=pl.BlockSpec((1, tm, D_MODEL), lambda i, j: (i, j, 0)),
        out_shape=jax.ShapeDtypeStruct((b, s, D_MODEL), F32),
        scratch_shapes=[pltpu.VMEM((D_MODEL, D_MODEL), BF16)],
        compiler_params=_cparams(("arbitrary", "arbitrary")),
        name="out_proj",
    )(ocT, oswT, rest, rest, rest, rest, rest, x, conv_w[0], w_out[0],
      final_norm_w.reshape(1, D_MODEL))
    return out
```

```python
import math
from functools import partial
from types import SimpleNamespace

import numpy as np
import jax
import jax.numpy as jnp
from jax import lax
from jax.experimental import pallas as pl
from jax.experimental.pallas import tpu as pltpu

D_MODEL = 1024
N_HEADS = 8
HEAD_DIM = 64
N_KV_HEADS = 2
GQA_REP = N_HEADS // N_KV_HEADS
D_ATTN = N_HEADS * HEAD_DIM
D_KV = N_KV_HEADS * HEAD_DIM
D_CONV = D_MODEL - D_ATTN
CMP_BLOCK = 32
CMP_STRIDE = 16
CMP_HIDDEN = 2 * HEAD_DIM
SEL_BLOCK = 64
N_SEL = 16
WINDOW = 512
N_BUCKETS = 32
MAX_DISTANCE = 128
EPS = 1e-6
NEG = -1e30

LOG2E = 1.4426950408889634
LANES = 128
SUBLANES = 8
BAND_BACK = -(-(-(-(MAX_DISTANCE + CMP_BLOCK - 1) // CMP_STRIDE)) // SUBLANES) * SUBLANES
TAKEN = -2.0
NOT_CAUSAL = -1.0
N_FORCED = 3
V_ROWS = HEAD_DIM + 16
ROW_TILE = 512
OUT_TILE = 1024
OUT_INPUT_BUFFERS = 3
KEY_TILE = 256
Q_TILE_CMP = 512
Q_TILE = 256
HALO_ROWS = 16
MAX_BLOCK_PAIRS = 8
MAX_LAG = 64.0
N_SLOTS = 2
VMEM_LIMIT = 48 * 1024 * 1024

F32 = jnp.float32
BF16 = jnp.bfloat16

_C_Q = 0
_C_CMP = D_ATTN
_C_SW = _C_CMP + 2 * D_KV
_C_REST = _C_SW + 4 * D_KV
_C_GATE = _C_REST + D_ATTN + 4 * D_CONV
_C_END = _C_GATE + LANES
N_GATE = 3 * N_HEADS


def _bucket_lower_bounds():
    max_exact = N_BUCKETS // 2
    d = np.arange(2 * MAX_DISTANCE, dtype=np.int32)
    nf = np.maximum(d, 1).astype(np.float32)
    large = max_exact + (np.log(nf / np.float32(max_exact)) / np.float32(math.log(MAX_DISTANCE / max_exact))
                         * np.float32(N_BUCKETS - max_exact)).astype(np.int32)
    bucket = np.where(d < max_exact, d, np.minimum(large, N_BUCKETS - 1))
    assert np.all(np.diff(bucket) >= 0) and np.all(bucket[MAX_DISTANCE:] == N_BUCKETS - 1)
    return [int(np.argmax(bucket >= b)) for b in range(N_BUCKETS)]


_BUCKET_LO = _bucket_lower_bounds()


def _rel_bias_delta(dist, bias_of_bucket):
    last = bias_of_bucket(N_BUCKETS - 1)
    val = jnp.full(dist.shape, (bias_of_bucket(0) - last) * LOG2E, F32)
    for b in range(1, N_BUCKETS - 1):
        val = jnp.where(dist >= _BUCKET_LO[b], (bias_of_bucket(b) - last) * LOG2E, val)
    val = jnp.where(dist >= _BUCKET_LO[N_BUCKETS - 1], 0.0, val)
    return jnp.where(dist < 0, NEG, val)


def _silu(z):
    return z * (1.0 / (1.0 + jnp.exp2(z * -LOG2E)))


def _proj_kernel(x_ref, nw_ref, wT_ref, qT_ref, kvc_ref, ksel_ref, vselT_ref, kwin_ref, vwinT_ref,
                 gsT_ref, rest_ref, kv_scr, w_scr, *, tm, nsel, kd):
    st = pl.program_id(1)

    @pl.when((pl.program_id(0) == 0) & (st == 0))
    def _():
        c0 = _C_REST
        chunk = KEY_TILE
        for dst, src, rows in ((0, 0, c0), (c0, c0 + N_GATE, _C_GATE - c0)):
            for r in range(0, rows, chunk):
                w_scr[dst + r:dst + r + chunk, :] = wT_ref[src + r:src + r + chunk, :].astype(BF16)
        w_scr[_C_GATE:_C_END, :] = jnp.concatenate(
            [wT_ref[c0:c0 + N_GATE, :], jnp.zeros((LANES - N_GATE, D_MODEL), F32)], axis=0).astype(BF16)

    x = x_ref[0]
    ms = jnp.mean(x * x, axis=-1, keepdims=True)
    h = (x * lax.rsqrt(ms + EPS) * nw_ref[...]).astype(BF16)

    def mm(lo, hi):
        return lax.dot_general(h, w_scr[lo:hi, :], (((1,), (1,)), ((), ())), preferred_element_type=F32)

    q = mm(_C_Q, _C_CMP) * (HEAD_DIM ** -0.5 * LOG2E)
    qT_ref[0] = q.T.astype(BF16)

    kvc = mm(_C_CMP, _C_SW)
    for xk in range(2):
        kv_scr[xk] = kvc[:, xk * D_KV:(xk + 1) * D_KV]
        for j in range(CMP_STRIDE):
            kvc_ref[xk, 0, :, j * D_KV:(j + 1) * D_KV] = kv_scr[xk, pl.ds(j, tm // CMP_STRIDE, stride=CMP_STRIDE), :]

    a = mm(_C_SW, _C_REST)
    pos = st * tm + lax.broadcasted_iota(jnp.int32, (tm, nsel), 0)
    col = lax.broadcasted_iota(jnp.int32, (tm, nsel), 1)
    onehot = jnp.where((pos // SEL_BLOCK) == col, 1.0, 0.0).astype(BF16)
    ones = jnp.ones((V_ROWS - HEAD_DIM, tm), BF16)
    zeros_k = jnp.zeros((tm, LANES - HEAD_DIM), BF16)
    vsT = a[:, D_KV:2 * D_KV].T
    vwT = a[:, 3 * D_KV:4 * D_KV].T
    for g in range(N_KV_HEADS):
        lo, hi = g * HEAD_DIM, (g + 1) * HEAD_DIM
        parts = [a[:, lo:hi].astype(BF16), onehot]
        if kd > HEAD_DIM + nsel:
            parts.append(jnp.zeros((tm, kd - HEAD_DIM - nsel), BF16))
        ksel_ref[0, g] = jnp.concatenate(parts, axis=1)
        kwin_ref[0, g] = jnp.concatenate([a[:, 2 * D_KV + lo:2 * D_KV + hi].astype(BF16), zeros_k], axis=1)
        vs = jnp.concatenate([vsT[lo:hi].astype(BF16), ones], axis=0)
        vw = jnp.concatenate([vwT[lo:hi].astype(BF16), ones], axis=0)
        for kt in range(tm // KEY_TILE):
            vselT_ref[0, g, kt] = vs[:, kt * KEY_TILE:(kt + 1) * KEY_TILE]
            vwinT_ref[0, g, kt] = vw[:, kt * KEY_TILE:(kt + 1) * KEY_TILE]

    r = mm(_C_REST, _C_GATE)
    z_attn, conv_h, conv_b, conv_c, z_conv = (r[:, i * D_CONV:(i + 1) * D_CONV] for i in range(5))
    rest_ref[0] = jnp.concatenate([z_attn, conv_c * conv_h, conv_b, z_conv], axis=1).astype(BF16)
    gates = jax.nn.sigmoid(mm(_C_GATE, _C_END))
    gsT_ref[0] = gates.T[:N_GATE]


def _compress_kernel(c_ref, pe_ref, w1_ref, w2_ref, lhs_tail_ref, aug_ref, t_ref, w1_scr, *, nch):
    @pl.when(pl.program_id(1) == 0)
    def _():
        w1_scr[...] = jnp.zeros(w1_scr.shape, BF16)
        for m in range(CMP_BLOCK // CMP_STRIDE):
            for j in range(CMP_STRIDE):
                blk = w1_ref[0, m * CMP_STRIDE + j].astype(BF16)
                for g in range(N_KV_HEADS):
                    r0 = (j * N_KV_HEADS + g) * HEAD_DIM
                    w1_scr[m, r0:r0 + HEAD_DIM, g * CMP_HIDDEN:(g + 1) * CMP_HIDDEN] = blk

    c = c_ref[0, 0]
    a0 = jnp.dot((c + pe_ref[0, 0]).astype(BF16), w1_scr[0], preferred_element_type=F32)
    a1 = jnp.dot((c + pe_ref[0, 1]).astype(BF16), w1_scr[1], preferred_element_type=F32)
    hid = a0 + pltpu.roll(a1, nch - 1, 0)
    out = jnp.dot(_silu(hid).astype(BF16), w2_ref[0], preferred_element_type=F32)
    grp = lax.broadcasted_iota(jnp.int32, (nch, LANES - HEAD_DIM), 0) // SUBLANES
    col = lax.broadcasted_iota(jnp.int32, (nch, LANES - HEAD_DIM), 1)
    onehot = jnp.where(grp == col, 1.0, 0.0).astype(BF16)
    out_t = out.T.astype(BF16)
    for g in range(N_KV_HEADS):
        lo, hi = g * HEAD_DIM, (g + 1) * HEAD_DIM
        aug_ref[0, 0, g] = jnp.concatenate([out[:, lo:hi].astype(BF16), onehot], axis=1)
        t_ref[0, 0, g] = jnp.concatenate([out_t[lo:hi], lhs_tail_ref[...]], axis=0)


def _cmp_kernel(rb_ref, qT_ref, kc_ref, lhs_ref, gs_ref, ocT_ref, selb_ref,
                qa_ref, band_ref, acc_ref, score_ref, ntk_ref, s0_ref, s1_ref, s2_ref, s3_ref,
                *, nch, nsel, tq, k_top, nqt):
    g = pl.program_id(1)
    qt = pl.program_id(2)
    s_refs = (s0_ref, s1_ref, s2_ref, s3_ref)
    cpq = tq // CMP_STRIDE
    win = cpq + BAND_BACK
    lim = cpq * (qt + 1)
    nq = GQA_REP * tq

    @pl.when(qt <= 1)
    def _():
        row = lax.broadcasted_iota(jnp.int32, (win, tq), 0) + jnp.maximum(cpq * qt - BAND_BACK, 0)
        lane = lax.broadcasted_iota(jnp.int32, (win, tq), 1)
        dist = (qt * tq + lane) - (row * CMP_STRIDE + CMP_BLOCK - 1)
        for r in range(GQA_REP):
            band_ref[:, r * tq:(r + 1) * tq] = _rel_bias_delta(dist, lambda b, r=r: rb_ref[b, g * GQA_REP + r])

    q = qT_ref[0]
    qa_ref[0:HEAD_DIM, :] = jnp.concatenate([q[r * HEAD_DIM:(r + 1) * HEAD_DIM] for r in range(GQA_REP)], axis=1)
    grp = lax.broadcasted_iota(jnp.int32, (LANES - HEAD_DIM, nq), 0)
    qa_ref[HEAD_DIM:LANES, :] = jnp.where(grp * SUBLANES >= lim, NEG, 0.0).astype(BF16)

    def run(rows, nblk):
        start = pl.multiple_of(jnp.clip(cpq * qt - BAND_BACK, 0, rows - win), SUBLANES)
        k = kc_ref[0, 0, 0, 0:rows, :]
        lhs = lhs_ref[0, 0, 0, 0:V_ROWS + nblk, 0:rows]
        cols = [slice(r * tq, (r + 1) * tq) for r in range(GQA_REP)]

        def qk(r):
            s_refs[r][0:rows, :] = jnp.dot(k, qa_ref[:, cols[r]], preferred_element_type=F32)

        ahead = 3
        for r in range(ahead):
            qk(r)
        for r in range(GQA_REP):
            if r + ahead < GQA_REP:
                qk(r + ahead)
            sr = s_refs[r]
            sr[pl.ds(start, win), :] = sr[pl.ds(start, win), :] + band_ref[:, cols[r]]
            s = sr[0:rows, :]
            p = jnp.exp2(s - jnp.max(s, axis=0, keepdims=True)).astype(BF16)
            acc_ref[0:V_ROWS + nblk, cols[r]] = jnp.dot(lhs, p, preferred_element_type=F32)

    n = lax.broadcasted_iota(jnp.int32, (nsel, tq), 0)
    lane = lax.broadcasted_iota(jnp.int32, (nsel, tq), 1)

    def forced_and_candidate(tile):
        blk = (tile * tq + lane) // SEL_BLOCK
        forced = (n == 0) | (n == blk) | (n == blk - 1)
        return forced, (n <= blk) & jnp.logical_not(forced), n <= blk

    def select(tile, nblk):
        score = score_ref[0:nblk, :]
        for _ in range(k_top - N_FORCED):
            score = jnp.where(score == jnp.max(score, axis=0, keepdims=True), TAKEN, score)
        taken = score == TAKEN
        slot = jnp.maximum(tile, 0)
        selb_ref[0, 0, slot, 0:nblk, :] = jnp.where(taken, 0.0, NEG).astype(BF16)
        if nblk < nsel:
            selb_ref[0, 0, slot, nblk:nsel, :] = jnp.full((nsel - nblk, tq), NEG, BF16)
        candidate = forced_and_candidate(tile)[1][0:nblk]
        ntk_ref[...] = jnp.sum(jnp.where(taken & candidate, 1.0, 0.0), axis=0, keepdims=True)

    def select_exact_if_tied(tile):
        @pl.when(jnp.max(ntk_ref[...]) > k_top - N_FORCED)
        def _():
            def pick_one(_, score):
                mx = jnp.max(score, axis=0, keepdims=True)
                idx = jnp.min(jnp.where(score == mx, n, nsel), axis=0, keepdims=True)
                return jnp.where(n == idx, TAKEN, score)

            taken = lax.fori_loop(0, k_top - N_FORCED, pick_one, score_ref[...]) == TAKEN
            selb_ref[0, 0, jnp.maximum(tile, 0)] = jnp.where(taken, 0.0, NEG).astype(BF16)

    @pl.when(qt == 0)
    def _():
        score_ref[...] = jnp.full((nsel, tq), TAKEN, F32)
        acc_ref[...] = jnp.zeros(acc_ref.shape, F32)

    quarter = nch // 4
    for v in range(1, 5):
        @pl.when((lim > (v - 1) * quarter) & (lim <= v * quarter))
        def _(v=v):
            run(v * quarter, v * nsel // 4)
            select(qt - 1, v * nsel // 4)

    select_exact_if_tied(qt - 1)

    l = acc_ref[HEAD_DIM:HEAD_DIM + 1, :]
    t4 = qt * tq + (lax.broadcasted_iota(jnp.int32, l.shape, 1) & (tq - 1))
    inv = jnp.where(t4 >= CMP_BLOCK - 1, 1.0 / jnp.maximum(l, 1e-30), 0.0)
    imp = None
    for r in range(GQA_REP):
        cs = slice(r * tq, (r + 1) * tq)
        ocT_ref[0, r * HEAD_DIM:(r + 1) * HEAD_DIM, :] = (acc_ref[0:HEAD_DIM, cs] * inv[:, cs]
                                                          * gs_ref[0, 0, 0, r:r + 1, :]).astype(BF16)
        part = acc_ref[V_ROWS:V_ROWS + nsel, cs] * inv[:, cs]
        imp = part if imp is None else imp + part

    forced, _, causal = forced_and_candidate(qt)
    score_ref[...] = jnp.where(forced, TAKEN, jnp.where(causal, imp, NOT_CAUSAL))

    @pl.when(qt == nqt - 1)
    def _():
        select(qt, nsel)
        select_exact_if_tied(qt)


N_BRANCH_STATE = 3 + 2 * N_SLOTS


def _branch_stages(rb_ref, k_ref, vT_ref, tbl_ref, qa_ref, s_ref, mt_ref, m_ref, acc_ref, lag_ref, *slot_refs,
                   tq, n_tbl_tiles, window):
    al_ref, p_ref = slot_refs[0:N_SLOTS], slot_refs[N_SLOTS:2 * N_SLOTS]
    g = pl.program_id(1)
    qt = pl.program_id(2)
    nq = GQA_REP * tq

    @pl.when(qt == 0)
    def _():
        for j in range(n_tbl_tiles):
            key = lax.broadcasted_iota(jnp.int32, (tq, tq), 0) + j * tq
            qry = lax.broadcasted_iota(jnp.int32, (tq, tq), 1) + (n_tbl_tiles - 1) * tq
            dist = qry - key
            for r in range(GQA_REP):
                val = _rel_bias_delta(dist, lambda b, r=r: rb_ref[b, g * GQA_REP + r])
                if window is not None:
                    val = jnp.where(dist >= window, NEG, val)
                tbl_ref[j * tq:(j + 1) * tq, r * tq:(r + 1) * tq] = val

    def reset():
        m_ref[...] = jnp.full((1, nq), NEG, F32)
        acc_ref[...] = jnp.zeros((V_ROWS, nq), F32)

    reset()

    def serial(t, tbl_tile):
        k = k_ref[0, 0, pl.ds(pl.multiple_of(t * tq, tq), tq), :]
        s = jnp.dot(k, qa_ref[...], preferred_element_type=F32)
        if tbl_tile is not None:
            s = s + tbl_ref[tbl_tile * tq:(tbl_tile + 1) * tq, :]
        s_ref[...] = s
        mt_ref[...] = jnp.max(s, axis=0, keepdims=True)
        m_old = m_ref[...]
        m_new = jnp.maximum(m_old, mt_ref[...])
        m_ref[...] = m_new
        p_ref[0][...] = jnp.exp2(s_ref[...] - m_new).astype(BF16)
        acc_ref[...] = jnp.exp2(m_old - m_new) * acc_ref[...] + jnp.dot(vT_ref[0, 0, t], p_ref[0][...],
                                                                        preferred_element_type=F32)

    def stage_lagged(t, slot, tbl_tile, first=False):
        k = k_ref[0, 0, pl.ds(pl.multiple_of(t * tq, tq), tq), :]
        s = jnp.dot(k, qa_ref[...], preferred_element_type=F32)
        if tbl_tile is not None:
            s = s + tbl_ref[tbl_tile * tq:(tbl_tile + 1) * tq, :]
        m = s[0:1, :] if first else m_ref[...]
        mt = jnp.max(s, axis=0, keepdims=True)
        p_ref[slot][...] = jnp.exp2(s - m).astype(BF16)
        m_new = jnp.maximum(m, mt)
        m_ref[...] = m_new
        al_ref[slot][...] = jnp.exp2(m - m_new)
        lag_ref[...] = (mt - m) if first else jnp.maximum(lag_ref[...], mt - m)

    def stage_pv_lagged(t, slot):
        acc_ref[...] = (acc_ref[...] + jnp.dot(vT_ref[0, 0, t], p_ref[slot][...],
                                               preferred_element_type=F32)) * al_ref[slot][...]

    def lagged_step(t, slot, tbl_tile, prev=None):
        stage_lagged(t, slot, tbl_tile)
        stage_pv_lagged(t - 1 if prev is None else prev, 1 - slot)

    def exceeded():
        return jnp.max(lag_ref[...]) > MAX_LAG

    def gated_output(gs_ref):
        acc = acc_ref[...]
        o = acc[0:HEAD_DIM] / jnp.maximum(acc[HEAD_DIM:HEAD_DIM + 1], 1e-30)
        return [o[:, r * tq:(r + 1) * tq] * gs_ref[0, 0, 0, r:r + 1, :] for r in range(GQA_REP)]

    return SimpleNamespace(reset=reset, serial=serial, stage_lagged=stage_lagged, stage_pv_lagged=stage_pv_lagged,
                           lagged_step=lagged_step, exceeded=exceeded, gated_output=gated_output)


SEL_TBL_TILES = 2
WIN_TBL_TILES = 3


def _branches_kernel(rb_ref, qT_ref, selb_ref, ksel_ref, vselT_ref, kwin_ref, vwinT_ref, gs_sel_ref, gs_win_ref,
                     o_ref, tbl_ref, qa_ref, s_ref, mt_ref, *state, tq, nsel, kd):
    qt = pl.program_id(2)
    n = qt + 1
    nq = GQA_REP * tq

    q = qT_ref[0]
    qa_ref[0:HEAD_DIM, :] = jnp.concatenate([q[r * HEAD_DIM:(r + 1) * HEAD_DIM] for r in range(GQA_REP)], axis=1)
    qa_ref[HEAD_DIM:HEAD_DIM + nsel, :] = jnp.concatenate([selb_ref[0, 0, 0]] * GQA_REP, axis=1)
    if kd > HEAD_DIM + nsel:
        qa_ref[HEAD_DIM + nsel:kd, :] = jnp.zeros((kd - HEAD_DIM - nsel, nq), BF16)

    sel = _branch_stages(rb_ref, ksel_ref, vselT_ref, tbl_ref.at[0:SEL_TBL_TILES * tq], qa_ref, s_ref, mt_ref,
                         *state[:N_BRANCH_STATE], tq=tq, n_tbl_tiles=SEL_TBL_TILES, window=None)
    win = _branch_stages(rb_ref, kwin_ref, vwinT_ref,
                         tbl_ref.at[SEL_TBL_TILES * tq:(SEL_TBL_TILES + WIN_TBL_TILES) * tq], qa_ref.at[0:LANES],
                         s_ref, mt_ref, *state[N_BRANCH_STATE:], tq=tq, n_tbl_tiles=WIN_TBL_TILES, window=WINDOW)

    @pl.when((n >= 4) & ((n & 1) == 0))
    def _():
        sel.stage_lagged(0, 1, None, first=True)
        sel.lagged_step(1, 0, None)

    @pl.when((n >= 4) & ((n & 1) == 1))
    def _():
        sel.stage_lagged(0, 0, None, first=True)

    @pl.when(n >= 4)
    def _():
        t0 = 2 - (n & 1)
        pairs = (n - 2 - t0) >> 1

        def pair(t):
            sel.lagged_step(t, 1, None)
            sel.lagged_step(t + 1, 0, None)

        big = MAX_BLOCK_PAIRS

        def body(j, carry):
            for i in range(big):
                pair(t0 + 2 * (big * j + i))
            return carry
        lax.fori_loop(0, pairs >> (big.bit_length() - 1), body, 0)

        bp = big // 2
        while bp >= 1:
            @pl.when((pairs & bp) == bp)
            def _(bp=bp):
                t = n - 2 - 2 * (pairs & (2 * bp - 1))
                for i in range(bp):
                    pair(t + 2 * i)
            bp //= 2

    def window_tiles(after_first=lambda: None):
        win.stage_lagged(qt, 0, 2, first=True)
        after_first()
        win.lagged_step(qt - 1, 1, 1, prev=qt)
        win.lagged_step(qt - 2, 0, 0, prev=qt - 1)
        win.stage_pv_lagged(qt - 2, 0)

    def window_exact_if_exceeded():
        @pl.when(win.exceeded())
        def _():
            win.reset()
            for j in range(WIN_TBL_TILES):
                win.serial(qt - (WIN_TBL_TILES - 1) + j, j)

    @pl.when(n >= 4)
    def _():
        sel.lagged_step(n - 2, 1, 0)
        sel.lagged_step(n - 1, 0, 1)
        window_tiles(after_first=lambda: sel.stage_pv_lagged(n - 1, 0))

        @pl.when(sel.exceeded())
        def _():
            sel.reset()

            def redo(t, carry):
                sel.serial(t, None)
                return carry
            lax.fori_loop(0, n - 2, redo, 0)
            sel.serial(n - 2, 0)
            sel.serial(n - 1, 1)

        window_exact_if_exceeded()

    @pl.when(n == 3)
    def _():
        sel.serial(0, None)

    @pl.when((n >= 2) & (n < 4))
    def _():
        sel.serial(qt - 1, 0)

    @pl.when(n < 4)
    def _():
        sel.serial(qt, 1)

    @pl.when(n == 3)
    def _():
        window_tiles()
        window_exact_if_exceeded()

    @pl.when(qt == 1)
    def _():
        win.serial(0, 1)

    @pl.when(qt < WIN_TBL_TILES - 1)
    def _():
        win.serial(qt, 2)

    for r, (so, wo) in enumerate(zip(sel.gated_output(gs_sel_ref), win.gated_output(gs_win_ref))):
        o_ref[0, r * HEAD_DIM:(r + 1) * HEAD_DIM, :] = (so + wo).astype(o_ref.dtype)


def _out_step(oc_ref, osw_ref, za_ref, u_ref, cb_ref, zc_ref, hu_ref, x_ref, keep_ref,
              cw_ref, wo_ref, fw_ref, out_ref):
    f32 = lambda ref: ref[0].astype(F32)
    attn = (f32(oc_ref) + f32(osw_ref)).T * _silu(f32(za_ref))

    u = f32(u_ref)
    halo = f32(hu_ref) * keep_ref[0, 0, 0:1, 0:1]
    row = lax.broadcasted_iota(jnp.int32, u.shape, 0)
    h1, h2 = halo[HALO_ROWS - 1:HALO_ROWS], halo[HALO_ROWS - 2:HALO_ROWS - 1]
    u1 = jnp.where(row == 0, h1, pltpu.roll(u, 1, 0))
    u2 = jnp.where(row == 0, h2, jnp.where(row == 1, h1, pltpu.roll(u, 2, 0)))
    y = cw_ref[0:1] * u2 + cw_ref[1:2] * u1 + cw_ref[2:3] * u
    conv = f32(cb_ref) * y * _silu(f32(zc_ref))

    mixed = jnp.concatenate([attn, conv], axis=1).astype(BF16)
    y2 = x_ref[0] + jnp.dot(mixed, wo_ref[...], preferred_element_type=F32)
    ms = jnp.mean(y2 * y2, axis=-1, keepdims=True)
    out_ref[0] = y2 * lax.rsqrt(ms + EPS) * fw_ref[...]


def _out_kernel(*refs, grid, in_specs, out_spec):
    pltpu.emit_pipeline(_out_step, grid=grid, in_specs=in_specs, out_specs=[out_spec])(*refs)


def _overlap_t(nsel, nch):
    c = np.arange(nch)[None, :] * CMP_STRIDE
    n = np.arange(nsel)[:, None] * SEL_BLOCK
    return ((c < n + SEL_BLOCK) & (c + CMP_BLOCK > n)).astype(np.float32)


def _cparams(sem):
    return pltpu.CompilerParams(dimension_semantics=sem, vmem_limit_bytes=VMEM_LIMIT)


def kernel(x, norm_w, w_in, w_ck1, w_ck2, pe_k, w_cv1, w_cv2, pe_v, conv_w, w_out, rel_bias, final_norm_w):
    b, s, _ = x.shape
    assert norm_w.shape[0] == 1 and s % ROW_TILE == 0 and ROW_TILE % KEY_TILE == 0
    assert WINDOW == 2 * Q_TILE and KEY_TILE == Q_TILE
    tm = ROW_TILE
    nt = s // tm
    nkt = s // KEY_TILE
    kpt = tm // KEY_TILE
    nch = s // CMP_STRIDE
    nsel = s // SEL_BLOCK
    k_top = min(N_SEL, nsel)
    kd = -(-(HEAD_DIM + nsel) // LANES) * LANES
    G, R = N_KV_HEADS, GQA_REP

    w_in_t = jnp.swapaxes(w_in[0], 0, 1)
    assert w_in_t.shape[0] == _C_GATE + N_GATE
    eye = jnp.eye(G, dtype=F32)

    def expand_pe(pe):
        pe = pe.reshape(2, CMP_STRIDE, 1, HEAD_DIM)
        return jnp.broadcast_to(pe, (2, CMP_STRIDE, G, HEAD_DIM)).reshape(2, 1, CMP_STRIDE * D_KV)

    def expand2(w2):
        return jnp.einsum('hd,ab->ahbd', w2, eye).reshape(G * CMP_HIDDEN, D_KV)

    w1 = jnp.concatenate([w_ck1, w_cv1], axis=0)
    pee = jnp.stack([expand_pe(pe_k[0]), expand_pe(pe_v[0])])
    w2e = jnp.stack([expand2(w_ck2[0]), expand2(w_cv2[0])]).astype(BF16)
    rel_bias = rel_bias.astype(F32)
    smem = pl.BlockSpec(memory_space=pltpu.SMEM)

    n_rest = D_ATTN + 3 * D_CONV
    cw = CMP_STRIDE * D_KV
    qT, kvc, ksel, vselT, kwin, vwinT, gsT, rest = pl.pallas_call(
        partial(_proj_kernel, tm=tm, nsel=nsel, kd=kd),
        grid=(b, nt),
        in_specs=[pl.BlockSpec((1, tm, D_MODEL), lambda i, j: (i, j, 0)),
                  pl.BlockSpec((1, D_MODEL), lambda i, j: (0, 0)),
                  pl.BlockSpec((_C_GATE + N_GATE, D_MODEL), lambda i, j: (0, 0), pipeline_mode=pl.Buffered(1))],
        out_specs=[pl.BlockSpec((1, D_ATTN, tm), lambda i, j: (i, 0, j)),
                   pl.BlockSpec((2, 1, tm // CMP_STRIDE, cw), lambda i, j: (0, i, j, 0)),
                   pl.BlockSpec((1, G, tm, kd), lambda i, j: (i, 0, j, 0)),
                   pl.BlockSpec((1, G, kpt, V_ROWS, KEY_TILE), lambda i, j: (i, 0, j, 0, 0)),
                   pl.BlockSpec((1, G, tm, LANES), lambda i, j: (i, 0, j, 0)),
                   pl.BlockSpec((1, G, kpt, V_ROWS, KEY_TILE), lambda i, j: (i, 0, j, 0, 0)),
                   pl.BlockSpec((1, N_GATE, tm), lambda i, j: (i, 0, j)),
                   pl.BlockSpec((1, tm, n_rest), lambda i, j: (i, j, 0))],
        out_shape=[jax.ShapeDtypeStruct((b, D_ATTN, s), BF16),
                   jax.ShapeDtypeStruct((2, b, nch, cw), F32),
                   jax.ShapeDtypeStruct((b, G, s, kd), BF16),
                   jax.ShapeDtypeStruct((b, G, nkt, V_ROWS, KEY_TILE), BF16),
                   jax.ShapeDtypeStruct((b, G, s, LANES), BF16),
                   jax.ShapeDtypeStruct((b, G, nkt, V_ROWS, KEY_TILE), BF16),
                   jax.ShapeDtypeStruct((b, N_GATE, s), F32),
                   jax.ShapeDtypeStruct((b, s, n_rest), BF16)],
        scratch_shapes=[pltpu.VMEM((2, tm, D_KV), F32),
                        pltpu.VMEM((_C_END, D_MODEL), BF16)],
        compiler_params=_cparams(("arbitrary", "arbitrary")),
        name="in_proj",
    )(x, norm_w, w_in_t)
    gs = gsT.reshape(b, 3, G, R, s)

    lhs_rows = V_ROWS + nsel
    lhs_tail = jnp.asarray(np.concatenate([np.ones((V_ROWS - HEAD_DIM, nch), np.float32),
                                           _overlap_t(nsel, nch)], axis=0), BF16)
    kc_aug, c_lhs = pl.pallas_call(
        partial(_compress_kernel, nch=nch),
        grid=(2, b),
        in_specs=[pl.BlockSpec((1, 1, nch, cw), lambda xk, i: (xk, i, 0, 0)),
                  pl.BlockSpec((1, 2, 1, cw), lambda xk, i: (xk, 0, 0, 0)),
                  pl.BlockSpec((1, CMP_BLOCK, HEAD_DIM, CMP_HIDDEN), lambda xk, i: (xk, 0, 0, 0)),
                  pl.BlockSpec((1, G * CMP_HIDDEN, D_KV), lambda xk, i: (xk, 0, 0)),
                  pl.BlockSpec((lhs_rows - HEAD_DIM, nch), lambda xk, i: (0, 0))],
        out_specs=[pl.BlockSpec((1, 1, G, nch, LANES), lambda xk, i: (xk, i, 0, 0, 0)),
                   pl.BlockSpec((1, 1, G, lhs_rows, nch), lambda xk, i: (xk, i, 0, 0, 0))],
        out_shape=[jax.ShapeDtypeStruct((2, b, G, nch, LANES), BF16),
                   jax.ShapeDtypeStruct((2, b, G, lhs_rows, nch), BF16)],
        scratch_shapes=[pltpu.VMEM((CMP_BLOCK // CMP_STRIDE, cw, G * CMP_HIDDEN), BF16)],
        compiler_params=_cparams(("arbitrary", "arbitrary")),
        name="compress",
    )(kvc, pee, w1, w2e, lhs_tail)

    tq3 = Q_TILE_CMP
    cpq = tq3 // CMP_STRIDE
    assert nch // SUBLANES <= LANES - HEAD_DIM and k_top > N_FORCED and (nch // 4) % LANES == 0
    ocT, selb = pl.pallas_call(
        partial(_cmp_kernel, nch=nch, nsel=nsel, tq=tq3, k_top=k_top, nqt=s // tq3),
        grid=(b, G, s // tq3),
        in_specs=[smem,
                  pl.BlockSpec((1, R * HEAD_DIM, tq3), lambda i, g, t: (i, g, t)),
                  pl.BlockSpec((1, 1, 1, nch, LANES), lambda i, g, t: (0, i, g, 0, 0)),
                  pl.BlockSpec((1, 1, 1, lhs_rows, nch), lambda i, g, t: (1, i, g, 0, 0)),
                  pl.BlockSpec((1, 1, 1, R, tq3), lambda i, g, t: (i, 0, g, 0, t))],
        out_specs=[pl.BlockSpec((1, R * HEAD_DIM, tq3), lambda i, g, t: (i, g, t)),
                   pl.BlockSpec((1, 1, s // tq3, nsel, tq3), lambda i, g, t: (i, g, 0, 0, 0))],
        out_shape=[jax.ShapeDtypeStruct((b, D_ATTN, s), BF16),
                   jax.ShapeDtypeStruct((b, G, s // tq3, nsel, tq3), BF16)],
        scratch_shapes=[pltpu.VMEM((LANES, R * tq3), BF16),
                        pltpu.VMEM((cpq + BAND_BACK, R * tq3), F32),
                        pltpu.VMEM((lhs_rows, R * tq3), F32),
                        pltpu.VMEM((nsel, tq3), F32),
                        pltpu.VMEM((1, tq3), F32)]
                       + [pltpu.VMEM((nch, tq3), F32)] * R,
        compiler_params=_cparams(("parallel", "parallel", "arbitrary")),
        name="cmp_select",
    )(rel_bias, qT, kc_aug, c_lhs, gs)

    tq = Q_TILE
    nq = R * tq

    o_spec = pl.BlockSpec((1, R * HEAD_DIM, tq), lambda i, g, t: (i, g, t))
    k_spec = lambda kdim: pl.BlockSpec((1, 1, s, kdim), lambda i, g, t: (i, g, 0, 0))
    v_spec = pl.BlockSpec((1, 1, nkt, V_ROWS, KEY_TILE), lambda i, g, t: (i, g, 0, 0, 0))
    gs_spec = lambda branch: pl.BlockSpec((1, 1, 1, R, tq), lambda i, g, t: (i, branch, g, 0, t))
    oswT = pl.pallas_call(
        partial(_branches_kernel, tq=tq, nsel=nsel, kd=kd),
        grid=(b, G, s // tq),
        in_specs=[smem, o_spec,
                  pl.BlockSpec((1, 1, 1, nsel, tq), lambda i, g, t: (i, g, t // (tq3 // tq), 0, t % (tq3 // tq))),
                  k_spec(kd), v_spec, k_spec(LANES), v_spec, gs_spec(1), gs_spec(2)],
        out_specs=o_spec,
        out_shape=jax.ShapeDtypeStruct((b, D_ATTN, s), BF16),
        scratch_shapes=[pltpu.VMEM(((SEL_TBL_TILES + WIN_TBL_TILES) * tq, nq), F32),
                        pltpu.VMEM((kd, nq), BF16),
                        pltpu.VMEM((tq, nq), F32),
                        pltpu.VMEM((1, nq), F32)]
                       + ([pltpu.VMEM((1, nq), F32),
                           pltpu.VMEM((V_ROWS, nq), F32),
                           pltpu.VMEM((1, nq), F32)]
                          + [pltpu.VMEM((1, nq), F32)] * N_SLOTS
                          + [pltpu.VMEM((tq, nq), BF16)] * N_SLOTS) * 2,
        compiler_params=_cparams(("parallel", "parallel", "arbitrary")),
        name="sel_win_attn",
    )(rel_bias, qT, selb, ksel, vselT, kwin, vwinT, gs, gs)

    tm, nt = OUT_TILE, s // OUT_TILE
    hb = tm // HALO_ROWS
    deep = pl.Buffered(OUT_INPUT_BUFFERS)
    col = lambda c: pl.BlockSpec((1, tm, D_CONV), lambda i, j, c=c: (i, j, c), pipeline_mode=deep)
    halo = lambda c: pl.BlockSpec((1, HALO_ROWS, D_CONV), lambda i, j, c=c: (i, jnp.maximum(j * hb - 1, 0), c))
    oT = pl.BlockSpec((1, D_ATTN, tm), lambda i, j: (i, 0, j), pipeline_mode=deep)
    inner_in = [oT, oT, col(0), col(1), col(2), col(3), halo(1),
                pl.BlockSpec((1, tm, D_MODEL), lambda i, j: (i, j, 0), pipeline_mode=deep),
                pl.BlockSpec((1, 1, SUBLANES, LANES), lambda i, j: (i, j, 0, 0)),
                pl.BlockSpec((3, D_CONV), lambda i, j: (0, 0)),
                pl.BlockSpec((D_MODEL, D_MODEL), lambda i, j: (0, 0)),
                pl.BlockSpec((1, D_MODEL), lambda i, j: (0, 0))]
    inner_out = pl.BlockSpec((1, tm, D_MODEL), lambda i, j: (i, j, 0))
    keep = jnp.broadcast_to((jnp.arange(nt) > 0).astype(F32)[None, :, None, None], (b, nt, SUBLANES, LANES))
    hbm = pl.BlockSpec(memory_space=pl.ANY)
    out = pl.pallas_call(
        partial(_out_kernel, grid=(b, nt), in_specs=inner_in, out_spec=inner_out),
        in_specs=[hbm] * len(inner_in),
        out_specs=hbm,
        out_shape=jax.ShapeDtypeStruct((b, s, D_MODEL), F32),
        compiler_params=pltpu.CompilerParams(vmem_limit_bytes=VMEM_LIMIT),
        name="out_proj",
    )(ocT, oswT, rest, rest, rest, rest, rest, x, keep, conv_w[0], w_out[0].astype(BF16),
      final_norm_w.reshape(1, D_MODEL))
    return out
```

```python
import math
from functools import partial
from types import SimpleNamespace

import numpy as np
import jax
import jax.numpy as jnp
from jax import lax
from jax.experimental import pallas as pl
from jax.experimental.pallas import tpu as pltpu

D_MODEL = 1024
N_HEADS = 8
HEAD_DIM = 64
N_KV_HEADS = 2
GQA_REP = N_HEADS // N_KV_HEADS
D_ATTN = N_HEADS * HEAD_DIM
D_KV = N_KV_HEADS * HEAD_DIM
D_CONV = D_MODEL - D_ATTN
CMP_BLOCK = 32
CMP_STRIDE = 16
CMP_HIDDEN = 2 * HEAD_DIM
SEL_BLOCK = 64
N_SEL = 16
WINDOW = 512
N_BUCKETS = 32
MAX_DISTANCE = 128
EPS = 1e-6
NEG = -1e30

LOG2E = 1.4426950408889634
LANES = 128
SUBLANES = 8
BAND_BACK = -(-(-(-(MAX_DISTANCE + CMP_BLOCK - 1) // CMP_STRIDE)) // SUBLANES) * SUBLANES
TAKEN = -2.0
NOT_CAUSAL = -1.0
N_FORCED = 3
V_ROWS = HEAD_DIM + 16
ROW_TILE = 1024
PROJ_VMEM_LIMIT = 58 * 1024 * 1024
OUT_TILE = 1024
OUT_INPUT_BUFFERS = 3
KEY_TILE = 256
Q_TILE_CMP = 512
Q_TILE = 256
HALO_ROWS = 16
MAX_BLOCK_PAIRS = 8
MAX_LAG = 64.0
N_SLOTS = 2
VMEM_LIMIT = 48 * 1024 * 1024

F32 = jnp.float32
BF16 = jnp.bfloat16

_C_Q = 0
_C_CMP = D_ATTN
_C_SW = _C_CMP + 2 * D_KV
_C_REST = _C_SW + 4 * D_KV
_C_GATE = _C_REST + D_ATTN + 4 * D_CONV
_C_END = _C_GATE + LANES
N_GATE = 3 * N_HEADS


def _bucket_lower_bounds():
    max_exact = N_BUCKETS // 2
    d = np.arange(2 * MAX_DISTANCE, dtype=np.int32)
    nf = np.maximum(d, 1).astype(np.float32)
    large = max_exact + (np.log(nf / np.float32(max_exact)) / np.float32(math.log(MAX_DISTANCE / max_exact))
                         * np.float32(N_BUCKETS - max_exact)).astype(np.int32)
    bucket = np.where(d < max_exact, d, np.minimum(large, N_BUCKETS - 1))
    assert np.all(np.diff(bucket) >= 0) and np.all(bucket[MAX_DISTANCE:] == N_BUCKETS - 1)
    return [int(np.argmax(bucket >= b)) for b in range(N_BUCKETS)]


_BUCKET_LO = _bucket_lower_bounds()


def _rel_bias_delta(dist, bias_of_bucket):
    last = bias_of_bucket(N_BUCKETS - 1)
    val = jnp.full(dist.shape, (bias_of_bucket(0) - last) * LOG2E, F32)
    for b in range(1, N_BUCKETS - 1):
        val = jnp.where(dist >= _BUCKET_LO[b], (bias_of_bucket(b) - last) * LOG2E, val)
    val = jnp.where(dist >= _BUCKET_LO[N_BUCKETS - 1], 0.0, val)
    return jnp.where(dist < 0, NEG, val)


def _silu(z):
    return z * (1.0 / (1.0 + jnp.exp2(z * -LOG2E)))


def _proj_kernel(x_ref, nw_ref, wT_ref, qT_ref, kvc_ref, ksel_ref, vselT_ref, kwin_ref, vwinT_ref,
                 gsT_ref, rest_ref, kv_scr, w_scr, *, tm, nsel, kd):
    st = pl.program_id(1)

    @pl.when((pl.program_id(0) == 0) & (st == 0))
    def _():
        c0 = _C_REST
        chunk = KEY_TILE
        for dst, src, rows in ((0, 0, c0), (c0, c0 + N_GATE, _C_GATE - c0)):
            for r in range(0, rows, chunk):
                w_scr[dst + r:dst + r + chunk, :] = wT_ref[src + r:src + r + chunk, :].astype(BF16)
        w_scr[_C_GATE:_C_END, :] = jnp.concatenate(
            [wT_ref[c0:c0 + N_GATE, :], jnp.zeros((LANES - N_GATE, D_MODEL), F32)], axis=0).astype(BF16)

    x = x_ref[0]
    ms = jnp.mean(x * x, axis=-1, keepdims=True)
    h = (x * lax.rsqrt(ms + EPS) * nw_ref[...]).astype(BF16)

    def mm(lo, hi):
        return lax.dot_general(h, w_scr[lo:hi, :], (((1,), (1,)), ((), ())), preferred_element_type=F32)

    q = mm(_C_Q, _C_CMP) * (HEAD_DIM ** -0.5 * LOG2E)
    qT_ref[0] = q.T.astype(BF16)

    kvc = mm(_C_CMP, _C_SW)
    for xk in range(2):
        kv_scr[xk] = kvc[:, xk * D_KV:(xk + 1) * D_KV]
        for j in range(CMP_STRIDE):
            kvc_ref[xk, 0, :, j * D_KV:(j + 1) * D_KV] = kv_scr[xk, pl.ds(j, tm // CMP_STRIDE, stride=CMP_STRIDE), :]

    a = mm(_C_SW, _C_REST)
    pos = st * tm + lax.broadcasted_iota(jnp.int32, (tm, nsel), 0)
    col = lax.broadcasted_iota(jnp.int32, (tm, nsel), 1)
    onehot = jnp.where((pos // SEL_BLOCK) == col, 1.0, 0.0).astype(BF16)
    ones = jnp.ones((V_ROWS - HEAD_DIM, tm), BF16)
    zeros_k = jnp.zeros((tm, LANES - HEAD_DIM), BF16)
    vsT = a[:, D_KV:2 * D_KV].T
    vwT = a[:, 3 * D_KV:4 * D_KV].T
    for g in range(N_KV_HEADS):
        lo, hi = g * HEAD_DIM, (g + 1) * HEAD_DIM
        parts = [a[:, lo:hi].astype(BF16), onehot]
        if kd > HEAD_DIM + nsel:
            parts.append(jnp.zeros((tm, kd - HEAD_DIM - nsel), BF16))
        ksel_ref[0, g] = jnp.concatenate(parts, axis=1)
        kwin_ref[0, g] = jnp.concatenate([a[:, 2 * D_KV + lo:2 * D_KV + hi].astype(BF16), zeros_k], axis=1)
        vs = jnp.concatenate([vsT[lo:hi].astype(BF16), ones], axis=0)
        vw = jnp.concatenate([vwT[lo:hi].astype(BF16), ones], axis=0)
        for kt in range(tm // KEY_TILE):
            vselT_ref[0, g, kt] = vs[:, kt * KEY_TILE:(kt + 1) * KEY_TILE]
            vwinT_ref[0, g, kt] = vw[:, kt * KEY_TILE:(kt + 1) * KEY_TILE]

    r = mm(_C_REST, _C_GATE)
    z_attn, conv_h, conv_b, conv_c, z_conv = (r[:, i * D_CONV:(i + 1) * D_CONV] for i in range(5))
    rest_ref[0] = jnp.concatenate([z_attn, conv_c * conv_h, conv_b, z_conv], axis=1).astype(BF16)
    gates = jax.nn.sigmoid(mm(_C_GATE, _C_END))
    gsT_ref[0] = gates.T[:N_GATE]


def _compress_kernel(c_ref, pe_ref, w1_ref, w2_ref, lhs_tail_ref, aug_ref, t_ref, w1_scr, *, nch):
    @pl.when(pl.program_id(1) == 0)
    def _():
        w1_scr[...] = jnp.zeros(w1_scr.shape, BF16)
        for m in range(CMP_BLOCK // CMP_STRIDE):
            for j in range(CMP_STRIDE):
                blk = w1_ref[0, m * CMP_STRIDE + j].astype(BF16)
                for g in range(N_KV_HEADS):
                    r0 = (j * N_KV_HEADS + g) * HEAD_DIM
                    w1_scr[m, r0:r0 + HEAD_DIM, g * CMP_HIDDEN:(g + 1) * CMP_HIDDEN] = blk

    c = c_ref[0, 0]
    a0 = jnp.dot((c + pe_ref[0, 0]).astype(BF16), w1_scr[0], preferred_element_type=F32)
    a1 = jnp.dot((c + pe_ref[0, 1]).astype(BF16), w1_scr[1], preferred_element_type=F32)
    hid = a0 + pltpu.roll(a1, nch - 1, 0)
    out = jnp.dot(_silu(hid).astype(BF16), w2_ref[0], preferred_element_type=F32)
    grp = lax.broadcasted_iota(jnp.int32, (nch, LANES - HEAD_DIM), 0) // SUBLANES
    col = lax.broadcasted_iota(jnp.int32, (nch, LANES - HEAD_DIM), 1)
    onehot = jnp.where(grp == col, 1.0, 0.0).astype(BF16)
    out_t = out.T.astype(BF16)
    for g in range(N_KV_HEADS):
        lo, hi = g * HEAD_DIM, (g + 1) * HEAD_DIM
        aug_ref[0, 0, g] = jnp.concatenate([out[:, lo:hi].astype(BF16), onehot], axis=1)
        t_ref[0, 0, g] = jnp.concatenate([out_t[lo:hi], lhs_tail_ref[...]], axis=0)


def _cmp_kernel(rb_ref, qT_ref, kc_ref, lhs_ref, gs_ref, ocT_ref, selb_ref,
                qa_ref, band_ref, acc_ref, score_ref, ntk_ref, s0_ref, s1_ref, s2_ref, s3_ref,
                *, nch, nsel, tq, k_top, nqt):
    g = pl.program_id(1)
    qt = pl.program_id(2)
    s_refs = (s0_ref, s1_ref, s2_ref, s3_ref)
    cpq = tq // CMP_STRIDE
    win = cpq + BAND_BACK
    lim = cpq * (qt + 1)
    nq = GQA_REP * tq

    @pl.when(qt <= 1)
    def _():
        row = lax.broadcasted_iota(jnp.int32, (win, tq), 0) + jnp.maximum(cpq * qt - BAND_BACK, 0)
        lane = lax.broadcasted_iota(jnp.int32, (win, tq), 1)
        dist = (qt * tq + lane) - (row * CMP_STRIDE + CMP_BLOCK - 1)
        for r in range(GQA_REP):
            band_ref[:, r * tq:(r + 1) * tq] = _rel_bias_delta(dist, lambda b, r=r: rb_ref[b, g * GQA_REP + r])

    q = qT_ref[0]
    qa_ref[0:HEAD_DIM, :] = jnp.concatenate([q[r * HEAD_DIM:(r + 1) * HEAD_DIM] for r in range(GQA_REP)], axis=1)
    grp = lax.broadcasted_iota(jnp.int32, (LANES - HEAD_DIM, nq), 0)
    qa_ref[HEAD_DIM:LANES, :] = jnp.where(grp * SUBLANES >= lim, NEG, 0.0).astype(BF16)

    def run(rows, nblk):
        start = pl.multiple_of(jnp.clip(cpq * qt - BAND_BACK, 0, rows - win), SUBLANES)
        k = kc_ref[0, 0, 0, 0:rows, :]
        lhs = lhs_ref[0, 0, 0, 0:V_ROWS + nblk, 0:rows]
        cols = [slice(r * tq, (r + 1) * tq) for r in range(GQA_REP)]

        def qk(r):
            s_refs[r][0:rows, :] = jnp.dot(k, qa_ref[:, cols[r]], preferred_element_type=F32)

        ahead = 3
        for r in range(ahead):
            qk(r)
        for r in range(GQA_REP):
            if r + ahead < GQA_REP:
                qk(r + ahead)
            sr = s_refs[r]
            sr[pl.ds(start, win), :] = sr[pl.ds(start, win), :] + band_ref[:, cols[r]]
            s = sr[0:rows, :]
            p = jnp.exp2(s - jnp.max(s, axis=0, keepdims=True)).astype(BF16)
            acc_ref[0:V_ROWS + nblk, cols[r]] = jnp.dot(lhs, p, preferred_element_type=F32)

    n = lax.broadcasted_iota(jnp.int32, (nsel, tq), 0)
    lane = lax.broadcasted_iota(jnp.int32, (nsel, tq), 1)

    def forced_and_candidate(tile):
        blk = (tile * tq + lane) // SEL_BLOCK
        forced = (n == 0) | (n == blk) | (n == blk - 1)
        return forced, (n <= blk) & jnp.logical_not(forced), n <= blk

    def select(tile, nblk):
        score = score_ref[0:nblk, :]
        for _ in range(k_top - N_FORCED):
            score = jnp.where(score == jnp.max(score, axis=0, keepdims=True), TAKEN, score)
        taken = score == TAKEN
        slot = jnp.maximum(tile, 0)
        selb_ref[0, 0, slot, 0:nblk, :] = jnp.where(taken, 0.0, NEG).astype(BF16)
        if nblk < nsel:
            selb_ref[0, 0, slot, nblk:nsel, :] = jnp.full((nsel - nblk, tq), NEG, BF16)
        candidate = forced_and_candidate(tile)[1][0:nblk]
        ntk_ref[...] = jnp.sum(jnp.where(taken & candidate, 1.0, 0.0), axis=0, keepdims=True)

    def select_exact_if_tied(tile):
        @pl.when(jnp.max(ntk_ref[...]) > k_top - N_FORCED)
        def _():
            def pick_one(_, score):
                mx = jnp.max(score, axis=0, keepdims=True)
                idx = jnp.min(jnp.where(score == mx, n, nsel), axis=0, keepdims=True)
                return jnp.where(n == idx, TAKEN, score)

            taken = lax.fori_loop(0, k_top - N_FORCED, pick_one, score_ref[...]) == TAKEN
            selb_ref[0, 0, jnp.maximum(tile, 0)] = jnp.where(taken, 0.0, NEG).astype(BF16)

    @pl.when(qt == 0)
    def _():
        score_ref[...] = jnp.full((nsel, tq), TAKEN, F32)
        acc_ref[...] = jnp.zeros(acc_ref.shape, F32)

    quarter = nch // 4
    for v in range(1, 5):
        @pl.when((lim > (v - 1) * quarter) & (lim <= v * quarter))
        def _(v=v):
            run(v * quarter, v * nsel // 4)
            select(qt - 1, v * nsel // 4)

    select_exact_if_tied(qt - 1)

    l = acc_ref[HEAD_DIM:HEAD_DIM + 1, :]
    t4 = qt * tq + (lax.broadcasted_iota(jnp.int32, l.shape, 1) & (tq - 1))
    inv = jnp.where(t4 >= CMP_BLOCK - 1, 1.0 / jnp.maximum(l, 1e-30), 0.0)
    imp = None
    for r in range(GQA_REP):
        cs = slice(r * tq, (r + 1) * tq)
        ocT_ref[0, r * HEAD_DIM:(r + 1) * HEAD_DIM, :] = (acc_ref[0:HEAD_DIM, cs] * inv[:, cs]
                                                          * gs_ref[0, 0, 0, r:r + 1, :]).astype(BF16)
        part = acc_ref[V_ROWS:V_ROWS + nsel, cs] * inv[:, cs]
        imp = part if imp is None else imp + part

    forced, _, causal = forced_and_candidate(qt)
    score_ref[...] = jnp.where(forced, TAKEN, jnp.where(causal, imp, NOT_CAUSAL))

    @pl.when(qt == nqt - 1)
    def _():
        select(qt, nsel)
        select_exact_if_tied(qt)


N_BRANCH_STATE = 3 + 2 * N_SLOTS


def _branch_stages(rb_ref, k_ref, vT_ref, tbl_ref, qa_ref, s_ref, mt_ref, m_ref, acc_ref, lag_ref, *slot_refs,
                   tq, n_tbl_tiles, window):
    al_ref, p_ref = slot_refs[0:N_SLOTS], slot_refs[N_SLOTS:2 * N_SLOTS]
    g = pl.program_id(1)
    qt = pl.program_id(2)
    nq = GQA_REP * tq

    @pl.when(qt == 0)
    def _():
        for j in range(n_tbl_tiles):
            key = lax.broadcasted_iota(jnp.int32, (tq, tq), 0) + j * tq
            qry = lax.broadcasted_iota(jnp.int32, (tq, tq), 1) + (n_tbl_tiles - 1) * tq
            dist = qry - key
            for r in range(GQA_REP):
                val = _rel_bias_delta(dist, lambda b, r=r: rb_ref[b, g * GQA_REP + r])
                if window is not None:
                    val = jnp.where(dist >= window, NEG, val)
                tbl_ref[j * tq:(j + 1) * tq, r * tq:(r + 1) * tq] = val

    def reset():
        m_ref[...] = jnp.full((1, nq), NEG, F32)
        acc_ref[...] = jnp.zeros((V_ROWS, nq), F32)

    reset()

    def serial(t, tbl_tile):
        k = k_ref[0, 0, pl.ds(pl.multiple_of(t * tq, tq), tq), :]
        s = jnp.dot(k, qa_ref[...], preferred_element_type=F32)
        if tbl_tile is not None:
            s = s + tbl_ref[tbl_tile * tq:(tbl_tile + 1) * tq, :]
        s_ref[...] = s
        mt_ref[...] = jnp.max(s, axis=0, keepdims=True)
        m_old = m_ref[...]
        m_new = jnp.maximum(m_old, mt_ref[...])
        m_ref[...] = m_new
        p_ref[0][...] = jnp.exp2(s_ref[...] - m_new).astype(BF16)
        acc_ref[...] = jnp.exp2(m_old - m_new) * acc_ref[...] + jnp.dot(vT_ref[0, 0, t], p_ref[0][...],
                                                                        preferred_element_type=F32)

    def stage_lagged(t, slot, tbl_tile, first=False):
        k = k_ref[0, 0, pl.ds(pl.multiple_of(t * tq, tq), tq), :]
        s = jnp.dot(k, qa_ref[...], preferred_element_type=F32)
        if tbl_tile is not None:
            s = s + tbl_ref[tbl_tile * tq:(tbl_tile + 1) * tq, :]
        m = s[0:1, :] if first else m_ref[...]
        mt = jnp.max(s, axis=0, keepdims=True)
        p_ref[slot][...] = jnp.exp2(s - m).astype(BF16)
        m_new = jnp.maximum(m, mt)
        m_ref[...] = m_new
        al_ref[slot][...] = jnp.exp2(m - m_new)
        lag_ref[...] = (mt - m) if first else jnp.maximum(lag_ref[...], mt - m)

    def stage_pv_lagged(t, slot):
        acc_ref[...] = (acc_ref[...] + jnp.dot(vT_ref[0, 0, t], p_ref[slot][...],
                                               preferred_element_type=F32)) * al_ref[slot][...]

    def lagged_step(t, slot, tbl_tile, prev=None):
        stage_lagged(t, slot, tbl_tile)
        stage_pv_lagged(t - 1 if prev is None else prev, 1 - slot)

    def exceeded():
        return jnp.max(lag_ref[...]) > MAX_LAG

    def gated_output(gs_ref):
        acc = acc_ref[...]
        o = acc[0:HEAD_DIM] / jnp.maximum(acc[HEAD_DIM:HEAD_DIM + 1], 1e-30)
        return [o[:, r * tq:(r + 1) * tq] * gs_ref[0, 0, 0, r:r + 1, :] for r in range(GQA_REP)]

    return SimpleNamespace(reset=reset, serial=serial, stage_lagged=stage_lagged, stage_pv_lagged=stage_pv_lagged,
                           lagged_step=lagged_step, exceeded=exceeded, gated_output=gated_output)


SEL_TBL_TILES = 2
WIN_TBL_TILES = 3


def _branches_kernel(rb_ref, qT_ref, selb_ref, ksel_ref, vselT_ref, kwin_ref, vwinT_ref, gs_sel_ref, gs_win_ref,
                     o_ref, tbl_ref, qa_ref, s_ref, mt_ref, *state, tq, nsel, kd):
    qt = pl.program_id(2)
    n = qt + 1
    nq = GQA_REP * tq

    q = qT_ref[0]
    qa_ref[0:HEAD_DIM, :] = jnp.concatenate([q[r * HEAD_DIM:(r + 1) * HEAD_DIM] for r in range(GQA_REP)], axis=1)
    qa_ref[HEAD_DIM:HEAD_DIM + nsel, :] = jnp.concatenate([selb_ref[0, 0, 0]] * GQA_REP, axis=1)
    if kd > HEAD_DIM + nsel:
        qa_ref[HEAD_DIM + nsel:kd, :] = jnp.zeros((kd - HEAD_DIM - nsel, nq), BF16)

    sel = _branch_stages(rb_ref, ksel_ref, vselT_ref, tbl_ref.at[0:SEL_TBL_TILES * tq], qa_ref, s_ref, mt_ref,
                         *state[:N_BRANCH_STATE], tq=tq, n_tbl_tiles=SEL_TBL_TILES, window=None)
    win = _branch_stages(rb_ref, kwin_ref, vwinT_ref,
                         tbl_ref.at[SEL_TBL_TILES * tq:(SEL_TBL_TILES + WIN_TBL_TILES) * tq], qa_ref.at[0:LANES],
                         s_ref, mt_ref, *state[N_BRANCH_STATE:], tq=tq, n_tbl_tiles=WIN_TBL_TILES, window=WINDOW)

    @pl.when((n >= 4) & ((n & 1) == 0))
    def _():
        sel.stage_lagged(0, 1, None, first=True)
        sel.lagged_step(1, 0, None)

    @pl.when((n >= 4) & ((n & 1) == 1))
    def _():
        sel.stage_lagged(0, 0, None, first=True)

    @pl.when(n >= 4)
    def _():
        t0 = 2 - (n & 1)
        pairs = (n - 2 - t0) >> 1

        def pair(t):
            sel.lagged_step(t, 1, None)
            sel.lagged_step(t + 1, 0, None)

        big = MAX_BLOCK_PAIRS

        def body(j, carry):
            for i in range(big):
                pair(t0 + 2 * (big * j + i))
            return carry
        lax.fori_loop(0, pairs >> (big.bit_length() - 1), body, 0)

        bp = big // 2
        while bp >= 1:
            @pl.when((pairs & bp) == bp)
            def _(bp=bp):
                t = n - 2 - 2 * (pairs & (2 * bp - 1))
                for i in range(bp):
                    pair(t + 2 * i)
            bp //= 2

    def window_tiles(after_first=lambda: None):
        win.stage_lagged(qt, 0, 2, first=True)
        after_first()
        win.lagged_step(qt - 1, 1, 1, prev=qt)
        win.lagged_step(qt - 2, 0, 0, prev=qt - 1)
        win.stage_pv_lagged(qt - 2, 0)

    def window_exact_if_exceeded():
        @pl.when(win.exceeded())
        def _():
            win.reset()
            for j in range(WIN_TBL_TILES):
                win.serial(qt - (WIN_TBL_TILES - 1) + j, j)

    @pl.when(n >= 4)
    def _():
        sel.lagged_step(n - 2, 1, 0)
        sel.lagged_step(n - 1, 0, 1)
        window_tiles(after_first=lambda: sel.stage_pv_lagged(n - 1, 0))

        @pl.when(sel.exceeded())
        def _():
            sel.reset()

            def redo(t, carry):
                sel.serial(t, None)
                return carry
            lax.fori_loop(0, n - 2, redo, 0)
            sel.serial(n - 2, 0)
            sel.serial(n - 1, 1)

        window_exact_if_exceeded()

    @pl.when(n == 3)
    def _():
        sel.serial(0, None)

    @pl.when((n >= 2) & (n < 4))
    def _():
        sel.serial(qt - 1, 0)

    @pl.when(n < 4)
    def _():
        sel.serial(qt, 1)

    @pl.when(n == 3)
    def _():
        window_tiles()
        window_exact_if_exceeded()

    @pl.when(qt == 1)
    def _():
        win.serial(0, 1)

    @pl.when(qt < WIN_TBL_TILES - 1)
    def _():
        win.serial(qt, 2)

    for r, (so, wo) in enumerate(zip(sel.gated_output(gs_sel_ref), win.gated_output(gs_win_ref))):
        o_ref[0, r * HEAD_DIM:(r + 1) * HEAD_DIM, :] = (so + wo).astype(o_ref.dtype)


def _out_step(oc_ref, osw_ref, za_ref, u_ref, cb_ref, zc_ref, hu_ref, x_ref, keep_ref,
              cw_ref, wo_ref, fw_ref, out_ref):
    f32 = lambda ref: ref[0].astype(F32)
    attn = (f32(oc_ref) + f32(osw_ref)).T * _silu(f32(za_ref))

    u = f32(u_ref)
    halo = f32(hu_ref) * keep_ref[0, 0, 0:1, 0:1]
    row = lax.broadcasted_iota(jnp.int32, u.shape, 0)
    h1, h2 = halo[HALO_ROWS - 1:HALO_ROWS], halo[HALO_ROWS - 2:HALO_ROWS - 1]
    u1 = jnp.where(row == 0, h1, pltpu.roll(u, 1, 0))
    u2 = jnp.where(row == 0, h2, jnp.where(row == 1, h1, pltpu.roll(u, 2, 0)))
    y = cw_ref[0:1] * u2 + cw_ref[1:2] * u1 + cw_ref[2:3] * u
    conv = f32(cb_ref) * y * _silu(f32(zc_ref))

    mixed = jnp.concatenate([attn, conv], axis=1).astype(BF16)
    y2 = x_ref[0] + jnp.dot(mixed, wo_ref[...], preferred_element_type=F32)
    ms = jnp.mean(y2 * y2, axis=-1, keepdims=True)
    out_ref[0] = y2 * lax.rsqrt(ms + EPS) * fw_ref[...]


def _out_kernel(*refs, grid, in_specs, out_spec):
    pltpu.emit_pipeline(_out_step, grid=grid, in_specs=in_specs, out_specs=[out_spec])(*refs)


def _overlap_t(nsel, nch):
    c = np.arange(nch)[None, :] * CMP_STRIDE
    n = np.arange(nsel)[:, None] * SEL_BLOCK
    return ((c < n + SEL_BLOCK) & (c + CMP_BLOCK > n)).astype(np.float32)


def _cparams(sem, vmem_limit=VMEM_LIMIT):
    return pltpu.CompilerParams(dimension_semantics=sem, vmem_limit_bytes=vmem_limit)


def kernel(x, norm_w, w_in, w_ck1, w_ck2, pe_k, w_cv1, w_cv2, pe_v, conv_w, w_out, rel_bias, final_norm_w):
    b, s, _ = x.shape
    assert norm_w.shape[0] == 1 and s % ROW_TILE == 0 and ROW_TILE % KEY_TILE == 0
    assert WINDOW == 2 * Q_TILE and KEY_TILE == Q_TILE
    tm = ROW_TILE
    nt = s // tm
    nkt = s // KEY_TILE
    kpt = tm // KEY_TILE
    nch = s // CMP_STRIDE
    nsel = s // SEL_BLOCK
    k_top = min(N_SEL, nsel)
    kd = -(-(HEAD_DIM + nsel) // LANES) * LANES
    G, R = N_KV_HEADS, GQA_REP

    w_in_t = jnp.swapaxes(w_in[0], 0, 1)
    assert w_in_t.shape[0] == _C_GATE + N_GATE
    eye = jnp.eye(G, dtype=F32)

    def expand_pe(pe):
        pe = pe.reshape(2, CMP_STRIDE, 1, HEAD_DIM)
        return jnp.broadcast_to(pe, (2, CMP_STRIDE, G, HEAD_DIM)).reshape(2, 1, CMP_STRIDE * D_KV)

    def expand2(w2):
        return jnp.einsum('hd,ab->ahbd', w2, eye).reshape(G * CMP_HIDDEN, D_KV)

    w1 = jnp.concatenate([w_ck1, w_cv1], axis=0)
    pee = jnp.stack([expand_pe(pe_k[0]), expand_pe(pe_v[0])])
    w2e = jnp.stack([expand2(w_ck2[0]), expand2(w_cv2[0])]).astype(BF16)
    rel_bias = rel_bias.astype(F32)
    smem = pl.BlockSpec(memory_space=pltpu.SMEM)

    n_rest = D_ATTN + 3 * D_CONV
    cw = CMP_STRIDE * D_KV
    qT, kvc, ksel, vselT, kwin, vwinT, gsT, rest = pl.pallas_call(
        partial(_proj_kernel, tm=tm, nsel=nsel, kd=kd),
        grid=(b, nt),
        in_specs=[pl.BlockSpec((1, tm, D_MODEL), lambda i, j: (i, j, 0)),
                  pl.BlockSpec((1, D_MODEL), lambda i, j: (0, 0)),
                  pl.BlockSpec((_C_GATE + N_GATE, D_MODEL), lambda i, j: (0, 0), pipeline_mode=pl.Buffered(1))],
        out_specs=[pl.BlockSpec((1, D_ATTN, tm), lambda i, j: (i, 0, j)),
                   pl.BlockSpec((2, 1, tm // CMP_STRIDE, cw), lambda i, j: (0, i, j, 0)),
                   pl.BlockSpec((1, G, tm, kd), lambda i, j: (i, 0, j, 0)),
                   pl.BlockSpec((1, G, kpt, V_ROWS, KEY_TILE), lambda i, j: (i, 0, j, 0, 0)),
                   pl.BlockSpec((1, G, tm, LANES), lambda i, j: (i, 0, j, 0)),
                   pl.BlockSpec((1, G, kpt, V_ROWS, KEY_TILE), lambda i, j: (i, 0, j, 0, 0)),
                   pl.BlockSpec((1, N_GATE, tm), lambda i, j: (i, 0, j)),
                   pl.BlockSpec((1, tm, n_rest), lambda i, j: (i, j, 0))],
        out_shape=[jax.ShapeDtypeStruct((b, D_ATTN, s), BF16),
                   jax.ShapeDtypeStruct((2, b, nch, cw), F32),
                   jax.ShapeDtypeStruct((b, G, s, kd), BF16),
                   jax.ShapeDtypeStruct((b, G, nkt, V_ROWS, KEY_TILE), BF16),
                   jax.ShapeDtypeStruct((b, G, s, LANES), BF16),
                   jax.ShapeDtypeStruct((b, G, nkt, V_ROWS, KEY_TILE), BF16),
                   jax.ShapeDtypeStruct((b, N_GATE, s), F32),
                   jax.ShapeDtypeStruct((b, s, n_rest), BF16)],
        scratch_shapes=[pltpu.VMEM((2, tm, D_KV), F32),
                        pltpu.VMEM((_C_END, D_MODEL), BF16)],
        compiler_params=_cparams(("arbitrary", "arbitrary"), PROJ_VMEM_LIMIT),
        name="in_proj",
    )(x, norm_w, w_in_t)
    gs = gsT.reshape(b, 3, G, R, s)

    lhs_rows = V_ROWS + nsel
    lhs_tail = jnp.asarray(np.concatenate([np.ones((V_ROWS - HEAD_DIM, nch), np.float32),
                                           _overlap_t(nsel, nch)], axis=0), BF16)
    kc_aug, c_lhs = pl.pallas_call(
        partial(_compress_kernel, nch=nch),
        grid=(2, b),
        in_specs=[pl.BlockSpec((1, 1, nch, cw), lambda xk, i: (xk, i, 0, 0)),
                  pl.BlockSpec((1, 2, 1, cw), lambda xk, i: (xk, 0, 0, 0)),
                  pl.BlockSpec((1, CMP_BLOCK, HEAD_DIM, CMP_HIDDEN), lambda xk, i: (xk, 0, 0, 0)),
                  pl.BlockSpec((1, G * CMP_HIDDEN, D_KV), lambda xk, i: (xk, 0, 0)),
                  pl.BlockSpec((lhs_rows - HEAD_DIM, nch), lambda xk, i: (0, 0))],
        out_specs=[pl.BlockSpec((1, 1, G, nch, LANES), lambda xk, i: (xk, i, 0, 0, 0)),
                   pl.BlockSpec((1, 1, G, lhs_rows, nch), lambda xk, i: (xk, i, 0, 0, 0))],
        out_shape=[jax.ShapeDtypeStruct((2, b, G, nch, LANES), BF16),
                   jax.ShapeDtypeStruct((2, b, G, lhs_rows, nch), BF16)],
        scratch_shapes=[pltpu.VMEM((CMP_BLOCK // CMP_STRIDE, cw, G * CMP_HIDDEN), BF16)],
        compiler_params=_cparams(("arbitrary", "arbitrary")),
        name="compress",
    )(kvc, pee, w1, w2e, lhs_tail)

    tq3 = Q_TILE_CMP
    cpq = tq3 // CMP_STRIDE
    assert nch // SUBLANES <= LANES - HEAD_DIM and k_top > N_FORCED and (nch // 4) % LANES == 0
    ocT, selb = pl.pallas_call(
        partial(_cmp_kernel, nch=nch, nsel=nsel, tq=tq3, k_top=k_top, nqt=s // tq3),
        grid=(b, G, s // tq3),
        in_specs=[smem,
                  pl.BlockSpec((1, R * HEAD_DIM, tq3), lambda i, g, t: (i, g, t)),
                  pl.BlockSpec((1, 1, 1, nch, LANES), lambda i, g, t: (0, i, g, 0, 0)),
                  pl.BlockSpec((1, 1, 1, lhs_rows, nch), lambda i, g, t: (1, i, g, 0, 0)),
                  pl.BlockSpec((1, 1, 1, R, tq3), lambda i, g, t: (i, 0, g, 0, t))],
        out_specs=[pl.BlockSpec((1, R * HEAD_DIM, tq3), lambda i, g, t: (i, g, t)),
                   pl.BlockSpec((1, 1, s // tq3, nsel, tq3), lambda i, g, t: (i, g, 0, 0, 0))],
        out_shape=[jax.ShapeDtypeStruct((b, D_ATTN, s), BF16),
                   jax.ShapeDtypeStruct((b, G, s // tq3, nsel, tq3), BF16)],
        scratch_shapes=[pltpu.VMEM((LANES, R * tq3), BF16),
                        pltpu.VMEM((cpq + BAND_BACK, R * tq3), F32),
                        pltpu.VMEM((lhs_rows, R * tq3), F32),
                        pltpu.VMEM((nsel, tq3), F32),
                        pltpu.VMEM((1, tq3), F32)]
                       + [pltpu.VMEM((nch, tq3), F32)] * R,
        compiler_params=_cparams(("parallel", "parallel", "arbitrary")),
        name="cmp_select",
    )(rel_bias, qT, kc_aug, c_lhs, gs)

    tq = Q_TILE
    nq = R * tq

    o_spec = pl.BlockSpec((1, R * HEAD_DIM, tq), lambda i, g, t: (i, g, t))
    k_spec = lambda kdim: pl.BlockSpec((1, 1, s, kdim), lambda i, g, t: (i, g, 0, 0))
    v_spec = pl.BlockSpec((1, 1, nkt, V_ROWS, KEY_TILE), lambda i, g, t: (i, g, 0, 0, 0))
    gs_spec = lambda branch: pl.BlockSpec((1, 1, 1, R, tq), lambda i, g, t: (i, branch, g, 0, t))
    oswT = pl.pallas_call(
        partial(_branches_kernel, tq=tq, nsel=nsel, kd=kd),
        grid=(b, G, s // tq),
        in_specs=[smem, o_spec,
                  pl.BlockSpec((1, 1, 1, nsel, tq), lambda i, g, t: (i, g, t // (tq3 // tq), 0, t % (tq3 // tq))),
                  k_spec(kd), v_spec, k_spec(LANES), v_spec, gs_spec(1), gs_spec(2)],
        out_specs=o_spec,
        out_shape=jax.ShapeDtypeStruct((b, D_ATTN, s), BF16),
        scratch_shapes=[pltpu.VMEM(((SEL_TBL_TILES + WIN_TBL_TILES) * tq, nq), F32),
                        pltpu.VMEM((kd, nq), BF16),
                        pltpu.VMEM((tq, nq), F32),
                        pltpu.VMEM((1, nq), F32)]
                       + ([pltpu.VMEM((1, nq), F32),
                           pltpu.VMEM((V_ROWS, nq), F32),
                           pltpu.VMEM((1, nq), F32)]
                          + [pltpu.VMEM((1, nq), F32)] * N_SLOTS
                          + [pltpu.VMEM((tq, nq), BF16)] * N_SLOTS) * 2,
        compiler_params=_cparams(("parallel", "parallel", "arbitrary")),
        name="sel_win_attn",
    )(rel_bias, qT, selb, ksel, vselT, kwin, vwinT, gs, gs)

    tm, nt = OUT_TILE, s // OUT_TILE
    hb = tm // HALO_ROWS
    deep = pl.Buffered(OUT_INPUT_BUFFERS)
    col = lambda c: pl.BlockSpec((1, tm, D_CONV), lambda i, j, c=c: (i, j, c), pipeline_mode=deep)
    halo = lambda c: pl.BlockSpec((1, HALO_ROWS, D_CONV), lambda i, j, c=c: (i, jnp.maximum(j * hb - 1, 0), c))
    oT = pl.BlockSpec((1, D_ATTN, tm), lambda i, j: (i, 0, j), pipeline_mode=deep)
    inner_in = [oT, oT, col(0), col(1), col(2), col(3), halo(1),
                pl.BlockSpec((1, tm, D_MODEL), lambda i, j: (i, j, 0), pipeline_mode=deep),
                pl.BlockSpec((1, 1, SUBLANES, LANES), lambda i, j: (i, j, 0, 0)),
                pl.BlockSpec((3, D_CONV), lambda i, j: (0, 0)),
                pl.BlockSpec((D_MODEL, D_MODEL), lambda i, j: (0, 0)),
                pl.BlockSpec((1, D_MODEL), lambda i, j: (0, 0))]
    inner_out = pl.BlockSpec((1, tm, D_MODEL), lambda i, j: (i, j, 0))
    keep = jnp.broadcast_to((jnp.arange(nt) > 0).astype(F32)[None, :, None, None], (b, nt, SUBLANES, LANES))
    hbm = pl.BlockSpec(memory_space=pl.ANY)
    out = pl.pallas_call(
        partial(_out_kernel, grid=(b, nt), in_specs=inner_in, out_spec=inner_out),
        in_specs=[hbm] * len(inner_in),
        out_specs=hbm,
        out_shape=jax.ShapeDtypeStruct((b, s, D_MODEL), F32),
        compiler_params=pltpu.CompilerParams(vmem_limit_bytes=VMEM_LIMIT),
        name="out_proj",
    )(ocT, oswT, rest, rest, rest, rest, rest, x, keep, conv_w[0], w_out[0].astype(BF16),
      final_norm_w.reshape(1, D_MODEL))
    return out
```
